```python
import math
import jax, jax.numpy as jnp
from jax import lax
import numpy as np

D_MODEL = 1024
BATCH = 8
SEQ = 2048
DEPTH = 2
DEC_BATCH = 128
DEC_SEQ = 4
PAST_LEN = 16384
PAGE_SIZE = 128

DN_HEADS = 6
DN_DK = 64
DN_DV = 64
DN_QK = DN_HEADS * DN_DK
DN_W = DN_HEADS * DN_DV
DN_CONV = 2 * DN_QK + DN_W
CONV_W = 4
RW_HEADS = 6
RW_N = 64
RW_W = RW_HEADS * RW_N
RW_DECAY_R = 64
RW_AAA_R = 64
RW_GATE_R = 160
RT_HEADS = 4
RT_DK = 32
RT_DV = 64
RT_QK = RT_HEADS * RT_DK
RT_W = RT_HEADS * RT_DV
MIX_W = DN_W + RW_W + RT_W
CHUNK = 64
ROPE_BASE = 10000.0
D_FF = 2816
N_EXP = 8
TOP_K = 2
D_EXP = 3584
N_DENSE = (DEPTH + 1) // 2
N_MOE = DEPTH // 2
NORM_EPS = 1e-6
RW_GN_EPS = 64e-5
RT_GN_EPS = 1e-5

DN_SIZES = (DN_CONV, DN_W, DN_HEADS, DN_HEADS)
RW_SIZES = (RW_W, RW_W, RW_W, RW_DECAY_R, RW_AAA_R, RW_GATE_R)
RT_SIZES = (RT_QK, RT_QK, RT_W, RT_W)
DN_P = sum(DN_SIZES)
RW_P = sum(RW_SIZES)
RT_P = sum(RT_SIZES)
P_IN = DN_P + RW_P + RT_P

kernel_name = 'hybrid_deltanet_rwkv7_retention_step'


def _split(z, sizes):
    idx = [int(i) for i in np.cumsum(sizes)[:-1]]
    return jnp.split(z, idx, axis=-1)


def _rmsnorm(x, g):
    xf = x.astype(jnp.float32)
    y = xf * lax.rsqrt(jnp.mean(xf * xf, axis=-1, keepdims=True) + NORM_EPS)
    return (y * g.astype(jnp.float32)).astype(x.dtype)


def _l2norm(x):
    return x * lax.rsqrt(jnp.sum(x * x, axis=-1, keepdims=True) + 1e-6)


def _head_groupnorm(y, eps):
    d = y - jnp.mean(y, axis=-1, keepdims=True)
    return d * lax.rsqrt(jnp.mean(d * d, axis=-1, keepdims=True) + eps)


def _to_chunks(t, n, c):
    return jnp.moveaxis(t.reshape((t.shape[0], n, c) + t.shape[2:]), 3, 1)


def _from_chunks(t):
    b, h, n, c = t.shape[:4]
    return jnp.transpose(t, (0, 2, 3, 1, 4)).reshape(b, n * c, h, t.shape[-1])


def _rotary(x, pos0):
    t_len, d = x.shape[1], x.shape[-1]
    half = d // 2
    inv = ROPE_BASE ** (-jnp.arange(half, dtype=jnp.float32) / half)
    ang = (pos0 + jnp.arange(t_len, dtype=jnp.float32))[:, None] * inv[None, :]
    cos = jnp.cos(ang)[None, :, None, :]
    sin = jnp.sin(ang)[None, :, None, :]
    x1, x2 = x[..., :half], x[..., half:]
    return jnp.concatenate([x1 * cos - x2 * sin, x1 * sin + x2 * cos], axis=-1)


def _chunk_gated_delta(q, k, v, g, beta, s0):
    t_len = q.shape[1]
    c = math.gcd(CHUNK, t_len)
    n = t_len // c
    q, k, v, g, beta = (_to_chunks(t, n, c) for t in (q, k, v, g, beta))
    gc = jnp.cumsum(g, axis=-1)
    causal = jnp.tril(jnp.ones((c, c), dtype=bool))
    strict = jnp.tril(jnp.ones((c, c), dtype=bool), -1)
    dmask = jnp.exp(jnp.where(causal, gc[..., :, None] - gc[..., None, :], -jnp.inf))
    kb = k * beta[..., None]
    amat = jnp.where(strict, jnp.einsum('bhnid,bhnjd->bhnij', kb, k) * dmask, 0.0)
    eye = jnp.eye(c, dtype=amat.dtype)
    tmat = lax.linalg.triangular_solve(amat + eye, jnp.broadcast_to(eye, amat.shape),
                                       left_side=True, lower=True, unit_diagonal=True)
    u0 = jnp.einsum('bhnij,bhnje->bhnie', tmat, v * beta[..., None])
    kcd = jnp.einsum('bhnij,bhnjd->bhnid', tmat, kb * jnp.exp(gc)[..., None])
    qk = jnp.einsum('bhnid,bhnjd->bhnij', q, k) * dmask
    q_dec = q * jnp.exp(gc)[..., None]
    k_dec = k * jnp.exp(gc[..., -1:] - gc)[..., None]
    g_last = jnp.exp(gc[..., -1])

    def step(s, xs):
        u0_n, kcd_n, qk_n, qd_n, kd_n, gl_n = xs
        u = u0_n - jnp.einsum('bhcd,bhde->bhce', kcd_n, s)
        o = jnp.einsum('bhcd,bhde->bhce', qd_n, s) + jnp.einsum('bhij,bhje->bhie', qk_n, u)
        s = s * gl_n[..., None, None] + jnp.einsum('bhcd,bhce->bhde', kd_n, u)
        return s, o

    xs = tuple(jnp.moveaxis(t, 2, 0) for t in (u0, kcd, qk, q_dec, k_dec, g_last))
    s, o = lax.scan(step, s0, xs)
    return _from_chunks(jnp.moveaxis(o, 0, 2)), s


def _delta_group(zdn, conv_buf, s0, conv_w, a_log, dt_bias, norm_w):
    b, t_len, _ = zdn.shape
    qkv, zg, a_in, b_in = _split(zdn, DN_SIZES)
    xpad = jnp.concatenate([conv_buf.astype(qkv.dtype), qkv], axis=1)
    conv = jax.nn.silu(sum(xpad[:, j:j + t_len] * conv_w[j] for j in range(CONV_W)))
    new_buf = xpad[:, t_len:]
    q, k, v = _split(conv.astype(jnp.float32), (DN_QK, DN_QK, DN_W))
    q = _l2norm(q.reshape(b, t_len, DN_HEADS, DN_DK)) * DN_DK ** -0.5
    k = _l2norm(k.reshape(b, t_len, DN_HEADS, DN_DK))
    v = v.reshape(b, t_len, DN_HEADS, DN_DV)
    beta = jax.nn.sigmoid(b_in.astype(jnp.float32))
    g = -jnp.exp(a_log.astype(jnp.float32)) * jax.nn.softplus(a_in.astype(jnp.float32) + dt_bias)
    o, s = _chunk_gated_delta(q, k, v, g, beta, s0.astype(jnp.float32))
    o = o * lax.rsqrt(jnp.mean(o * o, axis=-1, keepdims=True) + NORM_EPS) * norm_w
    o = o * jax.nn.silu(zg.astype(jnp.float32).reshape(b, t_len, DN_HEADS, DN_DV))
    return o.reshape(b, t_len, DN_W).astype(zdn.dtype), new_buf, s.astype(zdn.dtype)


def _rwkv7_scan(r, w, k, v, a, bb, s0):
    def step(s, xs):
        r_t, w_t, k_t, v_t, a_t, b_t = xs
        sa = jnp.einsum('bhvk,bhk->bhv', s, a_t)
        s = s * w_t[:, :, None, :] + sa[..., None] * b_t[:, :, None, :] + v_t[..., None] * k_t[:, :, None, :]
        return s, jnp.einsum('bhvk,bhk->bhv', s, r_t)

    xs = tuple(jnp.moveaxis(t, 1, 0) for t in (r, w, k, v, a, bb))
    s, y = lax.scan(step, s0, xs)
    return jnp.moveaxis(y, 0, 1), s


def _rwkv_group(zrw, shift_prev, s0, mu, w0, w2, a0, a2, g2, k_k, k_a, r_k, ln_g, ln_b):
    b, t_len, _ = zrw.shape
    prev = jnp.concatenate([shift_prev[:, None].astype(zrw.dtype), zrw[:, :-1]], axis=1)
    zs = zrw + (prev - zrw) * mu
    new_shift = zrw[:, -1]
    r, k, v, wl, al, gl = (t.astype(jnp.float32) for t in _split(zs, RW_SIZES))
    w = -jax.nn.softplus(-(w0 + jnp.tanh(wl) @ w2)) - 0.5
    a = jax.nn.sigmoid(a0 + al @ a2)
    g = jax.nn.sigmoid(gl) @ g2
    hd = lambda t: t.reshape(b, t_len, RW_HEADS, RW_N)
    kk = _l2norm(hd(k * k_k))
    k = k * (1.0 + (a - 1.0) * k_a)
    r_h, k_h, v_h, a_h = hd(r), hd(k), hd(v), hd(a)
    decay = jnp.exp(-jnp.exp(hd(w)))
    y, s = _rwkv7_scan(r_h, decay, k_h, v_h, -kk, kk * a_h, s0.astype(jnp.float32))
    y = _head_groupnorm(y, RW_GN_EPS).reshape(b, t_len, RW_W) * ln_g + ln_b
    bonus = jnp.sum(r_h * k_h * r_k, axis=-1, keepdims=True) * v_h
    y = (y + bonus.reshape(b, t_len, RW_W)) * g
    return y.astype(zrw.dtype), new_shift, s.astype(zrw.dtype)


def _chunk_retention(q, k, v, r0):
    t_len, n_heads = q.shape[1], q.shape[2]
    c = math.gcd(CHUNK, t_len)
    n = t_len // c
    q, k, v = (_to_chunks(t, n, c) for t in (q, k, v))
    lg = jnp.log1p(-jnp.exp2(-5.0 - jnp.arange(n_heads, dtype=jnp.float32)))
    idx = jnp.arange(c, dtype=jnp.float32)
    causal = jnp.tril(jnp.ones((c, c), dtype=bool))
    dmask = jnp.exp(jnp.where(causal, (idx[:, None] - idx[None, :]) * lg[:, None, None], -jnp.inf))
    inner = jnp.einsum('bhnij,bhnje->bhnie',
                       jnp.einsum('bhnid,bhnjd->bhnij', q, k) * dmask[None, :, None], v)
    q_dec = q * jnp.exp((idx + 1.0) * lg[:, None])[None, :, None, :, None]
    k_dec = k * jnp.exp((c - 1.0 - idx) * lg[:, None])[None, :, None, :, None]
    kv = jnp.einsum('bhncd,bhnce->bhnde', k_dec, v)
    g_chunk = jnp.exp(c * lg)[None, :, None, None]

    def step(rs, xs):
        qd_n, kv_n = xs
        return rs * g_chunk + kv_n, jnp.einsum('bhcd,bhde->bhce', qd_n, rs)

    rs, cross = lax.scan(step, r0, (jnp.moveaxis(q_dec, 2, 0), jnp.moveaxis(kv, 2, 0)))
    return _from_chunks(inner + jnp.moveaxis(cross, 0, 2)), rs


def _retention_group(zrt, s0, ln_g, pos0):
    b, t_len, _ = zrt.shape
    q, k, v, gate = (t.astype(jnp.float32) for t in _split(zrt, RT_SIZES))
    q = _rotary(q.reshape(b, t_len, RT_HEADS, RT_DK), pos0)
    k = _rotary(k.reshape(b, t_len, RT_HEADS, RT_DK), pos0) * RT_DK ** -0.5
    v = v.reshape(b, t_len, RT_HEADS, RT_DV)
    o, s = _chunk_retention(q, k, v, s0.astype(jnp.float32))
    o = _head_groupnorm(o, RT_GN_EPS).reshape(b, t_len, RT_W) * ln_g * jax.nn.silu(gate)
    return o.astype(zrt.dtype), s.astype(zrt.dtype)


def _swiglu(h, w1, w3, w2):
    return (jax.nn.silu(h @ w1) * (h @ w3)) @ w2


def _moe(h, router, w1, w3, w2):
    logits = jnp.einsum('btd,de->bte', h, router).astype(jnp.float32)
    top_v, top_i = lax.top_k(logits, TOP_K)
    gates = jax.nn.softmax(top_v, axis=-1)
    combine = jnp.einsum('btk,btke->bte', gates,
                         jax.nn.one_hot(top_i, N_EXP, dtype=jnp.float32)).astype(h.dtype)
    y = jnp.zeros_like(h)
    for e in range(N_EXP):
        y = y + combine[..., e:e + 1] * _swiglu(h, w1[e], w3[e], w2[e])
    return y


def _trunk(x, conv_buf, s_dn, shift, s_rw, s_rt, pos0, W):
    convs, dns, shifts, rws, rts = [], [], [], [], []
    for l in range(DEPTH):
        h = _rmsnorm(x, W['ln_mix'][l])
        z = h @ W['w_in'][l]
        zdn, zrw, zrt = _split(z, (DN_P, RW_P, RT_P))
        o_dn, nb, nsd = _delta_group(zdn, conv_buf[l], s_dn[l], W['dn_conv'][l], W['dn_a_log'][l],
                                     W['dn_dt_bias'][l], W['dn_norm'][l])
        o_rw, nsh, nsr = _rwkv_group(zrw, shift[l], s_rw[l], W['rw_mu'][l], W['rw_w0'][l], W['rw_w2'][l],
                                     W['rw_a0'][l], W['rw_a2'][l], W['rw_g2'][l], W['rw_kk'][l],
                                     W['rw_ka'][l], W['rw_rk'][l], W['rw_ln_g'][l], W['rw_ln_b'][l])
        o_rt, nst = _retention_group(zrt, s_rt[l], W['rt_ln'][l], pos0)
        x = x + jnp.concatenate([o_dn, o_rw, o_rt], axis=-1) @ W['w_out'][l]
        h = _rmsnorm(x, W['ln_ffn'][l])
        if l % 2 == 0:
            x = x + _swiglu(h, W['ffn_w1'][l // 2], W['ffn_w3'][l // 2], W['ffn_w2'][l // 2])
        else:
            x = x + _moe(h, W['moe_router'][l // 2], W['moe_w1'][l // 2], W['moe_w3'][l // 2], W['moe_w2'][l // 2])
        convs.append(nb)
        dns.append(nsd)
        shifts.append(nsh)
        rws.append(nsr)
        rts.append(nst)
    y = _rmsnorm(x, W['ln_final'])
    return (y, jnp.stack(convs), jnp.stack(dns), jnp.stack(shifts), jnp.stack(rws), jnp.stack(rts))


def setup_inputs(seed: int = 0) -> dict:
    key = jax.random.key(seed)
    ks = iter(jax.random.split(key, 48))
    nrm = lambda shape, s: s * jax.random.normal(next(ks), shape, jnp.float32)
    unif = lambda shape, lo, hi: jax.random.uniform(next(ks), shape, jnp.float32, lo, hi)
    dt = jnp.exp(unif((DEPTH, DN_HEADS), math.log(1e-3), math.log(1e-1)))
    return {
        'x_prompt': nrm((BATCH, SEQ, D_MODEL), 1.0),
        'x_sample': nrm((DEC_BATCH, DEC_SEQ, D_MODEL), 1.0),
        'state_dn_conv': nrm((DEPTH, DEC_BATCH, CONV_W - 1, DN_CONV), 1.0),
        'state_dn': nrm((DEPTH, DEC_BATCH, DN_HEADS, DN_DK, DN_DV), 0.3),
        'state_rw_shift': nrm((DEPTH, DEC_BATCH, RW_P), 1.0),
        'state_rw': nrm((DEPTH, DEC_BATCH, RW_HEADS, RW_N, RW_N), 0.3),
        'state_rt': nrm((DEPTH, DEC_BATCH, RT_HEADS, RT_DK, RT_DV), 0.3),
        'ln_mix': 1.0 + nrm((DEPTH, D_MODEL), 0.02),
        'w_in': nrm((DEPTH, D_MODEL, P_IN), D_MODEL ** -0.5),
        'dn_conv': nrm((DEPTH, CONV_W, DN_CONV), CONV_W ** -0.5),
        'dn_a_log': jnp.log(unif((DEPTH, DN_HEADS), 1.0, 16.0)),
        'dn_dt_bias': dt + jnp.log(-jnp.expm1(-dt)),
        'dn_norm': 1.0 + nrm((DEPTH, DN_DV), 0.02),
        'rw_mu': unif((DEPTH, RW_P), 0.0, 1.0),
        'rw_w0': jnp.linspace(-6.0, -1.0, RW_W, dtype=jnp.float32)[None, :] + nrm((DEPTH, RW_W), 0.1),
        'rw_w2': nrm((DEPTH, RW_DECAY_R, RW_W), 0.5 * RW_DECAY_R ** -0.5),
        'rw_a0': nrm((DEPTH, RW_W), 0.1),
        'rw_a2': nrm((DEPTH, RW_AAA_R, RW_W), 0.5 * RW_AAA_R ** -0.5),
        'rw_g2': nrm((DEPTH, RW_GATE_R, RW_W), RW_GATE_R ** -0.5),
        'rw_kk': 0.85 + nrm((DEPTH, RW_W), 0.02),
        'rw_ka': 1.0 + nrm((DEPTH, RW_W), 0.02),
        'rw_rk': nrm((DEPTH, RW_HEADS, RW_N), 0.1),
        'rw_ln_g': 1.0 + nrm((DEPTH, RW_W), 0.02),
        'rw_ln_b': nrm((DEPTH, RW_W), 0.02),
        'rt_ln': 1.0 + nrm((DEPTH, RT_W), 0.02),
        'w_out': nrm((DEPTH, MIX_W, D_MODEL), MIX_W ** -0.5),
        'ln_ffn': 1.0 + nrm((DEPTH, D_MODEL), 0.02),
        'ffn_w1': nrm((N_DENSE, D_MODEL, D_FF), D_MODEL ** -0.5),
        'ffn_w3': nrm((N_DENSE, D_MODEL, D_FF), D_MODEL ** -0.5),
        'ffn_w2': nrm((N_DENSE, D_FF, D_MODEL), D_FF ** -0.5),
        'moe_router': nrm((N_MOE, D_MODEL, N_EXP), D_MODEL ** -0.5),
        'moe_w1': nrm((N_MOE, N_EXP, D_MODEL, D_EXP), D_MODEL ** -0.5),
        'moe_w3': nrm((N_MOE, N_EXP, D_MODEL, D_EXP), D_MODEL ** -0.5),
        'moe_w2': nrm((N_MOE, N_EXP, D_EXP, D_MODEL), D_EXP ** -0.5),
        'ln_final': 1.0 + nrm((D_MODEL,), 0.02),
    }


def reference(x_prompt, x_sample, state_dn_conv, state_dn, state_rw_shift, state_rw, state_rt,
              ln_mix, w_in, dn_conv, dn_a_log, dn_dt_bias, dn_norm,
              rw_mu, rw_w0, rw_w2, rw_a0, rw_a2, rw_g2, rw_kk, rw_ka, rw_rk, rw_ln_g, rw_ln_b,
              rt_ln, w_out, ln_ffn, ffn_w1, ffn_w3, ffn_w2,
              moe_router, moe_w1, moe_w3, moe_w2, ln_final):
    W = dict(ln_mix=ln_mix, w_in=w_in, dn_conv=dn_conv, dn_a_log=dn_a_log, dn_dt_bias=dn_dt_bias,
             dn_norm=dn_norm, rw_mu=rw_mu, rw_w0=rw_w0, rw_w2=rw_w2, rw_a0=rw_a0, rw_a2=rw_a2,
             rw_g2=rw_g2, rw_kk=rw_kk, rw_ka=rw_ka, rw_rk=rw_rk, rw_ln_g=rw_ln_g, rw_ln_b=rw_ln_b,
             rt_ln=rt_ln, w_out=w_out, ln_ffn=ln_ffn, ffn_w1=ffn_w1, ffn_w3=ffn_w3, ffn_w2=ffn_w2,
             moe_router=moe_router, moe_w1=moe_w1, moe_w3=moe_w3, moe_w2=moe_w2, ln_final=ln_final)
    bp, dt = x_prompt.shape[0], x_prompt.dtype
    z_conv = jnp.zeros((DEPTH, bp, CONV_W - 1, DN_CONV), dt)
    z_dn = jnp.zeros((DEPTH, bp, DN_HEADS, DN_DK, DN_DV), dt)
    z_shift = jnp.zeros((DEPTH, bp, RW_P), dt)
    z_rw = jnp.zeros((DEPTH, bp, RW_HEADS, RW_N, RW_N), dt)
    z_rt = jnp.zeros((DEPTH, bp, RT_HEADS, RT_DK, RT_DV), dt)
    y_prompt, p_conv, p_dn, p_shift, p_rw, p_rt = _trunk(x_prompt, z_conv, z_dn, z_shift, z_rw, z_rt, 0, W)
    y_sample, s_conv, s_dn, s_shift, s_rw, s_rt = _trunk(x_sample, state_dn_conv, state_dn, state_rw_shift,
                                                         state_rw, state_rt, PAST_LEN, W)
    return (y_prompt, y_sample, p_conv, p_dn, p_shift, p_rw, p_rt, s_conv, s_dn, s_shift, s_rw, s_rt)
```

```python
import functools
import math

import numpy as np
import jax
import jax.numpy as jnp
from jax import lax
from jax.experimental import pallas as pl
from jax.experimental.pallas import tpu as pltpu

F32 = jnp.float32
BF16 = jnp.bfloat16
HI = lax.Precision.HIGHEST

D_MODEL = 1024
N_LAYERS = 2
HEAD = 64
DN_HEADS = 6
DN_QK = DN_HEADS * HEAD
DN_CONV = 3 * DN_QK
CONV_W = 4
RW_HEADS = 6
RW_W = RW_HEADS * HEAD
RW_GATE_R = 160
RT_HEADS = 4
RT_DK = 32
RT_QK = RT_HEADS * RT_DK
RT_W = RT_HEADS * HEAD
DN_P = DN_CONV + DN_QK + 2 * DN_HEADS
RW_P = 3 * RW_W + 64 + 64 + RW_GATE_R
RT_P = 2 * RT_QK + 2 * RT_W
LANE = 128
DN_SLAB = 1664
RW_SLAB = 1536
RT_SLAB = RT_P
P_SLABS = DN_SLAB + RW_SLAB + RT_SLAB
RW_GATE_PAD = RW_SLAB - (3 * RW_W + 128)
N_EXP = 8
D_FF = 2816
D_EXP = 3584
ROPE_BASE = 10000.0
NORM_EPS = 1e-6
RW_GN_EPS = 64e-5
RT_GN_EPS = 1e-5
TAIL = 8
VMEM_LIMIT = 56 * 1024 * 1024


def _dot(a, b, precision=HI):
    return jnp.dot(a, b, preferred_element_type=F32, precision=precision)


def _dot_nt(a, b, precision=HI):
    return lax.dot_general(a, b, (((1,), (1,)), ((), ())),
                           preferred_element_type=F32, precision=precision)


def _dot_tn(a, b, precision=HI):
    return lax.dot_general(a, b, (((0,), (0,)), ((), ())),
                           preferred_element_type=F32, precision=precision)


def _silu(x):
    return x * jax.nn.sigmoid(x)


def _softplus(x):
    return jnp.maximum(x, 0.0) + jnp.log1p(jnp.exp(-jnp.abs(x)))


def _rms(x, eps):
    return x * lax.rsqrt(jnp.mean(x * x, axis=-1, keepdims=True) + eps)


def _tri(c):
    r = lax.broadcasted_iota(jnp.int32, (c, c), 0)
    col = lax.broadcasted_iota(jnp.int32, (c, c), 1)
    return r >= col, r > col, r == col


def _neumann_inverse(n, eye_f, c):
    t = eye_f + n
    npow = n
    for _ in range(int(math.log2(c)) - 1):
        npow = _dot(npow, npow)
        t = t + _dot(t, npow)
    return t


def _inproj_body(x_ref, g_ref, w_ref, zdn_ref, zrw_ref, zrt_ref):
    h = (_rms(x_ref[...], NORM_EPS) * g_ref[...]).astype(BF16)
    zdn_ref[...] = _dot(h, w_ref[:, 0:DN_SLAB], None)
    zrw_ref[...] = _dot(h, w_ref[:, DN_SLAB:DN_SLAB + RW_SLAB], None)
    zrt_ref[...] = _dot(h, w_ref[:, DN_SLAB + RW_SLAB:P_SLABS], None)


def _inproj(x, gain, w):
    n = x.shape[0]
    tm = min(256, n)
    return pl.pallas_call(
        _inproj_body,
        grid=(n // tm,),
        in_specs=[pl.BlockSpec((tm, D_MODEL), lambda i: (i, 0)),
                  pl.BlockSpec((1, D_MODEL), lambda i: (0, 0)),
                  pl.BlockSpec((D_MODEL, P_SLABS), lambda i: (0, 0))],
        out_specs=[pl.BlockSpec((tm, DN_SLAB), lambda i: (i, 0)),
                   pl.BlockSpec((tm, RW_SLAB), lambda i: (i, 0)),
                   pl.BlockSpec((tm, RT_SLAB), lambda i: (i, 0))],
        out_shape=[jax.ShapeDtypeStruct((n, DN_SLAB), F32),
                   jax.ShapeDtypeStruct((n, RW_SLAB), F32),
                   jax.ShapeDtypeStruct((n, RT_SLAB), F32)],
        compiler_params=pltpu.CompilerParams(
            dimension_semantics=("arbitrary",), vmem_limit_bytes=VMEM_LIMIT),
        name="inproj",
    )(x, gain, w)


def _dn_body(c_len, n_valid, fresh, *refs):
    if fresh:
        z_ref, cw_ref, hp_ref, nw_ref, o_ref, so_ref, xe, s_scr = refs
    else:
        z_ref, c0_ref, s0_ref, cw_ref, hp_ref, nw_ref, o_ref, so_ref, xe, s_scr = refs
    ci = pl.program_id(1)

    @pl.when(ci == 0)
    def _():
        xe[0:TAIL, :] = jnp.zeros((TAIL, DN_CONV), F32)
        if fresh:
            s_scr[...] = jnp.zeros_like(s_scr)
        else:
            xe[TAIL - (CONV_W - 1):TAIL, :] = c0_ref[0]
            s_scr[...] = s0_ref[0]

    x = z_ref[:, 0:DN_CONV]
    xe[TAIL:TAIL + c_len, :] = x
    acc = x * cw_ref[CONV_W - 1:CONV_W, :]
    for j in range(CONV_W - 1):
        off = TAIL - (CONV_W - 1) + j
        acc = acc + xe[off:off + c_len, :] * cw_ref[j:j + 1, :]
    conv = _silu(acc)
    xe[0:TAIL, :] = xe[c_len:c_len + TAIL, :]

    incl, strict, eye = _tri(c_len)
    eye_f = eye.astype(F32)
    ones_cc = jnp.ones((c_len, c_len), F32)

    ab = z_ref[:, DN_CONV + DN_QK:DN_SLAB]
    g_tok = -jnp.exp(hp_ref[0:1, :]) * _softplus(ab + hp_ref[1:2, :])
    beta = jax.nn.sigmoid(ab)
    if n_valid < c_len:
        live = lax.broadcasted_iota(jnp.int32, (c_len, LANE), 0) < n_valid
        g_tok = jnp.where(live, g_tok, 0.0)
        beta = jnp.where(live, beta, 0.0)
    g_cum = _dot(incl.astype(F32), g_tok)

    for h in range(DN_HEADS):
        lo = HEAD * h
        q = conv[:, lo:lo + HEAD]
        k = conv[:, DN_QK + lo:DN_QK + lo + HEAD]
        v = conv[:, 2 * DN_QK + lo:2 * DN_QK + lo + HEAD]
        q = q * lax.rsqrt(jnp.sum(q * q, axis=-1, keepdims=True) + 1e-6) * (HEAD ** -0.5)
        k = k * lax.rsqrt(jnp.sum(k * k, axis=-1, keepdims=True) + 1e-6)
        gc = g_cum[:, h:h + 1]
        b = beta[:, DN_HEADS + h:DN_HEADS + h + 1]
        g_row = _dot(ones_cc, jnp.where(eye, gc, 0.0))
        dmask = jnp.where(incl, jnp.exp(jnp.minimum(gc - g_row, 0.0)), 0.0)
        kb = k * b
        n = -jnp.where(strict, _dot_nt(kb, k) * dmask, 0.0)
        t = _neumann_inverse(n, eye_f, c_len)
        egc = jnp.exp(gc)
        u0 = _dot(t, v * b)
        kcd = _dot(t, kb * egc)
        qk = _dot_nt(q, k) * dmask
        s = s_scr[h]
        u = u0 - _dot(kcd, s)
        o = _dot(q * egc, s) + _dot(qk, u)
        g_last = gc[c_len - 1:c_len, :]
        s_scr[h] = s * jnp.exp(g_last) + _dot_tn(k * jnp.exp(g_last - gc), u)
        o = _rms(o, NORM_EPS) * nw_ref[...]
        zg = z_ref[:, DN_CONV + lo:DN_CONV + lo + HEAD]
        o_ref[:, lo:lo + HEAD] = o * _silu(zg)

    @pl.when(ci == pl.num_programs(1) - 1)
    def _():
        so_ref[0] = s_scr[...]


def _dn_group(z, conv0, s0, conv_w, head_par, norm_w, n_seq, t_pad, c_len, n_valid):
    fresh = s0 is None
    nc = t_pad // c_len
    assert n_valid == c_len or nc == 1
    row = lambda b, c: (b * nc + c, 0)
    const2 = lambda b, c: (0, 0)
    in_specs = [pl.BlockSpec((c_len, DN_SLAB), row)]
    args = [z]
    if not fresh:
        in_specs += [pl.BlockSpec((1, CONV_W - 1, DN_CONV), lambda b, c: (b, 0, 0)),
                     pl.BlockSpec((1, DN_HEADS, HEAD, HEAD), lambda b, c: (b, 0, 0, 0))]
        args += [conv0, s0]
    in_specs += [pl.BlockSpec((CONV_W, DN_CONV), const2),
                 pl.BlockSpec((2, LANE), const2),
                 pl.BlockSpec((1, HEAD), const2)]
    args += [conv_w, head_par, norm_w]
    return pl.pallas_call(
        functools.partial(_dn_body, c_len, n_valid, fresh),
        grid=(n_seq, nc),
        in_specs=in_specs,
        out_specs=[pl.BlockSpec((c_len, DN_QK), row),
                   pl.BlockSpec((1, DN_HEADS, HEAD, HEAD), lambda b, c: (b, 0, 0, 0))],
        out_shape=[jax.ShapeDtypeStruct((n_seq * t_pad, DN_QK), F32),
                   jax.ShapeDtypeStruct((n_seq, DN_HEADS, HEAD, HEAD), F32)],
        scratch_shapes=[pltpu.VMEM((c_len + TAIL, DN_CONV), F32),
                        pltpu.VMEM((DN_HEADS, HEAD, HEAD), F32)],
        compiler_params=pltpu.CompilerParams(
            dimension_semantics=("arbitrary", "arbitrary"), vmem_limit_bytes=VMEM_LIMIT),
        name="dn_group",
    )(*args)


def _rw_body(c_len, n_valid, fresh, *refs):
    if fresh:
        (z_ref, mu_ref, w0_ref, w2_ref, a0_ref, a2_ref, g2_ref, kk_ref, ka_ref, rk_ref,
         lng_ref, lnb_ref, o_ref, so_ref, xe, s_scr) = refs
    else:
        (z_ref, sh0_ref, s0_ref, mu_ref, w0_ref, w2_ref, a0_ref, a2_ref, g2_ref, kk_ref, ka_ref,
         rk_ref, lng_ref, lnb_ref, o_ref, so_ref, xe, s_scr) = refs
    ci = pl.program_id(1)

    @pl.when(ci == 0)
    def _():
        xe[0:TAIL, :] = jnp.zeros((TAIL, RW_SLAB), F32)
        if fresh:
            s_scr[...] = jnp.zeros_like(s_scr)
        else:
            xe[TAIL - 1:TAIL, :] = sh0_ref[0]
            s_scr[...] = s0_ref[0]

    x = z_ref[...]
    xe[TAIL:TAIL + c_len, :] = x
    prev = xe[TAIL - 1:TAIL - 1 + c_len, :]
    zs = x + (prev - x) * mu_ref[...]
    xe[0:TAIL, :] = xe[c_len:c_len + TAIL, :]

    r = zs[:, 0:RW_W]
    k = zs[:, RW_W:2 * RW_W]
    v = zs[:, 2 * RW_W:3 * RW_W]
    wl = zs[:, 3 * RW_W:3 * RW_W + 64]
    al = zs[:, 3 * RW_W + 64:3 * RW_W + 128]
    gl = zs[:, 3 * RW_W + 128:RW_SLAB]
    w_log = -_softplus(-(w0_ref[...] + _dot(jnp.tanh(wl), w2_ref[...]))) - 0.5
    a = jax.nn.sigmoid(a0_ref[...] + _dot(al, a2_ref[...]))
    g = _dot(jax.nn.sigmoid(gl), g2_ref[...])
    kk_in = k * kk_ref[...]
    k2 = k * (1.0 + (a - 1.0) * ka_ref[...])
    lw = -jnp.exp(w_log)
    if n_valid < c_len:
        live = lax.broadcasted_iota(jnp.int32, (c_len, RW_W), 0) < n_valid
        lw = jnp.where(live, lw, 0.0)
        k2 = jnp.where(live, k2, 0.0)
        kk_in = jnp.where(live, kk_in, 0.0)

    incl, strict, eye = _tri(c_len)
    eye_f = eye.astype(F32)
    gcum = _dot(incl.astype(F32), lw)
    e_pos = jnp.exp(gcum)
    e_neg = jnp.exp(-gcum)
    e_prev = jnp.exp(gcum - lw)
    g_end = gcum[c_len - 1:c_len, :]
    e_end = jnp.exp(g_end)
    e_rest = jnp.exp(g_end - gcum)

    for h in range(RW_HEADS):
        sl = slice(HEAD * h, HEAD * (h + 1))
        kkh = kk_in[:, sl]
        kkh = kkh * lax.rsqrt(jnp.sum(kkh * kkh, axis=-1, keepdims=True) + 1e-6)
        bv = kkh * a[:, sl]
        at = -kkh * e_prev[:, sl]
        bt = bv * e_neg[:, sl]
        kt = k2[:, sl] * e_neg[:, sl]
        rt = r[:, sl] * e_pos[:, sl]
        vh = v[:, sl]
        nab = jnp.where(strict, _dot_nt(at, bt), 0.0)
        t = _neumann_inverse(nab, eye_f, c_len)
        nak = jnp.where(strict, _dot_nt(at, kt), 0.0)
        ta = _dot(t, at)
        u0 = _dot(t, _dot(nak, vh))
        rb = jnp.where(incl, _dot_nt(rt, bt), 0.0)
        rkm = jnp.where(incl, _dot_nt(rt, kt), 0.0)
        s = s_scr[h]
        p = _dot_nt(ta, s) + u0
        y = _dot_nt(rt, s) + _dot(rb, p) + _dot(rkm, vh)
        s_scr[h] = (s * e_end[:, sl] + _dot_tn(p, bv * e_rest[:, sl])
                    + _dot_tn(vh, k2[:, sl] * e_rest[:, sl]))
        d = y - jnp.mean(y, axis=-1, keepdims=True)
        yn = d * lax.rsqrt(jnp.mean(d * d, axis=-1, keepdims=True) + RW_GN_EPS)
        yn = yn * lng_ref[:, sl] + lnb_ref[:, sl]
        bonus = jnp.sum(r[:, sl] * k2[:, sl] * rk_ref[:, sl], axis=-1, keepdims=True) * vh
        o_ref[:, sl] = (yn + bonus) * g[:, sl]

    @pl.when(ci == pl.num_programs(1) - 1)
    def _():
        so_ref[0] = s_scr[...]


def _rw_group(z, shift0, s0, par, n_seq, t_pad, c_len, n_valid):
    fresh = s0 is None
    nc = t_pad // c_len
    assert n_valid == c_len or nc == 1
    row = lambda b, c: (b * nc + c, 0)
    const2 = lambda b, c: (0, 0)
    in_specs = [pl.BlockSpec((c_len, RW_SLAB), row)]
    args = [z]
    if not fresh:
        in_specs += [pl.BlockSpec((1, 1, RW_SLAB), lambda b, c: (b, 0, 0)),
                     pl.BlockSpec((1, RW_HEADS, HEAD, HEAD), lambda b, c: (b, 0, 0, 0))]
        args += [shift0, s0]
    for p in par:
        in_specs.append(pl.BlockSpec(p.shape, const2))
        args.append(p)
    return pl.pallas_call(
        functools.partial(_rw_body, c_len, n_valid, fresh),
        grid=(n_seq, nc),
        in_specs=in_specs,
        out_specs=[pl.BlockSpec((c_len, RW_W), row),
                   pl.BlockSpec((1, RW_HEADS, HEAD, HEAD), lambda b, c: (b, 0, 0, 0))],
        out_shape=[jax.ShapeDtypeStruct((n_seq * t_pad, RW_W), F32),
                   jax.ShapeDtypeStruct((n_seq, RW_HEADS, HEAD, HEAD), F32)],
        scratch_shapes=[pltpu.VMEM((c_len + TAIL, RW_SLAB), F32),
                        pltpu.VMEM((RW_HEADS, HEAD, HEAD), F32)],
        compiler_params=pltpu.CompilerParams(
            dimension_semantics=("arbitrary", "arbitrary"), vmem_limit_bytes=VMEM_LIMIT),
        name="rw_group",
    )(*args)


def _rt_body(c_len, n_valid, fresh, *refs):
    if fresh:
        z_ref, cos_ref, sin_ref, lng_ref, o_ref, so_ref, s_scr = refs
    else:
        z_ref, cos_ref, sin_ref, s0_ref, lng_ref, o_ref, so_ref, s_scr = refs
    ci = pl.program_id(1)

    @pl.when(ci == 0)
    def _():
        if fresh:
            s_scr[...] = jnp.zeros_like(s_scr)
        else:
            s_scr[...] = s0_ref[0]

    lane = lax.broadcasted_iota(jnp.int32, (c_len, RT_QK), 1)
    first_half = (lane & (RT_DK - 1)) < (RT_DK // 2)
    cos = cos_ref[...]
    sin = sin_ref[...]

    def rotary(t):
        partner = jnp.where(first_half,
                            pltpu.roll(t, RT_QK - RT_DK // 2, axis=1),
                            pltpu.roll(t, RT_DK // 2, axis=1))
        return t * cos + partner * sin

    q = rotary(z_ref[:, 0:RT_QK])
    k = rotary(z_ref[:, RT_QK:2 * RT_QK]) * (RT_DK ** -0.5)
    incl, _, _ = _tri(c_len)
    ri = lax.broadcasted_iota(jnp.int32, (c_len, c_len), 0)
    cj = lax.broadcasted_iota(jnp.int32, (c_len, c_len), 1)
    delta = (ri - cj).astype(F32)
    idx = lax.broadcasted_iota(jnp.int32, (c_len, 1), 0).astype(F32)

    for h in range(RT_HEADS):
        lg = math.log1p(-(2.0 ** (-5.0 - h)))
        qh = q[:, RT_DK * h:RT_DK * (h + 1)]
        kh = k[:, RT_DK * h:RT_DK * (h + 1)]
        vh = z_ref[:, 2 * RT_QK + HEAD * h:2 * RT_QK + HEAD * (h + 1)]
        dmask = jnp.where(incl, jnp.exp(delta * lg), 0.0)
        inner = _dot(_dot_nt(qh, kh) * dmask, vh)
        q_dec = qh * jnp.exp((idx + 1.0) * lg)
        k_dec = kh * jnp.where(idx < n_valid, jnp.exp((n_valid - 1.0 - idx) * lg), 0.0)
        s = s_scr[h]
        o = inner + _dot(q_dec, s)
        s_scr[h] = s * math.exp(n_valid * lg) + _dot_tn(k_dec, vh)
        d = o - jnp.mean(o, axis=-1, keepdims=True)
        on = d * lax.rsqrt(jnp.mean(d * d, axis=-1, keepdims=True) + RT_GN_EPS)
        gate = z_ref[:, 2 * RT_QK + RT_W + HEAD * h:2 * RT_QK + RT_W + HEAD * (h + 1)]
        o_ref[:, HEAD * h:HEAD * (h + 1)] = on * lng_ref[:, HEAD * h:HEAD * (h + 1)] * _silu(gate)

    @pl.when(ci == pl.num_programs(1) - 1)
    def _():
        so_ref[0] = s_scr[...]


def _rt_group(z, cos, sin, s0, ln_g, n_seq, t_pad, c_len, n_valid):
    fresh = s0 is None
    nc = t_pad // c_len
    assert n_valid == c_len or nc == 1
    row = lambda b, c: (b * nc + c, 0)
    const2 = lambda b, c: (0, 0)
    in_specs = [pl.BlockSpec((c_len, RT_SLAB), row),
                pl.BlockSpec((c_len, RT_QK), lambda b, c: (c, 0)),
                pl.BlockSpec((c_len, RT_QK), lambda b, c: (c, 0))]
    args = [z, cos, sin]
    if not fresh:
        in_specs.append(pl.BlockSpec((1, RT_HEADS, RT_DK, HEAD), lambda b, c: (b, 0, 0, 0)))
        args.append(s0)
    in_specs.append(pl.BlockSpec((1, RT_W), const2))
    args.append(ln_g)
    return pl.pallas_call(
        functools.partial(_rt_body, c_len, n_valid, fresh),
        grid=(n_seq, nc),
        in_specs=in_specs,
        out_specs=[pl.BlockSpec((c_len, RT_W), row),
                   pl.BlockSpec((1, RT_HEADS, RT_DK, HEAD), lambda b, c: (b, 0, 0, 0))],
        out_shape=[jax.ShapeDtypeStruct((n_seq * t_pad, RT_W), F32),
                   jax.ShapeDtypeStruct((n_seq, RT_HEADS, RT_DK, HEAD), F32)],
        scratch_shapes=[pltpu.VMEM((RT_HEADS, RT_DK, HEAD), F32)],
        compiler_params=pltpu.CompilerParams(
            dimension_semantics=("arbitrary", "arbitrary"), vmem_limit_bytes=VMEM_LIMIT),
        name="rt_group",
    )(*args)


def _rotary_tables(pos0, t_pad):
    half = RT_DK // 2
    inv = ROPE_BASE ** (-np.arange(half, dtype=np.float64) / half)
    ang = (pos0 + np.arange(t_pad, dtype=np.float64))[:, None] * inv[None, :]
    cos = np.tile(np.concatenate([np.cos(ang), np.cos(ang)], axis=1), (1, RT_HEADS))
    sin = np.tile(np.concatenate([-np.sin(ang), np.sin(ang)], axis=1), (1, RT_HEADS))
    return jnp.asarray(cos, F32), jnp.asarray(sin, F32)


def _outproj_body(x_ref, odn_ref, orw_ref, ort_ref, w_ref, o_ref):
    acc = x_ref[...]
    acc = acc + _dot(odn_ref[...].astype(BF16), w_ref[0:DN_QK, :], None)
    acc = acc + _dot(orw_ref[...].astype(BF16), w_ref[DN_QK:DN_QK + RW_W, :], None)
    acc = acc + _dot(ort_ref[...].astype(BF16), w_ref[DN_QK + RW_W:D_MODEL, :], None)
    o_ref[...] = acc


def _outproj(x, o_dn, o_rw, o_rt, w):
    n = x.shape[0]
    tm = min(512, n)
    row = lambda i: (i, 0)
    return pl.pallas_call(
        _outproj_body,
        grid=(n // tm,),
        in_specs=[pl.BlockSpec((tm, D_MODEL), row),
                  pl.BlockSpec((tm, DN_QK), row),
                  pl.BlockSpec((tm, RW_W), row),
                  pl.BlockSpec((tm, RT_W), row),
                  pl.BlockSpec((D_MODEL, D_MODEL), lambda i: (0, 0))],
        out_specs=pl.BlockSpec((tm, D_MODEL), row),
        out_shape=jax.ShapeDtypeStruct((n, D_MODEL), F32),
        compiler_params=pltpu.CompilerParams(
            dimension_semantics=("arbitrary",), vmem_limit_bytes=VMEM_LIMIT),
        name="outproj",
    )(x, o_dn, o_rw, o_rt, w)


def _top2_combine(logits):
    lane = lax.broadcasted_iota(jnp.int32, logits.shape, 1).astype(F32)
    lg = jnp.where(lane < N_EXP, logits, -jnp.inf)
    m1 = jnp.max(lg, axis=-1, keepdims=True)
    i1 = jnp.min(jnp.where(lg == m1, lane, float(LANE)), axis=-1, keepdims=True)
    lg2 = jnp.where(lane == i1, -jnp.inf, lg)
    m2 = jnp.max(lg2, axis=-1, keepdims=True)
    i2 = jnp.min(jnp.where(lg2 == m2, lane, float(LANE)), axis=-1, keepdims=True)
    e2 = jnp.exp(m2 - m1)
    den = 1.0 + e2
    return jnp.where(lane == i1, 1.0 / den, 0.0) + jnp.where(lane == i2, e2 / den, 0.0)


def _ffn_body(routed, final_norm, *refs):
    refs = list(refs)
    x_ref, g_ref = refs[0], refs[1]
    pos = 2
    if routed:
        router_ref = refs[pos]
        pos += 1
    w1_ref, w3_ref, w2_ref = refs[pos:pos + 3]
    pos += 3
    if final_norm:
        gf_ref = refs[pos]
        pos += 1
    o_ref = refs[pos]
    h_scr, acc = refs[pos + 1], refs[pos + 2]
    if routed:
        comb = refs[pos + 3]
    e = pl.program_id(1)
    f = pl.program_id(2)

    @pl.when((e == 0) & (f == 0))
    def _():
        x = x_ref[...]
        h = _rms(x, NORM_EPS) * g_ref[...]
        h_scr[...] = h.astype(BF16)
        acc[...] = x
        if routed:
            comb[...] = _top2_combine(_dot(h, router_ref[...]))

    hb = h_scr[...]
    u = _dot(hb, w1_ref[0], None)
    gate = _dot(hb, w3_ref[0], None)
    y = _dot((_silu(u) * gate).astype(BF16), w2_ref[0], None)
    if routed:
        lane = lax.broadcasted_iota(jnp.int32, comb.shape, 1)
        y = y * jnp.sum(jnp.where(lane == e, comb[...], 0.0), axis=-1, keepdims=True)
    acc[...] += y

    @pl.when((e == pl.num_programs(1) - 1) & (f == pl.num_programs(2) - 1))
    def _():
        if final_norm:
            o_ref[...] = _rms(acc[...], NORM_EPS) * gf_ref[...]
        else:
            o_ref[...] = acc[...]


def _ffn(x, gain, router, w1, w3, w2, final_gain, tf):
    n = x.shape[0]
    n_e, _, d_f = w1.shape
    routed = router is not None
    final_norm = final_gain is not None
    tm = min(1024, n)
    in_specs = [pl.BlockSpec((tm, D_MODEL), lambda i, e, f: (i, 0)),
                pl.BlockSpec((1, D_MODEL), lambda i, e, f: (0, 0))]
    args = [x, gain]
    if routed:
        in_specs.append(pl.BlockSpec((D_MODEL, LANE), lambda i, e, f: (0, 0)))
        args.append(router)
    in_specs += [pl.BlockSpec((1, D_MODEL, tf), lambda i, e, f: (e, 0, f)),
                 pl.BlockSpec((1, D_MODEL, tf), lambda i, e, f: (e, 0, f)),
                 pl.BlockSpec((1, tf, D_MODEL), lambda i, e, f: (e, f, 0))]
    args += [w1, w3, w2]
    if final_norm:
        in_specs.append(pl.BlockSpec((1, D_MODEL), lambda i, e, f: (0, 0)))
        args.append(final_gain)
    scratch = [pltpu.VMEM((tm, D_MODEL), BF16), pltpu.VMEM((tm, D_MODEL), F32)]
    if routed:
        scratch.append(pltpu.VMEM((tm, LANE), F32))
    return pl.pallas_call(
        functools.partial(_ffn_body, routed, final_norm),
        grid=(n // tm, n_e, d_f // tf),
        in_specs=in_specs,
        out_specs=pl.BlockSpec((tm, D_MODEL), lambda i, e, f: (i, 0)),
        out_shape=jax.ShapeDtypeStruct((n, D_MODEL), F32),
        scratch_shapes=scratch,
        compiler_params=pltpu.CompilerParams(
            dimension_semantics=("arbitrary", "arbitrary", "arbitrary"),
            vmem_limit_bytes=VMEM_LIMIT),
        name="ffn_moe" if routed else "ffn_dense",
    )(*args)


def _pad_last(a, width):
    return jnp.pad(a, [(0, 0)] * (a.ndim - 1) + [(0, width - a.shape[-1])])


def _pack_params(p):
    w_in = p["w_in"]
    w_in = jnp.concatenate([_pad_last(w_in[..., :DN_P], DN_SLAB),
                            _pad_last(w_in[..., DN_P:DN_P + RW_P], RW_SLAB),
                            w_in[..., DN_P + RW_P:]], axis=-1).astype(BF16)
    head_par = jnp.stack([_pad_last(p["dn_a_log"], LANE), _pad_last(p["dn_dt_bias"], LANE)], axis=1)
    g2 = jnp.pad(p["rw_g2"], ((0, 0), (0, RW_GATE_PAD - RW_GATE_R), (0, 0)))
    row = lambda a: a[:, None, :]
    return dict(
        ln_mix=row(p["ln_mix"]), w_in=w_in, dn_conv=p["dn_conv"], dn_head=head_par,
        dn_norm=row(p["dn_norm"]),
        rw=[(row(_pad_last(p["rw_mu"], RW_SLAB))[l], row(p["rw_w0"])[l], p["rw_w2"][l],
             row(p["rw_a0"])[l], p["rw_a2"][l], g2[l], row(p["rw_kk"])[l], row(p["rw_ka"])[l],
             p["rw_rk"].reshape(N_LAYERS, 1, RW_W)[l], row(p["rw_ln_g"])[l], row(p["rw_ln_b"])[l])
            for l in range(N_LAYERS)],
        rt_ln=row(p["rt_ln"]), w_out=p["w_out"].astype(BF16), ln_ffn=row(p["ln_ffn"]),
        ffn_w1=p["ffn_w1"].astype(BF16), ffn_w3=p["ffn_w3"].astype(BF16),
        ffn_w2=p["ffn_w2"].astype(BF16),
        moe_router=_pad_last(p["moe_router"], LANE),
        moe_w1=p["moe_w1"].astype(BF16), moe_w3=p["moe_w3"].astype(BF16),
        moe_w2=p["moe_w2"].astype(BF16),
        ln_final=p["ln_final"][None, :],
    )


def _trunk(x, n_seq, t_real, t_pad, c_len, states, pos0, w):
    n_valid = min(c_len, t_real)
    cos, sin = _rotary_tables(pos0, t_pad)
    convs, dns, shifts, rws, rts = [], [], [], [], []
    for l in range(N_LAYERS):
        z_dn, z_rw, z_rt = _inproj(x, w["ln_mix"][l], w["w_in"][l])
        if states is None:
            conv0 = s_dn0 = shift0 = s_rw0 = s_rt0 = None
        else:
            conv0, s_dn0 = states[0][l], states[1][l]
            shift0 = _pad_last(states[2][l], RW_SLAB)[:, None, :]
            s_rw0, s_rt0 = states[3][l], states[4][l]
        o_dn, s_dn = _dn_group(z_dn, conv0, s_dn0, w["dn_conv"][l], w["dn_head"][l],
                               w["dn_norm"][l], n_seq, t_pad, c_len, n_valid)
        o_rw, s_rw = _rw_group(z_rw, shift0, s_rw0, w["rw"][l], n_seq, t_pad, c_len, n_valid)
        o_rt, s_rt = _rt_group(z_rt, cos, sin, s_rt0, w["rt_ln"][l], n_seq, t_pad, c_len, n_valid)
        x = _outproj(x, o_dn, o_rw, o_rt, w["w_out"][l])
        if l % 2 == 0:
            x = _ffn(x, w["ln_ffn"][l], None, w["ffn_w1"][l // 2:l // 2 + 1],
                     w["ffn_w3"][l // 2:l // 2 + 1], w["ffn_w2"][l // 2:l // 2 + 1],
                     w["ln_final"] if l == N_LAYERS - 1 else None, D_FF // 2)
        else:
            x = _ffn(x, w["ln_ffn"][l], w["moe_router"][l // 2], w["moe_w1"][l // 2],
                     w["moe_w3"][l // 2], w["moe_w2"][l // 2],
                     w["ln_final"] if l == N_LAYERS - 1 else None, D_EXP // 4)
        z_dn3 = z_dn.reshape(n_seq, t_pad, DN_SLAB)
        z_rw3 = z_rw.reshape(n_seq, t_pad, RW_SLAB)
        if t_real >= CONV_W - 1:
            convs.append(z_dn3[:, t_real - (CONV_W - 1):t_real, :DN_CONV])
        else:
            convs.append(jnp.concatenate([conv0[:, t_real:], z_dn3[:, :t_real, :DN_CONV]], axis=1))
        shifts.append(z_rw3[:, t_real - 1, :RW_P])
        dns.append(s_dn)
        rws.append(s_rw)
        rts.append(s_rt)
    return x, jnp.stack(convs), jnp.stack(dns), jnp.stack(shifts), jnp.stack(rws), jnp.stack(rts)


def kernel(x_prompt, x_sample, state_dn_conv, state_dn, state_rw_shift, state_rw, state_rt, ln_mix, w_in, dn_conv, dn_a_log, dn_dt_bias, dn_norm, rw_mu, rw_w0, rw_w2, rw_a0, rw_a2, rw_g2, rw_kk, rw_ka, rw_rk, rw_ln_g, rw_ln_b, rt_ln, w_out, ln_ffn, ffn_w1, ffn_w3, ffn_w2, moe_router, moe_w1, moe_w3, moe_w2, ln_final):
    w = _pack_params(dict(
        ln_mix=ln_mix, w_in=w_in, dn_conv=dn_conv, dn_a_log=dn_a_log, dn_dt_bias=dn_dt_bias,
        dn_norm=dn_norm, rw_mu=rw_mu, rw_w0=rw_w0, rw_w2=rw_w2, rw_a0=rw_a0, rw_a2=rw_a2,
        rw_g2=rw_g2, rw_kk=rw_kk, rw_ka=rw_ka, rw_rk=rw_rk, rw_ln_g=rw_ln_g, rw_ln_b=rw_ln_b,
        rt_ln=rt_ln, w_out=w_out, ln_ffn=ln_ffn, ffn_w1=ffn_w1, ffn_w3=ffn_w3, ffn_w2=ffn_w2,
        moe_router=moe_router, moe_w1=moe_w1, moe_w3=moe_w3, moe_w2=moe_w2, ln_final=ln_final))
    bp, tp, _ = x_prompt.shape
    bs, ts, _ = x_sample.shape
    past_len = 16384

    yp, *p_states = _trunk(x_prompt.reshape(bp * tp, D_MODEL), bp, tp, tp, 64, None, 0, w)
    y_prompt = yp.reshape(bp, tp, D_MODEL)

    ts_pad = 8
    xs = jnp.pad(x_sample, ((0, 0), (0, ts_pad - ts), (0, 0))).reshape(bs * ts_pad, D_MODEL)
    ys, *s_states = _trunk(xs, bs, ts, ts_pad, ts_pad,
                           (state_dn_conv, state_dn, state_rw_shift, state_rw, state_rt),
                           past_len, w)
    y_sample = ys.reshape(bs, ts_pad, D_MODEL)[:, :ts]
    return (y_prompt, y_sample, *p_states, *s_states)
```

```python
import functools
import math

import numpy as np
import jax
import jax.numpy as jnp
from jax import lax
from jax.experimental import pallas as pl
from jax.experimental.pallas import tpu as pltpu

F32 = jnp.float32
BF16 = jnp.bfloat16
HI = lax.Precision.HIGHEST

D_MODEL = 1024
N_LAYERS = 2
PAST_LEN = 16384
HEAD = 64
DN_HEADS = 6
DN_QK = DN_HEADS * HEAD
DN_CONV = 3 * DN_QK
CONV_W = 4
RW_HEADS = 6
RW_W = RW_HEADS * HEAD
RW_GATE_R = 160
RT_HEADS = 4
RT_DK = 32
RT_QK = RT_HEADS * RT_DK
RT_W = RT_HEADS * HEAD
DN_P = DN_CONV + DN_QK + 2 * DN_HEADS
RW_P = 3 * RW_W + 64 + 64 + RW_GATE_R
RT_P = 2 * RT_QK + 2 * RT_W
LANE = 128
DN_SLAB = 1664
RW_SLAB = 1536
RT_SLAB = RT_P
P_SLABS = DN_SLAB + RW_SLAB + RT_SLAB
RW_GATE_PAD = RW_SLAB - (3 * RW_W + 128)
N_EXP = 8
D_FF = 2816
D_EXP = 3584
ROPE_BASE = 10000.0
NORM_EPS = 1e-6
RW_GN_EPS = 64e-5
RT_GN_EPS = 1e-5
TAIL = 8
VMEM_LIMIT = 56 * 1024 * 1024


def _dot(a, b, precision=HI):
    return jnp.dot(a, b, preferred_element_type=F32, precision=precision)


def _dot_tn(a, b, precision=HI):
    return lax.dot_general(a, b, (((0,), (0,)), ((), ())),
                           preferred_element_type=F32, precision=precision)


def _mm(a, b):
    return jnp.dot(a.astype(BF16), b.astype(BF16), preferred_element_type=F32)


def _mm_nt(a, b):
    return lax.dot_general(a.astype(BF16), b.astype(BF16), (((1,), (1,)), ((), ())),
                           preferred_element_type=F32)


def _mm_tn(a, b):
    return lax.dot_general(a.astype(BF16), b.astype(BF16), (((0,), (0,)), ((), ())),
                           preferred_element_type=F32)


def _rows(*parts):
    return jnp.concatenate(parts, axis=0)


def _cols(*parts):
    return jnp.concatenate(parts, axis=1)


def _silu(x):
    return x * jax.nn.sigmoid(x)


def _softplus(x):
    return jnp.maximum(x, 0.0) + jnp.log1p(jnp.exp(-jnp.abs(x)))


def _rms(x, eps):
    return x * lax.rsqrt(jnp.mean(x * x, axis=-1, keepdims=True) + eps)


def _tri(c):
    r = lax.broadcasted_iota(jnp.int32, (c, c), 0)
    col = lax.broadcasted_iota(jnp.int32, (c, c), 1)
    return r >= col, r > col, r == col


def _neumann_levels(n_rows):
    return max(0, math.ceil(math.log2(n_rows)) - 1)


def _inproj_body(x_ref, g_ref, w_ref, zdn_ref, zrw_ref, zrt_ref):
    h = (_rms(x_ref[...], NORM_EPS) * g_ref[...]).astype(BF16)
    zdn_ref[...] = _dot(h, w_ref[:, 0:DN_SLAB], None)
    zrw_ref[...] = _dot(h, w_ref[:, DN_SLAB:DN_SLAB + RW_SLAB], None)
    zrt_ref[...] = _dot(h, w_ref[:, DN_SLAB + RW_SLAB:P_SLABS], None)


def _inproj(x, gain, w):
    n = x.shape[0]
    tm = min(256, n)
    return pl.pallas_call(
        _inproj_body,
        grid=(n // tm,),
        in_specs=[pl.BlockSpec((tm, D_MODEL), lambda i: (i, 0)),
                  pl.BlockSpec((1, D_MODEL), lambda i: (0, 0)),
                  pl.BlockSpec((D_MODEL, P_SLABS), lambda i: (0, 0))],
        out_specs=[pl.BlockSpec((tm, DN_SLAB), lambda i: (i, 0)),
                   pl.BlockSpec((tm, RW_SLAB), lambda i: (i, 0)),
                   pl.BlockSpec((tm, RT_SLAB), lambda i: (i, 0))],
        out_shape=[jax.ShapeDtypeStruct((n, DN_SLAB), F32),
                   jax.ShapeDtypeStruct((n, RW_SLAB), F32),
                   jax.ShapeDtypeStruct((n, RT_SLAB), F32)],
        compiler_params=pltpu.CompilerParams(
            dimension_semantics=("arbitrary",), vmem_limit_bytes=VMEM_LIMIT),
        name="inproj",
    )(x, gain, w)


def _seq_specs(n_par, c_len, width):
    return pl.BlockSpec((n_par, c_len, width), lambda b, c: (b, c, 0))


def _state_spec(n_par, shape):
    return pl.BlockSpec((n_par,) + shape, lambda b, c: (b,) + (0,) * len(shape))


def _const_spec(a):
    return pl.BlockSpec(a.shape, lambda b, c: (0,) * a.ndim)


def _dn_body(c_len, n_valid, fresh, n_par, *refs):
    if fresh:
        z_ref, cw_ref, hp_ref, nw_ref, o_ref, so_ref, xe, s_scr = refs
    else:
        z_ref, c0_ref, s0_ref, cw_ref, hp_ref, nw_ref, o_ref, so_ref, xe, s_scr = refs
    ci = pl.program_id(1)

    @pl.when(ci == 0)
    def _():
        xe[:, 0:TAIL, :] = jnp.zeros((n_par, TAIL, DN_CONV), F32)
        if fresh:
            s_scr[...] = jnp.zeros_like(s_scr)
        else:
            xe[:, TAIL - (CONV_W - 1):TAIL, :] = c0_ref[...]
            s_scr[...] = s0_ref[...]

    incl, strict, eye = _tri(c_len)
    eye_f = eye.astype(F32)
    tri_f = incl.astype(F32)
    live = lax.broadcasted_iota(jnp.int32, (c_len, LANE), 0) < n_valid
    neg_a = -jnp.exp(hp_ref[0:1, :])
    dt_bias = hp_ref[1:2, :]

    seqs = []
    for i in range(n_par):
        x = z_ref[i, :, 0:DN_CONV]
        xe[i, TAIL:TAIL + c_len, :] = x
        acc = x * cw_ref[CONV_W - 1:CONV_W, :]
        for j in range(CONV_W - 1):
            off = TAIL - (CONV_W - 1) + j
            acc = acc + xe[i, off:off + c_len, :] * cw_ref[j:j + 1, :]
        xe[i, 0:TAIL, :] = xe[i, c_len:c_len + TAIL, :]
        ab = z_ref[i, :, DN_CONV + DN_QK:DN_SLAB]
        g_tok = neg_a * _softplus(ab + dt_bias)
        beta = jax.nn.sigmoid(ab)
        if n_valid < c_len:
            g_tok = jnp.where(live, g_tok, 0.0)
            beta = jnp.where(live, beta, 0.0)
        seqs.append(dict(conv=_silu(acc), beta=beta, g_tok=g_tok))
    for sq in seqs:
        sq["g_cum"] = _dot(tri_f, sq["g_tok"])
    for sq in seqs:
        sq["g_cum_t"] = _dot_tn(sq["g_cum"], eye_f)

    chains = []
    for i, sq in enumerate(seqs):
        for h in range(DN_HEADS):
            lo = HEAD * h
            q = sq["conv"][:, lo:lo + HEAD]
            k = sq["conv"][:, DN_QK + lo:DN_QK + lo + HEAD]
            v = sq["conv"][:, 2 * DN_QK + lo:2 * DN_QK + lo + HEAD]
            q = q * lax.rsqrt(jnp.sum(q * q, axis=-1, keepdims=True) + 1e-6) * (HEAD ** -0.5)
            k = k * lax.rsqrt(jnp.sum(k * k, axis=-1, keepdims=True) + 1e-6)
            gc = sq["g_cum"][:, h:h + 1]
            b = sq["beta"][:, DN_HEADS + h:DN_HEADS + h + 1]
            dmask = jnp.where(
                incl, jnp.exp(jnp.minimum(gc - sq["g_cum_t"][h:h + 1, :], 0.0)), 0.0)
            chains.append(dict(i=i, h=h, q=q, k=k, v=v, gc=gc, b=b, dmask=dmask,
                               kb=k * b, egc=jnp.exp(gc)))
    for c in chains:
        c["kq"] = _mm_nt(_rows(c["kb"], c["q"]), c["k"])
    for c in chains:
        c["np"] = -jnp.where(strict, c["kq"][0:c_len] * c["dmask"], 0.0)
        c["t"] = eye_f + c["np"]
    for _ in range(_neumann_levels(n_valid)):
        for c in chains:
            c["np"] = _mm(c["np"], c["np"])
        for c in chains:
            c["t"] = c["t"] + _mm(c["t"], c["np"])
    for c in chains:
        c["tx"] = _mm(c["t"], _cols(c["v"] * c["b"], c["kb"] * c["egc"]))
    for c in chains:
        c["s"] = s_scr[c["i"], c["h"]]
        c["ss"] = _mm(_rows(c["tx"][:, HEAD:2 * HEAD], c["q"] * c["egc"]), c["s"])
    for c in chains:
        c["u"] = c["tx"][:, 0:HEAD] - c["ss"][0:c_len]
        g_last = c["gc"][c_len - 1:c_len, :]
        c["o"] = c["ss"][c_len:2 * c_len] + _mm(c["kq"][c_len:2 * c_len] * c["dmask"], c["u"])
        s_scr[c["i"], c["h"]] = (c["s"] * jnp.exp(g_last)
                                 + _mm_tn(c["k"] * jnp.exp(g_last - c["gc"]), c["u"]))
    for c in chains:
        lo = HEAD * c["h"]
        zg = z_ref[c["i"], :, DN_CONV + lo:DN_CONV + lo + HEAD]
        o_ref[c["i"], :, lo:lo + HEAD] = _rms(c["o"], NORM_EPS) * nw_ref[...] * _silu(zg)

    @pl.when(ci == pl.num_programs(1) - 1)
    def _():
        so_ref[...] = s_scr[...]


def _dn_group(z, conv0, s0, conv_w, head_par, norm_w, c_len, n_valid, n_par):
    n_seq, t_pad, _ = z.shape
    fresh = s0 is None
    nc = t_pad // c_len
    assert n_valid == c_len or nc == 1
    state = (DN_HEADS, HEAD, HEAD)
    in_specs = [_seq_specs(n_par, c_len, DN_SLAB)]
    args = [z]
    if not fresh:
        in_specs += [_state_spec(n_par, (CONV_W - 1, DN_CONV)), _state_spec(n_par, state)]
        args += [conv0, s0]
    for p in (conv_w, head_par, norm_w):
        in_specs.append(_const_spec(p))
        args.append(p)
    return pl.pallas_call(
        functools.partial(_dn_body, c_len, n_valid, fresh, n_par),
        grid=(n_seq // n_par, nc),
        in_specs=in_specs,
        out_specs=[_seq_specs(n_par, c_len, DN_QK), _state_spec(n_par, state)],
        out_shape=[jax.ShapeDtypeStruct((n_seq, t_pad, DN_QK), F32),
                   jax.ShapeDtypeStruct((n_seq,) + state, F32)],
        scratch_shapes=[pltpu.VMEM((n_par, c_len + TAIL, DN_CONV), F32),
                        pltpu.VMEM((n_par,) + state, F32)],
        compiler_params=pltpu.CompilerParams(
            dimension_semantics=("arbitrary", "arbitrary"), vmem_limit_bytes=VMEM_LIMIT),
        name="dn_group",
    )(*args)


def _rw_body(c_len, n_valid, fresh, n_par, *refs):
    if fresh:
        (z_ref, mu_ref, w0_ref, w2_ref, a0_ref, a2_ref, g2_ref, kk_ref, ka_ref, rk_ref,
         lng_ref, lnb_ref, o_ref, so_ref, xe, s_scr) = refs
    else:
        (z_ref, sh0_ref, s0_ref, mu_ref, w0_ref, w2_ref, a0_ref, a2_ref, g2_ref, kk_ref, ka_ref,
         rk_ref, lng_ref, lnb_ref, o_ref, so_ref, xe, s_scr) = refs
    ci = pl.program_id(1)

    @pl.when(ci == 0)
    def _():
        xe[:, 0:TAIL, :] = jnp.zeros((n_par, TAIL, RW_SLAB), F32)
        if fresh:
            s_scr[...] = jnp.zeros_like(s_scr)
        else:
            xe[:, TAIL - 1:TAIL, :] = sh0_ref[...]
            s_scr[...] = s0_ref[...]

    incl, strict, eye = _tri(c_len)
    eye_f = eye.astype(F32)
    tri_f = incl.astype(F32)
    live = lax.broadcasted_iota(jnp.int32, (c_len, RW_W), 0) < n_valid

    seqs = []
    for i in range(n_par):
        x = z_ref[i]
        xe[i, TAIL:TAIL + c_len, :] = x
        prev = xe[i, TAIL - 1:TAIL - 1 + c_len, :]
        zs = x + (prev - x) * mu_ref[...]
        xe[i, 0:TAIL, :] = xe[i, c_len:c_len + TAIL, :]
        seqs.append(dict(r=zs[:, 0:RW_W], k=zs[:, RW_W:2 * RW_W], v=zs[:, 2 * RW_W:3 * RW_W],
                         wl=zs[:, 3 * RW_W:3 * RW_W + 64], al=zs[:, 3 * RW_W + 64:3 * RW_W + 128],
                         gl=zs[:, 3 * RW_W + 128:RW_SLAB]))
    for sq in seqs:
        sq["w_lora"] = _dot(jnp.tanh(sq["wl"]), w2_ref[...])
        sq["a_lora"] = _dot(sq["al"], a2_ref[...])
        sq["g"] = _dot(jax.nn.sigmoid(sq["gl"]), g2_ref[...])
    for sq in seqs:
        w_log = -_softplus(-(w0_ref[...] + sq["w_lora"])) - 0.5
        a = jax.nn.sigmoid(a0_ref[...] + sq["a_lora"])
        kk_in = sq["k"] * kk_ref[...]
        k2 = sq["k"] * (1.0 + (a - 1.0) * ka_ref[...])
        lw = -jnp.exp(w_log)
        if n_valid < c_len:
            lw = jnp.where(live, lw, 0.0)
            k2 = jnp.where(live, k2, 0.0)
            kk_in = jnp.where(live, kk_in, 0.0)
        sq.update(a=a, kk_in=kk_in, k2=k2, lw=lw)
    for sq in seqs:
        sq["gcum"] = _dot(tri_f, sq["lw"])

    chains = []
    for i, sq in enumerate(seqs):
        gcum = sq["gcum"]
        e_pos = jnp.exp(gcum)
        e_neg = jnp.exp(-gcum)
        e_prev = jnp.exp(gcum - sq["lw"])
        g_end = gcum[c_len - 1:c_len, :]
        e_end = jnp.exp(g_end)
        e_rest = jnp.exp(g_end - gcum)
        for h in range(RW_HEADS):
            sl = slice(HEAD * h, HEAD * (h + 1))
            kkh = sq["kk_in"][:, sl]
            kkh = kkh * lax.rsqrt(jnp.sum(kkh * kkh, axis=-1, keepdims=True) + 1e-6)
            bv = kkh * sq["a"][:, sl]
            chains.append(dict(
                i=i, h=h, sl=sl, at=-kkh * e_prev[:, sl], bt=bv * e_neg[:, sl],
                kt=sq["k2"][:, sl] * e_neg[:, sl], rt=sq["r"][:, sl] * e_pos[:, sl],
                vh=sq["v"][:, sl], r=sq["r"][:, sl], k2=sq["k2"][:, sl], g=sq["g"][:, sl],
                e_end=e_end[:, sl], b_rest=bv * e_rest[:, sl], k_rest=sq["k2"][:, sl] * e_rest[:, sl]))
    for c in chains:
        c["pair"] = _mm_nt(_rows(c["at"], c["rt"]), _rows(c["bt"], c["kt"]))
    for c in chains:
        pair = c["pair"]
        c["np"] = jnp.where(strict, pair[0:c_len, 0:c_len], 0.0)
        c["t"] = eye_f + c["np"]
        c["nak"] = jnp.where(strict, pair[0:c_len, c_len:2 * c_len], 0.0)
        c["mix"] = _cols(jnp.where(incl, pair[c_len:2 * c_len, 0:c_len], 0.0),
                         jnp.where(incl, pair[c_len:2 * c_len, c_len:2 * c_len], 0.0))
    for c in chains:
        c["nakv"] = _mm(c["nak"], c["vh"])
    for _ in range(_neumann_levels(n_valid)):
        for c in chains:
            c["np"] = _mm(c["np"], c["np"])
        for c in chains:
            c["t"] = c["t"] + _mm(c["t"], c["np"])
    for c in chains:
        c["tx"] = _mm(c["t"], _cols(c["at"], c["nakv"]))
    for c in chains:
        c["s"] = s_scr[c["i"], c["h"]]
        c["ps"] = _mm_nt(_rows(c["tx"][:, 0:HEAD], c["rt"]), c["s"])
    for c in chains:
        p = c["ps"][0:c_len] + c["tx"][:, HEAD:2 * HEAD]
        pv = _rows(p, c["vh"])
        c["y"] = c["ps"][c_len:2 * c_len] + _mm(c["mix"], pv)
        s_scr[c["i"], c["h"]] = c["s"] * c["e_end"] + _mm_tn(pv, _rows(c["b_rest"], c["k_rest"]))
    for c in chains:
        sl = c["sl"]
        y = c["y"]
        d = y - jnp.mean(y, axis=-1, keepdims=True)
        yn = d * lax.rsqrt(jnp.mean(d * d, axis=-1, keepdims=True) + RW_GN_EPS)
        yn = yn * lng_ref[:, sl] + lnb_ref[:, sl]
        bonus = jnp.sum(c["r"] * c["k2"] * rk_ref[:, sl], axis=-1, keepdims=True) * c["vh"]
        o_ref[c["i"], :, sl] = (yn + bonus) * c["g"]

    @pl.when(ci == pl.num_programs(1) - 1)
    def _():
        so_ref[...] = s_scr[...]


def _rw_group(z, shift0, s0, par, c_len, n_valid, n_par):
    n_seq, t_pad, _ = z.shape
    fresh = s0 is None
    nc = t_pad // c_len
    assert n_valid == c_len or nc == 1
    state = (RW_HEADS, HEAD, HEAD)
    in_specs = [_seq_specs(n_par, c_len, RW_SLAB)]
    args = [z]
    if not fresh:
        in_specs += [_state_spec(n_par, (1, RW_SLAB)), _state_spec(n_par, state)]
        args += [shift0, s0]
    for p in par:
        in_specs.append(_const_spec(p))
        args.append(p)
    return pl.pallas_call(
        functools.partial(_rw_body, c_len, n_valid, fresh, n_par),
        grid=(n_seq // n_par, nc),
        in_specs=in_specs,
        out_specs=[_seq_specs(n_par, c_len, RW_W), _state_spec(n_par, state)],
        out_shape=[jax.ShapeDtypeStruct((n_seq, t_pad, RW_W), F32),
                   jax.ShapeDtypeStruct((n_seq,) + state, F32)],
        scratch_shapes=[pltpu.VMEM((n_par, c_len + TAIL, RW_SLAB), F32),
                        pltpu.VMEM((n_par,) + state, F32)],
        compiler_params=pltpu.CompilerParams(
            dimension_semantics=("arbitrary", "arbitrary"), vmem_limit_bytes=VMEM_LIMIT),
        name="rw_group",
    )(*args)


def _rt_body(c_len, n_valid, fresh, n_par, *refs):
    if fresh:
        z_ref, cos_ref, sin_ref, lng_ref, o_ref, so_ref, s_scr = refs
    else:
        z_ref, cos_ref, sin_ref, s0_ref, lng_ref, o_ref, so_ref, s_scr = refs
    ci = pl.program_id(1)

    @pl.when(ci == 0)
    def _():
        if fresh:
            s_scr[...] = jnp.zeros_like(s_scr)
        else:
            s_scr[...] = s0_ref[...]

    lane = lax.broadcasted_iota(jnp.int32, (c_len, RT_QK), 1)
    first_half = (lane & (RT_DK - 1)) < (RT_DK // 2)
    cos = cos_ref[...]
    sin = sin_ref[...]

    def rotary(t):
        partner = jnp.where(first_half,
                            pltpu.roll(t, RT_QK - RT_DK // 2, axis=1),
                            pltpu.roll(t, RT_DK // 2, axis=1))
        return t * cos + partner * sin

    incl, _, _ = _tri(c_len)
    ri = lax.broadcasted_iota(jnp.int32, (c_len, c_len), 0)
    cj = lax.broadcasted_iota(jnp.int32, (c_len, c_len), 1)
    delta = (ri - cj).astype(F32)
    idx = lax.broadcasted_iota(jnp.int32, (c_len, 1), 0).astype(F32)
    log_decay = [math.log1p(-(2.0 ** (-5.0 - h))) for h in range(RT_HEADS)]
    dmasks = [jnp.where(incl, jnp.exp(delta * lg), 0.0) for lg in log_decay]
    q_scale = [jnp.exp((idx + 1.0) * lg) for lg in log_decay]
    k_scale = [jnp.where(idx < n_valid, jnp.exp((n_valid - 1.0 - idx) * lg), 0.0) for lg in log_decay]

    chains = []
    for i in range(n_par):
        q = rotary(z_ref[i, :, 0:RT_QK])
        k = rotary(z_ref[i, :, RT_QK:2 * RT_QK]) * (RT_DK ** -0.5)
        for h in range(RT_HEADS):
            chains.append(dict(
                i=i, h=h, qh=q[:, RT_DK * h:RT_DK * (h + 1)], kh=k[:, RT_DK * h:RT_DK * (h + 1)],
                vh=z_ref[i, :, 2 * RT_QK + HEAD * h:2 * RT_QK + HEAD * (h + 1)]))
    for c in chains:
        c["qk"] = _mm_nt(c["qh"], c["kh"])
    for c in chains:
        h = c["h"]
        c["s"] = s_scr[c["i"], h]
        c["cross"] = _mm(c["qh"] * q_scale[h], c["s"])
        s_scr[c["i"], h] = (c["s"] * math.exp(n_valid * log_decay[h])
                            + _mm_tn(c["kh"] * k_scale[h], c["vh"]))
    for c in chains:
        c["o"] = _mm(c["qk"] * dmasks[c["h"]], c["vh"]) + c["cross"]
    for c in chains:
        i, h = c["i"], c["h"]
        o = c["o"]
        d = o - jnp.mean(o, axis=-1, keepdims=True)
        on = d * lax.rsqrt(jnp.mean(d * d, axis=-1, keepdims=True) + RT_GN_EPS)
        gate = z_ref[i, :, 2 * RT_QK + RT_W + HEAD * h:2 * RT_QK + RT_W + HEAD * (h + 1)]
        o_ref[i, :, HEAD * h:HEAD * (h + 1)] = on * lng_ref[:, HEAD * h:HEAD * (h + 1)] * _silu(gate)

    @pl.when(ci == pl.num_programs(1) - 1)
    def _():
        so_ref[...] = s_scr[...]


def _rt_group(z, cos, sin, s0, ln_g, c_len, n_valid, n_par):
    n_seq, t_pad, _ = z.shape
    fresh = s0 is None
    nc = t_pad // c_len
    assert n_valid == c_len or nc == 1
    state = (RT_HEADS, RT_DK, HEAD)
    table = pl.BlockSpec((c_len, RT_QK), lambda b, c: (c, 0))
    in_specs = [_seq_specs(n_par, c_len, RT_SLAB), table, table]
    args = [z, cos, sin]
    if not fresh:
        in_specs.append(_state_spec(n_par, state))
        args.append(s0)
    in_specs.append(_const_spec(ln_g))
    args.append(ln_g)
    return pl.pallas_call(
        functools.partial(_rt_body, c_len, n_valid, fresh, n_par),
        grid=(n_seq // n_par, nc),
        in_specs=in_specs,
        out_specs=[_seq_specs(n_par, c_len, RT_W), _state_spec(n_par, state)],
        out_shape=[jax.ShapeDtypeStruct((n_seq, t_pad, RT_W), F32),
                   jax.ShapeDtypeStruct((n_seq,) + state, F32)],
        scratch_shapes=[pltpu.VMEM((n_par,) + state, F32)],
        compiler_params=pltpu.CompilerParams(
            dimension_semantics=("arbitrary", "arbitrary"), vmem_limit_bytes=VMEM_LIMIT),
        name="rt_group",
    )(*args)


def _rotary_tables(pos0, t_pad):
    half = RT_DK // 2
    inv = ROPE_BASE ** (-np.arange(half, dtype=np.float64) / half)
    ang = (pos0 + np.arange(t_pad, dtype=np.float64))[:, None] * inv[None, :]
    cos = np.tile(np.concatenate([np.cos(ang), np.cos(ang)], axis=1), (1, RT_HEADS))
    sin = np.tile(np.concatenate([-np.sin(ang), np.sin(ang)], axis=1), (1, RT_HEADS))
    return jnp.asarray(cos, F32), jnp.asarray(sin, F32)


def _outproj_body(x_ref, odn_ref, orw_ref, ort_ref, w_ref, o_ref):
    acc = x_ref[...]
    acc = acc + _dot(odn_ref[...].astype(BF16), w_ref[0:DN_QK, :], None)
    acc = acc + _dot(orw_ref[...].astype(BF16), w_ref[DN_QK:DN_QK + RW_W, :], None)
    acc = acc + _dot(ort_ref[...].astype(BF16), w_ref[DN_QK + RW_W:D_MODEL, :], None)
    o_ref[...] = acc


def _outproj(x, o_dn, o_rw, o_rt, w):
    n = x.shape[0]
    tm = min(512, n)
    row = lambda i: (i, 0)
    return pl.pallas_call(
        _outproj_body,
        grid=(n // tm,),
        in_specs=[pl.BlockSpec((tm, D_MODEL), row),
                  pl.BlockSpec((tm, DN_QK), row),
                  pl.BlockSpec((tm, RW_W), row),
                  pl.BlockSpec((tm, RT_W), row),
                  pl.BlockSpec((D_MODEL, D_MODEL), lambda i: (0, 0))],
        out_specs=pl.BlockSpec((tm, D_MODEL), row),
        out_shape=jax.ShapeDtypeStruct((n, D_MODEL), F32),
        compiler_params=pltpu.CompilerParams(
            dimension_semantics=("arbitrary",), vmem_limit_bytes=VMEM_LIMIT),
        name="outproj",
    )(x, o_dn, o_rw, o_rt, w)


def _top2_combine(logits):
    lane = lax.broadcasted_iota(jnp.int32, logits.shape, 1).astype(F32)
    lg = jnp.where(lane < N_EXP, logits, -jnp.inf)
    m1 = jnp.max(lg, axis=-1, keepdims=True)
    i1 = jnp.min(jnp.where(lg == m1, lane, float(LANE)), axis=-1, keepdims=True)
    lg2 = jnp.where(lane == i1, -jnp.inf, lg)
    m2 = jnp.max(lg2, axis=-1, keepdims=True)
    i2 = jnp.min(jnp.where(lg2 == m2, lane, float(LANE)), axis=-1, keepdims=True)
    e2 = jnp.exp(m2 - m1)
    den = 1.0 + e2
    return jnp.where(lane == i1, 1.0 / den, 0.0) + jnp.where(lane == i2, e2 / den, 0.0)


def _ffn_body(routed, final_norm, *refs):
    refs = list(refs)
    x_ref, g_ref = refs[0], refs[1]
    pos = 2
    if routed:
        router_ref = refs[pos]
        pos += 1
    w1_ref, w3_ref, w2_ref = refs[pos:pos + 3]
    pos += 3
    if final_norm:
        gf_ref = refs[pos]
        pos += 1
    o_ref = refs[pos]
    h_scr, acc = refs[pos + 1], refs[pos + 2]
    if routed:
        comb = refs[pos + 3]
    e = pl.program_id(1)
    f = pl.program_id(2)

    @pl.when((e == 0) & (f == 0))
    def _():
        x = x_ref[...]
        h = _rms(x, NORM_EPS) * g_ref[...]
        h_scr[...] = h.astype(BF16)
        acc[...] = x
        if routed:
            comb[...] = _top2_combine(_dot(h, router_ref[...]))

    hb = h_scr[...]
    u = _dot(hb, w1_ref[0], None)
    gate = _dot(hb, w3_ref[0], None)
    y = _dot((_silu(u) * gate).astype(BF16), w2_ref[0], None)
    if routed:
        lane = lax.broadcasted_iota(jnp.int32, comb.shape, 1)
        y = y * jnp.sum(jnp.where(lane == e, comb[...], 0.0), axis=-1, keepdims=True)
    acc[...] += y

    @pl.when((e == pl.num_programs(1) - 1) & (f == pl.num_programs(2) - 1))
    def _():
        if final_norm:
            o_ref[...] = _rms(acc[...], NORM_EPS) * gf_ref[...]
        else:
            o_ref[...] = acc[...]


def _ffn(x, gain, router, w1, w3, w2, final_gain, tf):
    n = x.shape[0]
    n_e, _, d_f = w1.shape
    routed = router is not None
    final_norm = final_gain is not None
    tm = min(1024, n)
    in_specs = [pl.BlockSpec((tm, D_MODEL), lambda i, e, f: (i, 0)),
                pl.BlockSpec((1, D_MODEL), lambda i, e, f: (0, 0))]
    args = [x, gain]
    if routed:
        in_specs.append(pl.BlockSpec((D_MODEL, LANE), lambda i, e, f: (0, 0)))
        args.append(router)
    in_specs += [pl.BlockSpec((1, D_MODEL, tf), lambda i, e, f: (e, 0, f)),
                 pl.BlockSpec((1, D_MODEL, tf), lambda i, e, f: (e, 0, f)),
                 pl.BlockSpec((1, tf, D_MODEL), lambda i, e, f: (e, f, 0))]
    args += [w1, w3, w2]
    if final_norm:
        in_specs.append(pl.BlockSpec((1, D_MODEL), lambda i, e, f: (0, 0)))
        args.append(final_gain)
    scratch = [pltpu.VMEM((tm, D_MODEL), BF16), pltpu.VMEM((tm, D_MODEL), F32)]
    if routed:
        scratch.append(pltpu.VMEM((tm, LANE), F32))
    return pl.pallas_call(
        functools.partial(_ffn_body, routed, final_norm),
        grid=(n // tm, n_e, d_f // tf),
        in_specs=in_specs,
        out_specs=pl.BlockSpec((tm, D_MODEL), lambda i, e, f: (i, 0)),
        out_shape=jax.ShapeDtypeStruct((n, D_MODEL), F32),
        scratch_shapes=scratch,
        compiler_params=pltpu.CompilerParams(
            dimension_semantics=("arbitrary", "arbitrary", "arbitrary"),
            vmem_limit_bytes=VMEM_LIMIT),
        name="ffn_moe" if routed else "ffn_dense",
    )(*args)


def _pad_last(a, width):
    return jnp.pad(a, [(0, 0)] * (a.ndim - 1) + [(0, width - a.shape[-1])])


def _pack_params(p):
    w_in = p["w_in"]
    w_in = jnp.concatenate([_pad_last(w_in[..., :DN_P], DN_SLAB),
                            _pad_last(w_in[..., DN_P:DN_P + RW_P], RW_SLAB),
                            w_in[..., DN_P + RW_P:]], axis=-1).astype(BF16)
    head_par = jnp.stack([_pad_last(p["dn_a_log"], LANE), _pad_last(p["dn_dt_bias"], LANE)], axis=1)
    g2 = jnp.pad(p["rw_g2"], ((0, 0), (0, RW_GATE_PAD - RW_GATE_R), (0, 0)))
    row = lambda a: a[:, None, :]
    return dict(
        ln_mix=row(p["ln_mix"]), w_in=w_in, dn_conv=p["dn_conv"], dn_head=head_par,
        dn_norm=row(p["dn_norm"]),
        rw=[(row(_pad_last(p["rw_mu"], RW_SLAB))[l], row(p["rw_w0"])[l], p["rw_w2"][l],
             row(p["rw_a0"])[l], p["rw_a2"][l], g2[l], row(p["rw_kk"])[l], row(p["rw_ka"])[l],
             p["rw_rk"].reshape(N_LAYERS, 1, RW_W)[l], row(p["rw_ln_g"])[l], row(p["rw_ln_b"])[l])
            for l in range(N_LAYERS)],
        rt_ln=row(p["rt_ln"]), w_out=p["w_out"].astype(BF16), ln_ffn=row(p["ln_ffn"]),
        ffn_w1=p["ffn_w1"].astype(BF16), ffn_w3=p["ffn_w3"].astype(BF16),
        ffn_w2=p["ffn_w2"].astype(BF16),
        moe_router=_pad_last(p["moe_router"], LANE),
        moe_w1=p["moe_w1"].astype(BF16), moe_w3=p["moe_w3"].astype(BF16),
        moe_w2=p["moe_w2"].astype(BF16),
        ln_final=p["ln_final"][None, :],
    )


def _trunk(x, n_seq, t_real, t_pad, c_len, n_par, states, pos0, w):
    n_valid = min(c_len, t_real)
    cos, sin = _rotary_tables(pos0, t_pad)
    convs, dns, shifts, rws, rts = [], [], [], [], []
    seq = lambda a: a.reshape(n_seq, t_pad, a.shape[-1])
    flat = lambda a: a.reshape(n_seq * t_pad, a.shape[-1])
    for l in range(N_LAYERS):
        z_dn, z_rw, z_rt = (seq(z) for z in _inproj(x, w["ln_mix"][l], w["w_in"][l]))
        if states is None:
            conv0 = s_dn0 = shift0 = s_rw0 = s_rt0 = None
        else:
            conv0, s_dn0 = states[0][l], states[1][l]
            shift0 = _pad_last(states[2][l], RW_SLAB)[:, None, :]
            s_rw0, s_rt0 = states[3][l], states[4][l]
        o_dn, s_dn = _dn_group(z_dn, conv0, s_dn0, w["dn_conv"][l], w["dn_head"][l],
                               w["dn_norm"][l], c_len, n_valid, n_par)
        o_rw, s_rw = _rw_group(z_rw, shift0, s_rw0, w["rw"][l], c_len, n_valid, n_par)
        o_rt, s_rt = _rt_group(z_rt, cos, sin, s_rt0, w["rt_ln"][l], c_len, n_valid, n_par)
        x = _outproj(x, flat(o_dn), flat(o_rw), flat(o_rt), w["w_out"][l])
        if l % 2 == 0:
            x = _ffn(x, w["ln_ffn"][l], None, w["ffn_w1"][l // 2:l // 2 + 1],
                     w["ffn_w3"][l // 2:l // 2 + 1], w["ffn_w2"][l // 2:l // 2 + 1],
                     w["ln_final"] if l == N_LAYERS - 1 else None, D_FF // 2)
        else:
            x = _ffn(x, w["ln_ffn"][l], w["moe_router"][l // 2], w["moe_w1"][l // 2],
                     w["moe_w3"][l // 2], w["moe_w2"][l // 2],
                     w["ln_final"] if l == N_LAYERS - 1 else None, D_EXP // 4)
        if t_real >= CONV_W - 1:
            convs.append(z_dn[:, t_real - (CONV_W - 1):t_real, :DN_CONV])
        else:
            convs.append(jnp.concatenate([conv0[:, t_real:], z_dn[:, :t_real, :DN_CONV]], axis=1))
        shifts.append(z_rw[:, t_real - 1, :RW_P])
        dns.append(s_dn)
        rws.append(s_rw)
        rts.append(s_rt)
    return x, jnp.stack(convs), jnp.stack(dns), jnp.stack(shifts), jnp.stack(rws), jnp.stack(rts)


def kernel(x_prompt, x_sample, state_dn_conv, state_dn, state_rw_shift, state_rw, state_rt, ln_mix, w_in, dn_conv, dn_a_log, dn_dt_bias, dn_norm, rw_mu, rw_w0, rw_w2, rw_a0, rw_a2, rw_g2, rw_kk, rw_ka, rw_rk, rw_ln_g, rw_ln_b, rt_ln, w_out, ln_ffn, ffn_w1, ffn_w3, ffn_w2, moe_router, moe_w1, moe_w3, moe_w2, ln_final):
    w = _pack_params(dict(
        ln_mix=ln_mix, w_in=w_in, dn_conv=dn_conv, dn_a_log=dn_a_log, dn_dt_bias=dn_dt_bias,
        dn_norm=dn_norm, rw_mu=rw_mu, rw_w0=rw_w0, rw_w2=rw_w2, rw_a0=rw_a0, rw_a2=rw_a2,
        rw_g2=rw_g2, rw_kk=rw_kk, rw_ka=rw_ka, rw_rk=rw_rk, rw_ln_g=rw_ln_g, rw_ln_b=rw_ln_b,
        rt_ln=rt_ln, w_out=w_out, ln_ffn=ln_ffn, ffn_w1=ffn_w1, ffn_w3=ffn_w3, ffn_w2=ffn_w2,
        moe_router=moe_router, moe_w1=moe_w1, moe_w3=moe_w3, moe_w2=moe_w2, ln_final=ln_final))
    bp, tp, _ = x_prompt.shape
    bs, ts, _ = x_sample.shape

    yp, *p_states = _trunk(x_prompt.reshape(bp * tp, D_MODEL), bp, tp, tp, 64, 2, None, 0, w)
    y_prompt = yp.reshape(bp, tp, D_MODEL)

    ts_pad = 8
    xs = jnp.pad(x_sample, ((0, 0), (0, ts_pad - ts), (0, 0))).reshape(bs * ts_pad, D_MODEL)
    ys, *s_states = _trunk(xs, bs, ts, ts_pad, ts_pad, 4,
                           (state_dn_conv, state_dn, state_rw_shift, state_rw, state_rt),
                           PAST_LEN, w)
    y_sample = ys.reshape(bs, ts_pad, D_MODEL)[:, :ts]
    return (y_prompt, y_sample, *p_states, *s_states)
```

```python
import functools
import math

import numpy as np
import jax
import jax.numpy as jnp
from jax import lax
from jax.experimental import pallas as pl
from jax.experimental.pallas import tpu as pltpu

F32 = jnp.float32
BF16 = jnp.bfloat16
HI = lax.Precision.HIGHEST

D_MODEL = 1024
N_LAYERS = 2
PAST_LEN = 16384
HEAD = 64
DN_HEADS = 6
DN_QK = DN_HEADS * HEAD
DN_CONV = 3 * DN_QK
CONV_W = 4
RW_HEADS = 6
RW_W = RW_HEADS * HEAD
RW_GATE_R = 160
RT_HEADS = 4
RT_DK = 32
RT_QK = RT_HEADS * RT_DK
RT_W = RT_HEADS * HEAD
DN_P = DN_CONV + DN_QK + 2 * DN_HEADS
RW_P = 3 * RW_W + 64 + 64 + RW_GATE_R
RT_P = 2 * RT_QK + 2 * RT_W
LANE = 128
DN_SLAB = 1664
RW_SLAB = 1536
RT_SLAB = RT_P
P_SLABS = DN_SLAB + RW_SLAB + RT_SLAB
RW_GATE_PAD = RW_SLAB - (3 * RW_W + 128)
N_EXP = 8
D_FF = 2816
D_EXP = 3584
ROPE_BASE = 10000.0
NORM_EPS = 1e-6
RW_GN_EPS = 64e-5
RT_GN_EPS = 1e-5
TAIL = 8
VMEM_LIMIT = 56 * 1024 * 1024


def _dot(a, b, precision=HI):
    return jnp.dot(a, b, preferred_element_type=F32, precision=precision)


def _dot_tn(a, b, precision=HI):
    return lax.dot_general(a, b, (((0,), (0,)), ((), ())),
                           preferred_element_type=F32, precision=precision)


def _mm(a, b):
    return jnp.dot(a.astype(BF16), b.astype(BF16), preferred_element_type=F32)


def _mm_nt(a, b):
    return lax.dot_general(a.astype(BF16), b.astype(BF16), (((1,), (1,)), ((), ())),
                           preferred_element_type=F32)


def _mm_tn(a, b):
    return lax.dot_general(a.astype(BF16), b.astype(BF16), (((0,), (0,)), ((), ())),
                           preferred_element_type=F32)


def _rows(*parts):
    return jnp.concatenate(parts, axis=0)


def _cols(*parts):
    return jnp.concatenate(parts, axis=1)


def _silu(x):
    return x * jax.nn.sigmoid(x)


def _softplus(x):
    return jnp.maximum(x, 0.0) + jnp.log1p(jnp.exp(-jnp.abs(x)))


def _rms(x, eps):
    return x * lax.rsqrt(jnp.mean(x * x, axis=-1, keepdims=True) + eps)


def _tri(c):
    r = lax.broadcasted_iota(jnp.int32, (c, c), 0)
    col = lax.broadcasted_iota(jnp.int32, (c, c), 1)
    return r >= col, r > col, r == col


def _neumann_levels(n_rows):
    return max(0, math.ceil(math.log2(n_rows)) - 1)


def _inproj_body(x_ref, g_ref, w_ref, zdn_ref, zrw_ref, zrt_ref):
    h = (_rms(x_ref[...], NORM_EPS) * g_ref[...]).astype(BF16)
    zdn_ref[...] = _dot(h, w_ref[:, 0:DN_SLAB], None)
    zrw_ref[...] = _dot(h, w_ref[:, DN_SLAB:DN_SLAB + RW_SLAB], None)
    zrt_ref[...] = _dot(h, w_ref[:, DN_SLAB + RW_SLAB:P_SLABS], None)


def _inproj(x, gain, w):
    n = x.shape[0]
    tm = min(256, n)
    return pl.pallas_call(
        _inproj_body,
        grid=(n // tm,),
        in_specs=[pl.BlockSpec((tm, D_MODEL), lambda i: (i, 0)),
                  pl.BlockSpec((1, D_MODEL), lambda i: (0, 0)),
                  pl.BlockSpec((D_MODEL, P_SLABS), lambda i: (0, 0))],
        out_specs=[pl.BlockSpec((tm, DN_SLAB), lambda i: (i, 0)),
                   pl.BlockSpec((tm, RW_SLAB), lambda i: (i, 0)),
                   pl.BlockSpec((tm, RT_SLAB), lambda i: (i, 0))],
        out_shape=[jax.ShapeDtypeStruct((n, DN_SLAB), F32),
                   jax.ShapeDtypeStruct((n, RW_SLAB), F32),
                   jax.ShapeDtypeStruct((n, RT_SLAB), F32)],
        compiler_params=pltpu.CompilerParams(
            dimension_semantics=("arbitrary",), vmem_limit_bytes=VMEM_LIMIT),
        name="inproj",
    )(x, gain, w)


def _seq_specs(n_par, c_len, width):
    return pl.BlockSpec((n_par, c_len, width), lambda b, c: (b, c, 0))


def _state_spec(n_par, shape):
    return pl.BlockSpec((n_par,) + shape, lambda b, c: (b,) + (0,) * len(shape))


def _const_spec(a):
    return pl.BlockSpec(a.shape, lambda b, c: (0,) * a.ndim)


def _dn_body(c_len, n_valid, fresh, n_par, *refs):
    if fresh:
        z_ref, cw_ref, hp_ref, nw_ref, o_ref, so_ref, xe, s_scr = refs
    else:
        z_ref, c0_ref, s0_ref, cw_ref, hp_ref, nw_ref, o_ref, so_ref, xe, s_scr = refs
    ci = pl.program_id(1)

    @pl.when(ci == 0)
    def _():
        xe[:, 0:TAIL, :] = jnp.zeros((n_par, TAIL, DN_CONV), F32)
        if fresh:
            s_scr[...] = jnp.zeros_like(s_scr)
        else:
            xe[:, TAIL - (CONV_W - 1):TAIL, :] = c0_ref[...]
            s_scr[...] = s0_ref[...]

    incl, strict, eye = _tri(c_len)
    eye_f = eye.astype(F32)
    tri_f = incl.astype(F32)
    live = lax.broadcasted_iota(jnp.int32, (c_len, LANE), 0) < n_valid
    neg_a = -jnp.exp(hp_ref[0:1, :])
    dt_bias = hp_ref[1:2, :]

    seqs = []
    for i in range(n_par):
        x = z_ref[i, :, 0:DN_CONV]
        xe[i, TAIL:TAIL + c_len, :] = x
        acc = x * cw_ref[CONV_W - 1:CONV_W, :]
        for j in range(CONV_W - 1):
            off = TAIL - (CONV_W - 1) + j
            acc = acc + xe[i, off:off + c_len, :] * cw_ref[j:j + 1, :]
        xe[i, 0:TAIL, :] = xe[i, c_len:c_len + TAIL, :]
        ab = z_ref[i, :, DN_CONV + DN_QK:DN_SLAB]
        g_tok = neg_a * _softplus(ab + dt_bias)
        beta = jax.nn.sigmoid(ab)
        if n_valid < c_len:
            g_tok = jnp.where(live, g_tok, 0.0)
            beta = jnp.where(live, beta, 0.0)
        seqs.append(dict(conv=_silu(acc), beta=beta, g_tok=g_tok))
    for sq in seqs:
        sq["g_cum"] = _dot(tri_f, sq["g_tok"])
    for sq in seqs:
        sq["g_cum_t"] = _dot_tn(sq["g_cum"], eye_f)

    chains = []
    for i, sq in enumerate(seqs):
        for h in range(DN_HEADS):
            lo = HEAD * h
            q = sq["conv"][:, lo:lo + HEAD]
            k = sq["conv"][:, DN_QK + lo:DN_QK + lo + HEAD]
            v = sq["conv"][:, 2 * DN_QK + lo:2 * DN_QK + lo + HEAD]
            q = q * lax.rsqrt(jnp.sum(q * q, axis=-1, keepdims=True) + 1e-6) * (HEAD ** -0.5)
            k = k * lax.rsqrt(jnp.sum(k * k, axis=-1, keepdims=True) + 1e-6)
            gc = sq["g_cum"][:, h:h + 1]
            b = sq["beta"][:, DN_HEADS + h:DN_HEADS + h + 1]
            dmask = jnp.where(
                incl, jnp.exp(jnp.minimum(gc - sq["g_cum_t"][h:h + 1, :], 0.0)), 0.0)
            chains.append(dict(i=i, h=h, q=q, k=k, v=v, gc=gc, b=b, dmask=dmask,
                               kb=k * b, egc=jnp.exp(gc)))
    for c in chains:
        c["kq"] = _mm_nt(_rows(c["kb"], c["q"]), c["k"])
    for c in chains:
        c["np"] = -jnp.where(strict, c["kq"][0:c_len] * c["dmask"], 0.0)
        c["t"] = eye_f + c["np"]
    for _ in range(_neumann_levels(n_valid)):
        for c in chains:
            c["np"] = _mm(c["np"], c["np"])
        for c in chains:
            c["t"] = c["t"] + _mm(c["t"], c["np"])
    for c in chains:
        c["tx"] = _mm(c["t"], _cols(c["v"] * c["b"], c["kb"] * c["egc"]))
    for c in chains:
        c["s"] = s_scr[c["i"], c["h"]]
        c["ss"] = _mm(_rows(c["tx"][:, HEAD:2 * HEAD], c["q"] * c["egc"]), c["s"])
    for c in chains:
        c["u"] = c["tx"][:, 0:HEAD] - c["ss"][0:c_len]
        g_last = c["gc"][c_len - 1:c_len, :]
        c["o"] = c["ss"][c_len:2 * c_len] + _mm(c["kq"][c_len:2 * c_len] * c["dmask"], c["u"])
        s_scr[c["i"], c["h"]] = (c["s"] * jnp.exp(g_last)
                                 + _mm_tn(c["k"] * jnp.exp(g_last - c["gc"]), c["u"]))
    for c in chains:
        lo = HEAD * c["h"]
        zg = z_ref[c["i"], :, DN_CONV + lo:DN_CONV + lo + HEAD]
        o_ref[c["i"], :, lo:lo + HEAD] = _rms(c["o"], NORM_EPS) * nw_ref[...] * _silu(zg)

    @pl.when(ci == pl.num_programs(1) - 1)
    def _():
        so_ref[...] = s_scr[...]


def _dn_group(z, conv0, s0, conv_w, head_par, norm_w, c_len, n_valid, n_par):
    n_seq, t_pad, _ = z.shape
    fresh = s0 is None
    nc = t_pad // c_len
    assert n_valid == c_len or nc == 1
    state = (DN_HEADS, HEAD, HEAD)
    in_specs = [_seq_specs(n_par, c_len, DN_SLAB)]
    args = [z]
    if not fresh:
        in_specs += [_state_spec(n_par, (CONV_W - 1, DN_CONV)), _state_spec(n_par, state)]
        args += [conv0, s0]
    for p in (conv_w, head_par, norm_w):
        in_specs.append(_const_spec(p))
        args.append(p)
    return pl.pallas_call(
        functools.partial(_dn_body, c_len, n_valid, fresh, n_par),
        grid=(n_seq // n_par, nc),
        in_specs=in_specs,
        out_specs=[_seq_specs(n_par, c_len, DN_QK), _state_spec(n_par, state)],
        out_shape=[jax.ShapeDtypeStruct((n_seq, t_pad, DN_QK), F32),
                   jax.ShapeDtypeStruct((n_seq,) + state, F32)],
        scratch_shapes=[pltpu.VMEM((n_par, c_len + TAIL, DN_CONV), F32),
                        pltpu.VMEM((n_par,) + state, F32)],
        compiler_params=pltpu.CompilerParams(
            dimension_semantics=("arbitrary", "arbitrary"), vmem_limit_bytes=VMEM_LIMIT),
        name="dn_group",
    )(*args)


def _rw_body(c_len, n_valid, fresh, n_par, *refs):
    if fresh:
        (z_ref, mu_ref, w0_ref, w2_ref, a0_ref, a2_ref, g2_ref, kk_ref, ka_ref, rk_ref,
         lng_ref, lnb_ref, o_ref, so_ref, xe, s_scr) = refs
    else:
        (z_ref, sh0_ref, s0_ref, mu_ref, w0_ref, w2_ref, a0_ref, a2_ref, g2_ref, kk_ref, ka_ref,
         rk_ref, lng_ref, lnb_ref, o_ref, so_ref, xe, s_scr) = refs
    ci = pl.program_id(1)

    @pl.when(ci == 0)
    def _():
        xe[:, 0:TAIL, :] = jnp.zeros((n_par, TAIL, RW_SLAB), F32)
        if fresh:
            s_scr[...] = jnp.zeros_like(s_scr)
        else:
            xe[:, TAIL - 1:TAIL, :] = sh0_ref[...]
            s_scr[...] = s0_ref[...]

    incl, strict, eye = _tri(c_len)
    eye_f = eye.astype(F32)
    tri_f = incl.astype(F32)
    live = lax.broadcasted_iota(jnp.int32, (c_len, RW_W), 0) < n_valid

    seqs = []
    for i in range(n_par):
        x = z_ref[i]
        xe[i, TAIL:TAIL + c_len, :] = x
        prev = xe[i, TAIL - 1:TAIL - 1 + c_len, :]
        zs = x + (prev - x) * mu_ref[...]
        xe[i, 0:TAIL, :] = xe[i, c_len:c_len + TAIL, :]
        seqs.append(dict(r=zs[:, 0:RW_W], k=zs[:, RW_W:2 * RW_W], v=zs[:, 2 * RW_W:3 * RW_W],
                         wl=zs[:, 3 * RW_W:3 * RW_W + 64], al=zs[:, 3 * RW_W + 64:3 * RW_W + 128],
                         gl=zs[:, 3 * RW_W + 128:RW_SLAB]))
    for sq in seqs:
        sq["w_lora"] = _dot(jnp.tanh(sq["wl"]), w2_ref[...])
        sq["a_lora"] = _dot(sq["al"], a2_ref[...])
        sq["g"] = _dot(jax.nn.sigmoid(sq["gl"]), g2_ref[...])
    for sq in seqs:
        w_log = -_softplus(-(w0_ref[...] + sq["w_lora"])) - 0.5
        a = jax.nn.sigmoid(a0_ref[...] + sq["a_lora"])
        kk_in = sq["k"] * kk_ref[...]
        k2 = sq["k"] * (1.0 + (a - 1.0) * ka_ref[...])
        lw = -jnp.exp(w_log)
        if n_valid < c_len:
            lw = jnp.where(live, lw, 0.0)
            k2 = jnp.where(live, k2, 0.0)
            kk_in = jnp.where(live, kk_in, 0.0)
        sq.update(a=a, kk_in=kk_in, k2=k2, lw=lw)
    for sq in seqs:
        sq["gcum"] = _dot(tri_f, sq["lw"])

    chains = []
    for i, sq in enumerate(seqs):
        gcum = sq["gcum"]
        e_pos = jnp.exp(gcum)
        e_neg = jnp.exp(-gcum)
        e_prev = jnp.exp(gcum - sq["lw"])
        g_end = gcum[c_len - 1:c_len, :]
        e_end = jnp.exp(g_end)
        e_rest = jnp.exp(g_end - gcum)
        for h in range(RW_HEADS):
            sl = slice(HEAD * h, HEAD * (h + 1))
            kkh = sq["kk_in"][:, sl]
            kkh = kkh * lax.rsqrt(jnp.sum(kkh * kkh, axis=-1, keepdims=True) + 1e-6)
            bv = kkh * sq["a"][:, sl]
            chains.append(dict(
                i=i, h=h, sl=sl, at=-kkh * e_prev[:, sl], bt=bv * e_neg[:, sl],
                kt=sq["k2"][:, sl] * e_neg[:, sl], rt=sq["r"][:, sl] * e_pos[:, sl],
                vh=sq["v"][:, sl], r=sq["r"][:, sl], k2=sq["k2"][:, sl], g=sq["g"][:, sl],
                e_end=e_end[:, sl], b_rest=bv * e_rest[:, sl], k_rest=sq["k2"][:, sl] * e_rest[:, sl]))
    for c in chains:
        c["pair"] = _mm_nt(_rows(c["at"], c["rt"]), _rows(c["bt"], c["kt"]))
    for c in chains:
        pair = c["pair"]
        c["np"] = jnp.where(strict, pair[0:c_len, 0:c_len], 0.0)
        c["t"] = eye_f + c["np"]
        c["nak"] = jnp.where(strict, pair[0:c_len, c_len:2 * c_len], 0.0)
        c["mix"] = _cols(jnp.where(incl, pair[c_len:2 * c_len, 0:c_len], 0.0),
                         jnp.where(incl, pair[c_len:2 * c_len, c_len:2 * c_len], 0.0))
    for c in chains:
        c["nakv"] = _mm(c["nak"], c["vh"])
    for _ in range(_neumann_levels(n_valid)):
        for c in chains:
            c["np"] = _mm(c["np"], c["np"])
        for c in chains:
            c["t"] = c["t"] + _mm(c["t"], c["np"])
    for c in chains:
        c["tx"] = _mm(c["t"], _cols(c["at"], c["nakv"]))
    for c in chains:
        c["s"] = s_scr[c["i"], c["h"]]
        c["ps"] = _mm_nt(_rows(c["tx"][:, 0:HEAD], c["rt"]), c["s"])
    for c in chains:
        p = c["ps"][0:c_len] + c["tx"][:, HEAD:2 * HEAD]
        pv = _rows(p, c["vh"])
        c["y"] = c["ps"][c_len:2 * c_len] + _mm(c["mix"], pv)
        s_scr[c["i"], c["h"]] = c["s"] * c["e_end"] + _mm_tn(pv, _rows(c["b_rest"], c["k_rest"]))
    for c in chains:
        sl = c["sl"]
        y = c["y"]
        d = y - jnp.mean(y, axis=-1, keepdims=True)
        yn = d * lax.rsqrt(jnp.mean(d * d, axis=-1, keepdims=True) + RW_GN_EPS)
        yn = yn * lng_ref[:, sl] + lnb_ref[:, sl]
        bonus = jnp.sum(c["r"] * c["k2"] * rk_ref[:, sl], axis=-1, keepdims=True) * c["vh"]
        o_ref[c["i"], :, sl] = (yn + bonus) * c["g"]

    @pl.when(ci == pl.num_programs(1) - 1)
    def _():
        so_ref[...] = s_scr[...]


def _rw_group(z, shift0, s0, par, c_len, n_valid, n_par):
    n_seq, t_pad, _ = z.shape
    fresh = s0 is None
    nc = t_pad // c_len
    assert n_valid == c_len or nc == 1
    state = (RW_HEADS, HEAD, HEAD)
    in_specs = [_seq_specs(n_par, c_len, RW_SLAB)]
    args = [z]
    if not fresh:
        in_specs += [_state_spec(n_par, (1, RW_SLAB)), _state_spec(n_par, state)]
        args += [shift0, s0]
    for p in par:
        in_specs.append(_const_spec(p))
        args.append(p)
    return pl.pallas_call(
        functools.partial(_rw_body, c_len, n_valid, fresh, n_par),
        grid=(n_seq // n_par, nc),
        in_specs=in_specs,
        out_specs=[_seq_specs(n_par, c_len, RW_W), _state_spec(n_par, state)],
        out_shape=[jax.ShapeDtypeStruct((n_seq, t_pad, RW_W), F32),
                   jax.ShapeDtypeStruct((n_seq,) + state, F32)],
        scratch_shapes=[pltpu.VMEM((n_par, c_len + TAIL, RW_SLAB), F32),
                        pltpu.VMEM((n_par,) + state, F32)],
        compiler_params=pltpu.CompilerParams(
            dimension_semantics=("arbitrary", "arbitrary"), vmem_limit_bytes=VMEM_LIMIT),
        name="rw_group",
    )(*args)


def _rt_body(c_len, n_valid, fresh, n_par, *refs):
    if fresh:
        z_ref, cos_ref, sin_ref, lng_ref, o_ref, so_ref, s_scr = refs
    else:
        z_ref, cos_ref, sin_ref, s0_ref, lng_ref, o_ref, so_ref, s_scr = refs
    ci = pl.program_id(1)

    @pl.when(ci == 0)
    def _():
        if fresh:
            s_scr[...] = jnp.zeros_like(s_scr)
        else:
            s_scr[...] = s0_ref[...]

    lane = lax.broadcasted_iota(jnp.int32, (c_len, RT_QK), 1)
    first_half = (lane & (RT_DK - 1)) < (RT_DK // 2)
    cos = cos_ref[...]
    sin = sin_ref[...]

    def rotary(t):
        partner = jnp.where(first_half,
                            pltpu.roll(t, RT_QK - RT_DK // 2, axis=1),
                            pltpu.roll(t, RT_DK // 2, axis=1))
        return t * cos + partner * sin

    incl, _, _ = _tri(c_len)
    ri = lax.broadcasted_iota(jnp.int32, (c_len, c_len), 0)
    cj = lax.broadcasted_iota(jnp.int32, (c_len, c_len), 1)
    delta = (ri - cj).astype(F32)
    idx = lax.broadcasted_iota(jnp.int32, (c_len, 1), 0).astype(F32)
    log_decay = [math.log1p(-(2.0 ** (-5.0 - h))) for h in range(RT_HEADS)]
    dmasks = [jnp.where(incl, jnp.exp(delta * lg), 0.0) for lg in log_decay]
    q_scale = [jnp.exp((idx + 1.0) * lg) for lg in log_decay]
    k_scale = [jnp.where(idx < n_valid, jnp.exp((n_valid - 1.0 - idx) * lg), 0.0) for lg in log_decay]

    chains = []
    for i in range(n_par):
        q = rotary(z_ref[i, :, 0:RT_QK])
        k = rotary(z_ref[i, :, RT_QK:2 * RT_QK]) * (RT_DK ** -0.5)
        for h in range(RT_HEADS):
            chains.append(dict(
                i=i, h=h, qh=q[:, RT_DK * h:RT_DK * (h + 1)], kh=k[:, RT_DK * h:RT_DK * (h + 1)],
                vh=z_ref[i, :, 2 * RT_QK + HEAD * h:2 * RT_QK + HEAD * (h + 1)]))
    for c in chains:
        c["qk"] = _mm_nt(c["qh"], c["kh"])
    for c in chains:
        h = c["h"]
        c["s"] = s_scr[c["i"], h]
        c["cross"] = _mm(c["qh"] * q_scale[h], c["s"])
        s_scr[c["i"], h] = (c["s"] * math.exp(n_valid * log_decay[h])
                            + _mm_tn(c["kh"] * k_scale[h], c["vh"]))
    for c in chains:
        c["o"] = _mm(c["qk"] * dmasks[c["h"]], c["vh"]) + c["cross"]
    for c in chains:
        i, h = c["i"], c["h"]
        o = c["o"]
        d = o - jnp.mean(o, axis=-1, keepdims=True)
        on = d * lax.rsqrt(jnp.mean(d * d, axis=-1, keepdims=True) + RT_GN_EPS)
        gate = z_ref[i, :, 2 * RT_QK + RT_W + HEAD * h:2 * RT_QK + RT_W + HEAD * (h + 1)]
        o_ref[i, :, HEAD * h:HEAD * (h + 1)] = on * lng_ref[:, HEAD * h:HEAD * (h + 1)] * _silu(gate)

    @pl.when(ci == pl.num_programs(1) - 1)
    def _():
        so_ref[...] = s_scr[...]


def _rt_group(z, cos, sin, s0, ln_g, c_len, n_valid, n_par):
    n_seq, t_pad, _ = z.shape
    fresh = s0 is None
    nc = t_pad // c_len
    assert n_valid == c_len or nc == 1
    state = (RT_HEADS, RT_DK, HEAD)
    table = pl.BlockSpec((c_len, RT_QK), lambda b, c: (c, 0))
    in_specs = [_seq_specs(n_par, c_len, RT_SLAB), table, table]
    args = [z, cos, sin]
    if not fresh:
        in_specs.append(_state_spec(n_par, state))
        args.append(s0)
    in_specs.append(_const_spec(ln_g))
    args.append(ln_g)
    return pl.pallas_call(
        functools.partial(_rt_body, c_len, n_valid, fresh, n_par),
        grid=(n_seq // n_par, nc),
        in_specs=in_specs,
        out_specs=[_seq_specs(n_par, c_len, RT_W), _state_spec(n_par, state)],
        out_shape=[jax.ShapeDtypeStruct((n_seq, t_pad, RT_W), F32),
                   jax.ShapeDtypeStruct((n_seq,) + state, F32)],
        scratch_shapes=[pltpu.VMEM((n_par,) + state, F32)],
        compiler_params=pltpu.CompilerParams(
            dimension_semantics=("arbitrary", "arbitrary"), vmem_limit_bytes=VMEM_LIMIT),
        name="rt_group",
    )(*args)


def _rotary_tables(pos0, t_pad):
    half = RT_DK // 2
    inv = ROPE_BASE ** (-np.arange(half, dtype=np.float64) / half)
    ang = (pos0 + np.arange(t_pad, dtype=np.float64))[:, None] * inv[None, :]
    cos = np.tile(np.concatenate([np.cos(ang), np.cos(ang)], axis=1), (1, RT_HEADS))
    sin = np.tile(np.concatenate([-np.sin(ang), np.sin(ang)], axis=1), (1, RT_HEADS))
    return jnp.asarray(cos, F32), jnp.asarray(sin, F32)


def _outproj_body(x_ref, odn_ref, orw_ref, ort_ref, w_ref, o_ref):
    acc = x_ref[...]
    acc = acc + _dot(odn_ref[...].astype(BF16), w_ref[0:DN_QK, :], None)
    acc = acc + _dot(orw_ref[...].astype(BF16), w_ref[DN_QK:DN_QK + RW_W, :], None)
    acc = acc + _dot(ort_ref[...].astype(BF16), w_ref[DN_QK + RW_W:D_MODEL, :], None)
    o_ref[...] = acc


def _outproj(x, o_dn, o_rw, o_rt, w):
    n = x.shape[0]
    tm = min(512, n)
    row = lambda i: (i, 0)
    return pl.pallas_call(
        _outproj_body,
        grid=(n // tm,),
        in_specs=[pl.BlockSpec((tm, D_MODEL), row),
                  pl.BlockSpec((tm, DN_QK), row),
                  pl.BlockSpec((tm, RW_W), row),
                  pl.BlockSpec((tm, RT_W), row),
                  pl.BlockSpec((D_MODEL, D_MODEL), lambda i: (0, 0))],
        out_specs=pl.BlockSpec((tm, D_MODEL), row),
        out_shape=jax.ShapeDtypeStruct((n, D_MODEL), F32),
        compiler_params=pltpu.CompilerParams(
            dimension_semantics=("arbitrary",), vmem_limit_bytes=VMEM_LIMIT),
        name="outproj",
    )(x, o_dn, o_rw, o_rt, w)


def _ffn_body(final_norm, *refs):
    if final_norm:
        x_ref, g_ref, w1_ref, w3_ref, w2_ref, gf_ref, o_ref, h_scr, acc = refs
    else:
        x_ref, g_ref, w1_ref, w3_ref, w2_ref, o_ref, h_scr, acc = refs
    f = pl.program_id(1)

    @pl.when(f == 0)
    def _():
        x = x_ref[...]
        h_scr[...] = (_rms(x, NORM_EPS) * g_ref[...]).astype(BF16)
        acc[...] = x

    hb = h_scr[...]
    u = _dot(hb, w1_ref[...], None)
    gate = _dot(hb, w3_ref[...], None)
    acc[...] += _dot((_silu(u) * gate).astype(BF16), w2_ref[...], None)

    @pl.when(f == pl.num_programs(1) - 1)
    def _():
        if final_norm:
            o_ref[...] = _rms(acc[...], NORM_EPS) * gf_ref[...]
        else:
            o_ref[...] = acc[...]


def _ffn(x, gain, w1, w3, w2, final_gain, tf):
    n = x.shape[0]
    d_f = w1.shape[1]
    final_norm = final_gain is not None
    tm = min(1024, n)
    in_specs = [pl.BlockSpec((tm, D_MODEL), lambda i, f: (i, 0)),
                pl.BlockSpec((1, D_MODEL), lambda i, f: (0, 0)),
                pl.BlockSpec((D_MODEL, tf), lambda i, f: (0, f)),
                pl.BlockSpec((D_MODEL, tf), lambda i, f: (0, f)),
                pl.BlockSpec((tf, D_MODEL), lambda i, f: (f, 0))]
    args = [x, gain, w1, w3, w2]
    if final_norm:
        in_specs.append(pl.BlockSpec((1, D_MODEL), lambda i, f: (0, 0)))
        args.append(final_gain)
    return pl.pallas_call(
        functools.partial(_ffn_body, final_norm),
        grid=(n // tm, d_f // tf),
        in_specs=in_specs,
        out_specs=pl.BlockSpec((tm, D_MODEL), lambda i, f: (i, 0)),
        out_shape=jax.ShapeDtypeStruct((n, D_MODEL), F32),
        scratch_shapes=[pltpu.VMEM((tm, D_MODEL), BF16), pltpu.VMEM((tm, D_MODEL), F32)],
        compiler_params=pltpu.CompilerParams(
            dimension_semantics=("arbitrary", "arbitrary"), vmem_limit_bytes=VMEM_LIMIT),
        name="ffn_dense",
    )(*args)


def _route_body(x_ref, g_ref, router_ref, o_ref):
    h = _rms(x_ref[...], NORM_EPS) * g_ref[...]
    logits = _dot(h, router_ref[...])
    lane = lax.broadcasted_iota(jnp.int32, logits.shape, 1).astype(F32)
    lg = jnp.where(lane < N_EXP, logits, -jnp.inf)
    m1 = jnp.max(lg, axis=-1, keepdims=True)
    i1 = jnp.min(jnp.where(lg == m1, lane, float(LANE)), axis=-1, keepdims=True)
    lg2 = jnp.where(lane == i1, -jnp.inf, lg)
    m2 = jnp.max(lg2, axis=-1, keepdims=True)
    i2 = jnp.min(jnp.where(lg2 == m2, lane, float(LANE)), axis=-1, keepdims=True)
    e2 = jnp.exp(m2 - m1)
    den = 1.0 + e2
    o_ref[...] = (jnp.where(lane == 0.0, i1, 0.0) + jnp.where(lane == 1.0, i2, 0.0)
                  + jnp.where(lane == 2.0, 1.0 / den, 0.0) + jnp.where(lane == 3.0, e2 / den, 0.0))


def _route(x, gain, router):
    n = x.shape[0]
    tm = min(512, n)
    return pl.pallas_call(
        _route_body,
        grid=(n // tm,),
        in_specs=[pl.BlockSpec((tm, D_MODEL), lambda i: (i, 0)),
                  pl.BlockSpec((1, D_MODEL), lambda i: (0, 0)),
                  pl.BlockSpec((D_MODEL, LANE), lambda i: (0, 0))],
        out_specs=pl.BlockSpec((tm, LANE), lambda i: (i, 0)),
        out_shape=jax.ShapeDtypeStruct((n, LANE), F32),
        compiler_params=pltpu.CompilerParams(
            dimension_semantics=("arbitrary",), vmem_limit_bytes=VMEM_LIMIT),
        name="moe_route",
    )(x, gain, router)


def _row_gather(src_hbm, idx_ref, buf, sem, slot, n_rows):
    def copy(r):
        return pltpu.make_async_copy(src_hbm.at[pl.ds(idx_ref[0, 0, r], 1), :],
                                     buf.at[slot, pl.ds(r, 1), :], sem.at[slot])

    def start():
        def body(r, carry):
            copy(r).start()
            return carry
        lax.fori_loop(0, n_rows, body, 0, unroll=8)

    def wait():
        def body(r, carry):
            copy(r).wait()
            return carry
        lax.fori_loop(0, n_rows, body, 0, unroll=8)

    return start, wait


def _moe_ffn_body(tm, te_ref, nu_ref, tok_ref, tok_next_ref, x_hbm, g_ref, w1_ref, w3_ref, w2_ref,
                  ys_ref, xbuf, sem, h_scr, acc):
    i = pl.program_id(0)
    f = pl.program_id(1)
    slot = lax.rem(i, 2)
    used = i < nu_ref[0]

    @pl.when(f == 0)
    def _():
        start_next, _ = _row_gather(x_hbm, tok_next_ref, xbuf, sem, 1 - slot, tm)
        start_cur, wait_cur = _row_gather(x_hbm, tok_ref, xbuf, sem, slot, tm)
        pl.when(i == 0)(start_cur)
        pl.when(i + 1 < pl.num_programs(0))(start_next)
        wait_cur()
        h_scr[...] = (_rms(xbuf[slot], NORM_EPS) * g_ref[...]).astype(BF16)

    @pl.when(used)
    def _():
        hb = h_scr[...]
        u = _dot(hb, w1_ref[0], None)
        gate = _dot(hb, w3_ref[0], None)
        y = _dot((_silu(u) * gate).astype(BF16), w2_ref[0], None)

        @pl.when(f == 0)
        def _():
            acc[...] = y

        @pl.when(f > 0)
        def _():
            acc[...] += y

    @pl.when(f == pl.num_programs(1) - 1)
    def _():
        ys_ref[...] = jnp.where(used, acc[...], 0.0)


def _moe_ffn(x, gain, tok, tile_expert, n_used, w1, w3, w2, tm, tf):
    n_tiles = tok.shape[0]
    n_f = D_EXP // tf

    def w_block(i, f, te, nu):
        last = nu[0] - 1
        return te[jnp.minimum(i, last)], jnp.where(i <= last, f, n_f - 1)

    def w13_idx(i, f, te, nu):
        e, ff = w_block(i, f, te, nu)
        return (e, 0, ff)

    def w2_idx(i, f, te, nu):
        e, ff = w_block(i, f, te, nu)
        return (e, ff, 0)

    smem_tile = lambda idx: pl.BlockSpec((1, 1, tm), idx, memory_space=pltpu.SMEM)
    grid_spec = pltpu.PrefetchScalarGridSpec(
        num_scalar_prefetch=2,
        grid=(n_tiles, n_f),
        in_specs=[smem_tile(lambda i, f, te, nu: (i, 0, 0)),
                  smem_tile(lambda i, f, te, nu: (jnp.minimum(i + 1, n_tiles - 1), 0, 0)),
                  pl.BlockSpec(memory_space=pl.ANY),
                  pl.BlockSpec((1, D_MODEL), lambda i, f, te, nu: (0, 0)),
                  pl.BlockSpec((1, D_MODEL, tf), w13_idx),
                  pl.BlockSpec((1, D_MODEL, tf), w13_idx),
                  pl.BlockSpec((1, tf, D_MODEL), w2_idx)],
        out_specs=pl.BlockSpec((tm, D_MODEL), lambda i, f, te, nu: (i, 0)),
        scratch_shapes=[pltpu.VMEM((2, tm, D_MODEL), F32),
                        pltpu.SemaphoreType.DMA((2,)),
                        pltpu.VMEM((tm, D_MODEL), BF16),
                        pltpu.VMEM((tm, D_MODEL), F32)])
    return pl.pallas_call(
        functools.partial(_moe_ffn_body, tm),
        grid_spec=grid_spec,
        out_shape=jax.ShapeDtypeStruct((n_tiles * tm, D_MODEL), F32),
        compiler_params=pltpu.CompilerParams(
            dimension_semantics=("arbitrary", "arbitrary"), vmem_limit_bytes=VMEM_LIMIT),
        name="moe_ffn",
    )(tile_expert, n_used, tok, tok, x, gain, w1, w3, w2)


def _moe_combine_body(tm, final_norm, *refs):
    if final_norm:
        rows_ref, rows_next_ref, x_ref, route_ref, ys_hbm, gf_ref, o_ref, ybuf, sem = refs
    else:
        rows_ref, rows_next_ref, x_ref, route_ref, ys_hbm, o_ref, ybuf, sem = refs
    i = pl.program_id(0)
    slot = lax.rem(i, 2)
    start_next, _ = _row_gather(ys_hbm, rows_next_ref, ybuf, sem, 1 - slot, 2 * tm)
    start_cur, wait_cur = _row_gather(ys_hbm, rows_ref, ybuf, sem, slot, 2 * tm)
    pl.when(i == 0)(start_cur)
    pl.when(i + 1 < pl.num_programs(0))(start_next)
    wait_cur()
    route = route_ref[...]
    out = x_ref[...] + (route[:, 2:3] * ybuf[slot, 0:tm, :] + route[:, 3:4] * ybuf[slot, tm:2 * tm, :])
    if final_norm:
        out = _rms(out, NORM_EPS) * gf_ref[...]
    o_ref[...] = out


def _moe_combine(x, route, rows, ys, final_gain, tm):
    n = x.shape[0]
    n_tiles = n // tm
    final_norm = final_gain is not None
    smem_tile = lambda idx: pl.BlockSpec((1, 1, 2 * tm), idx, memory_space=pltpu.SMEM)
    in_specs = [smem_tile(lambda i: (i, 0, 0)),
                smem_tile(lambda i: (jnp.minimum(i + 1, n_tiles - 1), 0, 0)),
                pl.BlockSpec((tm, D_MODEL), lambda i: (i, 0)),
                pl.BlockSpec((tm, LANE), lambda i: (i, 0)),
                pl.BlockSpec(memory_space=pl.ANY)]
    args = [rows, rows, x, route, ys]
    if final_norm:
        in_specs.append(pl.BlockSpec((1, D_MODEL), lambda i: (0, 0)))
        args.append(final_gain)
    return pl.pallas_call(
        functools.partial(_moe_combine_body, tm, final_norm),
        grid=(n_tiles,),
        in_specs=in_specs,
        out_specs=pl.BlockSpec((tm, D_MODEL), lambda i: (i, 0)),
        out_shape=jax.ShapeDtypeStruct((n, D_MODEL), F32),
        scratch_shapes=[pltpu.VMEM((2, 2 * tm, D_MODEL), F32), pltpu.SemaphoreType.DMA((2,))],
        compiler_params=pltpu.CompilerParams(
            dimension_semantics=("arbitrary",), vmem_limit_bytes=VMEM_LIMIT),
        name="moe_combine",
    )(*args)


def _moe(x, gain, router, w1, w3, w2, final_gain):
    n = x.shape[0]
    tm = 512 if n >= 8192 else 256
    tm_c = 256
    n_tiles = -(-(2 * n + N_EXP * (tm - 1)) // tm)
    route = _route(x, gain, router)
    i1 = route[:, 0].astype(jnp.int32)
    i2 = route[:, 1].astype(jnp.int32)
    experts = jnp.arange(N_EXP, dtype=jnp.int32)
    hit = ((i1[:, None] == experts) | (i2[:, None] == experts)).astype(jnp.int32)
    rank = jnp.cumsum(hit, axis=0) - hit
    tiles_e = (jnp.sum(hit, axis=0) + tm - 1) // tm
    tile_end = jnp.cumsum(tiles_e)
    row_off = (tile_end - tiles_e) * tm
    row1 = row_off[i1] + jnp.take_along_axis(rank, i1[:, None], axis=1)[:, 0]
    row2 = row_off[i2] + jnp.take_along_axis(rank, i2[:, None], axis=1)[:, 0]
    token = jnp.arange(n, dtype=jnp.int32)
    tok = jnp.zeros((n_tiles * tm,), jnp.int32).at[row1].set(token).at[row2].set(token)
    tile_expert = jnp.minimum(
        jnp.searchsorted(tile_end, jnp.arange(n_tiles, dtype=jnp.int32), side="right"),
        N_EXP - 1).astype(jnp.int32)
    n_used = tile_end[N_EXP - 1:].astype(jnp.int32)
    ys = _moe_ffn(x, gain, tok.reshape(n_tiles, 1, tm), tile_expert, n_used, w1, w3, w2,
                  tm, D_EXP // 4)
    rows = jnp.concatenate([row1.reshape(n // tm_c, 1, tm_c), row2.reshape(n // tm_c, 1, tm_c)],
                           axis=2).astype(jnp.int32)
    return _moe_combine(x, route, rows, ys, final_gain, tm_c)


def _pad_last(a, width):
    return jnp.pad(a, [(0, 0)] * (a.ndim - 1) + [(0, width - a.shape[-1])])


def _pack_params(p):
    w_in = p["w_in"]
    w_in = jnp.concatenate([_pad_last(w_in[..., :DN_P], DN_SLAB),
                            _pad_last(w_in[..., DN_P:DN_P + RW_P], RW_SLAB),
                            w_in[..., DN_P + RW_P:]], axis=-1).astype(BF16)
    head_par = jnp.stack([_pad_last(p["dn_a_log"], LANE), _pad_last(p["dn_dt_bias"], LANE)], axis=1)
    g2 = jnp.pad(p["rw_g2"], ((0, 0), (0, RW_GATE_PAD - RW_GATE_R), (0, 0)))
    row = lambda a: a[:, None, :]
    return dict(
        ln_mix=row(p["ln_mix"]), w_in=w_in, dn_conv=p["dn_conv"], dn_head=head_par,
        dn_norm=row(p["dn_norm"]),
        rw=[(row(_pad_last(p["rw_mu"], RW_SLAB))[l], row(p["rw_w0"])[l], p["rw_w2"][l],
             row(p["rw_a0"])[l], p["rw_a2"][l], g2[l], row(p["rw_kk"])[l], row(p["rw_ka"])[l],
             p["rw_rk"].reshape(N_LAYERS, 1, RW_W)[l], row(p["rw_ln_g"])[l], row(p["rw_ln_b"])[l])
            for l in range(N_LAYERS)],
        rt_ln=row(p["rt_ln"]), w_out=p["w_out"].astype(BF16), ln_ffn=row(p["ln_ffn"]),
        ffn_w1=p["ffn_w1"].astype(BF16), ffn_w3=p["ffn_w3"].astype(BF16),
        ffn_w2=p["ffn_w2"].astype(BF16),
        moe_router=_pad_last(p["moe_router"], LANE),
        moe_w1=p["moe_w1"].astype(BF16), moe_w3=p["moe_w3"].astype(BF16),
        moe_w2=p["moe_w2"].astype(BF16),
        ln_final=p["ln_final"][None, :],
    )


def _trunk(x, n_seq, t_real, t_pad, c_len, n_par, states, pos0, w):
    n_valid = min(c_len, t_real)
    cos, sin = _rotary_tables(pos0, t_pad)
    convs, dns, shifts, rws, rts = [], [], [], [], []
    seq = lambda a: a.reshape(n_seq, t_pad, a.shape[-1])
    flat = lambda a: a.reshape(n_seq * t_pad, a.shape[-1])
    for l in range(N_LAYERS):
        z_dn, z_rw, z_rt = (seq(z) for z in _inproj(x, w["ln_mix"][l], w["w_in"][l]))
        if states is None:
            conv0 = s_dn0 = shift0 = s_rw0 = s_rt0 = None
        else:
            conv0, s_dn0 = states[0][l], states[1][l]
            shift0 = _pad_last(states[2][l], RW_SLAB)[:, None, :]
            s_rw0, s_rt0 = states[3][l], states[4][l]
        o_dn, s_dn = _dn_group(z_dn, conv0, s_dn0, w["dn_conv"][l], w["dn_head"][l],
                               w["dn_norm"][l], c_len, n_valid, n_par)
        o_rw, s_rw = _rw_group(z_rw, shift0, s_rw0, w["rw"][l], c_len, n_valid, n_par)
        o_rt, s_rt = _rt_group(z_rt, cos, sin, s_rt0, w["rt_ln"][l], c_len, n_valid, n_par)
        x = _outproj(x, flat(o_dn), flat(o_rw), flat(o_rt), w["w_out"][l])
        if l % 2 == 0:
            x = _ffn(x, w["ln_ffn"][l], w["ffn_w1"][l // 2], w["ffn_w3"][l // 2], w["ffn_w2"][l // 2],
                     w["ln_final"] if l == N_LAYERS - 1 else None, D_FF // 2)
        else:
            x = _moe(x, w["ln_ffn"][l], w["moe_router"][l // 2], w["moe_w1"][l // 2],
                     w["moe_w3"][l // 2], w["moe_w2"][l // 2],
                     w["ln_final"] if l == N_LAYERS - 1 else None)
        if t_real >= CONV_W - 1:
            convs.append(z_dn[:, t_real - (CONV_W - 1):t_real, :DN_CONV])
        else:
            convs.append(jnp.concatenate([conv0[:, t_real:], z_dn[:, :t_real, :DN_CONV]], axis=1))
        shifts.append(z_rw[:, t_real - 1, :RW_P])
        dns.append(s_dn)
        rws.append(s_rw)
        rts.append(s_rt)
    return x, jnp.stack(convs), jnp.stack(dns), jnp.stack(shifts), jnp.stack(rws), jnp.stack(rts)


def kernel(x_prompt, x_sample, state_dn_conv, state_dn, state_rw_shift, state_rw, state_rt, ln_mix, w_in, dn_conv, dn_a_log, dn_dt_bias, dn_norm, rw_mu, rw_w0, rw_w2, rw_a0, rw_a2, rw_g2, rw_kk, rw_ka, rw_rk, rw_ln_g, rw_ln_b, rt_ln, w_out, ln_ffn, ffn_w1, ffn_w3, ffn_w2, moe_router, moe_w1, moe_w3, moe_w2, ln_final):
    w = _pack_params(dict(
        ln_mix=ln_mix, w_in=w_in, dn_conv=dn_conv, dn_a_log=dn_a_log, dn_dt_bias=dn_dt_bias,
        dn_norm=dn_norm, rw_mu=rw_mu, rw_w0=rw_w0, rw_w2=rw_w2, rw_a0=rw_a0, rw_a2=rw_a2,
        rw_g2=rw_g2, rw_kk=rw_kk, rw_ka=rw_ka, rw_rk=rw_rk, rw_ln_g=rw_ln_g, rw_ln_b=rw_ln_b,
        rt_ln=rt_ln, w_out=w_out, ln_ffn=ln_ffn, ffn_w1=ffn_w1, ffn_w3=ffn_w3, ffn_w2=ffn_w2,
        moe_router=moe_router, moe_w1=moe_w1, moe_w3=moe_w3, moe_w2=moe_w2, ln_final=ln_final))
    bp, tp, _ = x_prompt.shape
    bs, ts, _ = x_sample.shape

    yp, *p_states = _trunk(x_prompt.reshape(bp * tp, D_MODEL), bp, tp, tp, 64, 2, None, 0, w)
    y_prompt = yp.reshape(bp, tp, D_MODEL)

    ts_pad = 8
    xs = jnp.pad(x_sample, ((0, 0), (0, ts_pad - ts), (0, 0))).reshape(bs * ts_pad, D_MODEL)
    ys, *s_states = _trunk(xs, bs, ts, ts_pad, ts_pad, 4,
                           (state_dn_conv, state_dn, state_rw_shift, state_rw, state_rt),
                           PAST_LEN, w)
    y_sample = ys.reshape(bs, ts_pad, D_MODEL)[:, :ts]
    return (y_prompt, y_sample, *p_states, *s_states)
```

```python
import functools
import math

import numpy as np
import jax
import jax.numpy as jnp
from jax import lax
from jax.experimental import pallas as pl
from jax.experimental.pallas import tpu as pltpu

F32 = jnp.float32
BF16 = jnp.bfloat16
HI = lax.Precision.HIGHEST

D_MODEL = 1024
N_LAYERS = 2
PAST_LEN = 16384
HEAD = 64
DN_HEADS = 6
DN_QK = DN_HEADS * HEAD
DN_CONV = 3 * DN_QK
CONV_W = 4
RW_HEADS = 6
RW_W = RW_HEADS * HEAD
RW_GATE_R = 160
RT_HEADS = 4
RT_DK = 32
RT_QK = RT_HEADS * RT_DK
RT_W = RT_HEADS * HEAD
DN_P = DN_CONV + DN_QK + 2 * DN_HEADS
RW_P = 3 * RW_W + 64 + 64 + RW_GATE_R
RT_P = 2 * RT_QK + 2 * RT_W
LANE = 128
DN_SLAB = 1664
RW_SLAB = 1536
RT_SLAB = RT_P
P_SLABS = DN_SLAB + RW_SLAB + RT_SLAB
RW_GATE_PAD = RW_SLAB - (3 * RW_W + 128)
N_EXP = 8
D_FF = 2816
D_EXP = 3584
ROPE_BASE = 10000.0
NORM_EPS = 1e-6
RW_GN_EPS = 64e-5
RT_GN_EPS = 1e-5
TAIL = 8
VMEM_LIMIT = 56 * 1024 * 1024


def _dot(a, b, precision=HI):
    return jnp.dot(a, b, preferred_element_type=F32, precision=precision)


def _split3(x):
    x1 = x.astype(BF16)
    r1 = x - x1.astype(F32)
    x2 = r1.astype(BF16)
    x3 = (r1 - x2.astype(F32)).astype(BF16)
    return x1, x2, x3


def _select_sum(sel, x):
    sel = sel.astype(BF16)
    return sum(jnp.dot(sel, p, preferred_element_type=F32) for p in _split3(x))


def _select_sum_nt(sel, x):
    sel = sel.astype(BF16)
    return sum(lax.dot_general(sel, p, (((1,), (1,)), ((), ())), preferred_element_type=F32)
               for p in _split3(x))


def _mm(a, b):
    return jnp.dot(a.astype(BF16), b.astype(BF16), preferred_element_type=F32)


def _mm_nt(a, b):
    return lax.dot_general(a.astype(BF16), b.astype(BF16), (((1,), (1,)), ((), ())),
                           preferred_element_type=F32)


def _mm_tn(a, b):
    return lax.dot_general(a.astype(BF16), b.astype(BF16), (((0,), (0,)), ((), ())),
                           preferred_element_type=F32)


def _rows(*parts):
    return jnp.concatenate(parts, axis=0)


def _cols(*parts):
    return jnp.concatenate(parts, axis=1)


def _silu(x):
    return x * jax.nn.sigmoid(x)


def _softplus(x):
    return jnp.maximum(x, 0.0) + jnp.log1p(jnp.exp(-jnp.abs(x)))


def _rms(x, eps):
    return x * lax.rsqrt(jnp.mean(x * x, axis=-1, keepdims=True) + eps)


def _tri(c):
    r = lax.broadcasted_iota(jnp.int32, (c, c), 0)
    col = lax.broadcasted_iota(jnp.int32, (c, c), 1)
    return r >= col, r > col, r == col


def _dot_nt(a, b, precision=HI):
    return lax.dot_general(a, b, (((1,), (1,)), ((), ())),
                           preferred_element_type=F32, precision=precision)


def _group_ones():
    r = lax.broadcasted_iota(jnp.int32, (LANE, LANE), 0)
    c = lax.broadcasted_iota(jnp.int32, (LANE, LANE), 1)
    return ((r < HEAD) == (c < HEAD)).astype(BF16)


def _group_sum(x, ones_blk):
    hi = x.astype(BF16)
    lo = (x - hi.astype(F32)).astype(BF16)
    out = [jnp.dot(hi[:, j:j + LANE], ones_blk, preferred_element_type=F32)
           + jnp.dot(lo[:, j:j + LANE], ones_blk, preferred_element_type=F32)
           for j in range(0, x.shape[1], LANE)]
    return out[0] if len(out) == 1 else jnp.concatenate(out, axis=1)


def _neumann_levels(n_rows):
    return max(0, math.ceil(math.log2(n_rows)) - 1)


def _inproj_body(x_ref, g_ref, w_ref, zdn_ref, zrw_ref, zrt_ref):
    h = (_rms(x_ref[...], NORM_EPS) * g_ref[...]).astype(BF16)
    zdn_ref[...] = _dot(h, w_ref[:, 0:DN_SLAB], None)
    zrw_ref[...] = _dot(h, w_ref[:, DN_SLAB:DN_SLAB + RW_SLAB], None)
    zrt_ref[...] = _dot(h, w_ref[:, DN_SLAB + RW_SLAB:P_SLABS], None)


def _inproj(x, gain, w):
    n = x.shape[0]
    tm = min(256, n)
    return pl.pallas_call(
        _inproj_body,
        grid=(n // tm,),
        in_specs=[pl.BlockSpec((tm, D_MODEL), lambda i: (i, 0)),
                  pl.BlockSpec((1, D_MODEL), lambda i: (0, 0)),
                  pl.BlockSpec((D_MODEL, P_SLABS), lambda i: (0, 0))],
        out_specs=[pl.BlockSpec((tm, DN_SLAB), lambda i: (i, 0)),
                   pl.BlockSpec((tm, RW_SLAB), lambda i: (i, 0)),
                   pl.BlockSpec((tm, RT_SLAB), lambda i: (i, 0))],
        out_shape=[jax.ShapeDtypeStruct((n, DN_SLAB), F32),
                   jax.ShapeDtypeStruct((n, RW_SLAB), F32),
                   jax.ShapeDtypeStruct((n, RT_SLAB), F32)],
        compiler_params=pltpu.CompilerParams(
            dimension_semantics=("arbitrary",), vmem_limit_bytes=VMEM_LIMIT),
        name="inproj",
    )(x, gain, w)


def _seq_specs(n_par, c_len, width):
    return pl.BlockSpec((n_par, c_len, width), lambda b, c: (b, c, 0))


def _state_spec(n_par, shape):
    return pl.BlockSpec((n_par,) + shape, lambda b, c: (b,) + (0,) * len(shape))


def _const_spec(a):
    return pl.BlockSpec(a.shape, lambda b, c: (0,) * a.ndim)


def _dn_body(c_len, n_valid, fresh, n_par, *refs):
    if fresh:
        z_ref, cw_ref, hp_ref, nw_ref, o_ref, so_ref, xe, s_scr, o_raw = refs
    else:
        z_ref, c0_ref, s0_ref, cw_ref, hp_ref, nw_ref, o_ref, so_ref, xe, s_scr, o_raw = refs
    ci = pl.program_id(1)

    @pl.when(ci == 0)
    def _():
        xe[:, 0:TAIL, :] = jnp.zeros((n_par, TAIL, DN_CONV), F32)
        if fresh:
            s_scr[...] = jnp.zeros_like(s_scr)
        else:
            xe[:, TAIL - (CONV_W - 1):TAIL, :] = c0_ref[...]
            s_scr[...] = s0_ref[...]

    incl, strict, eye = _tri(c_len)
    eye_f = eye.astype(F32)
    tri_f = incl.astype(F32)
    live = lax.broadcasted_iota(jnp.int32, (c_len, LANE), 0) < n_valid
    neg_a = -jnp.exp(hp_ref[0:1, :])
    dt_bias = hp_ref[1:2, :]

    ones_blk = _group_ones()
    eye_h = _tri(HEAD)[2].astype(F32)
    eye_l = (lax.broadcasted_iota(jnp.int32, (8, LANE), 0)
             == lax.broadcasted_iota(jnp.int32, (8, LANE), 1)).astype(F32)

    seqs = []
    for i in range(n_par):
        x = z_ref[i, :, 0:DN_CONV]
        xe[i, TAIL:TAIL + c_len, :] = x
        acc = x * cw_ref[CONV_W - 1:CONV_W, :]
        for j in range(CONV_W - 1):
            off = TAIL - (CONV_W - 1) + j
            acc = acc + xe[i, off:off + c_len, :] * cw_ref[j:j + 1, :]
        xe[i, 0:TAIL, :] = xe[i, c_len:c_len + TAIL, :]
        ab = z_ref[i, :, DN_CONV + DN_QK:DN_SLAB]
        g_tok = neg_a * _softplus(ab + dt_bias)
        beta = jax.nn.sigmoid(ab)
        if n_valid < c_len:
            g_tok = jnp.where(live, g_tok, 0.0)
            beta = jnp.where(live, beta, 0.0)
        seqs.append(dict(conv=_silu(acc), beta=beta, g_tok=g_tok))
    for sq in seqs:
        sq["g_cum"] = _select_sum(tri_f, sq["g_tok"])
        qk = sq["conv"][:, 0:2 * DN_QK]
        sq["qk"] = qk * lax.rsqrt(_group_sum(qk * qk, ones_blk) + 1e-6)
    for sq in seqs:
        sq["g_cum_t"] = _select_sum_nt(eye_l, sq["g_cum"])

    chains = []
    for i, sq in enumerate(seqs):
        for h in range(DN_HEADS):
            lo = HEAD * h
            q = sq["qk"][:, lo:lo + HEAD] * (HEAD ** -0.5)
            k = sq["qk"][:, DN_QK + lo:DN_QK + lo + HEAD]
            v = sq["conv"][:, 2 * DN_QK + lo:2 * DN_QK + lo + HEAD]
            gc = sq["g_cum"][:, h:h + 1]
            gc_row = sq["g_cum_t"][h:h + 1, :]
            b = sq["beta"][:, DN_HEADS + h:DN_HEADS + h + 1]
            dmask = jnp.where(incl, jnp.exp(jnp.minimum(gc - gc_row, 0.0)), 0.0)
            g_last = gc[c_len - 1:c_len, :]
            chains.append(dict(i=i, h=h, q=q, k=k, v=v, b=b, dmask=dmask, kb=k * b, egc=jnp.exp(gc),
                               k_decay=jnp.exp(g_last - gc_row), s_decay=jnp.exp(g_last)))
    for c in chains:
        c["kq"] = _mm_nt(_rows(c["kb"], c["q"], eye_h), c["k"])
    for c in chains:
        c["np"] = -jnp.where(strict, c["kq"][0:c_len] * c["dmask"], 0.0)
        c["t"] = eye_f + c["np"]
    for _ in range(_neumann_levels(n_valid)):
        for c in chains:
            c["np"] = _mm(c["np"], c["np"])
        for c in chains:
            c["t"] = c["t"] + _mm(c["t"], c["np"])
    for c in chains:
        c["tx"] = _mm(c["t"], _cols(c["v"] * c["b"], c["kb"] * c["egc"]))
    for c in chains:
        qk = c["kq"][c_len:2 * c_len] * c["dmask"]
        kd_t = c["kq"][2 * c_len:2 * c_len + HEAD] * c["k_decay"]
        c["w"] = _mm(_rows(qk, kd_t), c["tx"])
    for c in chains:
        w = c["w"]
        c["s"] = s_scr[c["i"], c["h"]]
        c["res"] = _mm(_rows(c["q"] * c["egc"] - w[0:c_len, HEAD:2 * HEAD],
                             w[c_len:c_len + HEAD, HEAD:2 * HEAD]), c["s"])
    for c in chains:
        w, res = c["w"], c["res"]
        lo = HEAD * c["h"]
        s_scr[c["i"], c["h"]] = (c["s"] * c["s_decay"] - res[c_len:c_len + HEAD]
                                 + w[c_len:c_len + HEAD, 0:HEAD])
        o_raw[c["i"], :, lo:lo + HEAD] = res[0:c_len] + w[0:c_len, 0:HEAD]
    for i in range(n_par):
        o = o_raw[i]
        ms = _group_sum(o * o, ones_blk) * (1.0 / HEAD)
        zg = z_ref[i, :, DN_CONV:DN_CONV + DN_QK]
        o_ref[i] = o * lax.rsqrt(ms + NORM_EPS) * nw_ref[...] * _silu(zg)

    @pl.when(ci == pl.num_programs(1) - 1)
    def _():
        so_ref[...] = s_scr[...]


def _dn_group(z, conv0, s0, conv_w, head_par, norm_w, c_len, n_valid, n_par):
    n_seq, t_pad, _ = z.shape
    fresh = s0 is None
    nc = t_pad // c_len
    assert n_valid == c_len or nc == 1
    state = (DN_HEADS, HEAD, HEAD)
    in_specs = [_seq_specs(n_par, c_len, DN_SLAB)]
    args = [z]
    if not fresh:
        in_specs += [_state_spec(n_par, (CONV_W - 1, DN_CONV)), _state_spec(n_par, state)]
        args += [conv0, s0]
    for p in (conv_w, head_par, norm_w):
        in_specs.append(_const_spec(p))
        args.append(p)
    return pl.pallas_call(
        functools.partial(_dn_body, c_len, n_valid, fresh, n_par),
        grid=(n_seq // n_par, nc),
        in_specs=in_specs,
        out_specs=[_seq_specs(n_par, c_len, DN_QK), _state_spec(n_par, state)],
        out_shape=[jax.ShapeDtypeStruct((n_seq, t_pad, DN_QK), F32),
                   jax.ShapeDtypeStruct((n_seq,) + state, F32)],
        scratch_shapes=[pltpu.VMEM((n_par, c_len + TAIL, DN_CONV), F32),
                        pltpu.VMEM((n_par,) + state, F32),
                        pltpu.VMEM((n_par, c_len, DN_QK), F32)],
        compiler_params=pltpu.CompilerParams(
            dimension_semantics=("arbitrary", "arbitrary"), vmem_limit_bytes=VMEM_LIMIT),
        name="dn_group",
    )(*args)


def _rw_body(c_len, n_valid, fresh, n_par, *refs):
    if fresh:
        (z_ref, mu_ref, w0_ref, w2_ref, a0_ref, a2_ref, g2_ref, kk_ref, ka_ref, rk_ref,
         lng_ref, lnb_ref, o_ref, so_ref, xe, s_scr, y_raw) = refs
    else:
        (z_ref, sh0_ref, s0_ref, mu_ref, w0_ref, w2_ref, a0_ref, a2_ref, g2_ref, kk_ref, ka_ref,
         rk_ref, lng_ref, lnb_ref, o_ref, so_ref, xe, s_scr, y_raw) = refs
    ci = pl.program_id(1)

    @pl.when(ci == 0)
    def _():
        xe[:, 0:TAIL, :] = jnp.zeros((n_par, TAIL, RW_SLAB), F32)
        if fresh:
            s_scr[...] = jnp.zeros_like(s_scr)
        else:
            xe[:, TAIL - 1:TAIL, :] = sh0_ref[...]
            s_scr[...] = s0_ref[...]

    incl, strict, eye = _tri(c_len)
    eye_f = eye.astype(F32)
    tri_f = incl.astype(F32)
    live = lax.broadcasted_iota(jnp.int32, (c_len, RW_W), 0) < n_valid
    ones_blk = _group_ones()
    eye_h = _tri(HEAD)[2].astype(F32)

    seqs = []
    for i in range(n_par):
        x = z_ref[i]
        xe[i, TAIL:TAIL + c_len, :] = x
        prev = xe[i, TAIL - 1:TAIL - 1 + c_len, :]
        zs = x + (prev - x) * mu_ref[...]
        xe[i, 0:TAIL, :] = xe[i, c_len:c_len + TAIL, :]
        seqs.append(dict(r=zs[:, 0:RW_W], k=zs[:, RW_W:2 * RW_W], v=zs[:, 2 * RW_W:3 * RW_W],
                         wl=zs[:, 3 * RW_W:3 * RW_W + 64], al=zs[:, 3 * RW_W + 64:3 * RW_W + 128],
                         gl=zs[:, 3 * RW_W + 128:RW_SLAB]))
    for sq in seqs:
        sq["w_lora"] = _mm(jnp.tanh(sq["wl"]), w2_ref[...])
        sq["a_lora"] = _mm(sq["al"], a2_ref[...])
        sq["g"] = _mm(jax.nn.sigmoid(sq["gl"]), g2_ref[...])
    for sq in seqs:
        w_log = -_softplus(-(w0_ref[...] + sq["w_lora"])) - 0.5
        a = jax.nn.sigmoid(a0_ref[...] + sq["a_lora"])
        kk_in = sq["k"] * kk_ref[...]
        k2 = sq["k"] * (1.0 + (a - 1.0) * ka_ref[...])
        lw = -jnp.exp(w_log)
        if n_valid < c_len:
            lw = jnp.where(live, lw, 0.0)
            k2 = jnp.where(live, k2, 0.0)
            kk_in = jnp.where(live, kk_in, 0.0)
        sq.update(a=a, kk_in=kk_in, k2=k2, lw=lw)
    for sq in seqs:
        sq["gcum"] = _select_sum(tri_f, sq["lw"])
        kk_in = sq["kk_in"]
        sq["kk"] = kk_in * lax.rsqrt(_group_sum(kk_in * kk_in, ones_blk) + 1e-6)
    for sq in seqs:
        gcum = sq["gcum"]
        e_neg = jnp.exp(-gcum)
        g_end = gcum[c_len - 1:c_len, :]
        e_rest = jnp.exp(g_end - gcum)
        bv = sq["kk"] * sq["a"]
        sq.update(at=-sq["kk"] * jnp.exp(gcum - sq["lw"]), bt=bv * e_neg, kt=sq["k2"] * e_neg,
                  rt=sq["r"] * jnp.exp(gcum), b_rest=bv * e_rest, k_rest=sq["k2"] * e_rest,
                  e_end=jnp.exp(g_end))

    chains = []
    for i, sq in enumerate(seqs):
        for h in range(RW_HEADS):
            sl = slice(HEAD * h, HEAD * (h + 1))
            chains.append(dict(i=i, h=h, sl=sl, at=sq["at"][:, sl], bt=sq["bt"][:, sl],
                               kt=sq["kt"][:, sl], rt=sq["rt"][:, sl], vh=sq["v"][:, sl],
                               e_end=sq["e_end"][:, sl],
                               rest=_rows(sq["b_rest"][:, sl], sq["k_rest"][:, sl])))
    for c in chains:
        c["pair"] = _mm_nt(_rows(c["at"], c["rt"], eye_h), _rows(c["bt"], c["kt"], c["at"], c["vh"]))
    for c in chains:
        pair = c["pair"]
        c["np"] = jnp.where(strict, pair[0:c_len, 0:c_len], 0.0)
        c["t"] = eye_f + c["np"]
        c["nak"] = jnp.where(strict, pair[0:c_len, c_len:2 * c_len], 0.0)
        c["mix"] = _cols(jnp.where(incl, pair[c_len:2 * c_len, 0:c_len], 0.0),
                         jnp.where(incl, pair[c_len:2 * c_len, c_len:2 * c_len], 0.0))
        c["at_t"] = pair[2 * c_len:2 * c_len + HEAD, 2 * c_len:3 * c_len]
        c["v_t"] = pair[2 * c_len:2 * c_len + HEAD, 3 * c_len:4 * c_len]
    for c in chains:
        c["nakv"] = _mm(c["nak"], c["vh"])
        c["nakv_t"] = _mm_nt(c["v_t"], c["nak"])
    for _ in range(_neumann_levels(n_valid)):
        for c in chains:
            c["np"] = _mm(c["np"], c["np"])
        for c in chains:
            c["t"] = c["t"] + _mm(c["t"], c["np"])
    for c in chains:
        c["tx"] = _mm(c["t"], _cols(c["at"], c["nakv"]))
        c["tx_t"] = _mm_nt(_rows(c["at_t"], c["nakv_t"]), c["t"])
    for c in chains:
        zero = jnp.zeros((c_len, HEAD), F32)
        c["zy"] = _mm(c["mix"], _rows(c["tx"], _cols(zero, c["vh"])))
        ta_t = c["tx_t"][0:HEAD]
        c["m"] = _mm(_rows(_cols(ta_t, jnp.zeros_like(ta_t)), _cols(c["tx_t"][HEAD:2 * HEAD], c["v_t"])),
                     c["rest"])
    for c in chains:
        c["s"] = s_scr[c["i"], c["h"]]
        c["y"] = _mm_nt(c["rt"] + c["zy"][:, 0:HEAD], c["s"])
        c["sm"] = _mm(c["s"], c["m"][0:HEAD])
    for c in chains:
        s_scr[c["i"], c["h"]] = c["s"] * c["e_end"] + c["sm"] + c["m"][HEAD:2 * HEAD]
        y_raw[c["i"], :, c["sl"]] = c["y"] + c["zy"][:, HEAD:2 * HEAD]
    for i, sq in enumerate(seqs):
        y = y_raw[i]
        d = y - _group_sum(y, ones_blk) * (1.0 / HEAD)
        yn = d * lax.rsqrt(_group_sum(d * d, ones_blk) * (1.0 / HEAD) + RW_GN_EPS)
        yn = yn * lng_ref[...] + lnb_ref[...]
        bonus = _group_sum(sq["r"] * sq["k2"] * rk_ref[...], ones_blk) * sq["v"]
        o_ref[i] = (yn + bonus) * sq["g"]

    @pl.when(ci == pl.num_programs(1) - 1)
    def _():
        so_ref[...] = s_scr[...]


def _rw_group(z, shift0, s0, par, c_len, n_valid, n_par):
    n_seq, t_pad, _ = z.shape
    fresh = s0 is None
    nc = t_pad // c_len
    assert n_valid == c_len or nc == 1
    state = (RW_HEADS, HEAD, HEAD)
    in_specs = [_seq_specs(n_par, c_len, RW_SLAB)]
    args = [z]
    if not fresh:
        in_specs += [_state_spec(n_par, (1, RW_SLAB)), _state_spec(n_par, state)]
        args += [shift0, s0]
    for p in par:
        in_specs.append(_const_spec(p))
        args.append(p)
    return pl.pallas_call(
        functools.partial(_rw_body, c_len, n_valid, fresh, n_par),
        grid=(n_seq // n_par, nc),
        in_specs=in_specs,
        out_specs=[_seq_specs(n_par, c_len, RW_W), _state_spec(n_par, state)],
        out_shape=[jax.ShapeDtypeStruct((n_seq, t_pad, RW_W), F32),
                   jax.ShapeDtypeStruct((n_seq,) + state, F32)],
        scratch_shapes=[pltpu.VMEM((n_par, c_len + TAIL, RW_SLAB), F32),
                        pltpu.VMEM((n_par,) + state, F32),
                        pltpu.VMEM((n_par, c_len, RW_W), F32)],
        compiler_params=pltpu.CompilerParams(
            dimension_semantics=("arbitrary", "arbitrary"), vmem_limit_bytes=VMEM_LIMIT),
        name="rw_group",
    )(*args)


def _rt_body(c_len, n_valid, fresh, n_par, *refs):
    if fresh:
        z_ref, cos_ref, sin_ref, lng_ref, o_ref, so_ref, s_scr, o_raw = refs
    else:
        z_ref, cos_ref, sin_ref, s0_ref, lng_ref, o_ref, so_ref, s_scr, o_raw = refs
    ci = pl.program_id(1)

    @pl.when(ci == 0)
    def _():
        if fresh:
            s_scr[...] = jnp.zeros_like(s_scr)
        else:
            s_scr[...] = s0_ref[...]

    lane = lax.broadcasted_iota(jnp.int32, (c_len, RT_QK), 1)
    first_half = (lane & (RT_DK - 1)) < (RT_DK // 2)
    cos = cos_ref[...]
    sin = sin_ref[...]

    def rotary(t):
        partner = jnp.where(first_half,
                            pltpu.roll(t, RT_QK - RT_DK // 2, axis=1),
                            pltpu.roll(t, RT_DK // 2, axis=1))
        return t * cos + partner * sin

    incl, _, _ = _tri(c_len)
    ri = lax.broadcasted_iota(jnp.int32, (c_len, c_len), 0)
    cj = lax.broadcasted_iota(jnp.int32, (c_len, c_len), 1)
    delta = (ri - cj).astype(F32)
    idx = lax.broadcasted_iota(jnp.int32, (c_len, 1), 0).astype(F32)
    log_decay = [math.log1p(-(2.0 ** (-5.0 - h))) for h in range(RT_HEADS)]
    dmasks = [jnp.where(incl, jnp.exp(delta * lg), 0.0) for lg in log_decay]
    q_scale = [jnp.exp((idx + 1.0) * lg) for lg in log_decay]
    t_row = lax.broadcasted_iota(jnp.int32, (1, c_len), 1).astype(F32)
    k_scale = [jnp.where(t_row < n_valid, jnp.exp((n_valid - 1.0 - t_row) * lg), 0.0) for lg in log_decay]
    ones_blk = _group_ones()
    eye_k = _tri(RT_DK)[2].astype(F32)

    chains = []
    for i in range(n_par):
        q = rotary(z_ref[i, :, 0:RT_QK])
        k = rotary(z_ref[i, :, RT_QK:2 * RT_QK]) * (RT_DK ** -0.5)
        for h in range(RT_HEADS):
            chains.append(dict(
                i=i, h=h, qh=q[:, RT_DK * h:RT_DK * (h + 1)], kh=k[:, RT_DK * h:RT_DK * (h + 1)],
                vh=z_ref[i, :, 2 * RT_QK + HEAD * h:2 * RT_QK + HEAD * (h + 1)]))
    for c in chains:
        c["qk"] = _mm_nt(_rows(c["qh"], eye_k), c["kh"])
    for c in chains:
        h = c["h"]
        c["s"] = s_scr[c["i"], h]
        c["cross"] = _mm(c["qh"] * q_scale[h], c["s"])
        c["kv"] = _mm(c["qk"][c_len:c_len + RT_DK] * k_scale[h], c["vh"])
        c["inner"] = _mm(c["qk"][0:c_len] * dmasks[h], c["vh"])
    for c in chains:
        i, h = c["i"], c["h"]
        s_scr[i, h] = c["s"] * math.exp(n_valid * log_decay[h]) + c["kv"]
        o_raw[i, :, HEAD * h:HEAD * (h + 1)] = c["inner"] + c["cross"]
    for i in range(n_par):
        o = o_raw[i]
        d = o - _group_sum(o, ones_blk) * (1.0 / HEAD)
        on = d * lax.rsqrt(_group_sum(d * d, ones_blk) * (1.0 / HEAD) + RT_GN_EPS)
        o_ref[i] = on * lng_ref[...] * _silu(z_ref[i, :, 2 * RT_QK + RT_W:RT_SLAB])

    @pl.when(ci == pl.num_programs(1) - 1)
    def _():
        so_ref[...] = s_scr[...]


def _rt_group(z, cos, sin, s0, ln_g, c_len, n_valid, n_par):
    n_seq, t_pad, _ = z.shape
    fresh = s0 is None
    nc = t_pad // c_len
    assert n_valid == c_len or nc == 1
    state = (RT_HEADS, RT_DK, HEAD)
    table = pl.BlockSpec((c_len, RT_QK), lambda b, c: (c, 0))
    in_specs = [_seq_specs(n_par, c_len, RT_SLAB), table, table]
    args = [z, cos, sin]
    if not fresh:
        in_specs.append(_state_spec(n_par, state))
        args.append(s0)
    in_specs.append(_const_spec(ln_g))
    args.append(ln_g)
    return pl.pallas_call(
        functools.partial(_rt_body, c_len, n_valid, fresh, n_par),
        grid=(n_seq // n_par, nc),
        in_specs=in_specs,
        out_specs=[_seq_specs(n_par, c_len, RT_W), _state_spec(n_par, state)],
        out_shape=[jax.ShapeDtypeStruct((n_seq, t_pad, RT_W), F32),
                   jax.ShapeDtypeStruct((n_seq,) + state, F32)],
        scratch_shapes=[pltpu.VMEM((n_par,) + state, F32),
                        pltpu.VMEM((n_par, c_len, RT_W), F32)],
        compiler_params=pltpu.CompilerParams(
            dimension_semantics=("arbitrary", "arbitrary"), vmem_limit_bytes=VMEM_LIMIT),
        name="rt_group",
    )(*args)


def _rotary_tables(pos0, t_pad):
    half = RT_DK // 2
    inv = ROPE_BASE ** (-np.arange(half, dtype=np.float64) / half)
    ang = (pos0 + np.arange(t_pad, dtype=np.float64))[:, None] * inv[None, :]
    cos = np.tile(np.concatenate([np.cos(ang), np.cos(ang)], axis=1), (1, RT_HEADS))
    sin = np.tile(np.concatenate([-np.sin(ang), np.sin(ang)], axis=1), (1, RT_HEADS))
    return jnp.asarray(cos, F32), jnp.asarray(sin, F32)


def _outproj_body(x_ref, odn_ref, orw_ref, ort_ref, w_ref, o_ref):
    acc = x_ref[...]
    acc = acc + _dot(odn_ref[...].astype(BF16), w_ref[0:DN_QK, :], None)
    acc = acc + _dot(orw_ref[...].astype(BF16), w_ref[DN_QK:DN_QK + RW_W, :], None)
    acc = acc + _dot(ort_ref[...].astype(BF16), w_ref[DN_QK + RW_W:D_MODEL, :], None)
    o_ref[...] = acc


def _outproj(x, o_dn, o_rw, o_rt, w):
    n = x.shape[0]
    tm = min(512, n)
    row = lambda i: (i, 0)
    return pl.pallas_call(
        _outproj_body,
        grid=(n // tm,),
        in_specs=[pl.BlockSpec((tm, D_MODEL), row),
                  pl.BlockSpec((tm, DN_QK), row),
                  pl.BlockSpec((tm, RW_W), row),
                  pl.BlockSpec((tm, RT_W), row),
                  pl.BlockSpec((D_MODEL, D_MODEL), lambda i: (0, 0))],
        out_specs=pl.BlockSpec((tm, D_MODEL), row),
        out_shape=jax.ShapeDtypeStruct((n, D_MODEL), F32),
        compiler_params=pltpu.CompilerParams(
            dimension_semantics=("arbitrary",), vmem_limit_bytes=VMEM_LIMIT),
        name="outproj",
    )(x, o_dn, o_rw, o_rt, w)


def _ffn_body(final_norm, *refs):
    if final_norm:
        x_ref, g_ref, w1_ref, w3_ref, w2_ref, gf_ref, o_ref, h_scr, acc = refs
    else:
        x_ref, g_ref, w1_ref, w3_ref, w2_ref, o_ref, h_scr, acc = refs
    f = pl.program_id(1)

    @pl.when(f == 0)
    def _():
        x = x_ref[...]
        h_scr[...] = (_rms(x, NORM_EPS) * g_ref[...]).astype(BF16)
        acc[...] = x

    hb = h_scr[...]
    u = _dot(hb, w1_ref[...], None)
    gate = _dot(hb, w3_ref[...], None)
    acc[...] += _dot((_silu(u) * gate).astype(BF16), w2_ref[...], None)

    @pl.when(f == pl.num_programs(1) - 1)
    def _():
        if final_norm:
            o_ref[...] = _rms(acc[...], NORM_EPS) * gf_ref[...]
        else:
            o_ref[...] = acc[...]


def _ffn(x, gain, w1, w3, w2, final_gain, tf):
    n = x.shape[0]
    d_f = w1.shape[1]
    final_norm = final_gain is not None
    tm = min(1024, n)
    in_specs = [pl.BlockSpec((tm, D_MODEL), lambda i, f: (i, 0)),
                pl.BlockSpec((1, D_MODEL), lambda i, f: (0, 0)),
                pl.BlockSpec((D_MODEL, tf), lambda i, f: (0, f)),
                pl.BlockSpec((D_MODEL, tf), lambda i, f: (0, f)),
                pl.BlockSpec((tf, D_MODEL), lambda i, f: (f, 0))]
    args = [x, gain, w1, w3, w2]
    if final_norm:
        in_specs.append(pl.BlockSpec((1, D_MODEL), lambda i, f: (0, 0)))
        args.append(final_gain)
    return pl.pallas_call(
        functools.partial(_ffn_body, final_norm),
        grid=(n // tm, d_f // tf),
        in_specs=in_specs,
        out_specs=pl.BlockSpec((tm, D_MODEL), lambda i, f: (i, 0)),
        out_shape=jax.ShapeDtypeStruct((n, D_MODEL), F32),
        scratch_shapes=[pltpu.VMEM((tm, D_MODEL), BF16), pltpu.VMEM((tm, D_MODEL), F32)],
        compiler_params=pltpu.CompilerParams(
            dimension_semantics=("arbitrary", "arbitrary"), vmem_limit_bytes=VMEM_LIMIT),
        name="ffn_dense",
    )(*args)


def _route_body(x_ref, g_ref, router_ref, o_ref):
    h = _rms(x_ref[...], NORM_EPS) * g_ref[...]
    logits = _dot(h, router_ref[...])
    lane = lax.broadcasted_iota(jnp.int32, logits.shape, 1).astype(F32)
    lg = jnp.where(lane < N_EXP, logits, -jnp.inf)
    m1 = jnp.max(lg, axis=-1, keepdims=True)
    i1 = jnp.min(jnp.where(lg == m1, lane, float(LANE)), axis=-1, keepdims=True)
    lg2 = jnp.where(lane == i1, -jnp.inf, lg)
    m2 = jnp.max(lg2, axis=-1, keepdims=True)
    i2 = jnp.min(jnp.where(lg2 == m2, lane, float(LANE)), axis=-1, keepdims=True)
    e2 = jnp.exp(m2 - m1)
    den = 1.0 + e2
    o_ref[...] = (jnp.where(lane == 0.0, i1, 0.0) + jnp.where(lane == 1.0, i2, 0.0)
                  + jnp.where(lane == 2.0, 1.0 / den, 0.0) + jnp.where(lane == 3.0, e2 / den, 0.0))


def _route(x, gain, router):
    n = x.shape[0]
    tm = min(512, n)
    return pl.pallas_call(
        _route_body,
        grid=(n // tm,),
        in_specs=[pl.BlockSpec((tm, D_MODEL), lambda i: (i, 0)),
                  pl.BlockSpec((1, D_MODEL), lambda i: (0, 0)),
                  pl.BlockSpec((D_MODEL, LANE), lambda i: (0, 0))],
        out_specs=pl.BlockSpec((tm, LANE), lambda i: (i, 0)),
        out_shape=jax.ShapeDtypeStruct((n, LANE), F32),
        compiler_params=pltpu.CompilerParams(
            dimension_semantics=("arbitrary",), vmem_limit_bytes=VMEM_LIMIT),
        name="moe_route",
    )(x, gain, router)


def _row_gather(src_hbm, idx_ref, buf, sem, slot, n_rows):
    def copy(r):
        return pltpu.make_async_copy(src_hbm.at[pl.ds(idx_ref[0, 0, r], 1), :],
                                     buf.at[slot, pl.ds(r, 1), :], sem.at[slot])

    def start():
        def body(r, carry):
            copy(r).start()
            return carry
        lax.fori_loop(0, n_rows, body, 0, unroll=8)

    def wait():
        def body(r, carry):
            copy(r).wait()
            return carry
        lax.fori_loop(0, n_rows, body, 0, unroll=8)

    return start, wait


def _moe_ffn_body(tm, te_ref, nu_ref, tok_ref, tok_next_ref, x_hbm, g_ref, w1_ref, w3_ref, w2_ref,
                  ys_ref, xbuf, sem, h_scr, acc):
    i = pl.program_id(0)
    f = pl.program_id(1)
    slot = lax.rem(i, 2)
    used = i < nu_ref[0]

    @pl.when(f == 0)
    def _():
        start_next, _ = _row_gather(x_hbm, tok_next_ref, xbuf, sem, 1 - slot, tm)
        start_cur, wait_cur = _row_gather(x_hbm, tok_ref, xbuf, sem, slot, tm)
        pl.when(i == 0)(start_cur)
        pl.when(i + 1 < pl.num_programs(0))(start_next)
        wait_cur()
        h_scr[...] = (_rms(xbuf[slot], NORM_EPS) * g_ref[...]).astype(BF16)

    @pl.when(used)
    def _():
        hb = h_scr[...]
        u = _dot(hb, w1_ref[0], None)
        gate = _dot(hb, w3_ref[0], None)
        y = _dot((_silu(u) * gate).astype(BF16), w2_ref[0], None)

        @pl.when(f == 0)
        def _():
            acc[...] = y

        @pl.when(f > 0)
        def _():
            acc[...] += y

    @pl.when(f == pl.num_programs(1) - 1)
    def _():
        ys_ref[...] = jnp.where(used, acc[...], 0.0)


def _moe_ffn(x, gain, tok, tile_expert, n_used, w1, w3, w2, tm, tf):
    n_tiles = tok.shape[0]
    n_f = D_EXP // tf

    def w_block(i, f, te, nu):
        last = nu[0] - 1
        return te[jnp.minimum(i, last)], jnp.where(i <= last, f, n_f - 1)

    def w13_idx(i, f, te, nu):
        e, ff = w_block(i, f, te, nu)
        return (e, 0, ff)

    def w2_idx(i, f, te, nu):
        e, ff = w_block(i, f, te, nu)
        return (e, ff, 0)

    smem_tile = lambda idx: pl.BlockSpec((1, 1, tm), idx, memory_space=pltpu.SMEM)
    grid_spec = pltpu.PrefetchScalarGridSpec(
        num_scalar_prefetch=2,
        grid=(n_tiles, n_f),
        in_specs=[smem_tile(lambda i, f, te, nu: (i, 0, 0)),
                  smem_tile(lambda i, f, te, nu: (jnp.minimum(i + 1, n_tiles - 1), 0, 0)),
                  pl.BlockSpec(memory_space=pl.ANY),
                  pl.BlockSpec((1, D_MODEL), lambda i, f, te, nu: (0, 0)),
                  pl.BlockSpec((1, D_MODEL, tf), w13_idx),
                  pl.BlockSpec((1, D_MODEL, tf), w13_idx),
                  pl.BlockSpec((1, tf, D_MODEL), w2_idx)],
        out_specs=pl.BlockSpec((tm, D_MODEL), lambda i, f, te, nu: (i, 0)),
        scratch_shapes=[pltpu.VMEM((2, tm, D_MODEL), F32),
                        pltpu.SemaphoreType.DMA((2,)),
                        pltpu.VMEM((tm, D_MODEL), BF16),
                        pltpu.VMEM((tm, D_MODEL), F32)])
    return pl.pallas_call(
        functools.partial(_moe_ffn_body, tm),
        grid_spec=grid_spec,
        out_shape=jax.ShapeDtypeStruct((n_tiles * tm, D_MODEL), F32),
        compiler_params=pltpu.CompilerParams(
            dimension_semantics=("arbitrary", "arbitrary"), vmem_limit_bytes=VMEM_LIMIT),
        name="moe_ffn",
    )(tile_expert, n_used, tok, tok, x, gain, w1, w3, w2)


def _moe_combine_body(tm, final_norm, *refs):
    if final_norm:
        rows_ref, rows_next_ref, x_ref, route_ref, ys_hbm, gf_ref, o_ref, ybuf, sem = refs
    else:
        rows_ref, rows_next_ref, x_ref, route_ref, ys_hbm, o_ref, ybuf, sem = refs
    i = pl.program_id(0)
    slot = lax.rem(i, 2)
    start_next, _ = _row_gather(ys_hbm, rows_next_ref, ybuf, sem, 1 - slot, 2 * tm)
    start_cur, wait_cur = _row_gather(ys_hbm, rows_ref, ybuf, sem, slot, 2 * tm)
    pl.when(i == 0)(start_cur)
    pl.when(i + 1 < pl.num_programs(0))(start_next)
    wait_cur()
    route = route_ref[...]
    out = x_ref[...] + (route[:, 2:3] * ybuf[slot, 0:tm, :] + route[:, 3:4] * ybuf[slot, tm:2 * tm, :])
    if final_norm:
        out = _rms(out, NORM_EPS) * gf_ref[...]
    o_ref[...] = out


def _moe_combine(x, route, rows, ys, final_gain, tm):
    n = x.shape[0]
    n_tiles = n // tm
    final_norm = final_gain is not None
    smem_tile = lambda idx: pl.BlockSpec((1, 1, 2 * tm), idx, memory_space=pltpu.SMEM)
    in_specs = [smem_tile(lambda i: (i, 0, 0)),
                smem_tile(lambda i: (jnp.minimum(i + 1, n_tiles - 1), 0, 0)),
                pl.BlockSpec((tm, D_MODEL), lambda i: (i, 0)),
                pl.BlockSpec((tm, LANE), lambda i: (i, 0)),
                pl.BlockSpec(memory_space=pl.ANY)]
    args = [rows, rows, x, route, ys]
    if final_norm:
        in_specs.append(pl.BlockSpec((1, D_MODEL), lambda i: (0, 0)))
        args.append(final_gain)
    return pl.pallas_call(
        functools.partial(_moe_combine_body, tm, final_norm),
        grid=(n_tiles,),
        in_specs=in_specs,
        out_specs=pl.BlockSpec((tm, D_MODEL), lambda i: (i, 0)),
        out_shape=jax.ShapeDtypeStruct((n, D_MODEL), F32),
        scratch_shapes=[pltpu.VMEM((2, 2 * tm, D_MODEL), F32), pltpu.SemaphoreType.DMA((2,))],
        compiler_params=pltpu.CompilerParams(
            dimension_semantics=("arbitrary",), vmem_limit_bytes=VMEM_LIMIT),
        name="moe_combine",
    )(*args)


def _moe(x, gain, router, w1, w3, w2, final_gain):
    n = x.shape[0]
    tm = 512 if n >= 8192 else 256
    tm_c = 256
    n_tiles = -(-(2 * n + N_EXP * (tm - 1)) // tm)
    route = _route(x, gain, router)
    i1 = route[:, 0].astype(jnp.int32)
    i2 = route[:, 1].astype(jnp.int32)
    experts = jnp.arange(N_EXP, dtype=jnp.int32)
    hit = ((i1[:, None] == experts) | (i2[:, None] == experts)).astype(jnp.int32)
    rank = jnp.cumsum(hit, axis=0) - hit
    tiles_e = (jnp.sum(hit, axis=0) + tm - 1) // tm
    tile_end = jnp.cumsum(tiles_e)
    row_off = (tile_end - tiles_e) * tm
    row1 = row_off[i1] + jnp.take_along_axis(rank, i1[:, None], axis=1)[:, 0]
    row2 = row_off[i2] + jnp.take_along_axis(rank, i2[:, None], axis=1)[:, 0]
    token = jnp.arange(n, dtype=jnp.int32)
    tok = jnp.zeros((n_tiles * tm,), jnp.int32).at[row1].set(token).at[row2].set(token)
    tile_ids = jnp.arange(n_tiles, dtype=jnp.int32)
    tile_expert = jnp.minimum(
        jnp.sum((tile_end[None, :] <= tile_ids[:, None]).astype(jnp.int32), axis=1), N_EXP - 1)
    n_used = tile_end[N_EXP - 1:].astype(jnp.int32)
    ys = _moe_ffn(x, gain, tok.reshape(n_tiles, 1, tm), tile_expert, n_used, w1, w3, w2,
                  tm, D_EXP // 4)
    rows = jnp.concatenate([row1.reshape(n // tm_c, 1, tm_c), row2.reshape(n // tm_c, 1, tm_c)],
                           axis=2).astype(jnp.int32)
    return _moe_combine(x, route, rows, ys, final_gain, tm_c)


def _pad_last(a, width):
    return jnp.pad(a, [(0, 0)] * (a.ndim - 1) + [(0, width - a.shape[-1])])


def _pack_params(p):
    w_in = p["w_in"]
    w_in = jnp.concatenate([_pad_last(w_in[..., :DN_P], DN_SLAB),
                            _pad_last(w_in[..., DN_P:DN_P + RW_P], RW_SLAB),
                            w_in[..., DN_P + RW_P:]], axis=-1).astype(BF16)
    head_par = jnp.stack([_pad_last(p["dn_a_log"], LANE), _pad_last(p["dn_dt_bias"], LANE)], axis=1)
    g2 = jnp.pad(p["rw_g2"], ((0, 0), (0, RW_GATE_PAD - RW_GATE_R), (0, 0)))
    row = lambda a: a[:, None, :]
    return dict(
        ln_mix=row(p["ln_mix"]), w_in=w_in, dn_conv=p["dn_conv"], dn_head=head_par,
        dn_norm=row(jnp.tile(p["dn_norm"], (1, DN_HEADS))),
        rw=[(row(_pad_last(p["rw_mu"], RW_SLAB))[l], row(p["rw_w0"])[l], p["rw_w2"][l],
             row(p["rw_a0"])[l], p["rw_a2"][l], g2[l], row(p["rw_kk"])[l], row(p["rw_ka"])[l],
             p["rw_rk"].reshape(N_LAYERS, 1, RW_W)[l], row(p["rw_ln_g"])[l], row(p["rw_ln_b"])[l])
            for l in range(N_LAYERS)],
        rt_ln=row(p["rt_ln"]), w_out=p["w_out"].astype(BF16), ln_ffn=row(p["ln_ffn"]),
        ffn_w1=p["ffn_w1"].astype(BF16), ffn_w3=p["ffn_w3"].astype(BF16),
        ffn_w2=p["ffn_w2"].astype(BF16),
        moe_router=_pad_last(p["moe_router"], LANE),
        moe_w1=p["moe_w1"].astype(BF16), moe_w3=p["moe_w3"].astype(BF16),
        moe_w2=p["moe_w2"].astype(BF16),
        ln_final=p["ln_final"][None, :],
    )


def _trunk(x, n_seq, t_real, t_pad, c_len, n_par, states, pos0, w):
    n_valid = min(c_len, t_real)
    cos, sin = _rotary_tables(pos0, t_pad)
    convs, dns, shifts, rws, rts = [], [], [], [], []
    seq = lambda a: a.reshape(n_seq, t_pad, a.shape[-1])
    flat = lambda a: a.reshape(n_seq * t_pad, a.shape[-1])
    for l in range(N_LAYERS):
        z_dn, z_rw, z_rt = (seq(z) for z in _inproj(x, w["ln_mix"][l], w["w_in"][l]))
        if states is None:
            conv0 = s_dn0 = shift0 = s_rw0 = s_rt0 = None
        else:
            conv0, s_dn0 = states[0][l], states[1][l]
            shift0 = _pad_last(states[2][l], RW_SLAB)[:, None, :]
            s_rw0, s_rt0 = states[3][l], states[4][l]
        o_dn, s_dn = _dn_group(z_dn, conv0, s_dn0, w["dn_conv"][l], w["dn_head"][l],
                               w["dn_norm"][l], c_len, n_valid, n_par)
        o_rw, s_rw = _rw_group(z_rw, shift0, s_rw0, w["rw"][l], c_len, n_valid, n_par)
        o_rt, s_rt = _rt_group(z_rt, cos, sin, s_rt0, w["rt_ln"][l], c_len, n_valid, n_par)
        x = _outproj(x, flat(o_dn), flat(o_rw), flat(o_rt), w["w_out"][l])
        if l % 2 == 0:
            x = _ffn(x, w["ln_ffn"][l], w["ffn_w1"][l // 2], w["ffn_w3"][l // 2], w["ffn_w2"][l // 2],
                     w["ln_final"] if l == N_LAYERS - 1 else None, D_FF // 2)
        else:
            x = _moe(x, w["ln_ffn"][l], w["moe_router"][l // 2], w["moe_w1"][l // 2],
                     w["moe_w3"][l // 2], w["moe_w2"][l // 2],
                     w["ln_final"] if l == N_LAYERS - 1 else None)
        if t_real >= CONV_W - 1:
            convs.append(z_dn[:, t_real - (CONV_W - 1):t_real, :DN_CONV])
        else:
            convs.append(jnp.concatenate([conv0[:, t_real:], z_dn[:, :t_real, :DN_CONV]], axis=1))
        shifts.append(z_rw[:, t_real - 1, :RW_P])
        dns.append(s_dn)
        rws.append(s_rw)
        rts.append(s_rt)
    return x, jnp.stack(convs), jnp.stack(dns), jnp.stack(shifts), jnp.stack(rws), jnp.stack(rts)


def kernel(x_prompt, x_sample, state_dn_conv, state_dn, state_rw_shift, state_rw, state_rt, ln_mix, w_in, dn_conv, dn_a_log, dn_dt_bias, dn_norm, rw_mu, rw_w0, rw_w2, rw_a0, rw_a2, rw_g2, rw_kk, rw_ka, rw_rk, rw_ln_g, rw_ln_b, rt_ln, w_out, ln_ffn, ffn_w1, ffn_w3, ffn_w2, moe_router, moe_w1, moe_w3, moe_w2, ln_final):
    w = _pack_params(dict(
        ln_mix=ln_mix, w_in=w_in, dn_conv=dn_conv, dn_a_log=dn_a_log, dn_dt_bias=dn_dt_bias,
        dn_norm=dn_norm, rw_mu=rw_mu, rw_w0=rw_w0, rw_w2=rw_w2, rw_a0=rw_a0, rw_a2=rw_a2,
        rw_g2=rw_g2, rw_kk=rw_kk, rw_ka=rw_ka, rw_rk=rw_rk, rw_ln_g=rw_ln_g, rw_ln_b=rw_ln_b,
        rt_ln=rt_ln, w_out=w_out, ln_ffn=ln_ffn, ffn_w1=ffn_w1, ffn_w3=ffn_w3, ffn_w2=ffn_w2,
        moe_router=moe_router, moe_w1=moe_w1, moe_w3=moe_w3, moe_w2=moe_w2, ln_final=ln_final))
    bp, tp, _ = x_prompt.shape
    bs, ts, _ = x_sample.shape

    yp, *p_states = _trunk(x_prompt.reshape(bp * tp, D_MODEL), bp, tp, tp, 64, 4, None, 0, w)
    y_prompt = yp.reshape(bp, tp, D_MODEL)

    ts_pad = 8
    xs = jnp.pad(x_sample, ((0, 0), (0, ts_pad - ts), (0, 0))).reshape(bs * ts_pad, D_MODEL)
    ys, *s_states = _trunk(xs, bs, ts, ts_pad, ts_pad, 4,
                           (state_dn_conv, state_dn, state_rw_shift, state_rw, state_rt),
                           PAST_LEN, w)
    y_sample = ys.reshape(bs, ts_pad, D_MODEL)[:, :ts]
    return (y_prompt, y_sample, *p_states, *s_states)
```

```python
import functools
import math

import numpy as np
import jax
import jax.numpy as jnp
from jax import lax
from jax.experimental import pallas as pl
from jax.experimental.pallas import tpu as pltpu

F32 = jnp.float32
BF16 = jnp.bfloat16
HI = lax.Precision.HIGHEST

D_MODEL = 1024
N_LAYERS = 2
PAST_LEN = 16384
HEAD = 64
DN_HEADS = 6
DN_QK = DN_HEADS * HEAD
DN_CONV = 3 * DN_QK
CONV_W = 4
RW_HEADS = 6
RW_W = RW_HEADS * HEAD
RW_GATE_R = 160
RT_HEADS = 4
RT_DK = 32
RT_QK = RT_HEADS * RT_DK
RT_W = RT_HEADS * HEAD
DN_P = DN_CONV + DN_QK + 2 * DN_HEADS
RW_P = 3 * RW_W + 64 + 64 + RW_GATE_R
RT_P = 2 * RT_QK + 2 * RT_W
LANE = 128
DN_SLAB = 1664
RW_SLAB = 1536
RT_SLAB = RT_P
P_SLABS = DN_SLAB + RW_SLAB + RT_SLAB
RW_GATE_PAD = RW_SLAB - (3 * RW_W + 128)
N_EXP = 8
D_FF = 2816
D_EXP = 3584
MOE_F_STEPS = 2
ROPE_BASE = 10000.0
NORM_EPS = 1e-6
RW_GN_EPS = 64e-5
RT_GN_EPS = 1e-5
TAIL = 8
VMEM_LIMIT = 56 * 1024 * 1024


def _dot(a, b, precision=HI):
    return jnp.dot(a, b, preferred_element_type=F32, precision=precision)


def _split3(x):
    x1 = x.astype(BF16)
    r1 = x - x1.astype(F32)
    x2 = r1.astype(BF16)
    x3 = (r1 - x2.astype(F32)).astype(BF16)
    return x1, x2, x3


def _select_sum(sel, x):
    sel = sel.astype(BF16)
    return sum(jnp.dot(sel, p, preferred_element_type=F32) for p in _split3(x))


def _select_sum_nt(sel, x):
    sel = sel.astype(BF16)
    return sum(lax.dot_general(sel, p, (((1,), (1,)), ((), ())), preferred_element_type=F32)
               for p in _split3(x))


def _mm(a, b):
    return jnp.dot(a.astype(BF16), b.astype(BF16), preferred_element_type=F32)


def _mm_nt(a, b):
    return lax.dot_general(a.astype(BF16), b.astype(BF16), (((1,), (1,)), ((), ())),
                           preferred_element_type=F32)


def _mm_tn(a, b):
    return lax.dot_general(a.astype(BF16), b.astype(BF16), (((0,), (0,)), ((), ())),
                           preferred_element_type=F32)


def _rows(*parts):
    return jnp.concatenate(parts, axis=0)


def _cols(*parts):
    return jnp.concatenate(parts, axis=1)


def _silu(x):
    return x * jax.nn.sigmoid(x)


def _softplus(x):
    return jnp.maximum(x, 0.0) + jnp.log1p(jnp.exp(-jnp.abs(x)))


def _rms(x, eps):
    return x * lax.rsqrt(jnp.mean(x * x, axis=-1, keepdims=True) + eps)


def _tri(c):
    r = lax.broadcasted_iota(jnp.int32, (c, c), 0)
    col = lax.broadcasted_iota(jnp.int32, (c, c), 1)
    return r >= col, r > col, r == col


def _dot_nt(a, b, precision=HI):
    return lax.dot_general(a, b, (((1,), (1,)), ((), ())),
                           preferred_element_type=F32, precision=precision)


def _group_ones():
    r = lax.broadcasted_iota(jnp.int32, (LANE, LANE), 0)
    c = lax.broadcasted_iota(jnp.int32, (LANE, LANE), 1)
    return ((r < HEAD) == (c < HEAD)).astype(BF16)


def _group_sum(x, ones_blk):
    hi = x.astype(BF16)
    lo = (x - hi.astype(F32)).astype(BF16)
    out = [jnp.dot(hi[:, j:j + LANE], ones_blk, preferred_element_type=F32)
           + jnp.dot(lo[:, j:j + LANE], ones_blk, preferred_element_type=F32)
           for j in range(0, x.shape[1], LANE)]
    return out[0] if len(out) == 1 else jnp.concatenate(out, axis=1)


def _neumann_levels(n_rows):
    return max(0, math.ceil(math.log2(n_rows)) - 1)


def _inproj_body(x_ref, g_ref, w_ref, zdn_ref, zrw_ref, zrt_ref):
    h = (_rms(x_ref[...], NORM_EPS) * g_ref[...]).astype(BF16)
    zdn_ref[...] = _dot(h, w_ref[:, 0:DN_SLAB], None)
    zrw_ref[...] = _dot(h, w_ref[:, DN_SLAB:DN_SLAB + RW_SLAB], None)
    zrt_ref[...] = _dot(h, w_ref[:, DN_SLAB + RW_SLAB:P_SLABS], None)


def _inproj(x, gain, w):
    n = x.shape[0]
    tm = min(512, n)
    return pl.pallas_call(
        _inproj_body,
        grid=(n // tm,),
        in_specs=[pl.BlockSpec((tm, D_MODEL), lambda i: (i, 0)),
                  pl.BlockSpec((1, D_MODEL), lambda i: (0, 0)),
                  pl.BlockSpec((D_MODEL, P_SLABS), lambda i: (0, 0))],
        out_specs=[pl.BlockSpec((tm, DN_SLAB), lambda i: (i, 0)),
                   pl.BlockSpec((tm, RW_SLAB), lambda i: (i, 0)),
                   pl.BlockSpec((tm, RT_SLAB), lambda i: (i, 0))],
        out_shape=[jax.ShapeDtypeStruct((n, DN_SLAB), F32),
                   jax.ShapeDtypeStruct((n, RW_SLAB), F32),
                   jax.ShapeDtypeStruct((n, RT_SLAB), F32)],
        compiler_params=pltpu.CompilerParams(
            dimension_semantics=("arbitrary",), vmem_limit_bytes=VMEM_LIMIT),
        name="inproj",
    )(x, gain, w)


def _seq_specs(n_par, c_len, width):
    return pl.BlockSpec((n_par, c_len, width), lambda b, c: (b, c, 0))


def _state_spec(n_par, shape):
    return pl.BlockSpec((n_par,) + shape, lambda b, c: (b,) + (0,) * len(shape))


def _const_spec(a):
    return pl.BlockSpec(a.shape, lambda b, c: (0,) * a.ndim)


def _dn_body(c_len, n_valid, fresh, n_par, *refs):
    if fresh:
        z_ref, cw_ref, hp_ref, nw_ref, o_ref, so_ref, xe, s_scr, o_raw = refs
    else:
        z_ref, c0_ref, s0_ref, cw_ref, hp_ref, nw_ref, o_ref, so_ref, xe, s_scr, o_raw = refs
    ci = pl.program_id(1)

    @pl.when(ci == 0)
    def _():
        xe[:, 0:TAIL, :] = jnp.zeros((n_par, TAIL, DN_CONV), F32)
        if fresh:
            s_scr[...] = jnp.zeros_like(s_scr)
        else:
            xe[:, TAIL - (CONV_W - 1):TAIL, :] = c0_ref[...]
            s_scr[...] = s0_ref[...]

    incl, strict, eye = _tri(c_len)
    eye_f = eye.astype(F32)
    tri_f = incl.astype(F32)
    live = lax.broadcasted_iota(jnp.int32, (c_len, LANE), 0) < n_valid
    neg_a = -jnp.exp(hp_ref[0:1, :])
    dt_bias = hp_ref[1:2, :]

    ones_blk = _group_ones()
    eye_h = _tri(HEAD)[2].astype(F32)
    eye_l = (lax.broadcasted_iota(jnp.int32, (8, LANE), 0)
             == lax.broadcasted_iota(jnp.int32, (8, LANE), 1)).astype(F32)

    seqs = []
    for i in range(n_par):
        x = z_ref[i, :, 0:DN_CONV]
        xe[i, TAIL:TAIL + c_len, :] = x
        acc = x * cw_ref[CONV_W - 1:CONV_W, :]
        for j in range(CONV_W - 1):
            off = TAIL - (CONV_W - 1) + j
            acc = acc + xe[i, off:off + c_len, :] * cw_ref[j:j + 1, :]
        xe[i, 0:TAIL, :] = xe[i, c_len:c_len + TAIL, :]
        ab = z_ref[i, :, DN_CONV + DN_QK:DN_SLAB]
        g_tok = neg_a * _softplus(ab + dt_bias)
        beta = jax.nn.sigmoid(ab)
        if n_valid < c_len:
            g_tok = jnp.where(live, g_tok, 0.0)
            beta = jnp.where(live, beta, 0.0)
        seqs.append(dict(conv=_silu(acc), beta=beta, g_tok=g_tok))
    for sq in seqs:
        sq["g_cum"] = _select_sum(tri_f, sq["g_tok"])
        qk = sq["conv"][:, 0:2 * DN_QK]
        sq["qk"] = qk * lax.rsqrt(_group_sum(qk * qk, ones_blk) + 1e-6)
    for sq in seqs:
        sq["g_cum_t"] = _select_sum_nt(eye_l, sq["g_cum"])

    chains = []
    for i, sq in enumerate(seqs):
        for h in range(DN_HEADS):
            lo = HEAD * h
            q = sq["qk"][:, lo:lo + HEAD] * (HEAD ** -0.5)
            k = sq["qk"][:, DN_QK + lo:DN_QK + lo + HEAD]
            v = sq["conv"][:, 2 * DN_QK + lo:2 * DN_QK + lo + HEAD]
            gc = sq["g_cum"][:, h:h + 1]
            gc_row = sq["g_cum_t"][h:h + 1, :]
            b = sq["beta"][:, DN_HEADS + h:DN_HEADS + h + 1]
            dmask = jnp.where(incl, jnp.exp(jnp.minimum(gc - gc_row, 0.0)), 0.0)
            g_last = gc[c_len - 1:c_len, :]
            chains.append(dict(i=i, h=h, q=q, k=k, v=v, b=b, dmask=dmask, kb=k * b, egc=jnp.exp(gc),
                               k_decay=jnp.exp(g_last - gc_row), s_decay=jnp.exp(g_last)))
    for c in chains:
        c["kq"] = _mm_nt(_rows(c["kb"], c["q"], eye_h), c["k"])
    for c in chains:
        c["np"] = -jnp.where(strict, c["kq"][0:c_len] * c["dmask"], 0.0)
        c["t"] = eye_f + c["np"]
    for _ in range(_neumann_levels(n_valid)):
        for c in chains:
            c["np"] = _mm(c["np"], c["np"])
        for c in chains:
            c["t"] = c["t"] + _mm(c["t"], c["np"])
    for c in chains:
        c["tx"] = _mm(c["t"], _cols(c["v"] * c["b"], c["kb"] * c["egc"]))
    for c in chains:
        qk = c["kq"][c_len:2 * c_len] * c["dmask"]
        kd_t = c["kq"][2 * c_len:2 * c_len + HEAD] * c["k_decay"]
        c["w"] = _mm(_rows(qk, kd_t), c["tx"])
    for c in chains:
        w = c["w"]
        c["s"] = s_scr[c["i"], c["h"]]
        c["res"] = _mm(_rows(c["q"] * c["egc"] - w[0:c_len, HEAD:2 * HEAD],
                             w[c_len:c_len + HEAD, HEAD:2 * HEAD]), c["s"])
    for c in chains:
        w, res = c["w"], c["res"]
        lo = HEAD * c["h"]
        s_scr[c["i"], c["h"]] = (c["s"] * c["s_decay"] - res[c_len:c_len + HEAD]
                                 + w[c_len:c_len + HEAD, 0:HEAD])
        o_raw[c["i"], :, lo:lo + HEAD] = res[0:c_len] + w[0:c_len, 0:HEAD]
    for i in range(n_par):
        o = o_raw[i]
        ms = _group_sum(o * o, ones_blk) * (1.0 / HEAD)
        zg = z_ref[i, :, DN_CONV:DN_CONV + DN_QK]
        o_ref[i] = o * lax.rsqrt(ms + NORM_EPS) * nw_ref[...] * _silu(zg)

    @pl.when(ci == pl.num_programs(1) - 1)
    def _():
        so_ref[...] = s_scr[...]


def _dn_group(z, conv0, s0, conv_w, head_par, norm_w, c_len, n_valid, n_par):
    n_seq, t_pad, _ = z.shape
    fresh = s0 is None
    nc = t_pad // c_len
    assert n_valid == c_len or nc == 1
    state = (DN_HEADS, HEAD, HEAD)
    in_specs = [_seq_specs(n_par, c_len, DN_SLAB)]
    args = [z]
    if not fresh:
        in_specs += [_state_spec(n_par, (CONV_W - 1, DN_CONV)), _state_spec(n_par, state)]
        args += [conv0, s0]
    for p in (conv_w, head_par, norm_w):
        in_specs.append(_const_spec(p))
        args.append(p)
    return pl.pallas_call(
        functools.partial(_dn_body, c_len, n_valid, fresh, n_par),
        grid=(n_seq // n_par, nc),
        in_specs=in_specs,
        out_specs=[_seq_specs(n_par, c_len, DN_QK), _state_spec(n_par, state)],
        out_shape=[jax.ShapeDtypeStruct((n_seq, t_pad, DN_QK), F32),
                   jax.ShapeDtypeStruct((n_seq,) + state, F32)],
        scratch_shapes=[pltpu.VMEM((n_par, c_len + TAIL, DN_CONV), F32),
                        pltpu.VMEM((n_par,) + state, F32),
                        pltpu.VMEM((n_par, c_len, DN_QK), F32)],
        compiler_params=pltpu.CompilerParams(
            dimension_semantics=("arbitrary", "arbitrary"), vmem_limit_bytes=VMEM_LIMIT),
        name="dn_group",
    )(*args)


def _rw_body(c_len, n_valid, fresh, n_par, *refs):
    if fresh:
        (z_ref, mu_ref, w0_ref, w2_ref, a0_ref, a2_ref, g2_ref, kk_ref, ka_ref, rk_ref,
         lng_ref, lnb_ref, o_ref, so_ref, xe, s_scr, y_raw) = refs
    else:
        (z_ref, sh0_ref, s0_ref, mu_ref, w0_ref, w2_ref, a0_ref, a2_ref, g2_ref, kk_ref, ka_ref,
         rk_ref, lng_ref, lnb_ref, o_ref, so_ref, xe, s_scr, y_raw) = refs
    ci = pl.program_id(1)

    @pl.when(ci == 0)
    def _():
        xe[:, 0:TAIL, :] = jnp.zeros((n_par, TAIL, RW_SLAB), F32)
        if fresh:
            s_scr[...] = jnp.zeros_like(s_scr)
        else:
            xe[:, TAIL - 1:TAIL, :] = sh0_ref[...]
            s_scr[...] = s0_ref[...]

    incl, strict, eye = _tri(c_len)
    eye_f = eye.astype(F32)
    tri_f = incl.astype(F32)
    live = lax.broadcasted_iota(jnp.int32, (c_len, RW_W), 0) < n_valid
    ones_blk = _group_ones()
    eye_h = _tri(HEAD)[2].astype(F32)

    seqs = []
    for i in range(n_par):
        x = z_ref[i]
        xe[i, TAIL:TAIL + c_len, :] = x
        prev = xe[i, TAIL - 1:TAIL - 1 + c_len, :]
        zs = x + (prev - x) * mu_ref[...]
        xe[i, 0:TAIL, :] = xe[i, c_len:c_len + TAIL, :]
        seqs.append(dict(r=zs[:, 0:RW_W], k=zs[:, RW_W:2 * RW_W], v=zs[:, 2 * RW_W:3 * RW_W],
                         wl=zs[:, 3 * RW_W:3 * RW_W + 64], al=zs[:, 3 * RW_W + 64:3 * RW_W + 128],
                         gl=zs[:, 3 * RW_W + 128:RW_SLAB]))
    for sq in seqs:
        sq["w_lora"] = _mm(jnp.tanh(sq["wl"]), w2_ref[...])
        sq["a_lora"] = _mm(sq["al"], a2_ref[...])
        sq["g"] = _mm(jax.nn.sigmoid(sq["gl"]), g2_ref[...])
    for sq in seqs:
        w_log = -_softplus(-(w0_ref[...] + sq["w_lora"])) - 0.5
        a = jax.nn.sigmoid(a0_ref[...] + sq["a_lora"])
        kk_in = sq["k"] * kk_ref[...]
        k2 = sq["k"] * (1.0 + (a - 1.0) * ka_ref[...])
        lw = -jnp.exp(w_log)
        if n_valid < c_len:
            lw = jnp.where(live, lw, 0.0)
            k2 = jnp.where(live, k2, 0.0)
            kk_in = jnp.where(live, kk_in, 0.0)
        sq.update(a=a, kk_in=kk_in, k2=k2, lw=lw)
    for sq in seqs:
        sq["gcum"] = _select_sum(tri_f, sq["lw"])
        kk_in = sq["kk_in"]
        sq["kk"] = kk_in * lax.rsqrt(_group_sum(kk_in * kk_in, ones_blk) + 1e-6)
    for sq in seqs:
        gcum = sq["gcum"]
        e_neg = jnp.exp(-gcum)
        g_end = gcum[c_len - 1:c_len, :]
        e_rest = jnp.exp(g_end - gcum)
        bv = sq["kk"] * sq["a"]
        sq.update(at=-sq["kk"] * jnp.exp(gcum - sq["lw"]), bt=bv * e_neg, kt=sq["k2"] * e_neg,
                  rt=sq["r"] * jnp.exp(gcum), b_rest=bv * e_rest, k_rest=sq["k2"] * e_rest,
                  e_end=jnp.exp(g_end))

    chains = []
    for i, sq in enumerate(seqs):
        for h in range(RW_HEADS):
            sl = slice(HEAD * h, HEAD * (h + 1))
            chains.append(dict(i=i, h=h, sl=sl, at=sq["at"][:, sl], bt=sq["bt"][:, sl],
                               kt=sq["kt"][:, sl], rt=sq["rt"][:, sl], vh=sq["v"][:, sl],
                               e_end=sq["e_end"][:, sl],
                               rest=_rows(sq["b_rest"][:, sl], sq["k_rest"][:, sl])))
    for c in chains:
        c["pair"] = _mm_nt(_rows(c["at"], c["rt"], eye_h), _rows(c["bt"], c["kt"], c["at"], c["vh"]))
    for c in chains:
        pair = c["pair"]
        c["np"] = jnp.where(strict, pair[0:c_len, 0:c_len], 0.0)
        c["t"] = eye_f + c["np"]
        c["nak"] = jnp.where(strict, pair[0:c_len, c_len:2 * c_len], 0.0)
        c["mix"] = _cols(jnp.where(incl, pair[c_len:2 * c_len, 0:c_len], 0.0),
                         jnp.where(incl, pair[c_len:2 * c_len, c_len:2 * c_len], 0.0))
        c["at_t"] = pair[2 * c_len:2 * c_len + HEAD, 2 * c_len:3 * c_len]
        c["v_t"] = pair[2 * c_len:2 * c_len + HEAD, 3 * c_len:4 * c_len]
    for c in chains:
        c["nakv"] = _mm(c["nak"], c["vh"])
        c["nakv_t"] = _mm_nt(c["v_t"], c["nak"])
    for _ in range(_neumann_levels(n_valid)):
        for c in chains:
            c["np"] = _mm(c["np"], c["np"])
        for c in chains:
            c["t"] = c["t"] + _mm(c["t"], c["np"])
    for c in chains:
        c["tx"] = _mm(c["t"], _cols(c["at"], c["nakv"]))
        c["tx_t"] = _mm_nt(_rows(c["at_t"], c["nakv_t"]), c["t"])
    for c in chains:
        zero = jnp.zeros((c_len, HEAD), F32)
        c["zy"] = _mm(c["mix"], _rows(c["tx"], _cols(zero, c["vh"])))
        ta_t = c["tx_t"][0:HEAD]
        c["m"] = _mm(_rows(_cols(ta_t, jnp.zeros_like(ta_t)), _cols(c["tx_t"][HEAD:2 * HEAD], c["v_t"])),
                     c["rest"])
    for c in chains:
        c["s"] = s_scr[c["i"], c["h"]]
        c["y"] = _mm_nt(c["rt"] + c["zy"][:, 0:HEAD], c["s"])
        c["sm"] = _mm(c["s"], c["m"][0:HEAD])
    for c in chains:
        s_scr[c["i"], c["h"]] = c["s"] * c["e_end"] + c["sm"] + c["m"][HEAD:2 * HEAD]
        y_raw[c["i"], :, c["sl"]] = c["y"] + c["zy"][:, HEAD:2 * HEAD]
    for i, sq in enumerate(seqs):
        y = y_raw[i]
        d = y - _group_sum(y, ones_blk) * (1.0 / HEAD)
        yn = d * lax.rsqrt(_group_sum(d * d, ones_blk) * (1.0 / HEAD) + RW_GN_EPS)
        yn = yn * lng_ref[...] + lnb_ref[...]
        bonus = _group_sum(sq["r"] * sq["k2"] * rk_ref[...], ones_blk) * sq["v"]
        o_ref[i] = (yn + bonus) * sq["g"]

    @pl.when(ci == pl.num_programs(1) - 1)
    def _():
        so_ref[...] = s_scr[...]


def _rw_group(z, shift0, s0, par, c_len, n_valid, n_par):
    n_seq, t_pad, _ = z.shape
    fresh = s0 is None
    nc = t_pad // c_len
    assert n_valid == c_len or nc == 1
    state = (RW_HEADS, HEAD, HEAD)
    in_specs = [_seq_specs(n_par, c_len, RW_SLAB)]
    args = [z]
    if not fresh:
        in_specs += [_state_spec(n_par, (1, RW_SLAB)), _state_spec(n_par, state)]
        args += [shift0, s0]
    for p in par:
        in_specs.append(_const_spec(p))
        args.append(p)
    return pl.pallas_call(
        functools.partial(_rw_body, c_len, n_valid, fresh, n_par),
        grid=(n_seq // n_par, nc),
        in_specs=in_specs,
        out_specs=[_seq_specs(n_par, c_len, RW_W), _state_spec(n_par, state)],
        out_shape=[jax.ShapeDtypeStruct((n_seq, t_pad, RW_W), F32),
                   jax.ShapeDtypeStruct((n_seq,) + state, F32)],
        scratch_shapes=[pltpu.VMEM((n_par, c_len + TAIL, RW_SLAB), F32),
                        pltpu.VMEM((n_par,) + state, F32),
                        pltpu.VMEM((n_par, c_len, RW_W), F32)],
        compiler_params=pltpu.CompilerParams(
            dimension_semantics=("arbitrary", "arbitrary"), vmem_limit_bytes=VMEM_LIMIT),
        name="rw_group",
    )(*args)


def _rt_body(c_len, n_valid, fresh, n_par, *refs):
    if fresh:
        z_ref, cos_ref, sin_ref, lng_ref, o_ref, so_ref, s_scr, o_raw = refs
    else:
        z_ref, cos_ref, sin_ref, s0_ref, lng_ref, o_ref, so_ref, s_scr, o_raw = refs
    ci = pl.program_id(1)

    @pl.when(ci == 0)
    def _():
        if fresh:
            s_scr[...] = jnp.zeros_like(s_scr)
        else:
            s_scr[...] = s0_ref[...]

    lane = lax.broadcasted_iota(jnp.int32, (c_len, RT_QK), 1)
    first_half = (lane & (RT_DK - 1)) < (RT_DK // 2)
    cos = cos_ref[...]
    sin = sin_ref[...]

    def rotary(t):
        partner = jnp.where(first_half,
                            pltpu.roll(t, RT_QK - RT_DK // 2, axis=1),
                            pltpu.roll(t, RT_DK // 2, axis=1))
        return t * cos + partner * sin

    incl, _, _ = _tri(c_len)
    ri = lax.broadcasted_iota(jnp.int32, (c_len, c_len), 0)
    cj = lax.broadcasted_iota(jnp.int32, (c_len, c_len), 1)
    delta = (ri - cj).astype(F32)
    idx = lax.broadcasted_iota(jnp.int32, (c_len, 1), 0).astype(F32)
    log_decay = [math.log1p(-(2.0 ** (-5.0 - h))) for h in range(RT_HEADS)]
    dmasks = [jnp.where(incl, jnp.exp(delta * lg), 0.0) for lg in log_decay]
    q_scale = [jnp.exp((idx + 1.0) * lg) for lg in log_decay]
    t_row = lax.broadcasted_iota(jnp.int32, (1, c_len), 1).astype(F32)
    k_scale = [jnp.where(t_row < n_valid, jnp.exp((n_valid - 1.0 - t_row) * lg), 0.0) for lg in log_decay]
    ones_blk = _group_ones()
    eye_k = _tri(RT_DK)[2].astype(F32)

    chains = []
    for i in range(n_par):
        q = rotary(z_ref[i, :, 0:RT_QK])
        k = rotary(z_ref[i, :, RT_QK:2 * RT_QK]) * (RT_DK ** -0.5)
        for h in range(RT_HEADS):
            chains.append(dict(
                i=i, h=h, qh=q[:, RT_DK * h:RT_DK * (h + 1)], kh=k[:, RT_DK * h:RT_DK * (h + 1)],
                vh=z_ref[i, :, 2 * RT_QK + HEAD * h:2 * RT_QK + HEAD * (h + 1)]))
    for c in chains:
        c["qk"] = _mm_nt(_rows(c["qh"], eye_k), c["kh"])
    for c in chains:
        h = c["h"]
        c["s"] = s_scr[c["i"], h]
        c["cross"] = _mm(c["qh"] * q_scale[h], c["s"])
        c["kv"] = _mm(c["qk"][c_len:c_len + RT_DK] * k_scale[h], c["vh"])
        c["inner"] = _mm(c["qk"][0:c_len] * dmasks[h], c["vh"])
    for c in chains:
        i, h = c["i"], c["h"]
        s_scr[i, h] = c["s"] * math.exp(n_valid * log_decay[h]) + c["kv"]
        o_raw[i, :, HEAD * h:HEAD * (h + 1)] = c["inner"] + c["cross"]
    for i in range(n_par):
        o = o_raw[i]
        d = o - _group_sum(o, ones_blk) * (1.0 / HEAD)
        on = d * lax.rsqrt(_group_sum(d * d, ones_blk) * (1.0 / HEAD) + RT_GN_EPS)
        o_ref[i] = on * lng_ref[...] * _silu(z_ref[i, :, 2 * RT_QK + RT_W:RT_SLAB])

    @pl.when(ci == pl.num_programs(1) - 1)
    def _():
        so_ref[...] = s_scr[...]


def _rt_group(z, cos, sin, s0, ln_g, c_len, n_valid, n_par):
    n_seq, t_pad, _ = z.shape
    fresh = s0 is None
    nc = t_pad // c_len
    assert n_valid == c_len or nc == 1
    state = (RT_HEADS, RT_DK, HEAD)
    table = pl.BlockSpec((c_len, RT_QK), lambda b, c: (c, 0))
    in_specs = [_seq_specs(n_par, c_len, RT_SLAB), table, table]
    args = [z, cos, sin]
    if not fresh:
        in_specs.append(_state_spec(n_par, state))
        args.append(s0)
    in_specs.append(_const_spec(ln_g))
    args.append(ln_g)
    return pl.pallas_call(
        functools.partial(_rt_body, c_len, n_valid, fresh, n_par),
        grid=(n_seq // n_par, nc),
        in_specs=in_specs,
        out_specs=[_seq_specs(n_par, c_len, RT_W), _state_spec(n_par, state)],
        out_shape=[jax.ShapeDtypeStruct((n_seq, t_pad, RT_W), F32),
                   jax.ShapeDtypeStruct((n_seq,) + state, F32)],
        scratch_shapes=[pltpu.VMEM((n_par,) + state, F32),
                        pltpu.VMEM((n_par, c_len, RT_W), F32)],
        compiler_params=pltpu.CompilerParams(
            dimension_semantics=("arbitrary", "arbitrary"), vmem_limit_bytes=VMEM_LIMIT),
        name="rt_group",
    )(*args)


def _rotary_tables(pos0, t_pad):
    half = RT_DK // 2
    inv = ROPE_BASE ** (-np.arange(half, dtype=np.float64) / half)
    ang = (pos0 + np.arange(t_pad, dtype=np.float64))[:, None] * inv[None, :]
    cos = np.tile(np.concatenate([np.cos(ang), np.cos(ang)], axis=1), (1, RT_HEADS))
    sin = np.tile(np.concatenate([-np.sin(ang), np.sin(ang)], axis=1), (1, RT_HEADS))
    return jnp.asarray(cos, F32), jnp.asarray(sin, F32)


def _outproj_body(x_ref, odn_ref, orw_ref, ort_ref, w_ref, o_ref):
    acc = x_ref[...]
    acc = acc + _dot(odn_ref[...].astype(BF16), w_ref[0:DN_QK, :], None)
    acc = acc + _dot(orw_ref[...].astype(BF16), w_ref[DN_QK:DN_QK + RW_W, :], None)
    acc = acc + _dot(ort_ref[...].astype(BF16), w_ref[DN_QK + RW_W:D_MODEL, :], None)
    o_ref[...] = acc


def _outproj(x, o_dn, o_rw, o_rt, w):
    n = x.shape[0]
    tm = min(512, n)
    row = lambda i: (i, 0)
    return pl.pallas_call(
        _outproj_body,
        grid=(n // tm,),
        in_specs=[pl.BlockSpec((tm, D_MODEL), row),
                  pl.BlockSpec((tm, DN_QK), row),
                  pl.BlockSpec((tm, RW_W), row),
                  pl.BlockSpec((tm, RT_W), row),
                  pl.BlockSpec((D_MODEL, D_MODEL), lambda i: (0, 0))],
        out_specs=pl.BlockSpec((tm, D_MODEL), row),
        out_shape=jax.ShapeDtypeStruct((n, D_MODEL), F32),
        compiler_params=pltpu.CompilerParams(
            dimension_semantics=("arbitrary",), vmem_limit_bytes=VMEM_LIMIT),
        name="outproj",
    )(x, o_dn, o_rw, o_rt, w)


def _ffn_body(final_norm, *refs):
    if final_norm:
        x_ref, g_ref, w1_ref, w3_ref, w2_ref, gf_ref, o_ref, h_scr, acc = refs
    else:
        x_ref, g_ref, w1_ref, w3_ref, w2_ref, o_ref, h_scr, acc = refs
    f = pl.program_id(1)

    @pl.when(f == 0)
    def _():
        x = x_ref[...]
        h_scr[...] = (_rms(x, NORM_EPS) * g_ref[...]).astype(BF16)
        acc[...] = x

    hb = h_scr[...]
    u = _dot(hb, w1_ref[...], None)
    gate = _dot(hb, w3_ref[...], None)
    acc[...] += _dot((_silu(u) * gate).astype(BF16), w2_ref[...], None)

    @pl.when(f == pl.num_programs(1) - 1)
    def _():
        if final_norm:
            o_ref[...] = _rms(acc[...], NORM_EPS) * gf_ref[...]
        else:
            o_ref[...] = acc[...]


def _ffn(x, gain, w1, w3, w2, final_gain, tf):
    n = x.shape[0]
    d_f = w1.shape[1]
    final_norm = final_gain is not None
    tm = min(1024, n)
    in_specs = [pl.BlockSpec((tm, D_MODEL), lambda i, f: (i, 0)),
                pl.BlockSpec((1, D_MODEL), lambda i, f: (0, 0)),
                pl.BlockSpec((D_MODEL, tf), lambda i, f: (0, f)),
                pl.BlockSpec((D_MODEL, tf), lambda i, f: (0, f)),
                pl.BlockSpec((tf, D_MODEL), lambda i, f: (f, 0))]
    args = [x, gain, w1, w3, w2]
    if final_norm:
        in_specs.append(pl.BlockSpec((1, D_MODEL), lambda i, f: (0, 0)))
        args.append(final_gain)
    return pl.pallas_call(
        functools.partial(_ffn_body, final_norm),
        grid=(n // tm, d_f // tf),
        in_specs=in_specs,
        out_specs=pl.BlockSpec((tm, D_MODEL), lambda i, f: (i, 0)),
        out_shape=jax.ShapeDtypeStruct((n, D_MODEL), F32),
        scratch_shapes=[pltpu.VMEM((tm, D_MODEL), BF16), pltpu.VMEM((tm, D_MODEL), F32)],
        compiler_params=pltpu.CompilerParams(
            dimension_semantics=("arbitrary", "arbitrary"), vmem_limit_bytes=VMEM_LIMIT),
        name="ffn_dense",
    )(*args)


def _route_body(x_ref, g_ref, router_ref, o_ref):
    h = _rms(x_ref[...], NORM_EPS) * g_ref[...]
    logits = _dot(h, router_ref[...])
    lane = lax.broadcasted_iota(jnp.int32, logits.shape, 1).astype(F32)
    lg = jnp.where(lane < N_EXP, logits, -jnp.inf)
    m1 = jnp.max(lg, axis=-1, keepdims=True)
    i1 = jnp.min(jnp.where(lg == m1, lane, float(LANE)), axis=-1, keepdims=True)
    lg2 = jnp.where(lane == i1, -jnp.inf, lg)
    m2 = jnp.max(lg2, axis=-1, keepdims=True)
    i2 = jnp.min(jnp.where(lg2 == m2, lane, float(LANE)), axis=-1, keepdims=True)
    e2 = jnp.exp(m2 - m1)
    den = 1.0 + e2
    o_ref[...] = (jnp.where(lane == 0.0, i1, 0.0) + jnp.where(lane == 1.0, i2, 0.0)
                  + jnp.where(lane == 2.0, 1.0 / den, 0.0) + jnp.where(lane == 3.0, e2 / den, 0.0))


def _route(x, gain, router):
    n = x.shape[0]
    tm = min(512, n)
    return pl.pallas_call(
        _route_body,
        grid=(n // tm,),
        in_specs=[pl.BlockSpec((tm, D_MODEL), lambda i: (i, 0)),
                  pl.BlockSpec((1, D_MODEL), lambda i: (0, 0)),
                  pl.BlockSpec((D_MODEL, LANE), lambda i: (0, 0))],
        out_specs=pl.BlockSpec((tm, LANE), lambda i: (i, 0)),
        out_shape=jax.ShapeDtypeStruct((n, LANE), F32),
        compiler_params=pltpu.CompilerParams(
            dimension_semantics=("arbitrary",), vmem_limit_bytes=VMEM_LIMIT),
        name="moe_route",
    )(x, gain, router)


def _row_gather(src_hbm, idx_ref, buf, sem, slot, n_rows):
    def copy(r):
        return pltpu.make_async_copy(src_hbm.at[pl.ds(idx_ref[0, 0, r], 1), :],
                                     buf.at[slot, pl.ds(r, 1), :], sem.at[slot])

    def start():
        def body(r, carry):
            copy(r).start()
            return carry
        lax.fori_loop(0, n_rows, body, 0, unroll=8)

    def wait():
        def body(r, carry):
            copy(r).wait()
            return carry
        lax.fori_loop(0, n_rows, body, 0, unroll=8)

    return start, wait, copy


def _moe_ffn_body(tm, te_ref, nu_ref, tok_ref, tok_next_ref, x_hbm, g_ref, w1_ref, w3_ref, w2_ref,
                  ys_ref, xbuf, sem, h_scr, acc):
    i = pl.program_id(0)
    f = pl.program_id(1)
    n_f = pl.num_programs(1)
    slot = lax.rem(i, 2)
    n_used = nu_ref[0]
    used = i < n_used
    rows_per_step = tm // MOE_F_STEPS
    start_cur, wait_cur, _ = _row_gather(x_hbm, tok_ref, xbuf, sem, slot, tm)
    _, wait_next, copy_next = _row_gather(x_hbm, tok_next_ref, xbuf, sem, 1 - slot, tm)

    @pl.when((f == 0) & (i <= n_used))
    def _():
        pl.when(i == 0)(start_cur)
        wait_cur()

    @pl.when((f == 0) & used)
    def _():
        h_scr[...] = (_rms(xbuf[slot], NORM_EPS) * g_ref[...]).astype(BF16)

    @pl.when(used)
    def _():
        hb = h_scr[...]
        u = _dot(hb, w1_ref[0], None)
        gate = _dot(hb, w3_ref[0], None)
        y = _dot((_silu(u) * gate).astype(BF16), w2_ref[0], None)
        for j in range(rows_per_step):
            copy_next(f * rows_per_step + j).start()

        @pl.when(f == 0)
        def _():
            acc[...] = y

        @pl.when(f > 0)
        def _():
            acc[...] += y

    @pl.when(f == n_f - 1)
    def _():
        ys_ref[...] = jnp.where(used, acc[...], 0.0)
        pl.when(used & (i == pl.num_programs(0) - 1))(wait_next)


def _moe_ffn(x, gain, tok, tile_expert, n_used, w1, w3, w2, tm, tf):
    n_tiles = tok.shape[0]
    n_f = D_EXP // tf

    def w_block(i, f, te, nu):
        last = nu[0] - 1
        return te[jnp.minimum(i, last)], jnp.where(i <= last, f, n_f - 1)

    def w13_idx(i, f, te, nu):
        e, ff = w_block(i, f, te, nu)
        return (e, 0, ff)

    def w2_idx(i, f, te, nu):
        e, ff = w_block(i, f, te, nu)
        return (e, ff, 0)

    smem_tile = lambda idx: pl.BlockSpec((1, 1, tm), idx, memory_space=pltpu.SMEM)
    grid_spec = pltpu.PrefetchScalarGridSpec(
        num_scalar_prefetch=2,
        grid=(n_tiles, n_f),
        in_specs=[smem_tile(lambda i, f, te, nu: (i, 0, 0)),
                  smem_tile(lambda i, f, te, nu: (jnp.minimum(i + 1, n_tiles - 1), 0, 0)),
                  pl.BlockSpec(memory_space=pl.ANY),
                  pl.BlockSpec((1, D_MODEL), lambda i, f, te, nu: (0, 0)),
                  pl.BlockSpec((1, D_MODEL, tf), w13_idx),
                  pl.BlockSpec((1, D_MODEL, tf), w13_idx),
                  pl.BlockSpec((1, tf, D_MODEL), w2_idx)],
        out_specs=pl.BlockSpec((tm, D_MODEL), lambda i, f, te, nu: (i, 0)),
        scratch_shapes=[pltpu.VMEM((2, tm, D_MODEL), F32),
                        pltpu.SemaphoreType.DMA((2,)),
                        pltpu.VMEM((tm, D_MODEL), BF16),
                        pltpu.VMEM((tm, D_MODEL), F32)])
    return pl.pallas_call(
        functools.partial(_moe_ffn_body, tm),
        grid_spec=grid_spec,
        out_shape=jax.ShapeDtypeStruct((n_tiles * tm, D_MODEL), F32),
        compiler_params=pltpu.CompilerParams(
            dimension_semantics=("arbitrary", "arbitrary"), vmem_limit_bytes=VMEM_LIMIT),
        name="moe_ffn",
    )(tile_expert, n_used, tok, tok, x, gain, w1, w3, w2)


def _moe_combine_body(tm, final_norm, *refs):
    if final_norm:
        rows_ref, rows_next_ref, x_ref, route_ref, ys_hbm, gf_ref, o_ref, ybuf, sem = refs
    else:
        rows_ref, rows_next_ref, x_ref, route_ref, ys_hbm, o_ref, ybuf, sem = refs
    i = pl.program_id(0)
    slot = lax.rem(i, 2)
    start_next, _, _ = _row_gather(ys_hbm, rows_next_ref, ybuf, sem, 1 - slot, 2 * tm)
    start_cur, wait_cur, _ = _row_gather(ys_hbm, rows_ref, ybuf, sem, slot, 2 * tm)
    pl.when(i == 0)(start_cur)
    pl.when(i + 1 < pl.num_programs(0))(start_next)
    wait_cur()
    route = route_ref[...]
    out = x_ref[...] + (route[:, 2:3] * ybuf[slot, 0:tm, :] + route[:, 3:4] * ybuf[slot, tm:2 * tm, :])
    if final_norm:
        out = _rms(out, NORM_EPS) * gf_ref[...]
    o_ref[...] = out


def _moe_combine(x, route, rows, ys, final_gain, tm):
    n = x.shape[0]
    n_tiles = n // tm
    final_norm = final_gain is not None
    smem_tile = lambda idx: pl.BlockSpec((1, 1, 2 * tm), idx, memory_space=pltpu.SMEM)
    in_specs = [smem_tile(lambda i: (i, 0, 0)),
                smem_tile(lambda i: (jnp.minimum(i + 1, n_tiles - 1), 0, 0)),
                pl.BlockSpec((tm, D_MODEL), lambda i: (i, 0)),
                pl.BlockSpec((tm, LANE), lambda i: (i, 0)),
                pl.BlockSpec(memory_space=pl.ANY)]
    args = [rows, rows, x, route, ys]
    if final_norm:
        in_specs.append(pl.BlockSpec((1, D_MODEL), lambda i: (0, 0)))
        args.append(final_gain)
    return pl.pallas_call(
        functools.partial(_moe_combine_body, tm, final_norm),
        grid=(n_tiles,),
        in_specs=in_specs,
        out_specs=pl.BlockSpec((tm, D_MODEL), lambda i: (i, 0)),
        out_shape=jax.ShapeDtypeStruct((n, D_MODEL), F32),
        scratch_shapes=[pltpu.VMEM((2, 2 * tm, D_MODEL), F32), pltpu.SemaphoreType.DMA((2,))],
        compiler_params=pltpu.CompilerParams(
            dimension_semantics=("arbitrary",), vmem_limit_bytes=VMEM_LIMIT),
        name="moe_combine",
    )(*args)


def _moe(x, gain, router, w1, w3, w2, final_gain):
    n = x.shape[0]
    tm = 512 if n >= 8192 else 256
    tm_c = 256
    n_tiles = -(-(2 * n + N_EXP * (tm - 1)) // tm)
    route = _route(x, gain, router)
    i1 = route[:, 0].astype(jnp.int32)
    i2 = route[:, 1].astype(jnp.int32)
    experts = jnp.arange(N_EXP, dtype=jnp.int32)
    hit = ((i1[:, None] == experts) | (i2[:, None] == experts)).astype(jnp.int32)
    rank = jnp.cumsum(hit, axis=0) - hit
    tiles_e = (jnp.sum(hit, axis=0) + tm - 1) // tm
    tile_end = jnp.cumsum(tiles_e)
    row_off = (tile_end - tiles_e) * tm
    row1 = row_off[i1] + jnp.take_along_axis(rank, i1[:, None], axis=1)[:, 0]
    row2 = row_off[i2] + jnp.take_along_axis(rank, i2[:, None], axis=1)[:, 0]
    token = jnp.arange(n, dtype=jnp.int32)
    tok = jnp.zeros((n_tiles * tm,), jnp.int32).at[jnp.concatenate([row1, row2])].set(
        jnp.concatenate([token, token]), unique_indices=True)
    tile_ids = jnp.arange(n_tiles, dtype=jnp.int32)
    tile_expert = jnp.minimum(
        jnp.sum((tile_end[None, :] <= tile_ids[:, None]).astype(jnp.int32), axis=1), N_EXP - 1)
    n_used = tile_end[N_EXP - 1:].astype(jnp.int32)
    ys = _moe_ffn(x, gain, tok.reshape(n_tiles, 1, tm), tile_expert, n_used, w1, w3, w2,
                  tm, D_EXP // MOE_F_STEPS)
    rows = jnp.concatenate([row1.reshape(n // tm_c, 1, tm_c), row2.reshape(n // tm_c, 1, tm_c)],
                           axis=2).astype(jnp.int32)
    return _moe_combine(x, route, rows, ys, final_gain, tm_c)


def _pad_last(a, width):
    return jnp.pad(a, [(0, 0)] * (a.ndim - 1) + [(0, width - a.shape[-1])])


def _pack_params(p):
    w_in = p["w_in"]
    w_in = jnp.concatenate([_pad_last(w_in[..., :DN_P], DN_SLAB),
                            _pad_last(w_in[..., DN_P:DN_P + RW_P], RW_SLAB),
                            w_in[..., DN_P + RW_P:]], axis=-1).astype(BF16)
    head_par = jnp.stack([_pad_last(p["dn_a_log"], LANE), _pad_last(p["dn_dt_bias"], LANE)], axis=1)
    g2 = jnp.pad(p["rw_g2"], ((0, 0), (0, RW_GATE_PAD - RW_GATE_R), (0, 0)))
    row = lambda a: a[:, None, :]
    return dict(
        ln_mix=row(p["ln_mix"]), w_in=w_in, dn_conv=p["dn_conv"], dn_head=head_par,
        dn_norm=row(jnp.tile(p["dn_norm"], (1, DN_HEADS))),
        rw=[(row(_pad_last(p["rw_mu"], RW_SLAB))[l], row(p["rw_w0"])[l], p["rw_w2"][l],
             row(p["rw_a0"])[l], p["rw_a2"][l], g2[l], row(p["rw_kk"])[l], row(p["rw_ka"])[l],
             p["rw_rk"].reshape(N_LAYERS, 1, RW_W)[l], row(p["rw_ln_g"])[l], row(p["rw_ln_b"])[l])
            for l in range(N_LAYERS)],
        rt_ln=row(p["rt_ln"]), w_out=p["w_out"].astype(BF16), ln_ffn=row(p["ln_ffn"]),
        ffn_w1=p["ffn_w1"].astype(BF16), ffn_w3=p["ffn_w3"].astype(BF16),
        ffn_w2=p["ffn_w2"].astype(BF16),
        moe_router=_pad_last(p["moe_router"], LANE),
        moe_w1=p["moe_w1"].astype(BF16), moe_w3=p["moe_w3"].astype(BF16),
        moe_w2=p["moe_w2"].astype(BF16),
        ln_final=p["ln_final"][None, :],
    )


def _trunk(x, n_seq, t_real, t_pad, c_len, n_par, states, pos0, w):
    n_valid = min(c_len, t_real)
    cos, sin = _rotary_tables(pos0, t_pad)
    convs, dns, shifts, rws, rts = [], [], [], [], []
    seq = lambda a: a.reshape(n_seq, t_pad, a.shape[-1])
    flat = lambda a: a.reshape(n_seq * t_pad, a.shape[-1])
    for l in range(N_LAYERS):
        z_dn, z_rw, z_rt = (seq(z) for z in _inproj(x, w["ln_mix"][l], w["w_in"][l]))
        if states is None:
            conv0 = s_dn0 = shift0 = s_rw0 = s_rt0 = None
        else:
            conv0, s_dn0 = states[0][l], states[1][l]
            shift0 = _pad_last(states[2][l], RW_SLAB)[:, None, :]
            s_rw0, s_rt0 = states[3][l], states[4][l]
        o_dn, s_dn = _dn_group(z_dn, conv0, s_dn0, w["dn_conv"][l], w["dn_head"][l],
                               w["dn_norm"][l], c_len, n_valid, n_par)
        o_rw, s_rw = _rw_group(z_rw, shift0, s_rw0, w["rw"][l], c_len, n_valid, n_par)
        o_rt, s_rt = _rt_group(z_rt, cos, sin, s_rt0, w["rt_ln"][l], c_len, n_valid, n_par)
        x = _outproj(x, flat(o_dn), flat(o_rw), flat(o_rt), w["w_out"][l])
        if l % 2 == 0:
            x = _ffn(x, w["ln_ffn"][l], w["ffn_w1"][l // 2], w["ffn_w3"][l // 2], w["ffn_w2"][l // 2],
                     w["ln_final"] if l == N_LAYERS - 1 else None, D_FF // 2)
        else:
            xr = seq(x)[:, :t_real].reshape(n_seq * t_real, D_MODEL)
            xr = _moe(xr, w["ln_ffn"][l], w["moe_router"][l // 2], w["moe_w1"][l // 2],
                      w["moe_w3"][l // 2], w["moe_w2"][l // 2],
                      w["ln_final"] if l == N_LAYERS - 1 else None)
            x = xr if l == N_LAYERS - 1 else flat(
                jnp.pad(xr.reshape(n_seq, t_real, D_MODEL), ((0, 0), (0, t_pad - t_real), (0, 0))))
        if t_real >= CONV_W - 1:
            convs.append(z_dn[:, t_real - (CONV_W - 1):t_real, :DN_CONV])
        else:
            convs.append(jnp.concatenate([conv0[:, t_real:], z_dn[:, :t_real, :DN_CONV]], axis=1))
        shifts.append(z_rw[:, t_real - 1, :RW_P])
        dns.append(s_dn)
        rws.append(s_rw)
        rts.append(s_rt)
    return x, jnp.stack(convs), jnp.stack(dns), jnp.stack(shifts), jnp.stack(rws), jnp.stack(rts)


def kernel(x_prompt, x_sample, state_dn_conv, state_dn, state_rw_shift, state_rw, state_rt, ln_mix, w_in, dn_conv, dn_a_log, dn_dt_bias, dn_norm, rw_mu, rw_w0, rw_w2, rw_a0, rw_a2, rw_g2, rw_kk, rw_ka, rw_rk, rw_ln_g, rw_ln_b, rt_ln, w_out, ln_ffn, ffn_w1, ffn_w3, ffn_w2, moe_router, moe_w1, moe_w3, moe_w2, ln_final):
    w = _pack_params(dict(
        ln_mix=ln_mix, w_in=w_in, dn_conv=dn_conv, dn_a_log=dn_a_log, dn_dt_bias=dn_dt_bias,
        dn_norm=dn_norm, rw_mu=rw_mu, rw_w0=rw_w0, rw_w2=rw_w2, rw_a0=rw_a0, rw_a2=rw_a2,
        rw_g2=rw_g2, rw_kk=rw_kk, rw_ka=rw_ka, rw_rk=rw_rk, rw_ln_g=rw_ln_g, rw_ln_b=rw_ln_b,
        rt_ln=rt_ln, w_out=w_out, ln_ffn=ln_ffn, ffn_w1=ffn_w1, ffn_w3=ffn_w3, ffn_w2=ffn_w2,
        moe_router=moe_router, moe_w1=moe_w1, moe_w3=moe_w3, moe_w2=moe_w2, ln_final=ln_final))
    bp, tp, _ = x_prompt.shape
    bs, ts, _ = x_sample.shape

    yp, *p_states = _trunk(x_prompt.reshape(bp * tp, D_MODEL), bp, tp, tp, 64, 4, None, 0, w)
    y_prompt = yp.reshape(bp, tp, D_MODEL)

    ts_pad = 8
    xs = jnp.pad(x_sample, ((0, 0), (0, ts_pad - ts), (0, 0))).reshape(bs * ts_pad, D_MODEL)
    ys, *s_states = _trunk(xs, bs, ts, ts_pad, ts_pad, 8,
                           (state_dn_conv, state_dn, state_rw_shift, state_rw, state_rt),
                           PAST_LEN, w)
    y_sample = ys.reshape(bs, -1, D_MODEL)[:, :ts]
    return (y_prompt, y_sample, *p_states, *s_states)
```

```python
import functools
import math

import numpy as np
import jax
import jax.numpy as jnp
from jax import lax
from jax.experimental import pallas as pl
from jax.experimental.pallas import tpu as pltpu

F32 = jnp.float32
BF16 = jnp.bfloat16
HI = lax.Precision.HIGHEST

D_MODEL = 1024
N_LAYERS = 2
PAST_LEN = 16384
HEAD = 64
DN_HEADS = 6
DN_QK = DN_HEADS * HEAD
DN_CONV = 3 * DN_QK
CONV_W = 4
RW_HEADS = 6
RW_W = RW_HEADS * HEAD
RW_GATE_R = 160
RT_HEADS = 4
RT_DK = 32
RT_QK = RT_HEADS * RT_DK
RT_W = RT_HEADS * HEAD
DN_P = DN_CONV + DN_QK + 2 * DN_HEADS
RW_P = 3 * RW_W + 64 + 64 + RW_GATE_R
RT_P = 2 * RT_QK + 2 * RT_W
LANE = 128
DN_SLAB = 1664
RW_SLAB = 1536
RT_SLAB = RT_P
P_SLABS = DN_SLAB + RW_SLAB + RT_SLAB
RW_GATE_PAD = RW_SLAB - (3 * RW_W + 128)
N_EXP = 8
D_FF = 2816
D_EXP = 3584
MOE_F_STEPS = 2
ROPE_BASE = 10000.0
NORM_EPS = 1e-6
RW_GN_EPS = 64e-5
RT_GN_EPS = 1e-5
TAIL = 8
VMEM_LIMIT = 56 * 1024 * 1024


def _dot(a, b, precision=HI):
    return jnp.dot(a, b, preferred_element_type=F32, precision=precision)


def _split3(x):
    x1 = x.astype(BF16)
    r1 = x - x1.astype(F32)
    x2 = r1.astype(BF16)
    x3 = (r1 - x2.astype(F32)).astype(BF16)
    return x1, x2, x3


def _select_sum(sel, x):
    sel = sel.astype(BF16)
    return sum(jnp.dot(sel, p, preferred_element_type=F32) for p in _split3(x))


def _select_sum_nt(sel, x):
    sel = sel.astype(BF16)
    return sum(lax.dot_general(sel, p, (((1,), (1,)), ((), ())), preferred_element_type=F32)
               for p in _split3(x))


def _mm(a, b):
    return jnp.dot(a.astype(BF16), b.astype(BF16), preferred_element_type=F32)


def _mm_nt(a, b):
    return lax.dot_general(a.astype(BF16), b.astype(BF16), (((1,), (1,)), ((), ())),
                           preferred_element_type=F32)


def _mm_tn(a, b):
    return lax.dot_general(a.astype(BF16), b.astype(BF16), (((0,), (0,)), ((), ())),
                           preferred_element_type=F32)


def _rows(*parts):
    return jnp.concatenate(parts, axis=0)


def _cols(*parts):
    return jnp.concatenate(parts, axis=1)


def _silu(x):
    return x * jax.nn.sigmoid(x)


def _softplus(x):
    return jnp.maximum(x, 0.0) + jnp.log(1.0 + jnp.exp(-jnp.abs(x)))


def _rms(x, eps):
    return x * lax.rsqrt(jnp.mean(x * x, axis=-1, keepdims=True) + eps)


def _tri(c):
    r = lax.broadcasted_iota(jnp.int32, (c, c), 0)
    col = lax.broadcasted_iota(jnp.int32, (c, c), 1)
    return r >= col, r > col, r == col


def _dot_nt(a, b, precision=HI):
    return lax.dot_general(a, b, (((1,), (1,)), ((), ())),
                           preferred_element_type=F32, precision=precision)


def _group_ones():
    r = lax.broadcasted_iota(jnp.int32, (LANE, LANE), 0)
    c = lax.broadcasted_iota(jnp.int32, (LANE, LANE), 1)
    return ((r < HEAD) == (c < HEAD)).astype(BF16)


def _group_sum(x, ones_blk, terms=2):
    parts = _split3(x)[:terms]
    out = [sum(jnp.dot(p[:, j:j + LANE], ones_blk, preferred_element_type=F32) for p in parts)
           for j in range(0, x.shape[1], LANE)]
    return out[0] if len(out) == 1 else jnp.concatenate(out, axis=1)


def _bd(x):
    x = x.astype(BF16)
    first = lax.broadcasted_iota(jnp.int32, x.shape, 1) < x.shape[1] // 2
    zero = jnp.zeros_like(x)
    return jnp.concatenate([jnp.where(first, x, zero), jnp.where(first, zero, x)], axis=0)


class _PairTimeMasks:
    def __init__(self, c_len):
        row = lax.broadcasted_iota(jnp.int32, (c_len, 2 * c_len), 0)
        lane = lax.broadcasted_iota(jnp.int32, (c_len, 2 * c_len), 1)
        col = jnp.where(lane < c_len, lane, lane - c_len)
        self.incl = row >= col
        self.strict = row > col
        self.eye = (row == col).astype(F32)
        self.first = lax.broadcasted_iota(jnp.int32, (1, 2 * c_len), 1) < c_len


def _neumann_levels(n_rows):
    return max(0, math.ceil(math.log2(n_rows)) - 1)


def _inproj_body(x_ref, g_ref, w_ref, zdn_ref, zrw_ref, zrt_ref):
    h = (_rms(x_ref[...], NORM_EPS) * g_ref[...]).astype(BF16)
    zdn_ref[...] = _dot(h, w_ref[:, 0:DN_SLAB], None)
    zrw_ref[...] = _dot(h, w_ref[:, DN_SLAB:DN_SLAB + RW_SLAB], None)
    zrt_ref[...] = _dot(h, w_ref[:, DN_SLAB + RW_SLAB:P_SLABS], None)


def _inproj(x, gain, w):
    n = x.shape[0]
    tm = min(512, n)
    return pl.pallas_call(
        _inproj_body,
        grid=(n // tm,),
        in_specs=[pl.BlockSpec((tm, D_MODEL), lambda i: (i, 0)),
                  pl.BlockSpec((1, D_MODEL), lambda i: (0, 0)),
                  pl.BlockSpec((D_MODEL, P_SLABS), lambda i: (0, 0))],
        out_specs=[pl.BlockSpec((tm, DN_SLAB), lambda i: (i, 0)),
                   pl.BlockSpec((tm, RW_SLAB), lambda i: (i, 0)),
                   pl.BlockSpec((tm, RT_SLAB), lambda i: (i, 0))],
        out_shape=[jax.ShapeDtypeStruct((n, DN_SLAB), F32),
                   jax.ShapeDtypeStruct((n, RW_SLAB), F32),
                   jax.ShapeDtypeStruct((n, RT_SLAB), F32)],
        compiler_params=pltpu.CompilerParams(
            dimension_semantics=("arbitrary",), vmem_limit_bytes=VMEM_LIMIT),
        name="inproj",
    )(x, gain, w)


def _seq_specs(n_par, c_len, width):
    return pl.BlockSpec((n_par, c_len, width), lambda b, c: (b, c, 0))


def _state_spec(n_par, shape):
    return pl.BlockSpec((n_par,) + shape, lambda b, c: (b,) + (0,) * len(shape))


def _layer_state_spec(layer, n_par, shape):
    return pl.BlockSpec((None, n_par) + shape, lambda b, c: (layer, b) + (0,) * len(shape))


def _const_spec(a):
    return pl.BlockSpec(a.shape, lambda b, c: (0,) * a.ndim)


def _dn_body(c_len, n_valid, fresh, n_par, *refs):
    if fresh:
        z_ref, cw_ref, hp_ref, nw_ref, o_ref, so_ref, xe, s_scr, o_raw = refs
    else:
        z_ref, c0_ref, s0_ref, cw_ref, hp_ref, nw_ref, o_ref, so_ref, xe, s_scr, o_raw = refs
    ci = pl.program_id(1)

    @pl.when(ci == 0)
    def _():
        xe[:, 0:TAIL, :] = jnp.zeros((n_par, TAIL, DN_CONV), F32)
        s_scr[...] = jnp.zeros_like(s_scr)
        if not fresh:
            xe[:, TAIL - (CONV_W - 1):TAIL, :] = c0_ref[...]
            for p in range(DN_HEADS // 2):
                s_scr[:, p, 0:HEAD, 0:HEAD] = s0_ref[:, 2 * p]
                s_scr[:, p, HEAD:LANE, HEAD:LANE] = s0_ref[:, 2 * p + 1]

    tri_f = _tri(c_len)[0].astype(F32)
    live = lax.broadcasted_iota(jnp.int32, (c_len, LANE), 0) < n_valid
    neg_a = -jnp.exp(hp_ref[0:1, :])
    dt_bias = hp_ref[1:2, :]

    ones_blk = _group_ones()
    eye_l = _tri(LANE)[2].astype(F32)
    eye_8 = eye_l[0:8, :]

    def prepare():
        seqs = []
        for i in range(n_par):
            x = z_ref[i, :, 0:DN_CONV]
            xe[i, TAIL:TAIL + c_len, :] = x
            acc = x * cw_ref[CONV_W - 1:CONV_W, :]
            for j in range(CONV_W - 1):
                off = TAIL - (CONV_W - 1) + j
                acc = acc + xe[i, off:off + c_len, :] * cw_ref[j:j + 1, :]
            xe[i, 0:TAIL, :] = xe[i, c_len:c_len + TAIL, :]
            ab = z_ref[i, :, DN_CONV + DN_QK:DN_SLAB]
            g_tok = neg_a * _softplus(ab + dt_bias)
            beta = jax.nn.sigmoid(ab)
            if n_valid < c_len:
                g_tok = jnp.where(live, g_tok, 0.0)
                beta = jnp.where(live, beta, 0.0)
            conv = _silu(acc)
            seqs.append(dict(conv=conv, v=conv[:, 2 * DN_QK:DN_CONV], beta=beta, g_tok=g_tok,
                             gate=_silu(z_ref[i, :, DN_CONV:DN_CONV + DN_QK])))
        for sq in seqs:
            sq["g_cum"] = _select_sum(tri_f, sq["g_tok"])
            qk = sq["conv"][:, 0:2 * DN_QK]
            sq["qk"] = qk * lax.rsqrt(_group_sum(qk * qk, ones_blk) + 1e-6)
        for sq in seqs:
            sq["g_cum_t"] = _select_sum_nt(eye_8, sq["g_cum"])
        return seqs

    def advance(seqs):
        pt = _PairTimeMasks(c_len)
        first_h = lax.broadcasted_iota(jnp.int32, (1, LANE), 1) < HEAD
        first_rows = lax.broadcasted_iota(jnp.int32, (LANE, 1), 0) < HEAD
        chains = []
        for i, sq in enumerate(seqs):
            for p in range(DN_HEADS // 2):
                ha, hb = 2 * p, 2 * p + 1
                lo = LANE * p
                k = sq["qk"][:, DN_QK + lo:DN_QK + lo + LANE]
                v = sq["v"][:, lo:lo + LANE]
                gc_a, gc_b = sq["g_cum"][:, ha:ha + 1], sq["g_cum"][:, hb:hb + 1]
                gc_row = _cols(sq["g_cum_t"][ha:ha + 1, :], sq["g_cum_t"][hb:hb + 1, :])
                g_end = jnp.where(pt.first, gc_a[c_len - 1:c_len], gc_b[c_len - 1:c_len])
                b = jnp.where(first_h, sq["beta"][:, DN_HEADS + ha:DN_HEADS + ha + 1],
                              sq["beta"][:, DN_HEADS + hb:DN_HEADS + hb + 1])
                egc = jnp.exp(jnp.where(first_h, gc_a, gc_b))
                dmask = jnp.where(
                    pt.incl, jnp.exp(jnp.minimum(jnp.where(pt.first, gc_a, gc_b) - gc_row, 0.0)), 0.0)
                q = sq["qk"][:, lo:lo + LANE] * (HEAD ** -0.5)
                kb = k * b
                chains.append(dict(
                    i=i, p=p, k=k, kb=kb, q=q, qe=q * egc, vb=v * b, kbe=kb * egc, dmask=dmask,
                    k_decay=jnp.exp(g_end - gc_row),
                    s_decay=jnp.exp(jnp.where(first_rows, gc_a[c_len - 1:c_len],
                                              gc_b[c_len - 1:c_len]))))
        for c in chains:
            c["kq"] = _mm_nt(_rows(c["kb"], c["q"], eye_l), _bd(c["k"]))
        for c in chains:
            c["np"] = -jnp.where(pt.strict, c["kq"][0:c_len] * c["dmask"], 0.0)
            c["t"] = pt.eye + c["np"]
        for _ in range(_neumann_levels(n_valid)):
            for c in chains:
                c["np"] = _mm(c["np"], _bd(c["np"])).astype(BF16)
                c["np_bd"] = _bd(c["np"])
            for c in chains:
                c["t"] = c["t"] + _mm(c["t"], c["np_bd"])
        for c in chains:
            c["tx"] = _mm(c["t"], _cols(_bd(c["vb"]), _bd(c["kbe"])))
        for c in chains:
            qk = c["kq"][c_len:2 * c_len] * c["dmask"]
            kd_t = c["kq"][2 * c_len:2 * c_len + LANE] * c["k_decay"]
            c["w"] = _mm(_rows(qk, kd_t), _cols(_bd(c["tx"][:, 0:LANE]), _bd(c["tx"][:, LANE:2 * LANE])))
        for c in chains:
            w = c["w"]
            c["s"] = s_scr[c["i"], c["p"]]
            c["res"] = _mm(_rows(c["qe"] - w[0:c_len, LANE:2 * LANE],
                                 w[c_len:c_len + LANE, LANE:2 * LANE]), c["s"])
        for c in chains:
            w, res = c["w"], c["res"]
            lo = LANE * c["p"]
            s_scr[c["i"], c["p"]] = (c["s"] * c["s_decay"] - res[c_len:c_len + LANE]
                                     + w[c_len:c_len + LANE, 0:LANE])
            o_raw[c["i"], :, lo:lo + LANE] = res[0:c_len] + w[0:c_len, 0:LANE]
        for i, sq in enumerate(seqs):
            o = o_raw[i]
            ms = _group_sum(o * o, ones_blk, terms=1) * (1.0 / HEAD)
            o_ref[i] = o * lax.rsqrt(ms + NORM_EPS) * nw_ref[...] * sq["gate"]

    advance(prepare())

    @pl.when(ci == pl.num_programs(1) - 1)
    def _():
        for p in range(DN_HEADS // 2):
            so_ref[:, 2 * p] = s_scr[:, p, 0:HEAD, 0:HEAD]
            so_ref[:, 2 * p + 1] = s_scr[:, p, HEAD:LANE, HEAD:LANE]


def _dn_group(z, conv0, s0, conv_w, head_par, norm_w, layer, c_len, n_valid, n_par):
    n_seq, t_pad, _ = z.shape
    fresh = s0 is None
    nc = t_pad // c_len
    assert n_valid == c_len or nc == 1
    state = (DN_HEADS, HEAD, HEAD)
    in_specs = [_seq_specs(n_par, c_len, DN_SLAB)]
    args = [z]
    if not fresh:
        in_specs += [_layer_state_spec(layer, n_par, (CONV_W - 1, DN_CONV)),
                     _layer_state_spec(layer, n_par, state)]
        args += [conv0, s0]
    for p in (conv_w, head_par, norm_w):
        in_specs.append(_const_spec(p))
        args.append(p)
    return pl.pallas_call(
        functools.partial(_dn_body, c_len, n_valid, fresh, n_par),
        grid=(n_seq // n_par, nc),
        in_specs=in_specs,
        out_specs=[_seq_specs(n_par, c_len, DN_QK), _state_spec(n_par, state)],
        out_shape=[jax.ShapeDtypeStruct((n_seq, t_pad, DN_QK), F32),
                   jax.ShapeDtypeStruct((n_seq,) + state, F32)],
        scratch_shapes=[pltpu.VMEM((n_par, c_len + TAIL, DN_CONV), F32),
                        pltpu.VMEM((n_par, DN_HEADS // 2, LANE, LANE), F32),
                        pltpu.VMEM((n_par, c_len, DN_QK), F32)],
        compiler_params=pltpu.CompilerParams(
            dimension_semantics=("arbitrary", "arbitrary"), vmem_limit_bytes=VMEM_LIMIT),
        name="dn_group",
    )(*args)


def _rw_body(c_len, n_valid, fresh, n_par, *refs):
    if fresh:
        (z_ref, mu_ref, w0_ref, w2_ref, a0_ref, a2_ref, g2_ref, kk_ref, ka_ref, rk_ref,
         lng_ref, lnb_ref, o_ref, so_ref, xe, s_scr, y_raw) = refs
    else:
        (z_ref, sh0_ref, s0_ref, mu_ref, w0_ref, w2_ref, a0_ref, a2_ref, g2_ref, kk_ref, ka_ref,
         rk_ref, lng_ref, lnb_ref, o_ref, so_ref, xe, s_scr, y_raw) = refs
    ci = pl.program_id(1)

    @pl.when(ci == 0)
    def _():
        xe[:, 0:TAIL, :] = jnp.zeros((n_par, TAIL, RW_SLAB), F32)
        s_scr[...] = jnp.zeros_like(s_scr)
        if not fresh:
            xe[:, TAIL - 1:TAIL, :] = sh0_ref[...]
            for p in range(RW_HEADS // 2):
                s_scr[:, p, 0:HEAD, 0:HEAD] = s0_ref[:, 2 * p]
                s_scr[:, p, HEAD:LANE, HEAD:LANE] = s0_ref[:, 2 * p + 1]

    tri_f = _tri(c_len)[0].astype(F32)
    live = lax.broadcasted_iota(jnp.int32, (c_len, RW_W), 0) < n_valid
    ones_blk = _group_ones()
    eye_l = _tri(LANE)[2].astype(F32)

    seqs = []
    for i in range(n_par):
        x = z_ref[i]
        xe[i, TAIL:TAIL + c_len, :] = x
        prev = xe[i, TAIL - 1:TAIL - 1 + c_len, :]
        zs = x + (prev - x) * mu_ref[...]
        xe[i, 0:TAIL, :] = xe[i, c_len:c_len + TAIL, :]
        seqs.append(dict(r=zs[:, 0:RW_W], k=zs[:, RW_W:2 * RW_W], v=zs[:, 2 * RW_W:3 * RW_W],
                         wl=zs[:, 3 * RW_W:3 * RW_W + 64], al=zs[:, 3 * RW_W + 64:3 * RW_W + 128],
                         gl=zs[:, 3 * RW_W + 128:RW_SLAB]))
    for sq in seqs:
        sq["w_lora"] = _mm(jnp.tanh(sq["wl"]), w2_ref[...])
        sq["a_lora"] = _mm(sq["al"], a2_ref[...])
        sq["g"] = _mm(jax.nn.sigmoid(sq["gl"]), g2_ref[...])
    for sq in seqs:
        w_log = -_softplus(-(w0_ref[...] + sq["w_lora"])) - 0.5
        a = jax.nn.sigmoid(a0_ref[...] + sq["a_lora"])
        kk_in = sq["k"] * kk_ref[...]
        k2 = sq["k"] * (1.0 + (a - 1.0) * ka_ref[...])
        lw = -jnp.exp(w_log)
        if n_valid < c_len:
            lw = jnp.where(live, lw, 0.0)
            k2 = jnp.where(live, k2, 0.0)
            kk_in = jnp.where(live, kk_in, 0.0)
        sq.update(a=a, kk_in=kk_in, k2=k2, lw=lw)
    for sq in seqs:
        sq["gcum"] = _select_sum(tri_f, sq["lw"])
        kk_in = sq["kk_in"]
        sq["kk"] = kk_in * lax.rsqrt(_group_sum(kk_in * kk_in, ones_blk) + 1e-6)
    for sq in seqs:
        gcum = sq["gcum"]
        e_neg = jnp.exp(-gcum)
        g_end = gcum[c_len - 1:c_len, :]
        e_rest = jnp.exp(g_end - gcum)
        bv = sq["kk"] * sq["a"]
        sq.update(at=-sq["kk"] * jnp.exp(gcum - sq["lw"]), bt=bv * e_neg, kt=sq["k2"] * e_neg,
                  rt=sq["r"] * jnp.exp(gcum), b_rest=bv * e_rest, k_rest=sq["k2"] * e_rest,
                  e_end=jnp.exp(g_end))

    pt = _PairTimeMasks(c_len)
    c2 = 2 * c_len
    chains = []
    for i, sq in enumerate(seqs):
        for p in range(RW_HEADS // 2):
            sl = slice(LANE * p, LANE * (p + 1))
            chains.append(dict(i=i, p=p, sl=sl, at=sq["at"][:, sl], rt=sq["rt"][:, sl],
                               bt_bd=_bd(sq["bt"][:, sl]), kt_bd=_bd(sq["kt"][:, sl]),
                               v_bd=_bd(sq["v"][:, sl]), e_end=sq["e_end"][:, sl],
                               rest=_rows(_bd(sq["b_rest"][:, sl]), _bd(sq["k_rest"][:, sl]))))
    for c in chains:
        c["at_bd"] = _bd(c["at"])
        c["prod"] = _mm_nt(_rows(c["at"], c["rt"]), _rows(c["bt_bd"], c["kt_bd"]))
        c["tr"] = _mm_nt(eye_l, _rows(c["at_bd"], c["v_bd"]))
    for c in chains:
        prod = c["prod"]
        c["np"] = jnp.where(pt.strict, prod[0:c_len, 0:c2], 0.0)
        c["t"] = pt.eye + c["np"]
        c["nak"] = jnp.where(pt.strict, prod[0:c_len, c2:2 * c2], 0.0)
        c["mix"] = _cols(jnp.where(pt.incl, prod[c_len:c2, 0:c2], 0.0),
                         jnp.where(pt.incl, prod[c_len:c2, c2:2 * c2], 0.0))
        c["at_t"] = c["tr"][:, 0:c2]
        c["v_t"] = c["tr"][:, c2:2 * c2]
    for c in chains:
        c["nakv"] = _mm(c["nak"], c["v_bd"])
        c["nakv_t"] = _mm_nt(c["v_t"], _bd(c["nak"]))
    for _ in range(_neumann_levels(n_valid)):
        for c in chains:
            c["np"] = _mm(c["np"], _bd(c["np"])).astype(BF16)
            c["np_bd"] = _bd(c["np"])
        for c in chains:
            c["t"] = c["t"] + _mm(c["t"], c["np_bd"])
    for c in chains:
        c["tx"] = _mm(c["t"], _cols(c["at_bd"], _bd(c["nakv"])))
        c["tx_t"] = _mm_nt(_rows(c["at_t"], c["nakv_t"]), _bd(c["t"]))
    for c in chains:
        tx, tx_t = c["tx"], c["tx_t"]
        c["zy"] = _mm(c["mix"], _rows(_cols(_bd(tx[:, 0:LANE]), _bd(tx[:, LANE:2 * LANE])),
                                      _cols(jnp.zeros((c2, LANE), F32), c["v_bd"])))
        ta_t = tx_t[0:LANE]
        c["m"] = _mm(_rows(_cols(ta_t, jnp.zeros_like(ta_t)), _cols(tx_t[LANE:2 * LANE], c["v_t"])),
                     c["rest"])
    for c in chains:
        c["s"] = s_scr[c["i"], c["p"]]
        c["y"] = _mm_nt(c["rt"] + c["zy"][:, 0:LANE], c["s"])
        c["sm"] = _mm(c["s"], c["m"][0:LANE])
    for c in chains:
        s_scr[c["i"], c["p"]] = c["s"] * c["e_end"] + c["sm"] + c["m"][LANE:2 * LANE]
        y_raw[c["i"], :, c["sl"]] = c["y"] + c["zy"][:, LANE:2 * LANE]
    for i, sq in enumerate(seqs):
        y = y_raw[i]
        d = y - _group_sum(y, ones_blk) * (1.0 / HEAD)
        yn = d * lax.rsqrt(_group_sum(d * d, ones_blk, terms=1) * (1.0 / HEAD) + RW_GN_EPS)
        yn = yn * lng_ref[...] + lnb_ref[...]
        bonus = _group_sum(sq["r"] * sq["k2"] * rk_ref[...], ones_blk, terms=1) * sq["v"]
        o_ref[i] = (yn + bonus) * sq["g"]

    @pl.when(ci == pl.num_programs(1) - 1)
    def _():
        for p in range(RW_HEADS // 2):
            so_ref[:, 2 * p] = s_scr[:, p, 0:HEAD, 0:HEAD]
            so_ref[:, 2 * p + 1] = s_scr[:, p, HEAD:LANE, HEAD:LANE]


def _rw_group(z, shift0, s0, par, layer, c_len, n_valid, n_par):
    n_seq, t_pad, _ = z.shape
    fresh = s0 is None
    nc = t_pad // c_len
    assert n_valid == c_len or nc == 1
    state = (RW_HEADS, HEAD, HEAD)
    in_specs = [_seq_specs(n_par, c_len, RW_SLAB)]
    args = [z]
    if not fresh:
        in_specs += [_layer_state_spec(layer, n_par, (1, RW_SLAB)),
                     _layer_state_spec(layer, n_par, state)]
        args += [shift0, s0]
    for p in par:
        in_specs.append(_const_spec(p))
        args.append(p)
    return pl.pallas_call(
        functools.partial(_rw_body, c_len, n_valid, fresh, n_par),
        grid=(n_seq // n_par, nc),
        in_specs=in_specs,
        out_specs=[_seq_specs(n_par, c_len, RW_W), _state_spec(n_par, state)],
        out_shape=[jax.ShapeDtypeStruct((n_seq, t_pad, RW_W), F32),
                   jax.ShapeDtypeStruct((n_seq,) + state, F32)],
        scratch_shapes=[pltpu.VMEM((n_par, c_len + TAIL, RW_SLAB), F32),
                        pltpu.VMEM((n_par, RW_HEADS // 2, LANE, LANE), F32),
                        pltpu.VMEM((n_par, c_len, RW_W), F32)],
        compiler_params=pltpu.CompilerParams(
            dimension_semantics=("arbitrary", "arbitrary"), vmem_limit_bytes=VMEM_LIMIT),
        name="rw_group",
    )(*args)


def _rt_body(c_len, n_valid, fresh, n_par, *refs):
    if fresh:
        z_ref, cos_ref, sin_ref, lng_ref, o_ref, so_ref, s_scr, o_raw = refs
    else:
        z_ref, cos_ref, sin_ref, s0_ref, lng_ref, o_ref, so_ref, s_scr, o_raw = refs
    ci = pl.program_id(1)

    @pl.when(ci == 0)
    def _():
        if fresh:
            s_scr[...] = jnp.zeros_like(s_scr)
        else:
            s_scr[...] = s0_ref[...]

    lane = lax.broadcasted_iota(jnp.int32, (c_len, RT_QK), 1)
    first_half = (lane & (RT_DK - 1)) < (RT_DK // 2)
    cos = cos_ref[...]
    sin = sin_ref[...]

    def rotary(t):
        partner = jnp.where(first_half,
                            pltpu.roll(t, RT_QK - RT_DK // 2, axis=1),
                            pltpu.roll(t, RT_DK // 2, axis=1))
        return t * cos + partner * sin

    incl, _, _ = _tri(c_len)
    ri = lax.broadcasted_iota(jnp.int32, (c_len, c_len), 0)
    cj = lax.broadcasted_iota(jnp.int32, (c_len, c_len), 1)
    delta = (ri - cj).astype(F32)
    idx = lax.broadcasted_iota(jnp.int32, (c_len, 1), 0).astype(F32)
    log_decay = [math.log1p(-(2.0 ** (-5.0 - h))) for h in range(RT_HEADS)]
    dmasks = [jnp.where(incl, jnp.exp(delta * lg), 0.0) for lg in log_decay]
    q_scale = [jnp.exp((idx + 1.0) * lg) for lg in log_decay]
    t_row = lax.broadcasted_iota(jnp.int32, (1, c_len), 1).astype(F32)
    k_scale = [jnp.where(t_row < n_valid, jnp.exp((n_valid - 1.0 - t_row) * lg), 0.0) for lg in log_decay]
    ones_blk = _group_ones()
    eye_k = _tri(RT_DK)[2].astype(F32)

    chains = []
    for i in range(n_par):
        q = rotary(z_ref[i, :, 0:RT_QK])
        k = rotary(z_ref[i, :, RT_QK:2 * RT_QK]) * (RT_DK ** -0.5)
        for h in range(RT_HEADS):
            chains.append(dict(
                i=i, h=h, qh=q[:, RT_DK * h:RT_DK * (h + 1)], kh=k[:, RT_DK * h:RT_DK * (h + 1)],
                vh=z_ref[i, :, 2 * RT_QK + HEAD * h:2 * RT_QK + HEAD * (h + 1)]))
    for c in chains:
        c["qk"] = _mm_nt(_rows(c["qh"], eye_k), c["kh"])
    for c in chains:
        h = c["h"]
        c["s"] = s_scr[c["i"], h]
        c["cross"] = _mm(c["qh"] * q_scale[h], c["s"])
        c["kv"] = _mm(c["qk"][c_len:c_len + RT_DK] * k_scale[h], c["vh"])
        c["inner"] = _mm(c["qk"][0:c_len] * dmasks[h], c["vh"])
    for c in chains:
        i, h = c["i"], c["h"]
        s_scr[i, h] = c["s"] * math.exp(n_valid * log_decay[h]) + c["kv"]
        o_raw[i, :, HEAD * h:HEAD * (h + 1)] = c["inner"] + c["cross"]
    for i in range(n_par):
        o = o_raw[i]
        d = o - _group_sum(o, ones_blk) * (1.0 / HEAD)
        on = d * lax.rsqrt(_group_sum(d * d, ones_blk, terms=1) * (1.0 / HEAD) + RT_GN_EPS)
        o_ref[i] = on * lng_ref[...] * _silu(z_ref[i, :, 2 * RT_QK + RT_W:RT_SLAB])

    @pl.when(ci == pl.num_programs(1) - 1)
    def _():
        so_ref[...] = s_scr[...]


def _rt_group(z, cos, sin, s0, ln_g, layer, c_len, n_valid, n_par):
    n_seq, t_pad, _ = z.shape
    fresh = s0 is None
    nc = t_pad // c_len
    assert n_valid == c_len or nc == 1
    state = (RT_HEADS, RT_DK, HEAD)
    table = pl.BlockSpec((c_len, RT_QK), lambda b, c: (c, 0))
    in_specs = [_seq_specs(n_par, c_len, RT_SLAB), table, table]
    args = [z, cos, sin]
    if not fresh:
        in_specs.append(_layer_state_spec(layer, n_par, state))
        args.append(s0)
    in_specs.append(_const_spec(ln_g))
    args.append(ln_g)
    return pl.pallas_call(
        functools.partial(_rt_body, c_len, n_valid, fresh, n_par),
        grid=(n_seq // n_par, nc),
        in_specs=in_specs,
        out_specs=[_seq_specs(n_par, c_len, RT_W), _state_spec(n_par, state)],
        out_shape=[jax.ShapeDtypeStruct((n_seq, t_pad, RT_W), F32),
                   jax.ShapeDtypeStruct((n_seq,) + state, F32)],
        scratch_shapes=[pltpu.VMEM((n_par,) + state, F32),
                        pltpu.VMEM((n_par, c_len, RT_W), F32)],
        compiler_params=pltpu.CompilerParams(
            dimension_semantics=("arbitrary", "arbitrary"), vmem_limit_bytes=VMEM_LIMIT),
        name="rt_group",
    )(*args)


def _rotary_tables(pos0, t_pad):
    half = RT_DK // 2
    inv = ROPE_BASE ** (-np.arange(half, dtype=np.float64) / half)
    ang = (pos0 + np.arange(t_pad, dtype=np.float64))[:, None] * inv[None, :]
    cos = np.tile(np.concatenate([np.cos(ang), np.cos(ang)], axis=1), (1, RT_HEADS))
    sin = np.tile(np.concatenate([-np.sin(ang), np.sin(ang)], axis=1), (1, RT_HEADS))
    return jnp.asarray(cos, F32), jnp.asarray(sin, F32)


def _outproj_body(x_ref, odn_ref, orw_ref, ort_ref, w_ref, o_ref):
    acc = x_ref[...]
    acc = acc + _dot(odn_ref[...].astype(BF16), w_ref[0:DN_QK, :], None)
    acc = acc + _dot(orw_ref[...].astype(BF16), w_ref[DN_QK:DN_QK + RW_W, :], None)
    acc = acc + _dot(ort_ref[...].astype(BF16), w_ref[DN_QK + RW_W:D_MODEL, :], None)
    o_ref[...] = acc


def _outproj(x, o_dn, o_rw, o_rt, w):
    n = x.shape[0]
    tm = min(512, n)
    row = lambda i: (i, 0)
    return pl.pallas_call(
        _outproj_body,
        grid=(n // tm,),
        in_specs=[pl.BlockSpec((tm, D_MODEL), row),
                  pl.BlockSpec((tm, DN_QK), row),
                  pl.BlockSpec((tm, RW_W), row),
                  pl.BlockSpec((tm, RT_W), row),
                  pl.BlockSpec((D_MODEL, D_MODEL), lambda i: (0, 0))],
        out_specs=pl.BlockSpec((tm, D_MODEL), row),
        out_shape=jax.ShapeDtypeStruct((n, D_MODEL), F32),
        compiler_params=pltpu.CompilerParams(
            dimension_semantics=("arbitrary",), vmem_limit_bytes=VMEM_LIMIT),
        name="outproj",
    )(x, o_dn, o_rw, o_rt, w)


def _ffn_body(final_norm, *refs):
    if final_norm:
        x_ref, g_ref, w1_ref, w3_ref, w2_ref, gf_ref, o_ref, h_scr, acc = refs
    else:
        x_ref, g_ref, w1_ref, w3_ref, w2_ref, o_ref, h_scr, acc = refs
    f = pl.program_id(1)

    @pl.when(f == 0)
    def _():
        x = x_ref[...]
        h_scr[...] = (_rms(x, NORM_EPS) * g_ref[...]).astype(BF16)
        acc[...] = x

    hb = h_scr[...]
    u = _dot(hb, w1_ref[...], None)
    gate = _dot(hb, w3_ref[...], None)
    acc[...] += _dot((_silu(u) * gate).astype(BF16), w2_ref[...], None)

    @pl.when(f == pl.num_programs(1) - 1)
    def _():
        if final_norm:
            o_ref[...] = _rms(acc[...], NORM_EPS) * gf_ref[...]
        else:
            o_ref[...] = acc[...]


def _ffn(x, gain, w1, w3, w2, final_gain, tf):
    n = x.shape[0]
    d_f = w1.shape[1]
    final_norm = final_gain is not None
    tm = min(1024, n)
    in_specs = [pl.BlockSpec((tm, D_MODEL), lambda i, f: (i, 0)),
                pl.BlockSpec((1, D_MODEL), lambda i, f: (0, 0)),
                pl.BlockSpec((D_MODEL, tf), lambda i, f: (0, f)),
                pl.BlockSpec((D_MODEL, tf), lambda i, f: (0, f)),
                pl.BlockSpec((tf, D_MODEL), lambda i, f: (f, 0))]
    args = [x, gain, w1, w3, w2]
    if final_norm:
        in_specs.append(pl.BlockSpec((1, D_MODEL), lambda i, f: (0, 0)))
        args.append(final_gain)
    return pl.pallas_call(
        functools.partial(_ffn_body, final_norm),
        grid=(n // tm, d_f // tf),
        in_specs=in_specs,
        out_specs=pl.BlockSpec((tm, D_MODEL), lambda i, f: (i, 0)),
        out_shape=jax.ShapeDtypeStruct((n, D_MODEL), F32),
        scratch_shapes=[pltpu.VMEM((tm, D_MODEL), BF16), pltpu.VMEM((tm, D_MODEL), F32)],
        compiler_params=pltpu.CompilerParams(
            dimension_semantics=("arbitrary", "arbitrary"), vmem_limit_bytes=VMEM_LIMIT),
        name="ffn_dense",
    )(*args)


def _route_body(x_ref, g_ref, router_ref, o_ref):
    h = _rms(x_ref[...], NORM_EPS) * g_ref[...]
    logits = _dot(h, router_ref[...])
    lane = lax.broadcasted_iota(jnp.int32, logits.shape, 1).astype(F32)
    lg = jnp.where(lane < N_EXP, logits, -jnp.inf)
    m1 = jnp.max(lg, axis=-1, keepdims=True)
    i1 = jnp.min(jnp.where(lg == m1, lane, float(LANE)), axis=-1, keepdims=True)
    lg2 = jnp.where(lane == i1, -jnp.inf, lg)
    m2 = jnp.max(lg2, axis=-1, keepdims=True)
    i2 = jnp.min(jnp.where(lg2 == m2, lane, float(LANE)), axis=-1, keepdims=True)
    e2 = jnp.exp(m2 - m1)
    den = 1.0 + e2
    o_ref[...] = (jnp.where(lane == 0.0, i1, 0.0) + jnp.where(lane == 1.0, i2, 0.0)
                  + jnp.where(lane == 2.0, 1.0 / den, 0.0) + jnp.where(lane == 3.0, e2 / den, 0.0))


def _route(x, gain, router):
    n = x.shape[0]
    tm = min(512, n)
    return pl.pallas_call(
        _route_body,
        grid=(n // tm,),
        in_specs=[pl.BlockSpec((tm, D_MODEL), lambda i: (i, 0)),
                  pl.BlockSpec((1, D_MODEL), lambda i: (0, 0)),
                  pl.BlockSpec((D_MODEL, LANE), lambda i: (0, 0))],
        out_specs=pl.BlockSpec((tm, LANE), lambda i: (i, 0)),
        out_shape=jax.ShapeDtypeStruct((n, LANE), F32),
        compiler_params=pltpu.CompilerParams(
            dimension_semantics=("arbitrary",), vmem_limit_bytes=VMEM_LIMIT),
        name="moe_route",
    )(x, gain, router)


def _row_gather(src_hbm, idx_ref, buf, sem, slot, n_rows):
    def copy(r):
        return pltpu.make_async_copy(src_hbm.at[pl.ds(idx_ref[0, 0, r], 1), :],
                                     buf.at[slot, pl.ds(r, 1), :], sem.at[slot])

    def start():
        def body(r, carry):
            copy(r).start()
            return carry
        lax.fori_loop(0, n_rows, body, 0, unroll=8)

    def wait():
        def body(r, carry):
            copy(r).wait()
            return carry
        lax.fori_loop(0, n_rows, body, 0, unroll=8)

    return start, wait, copy


def _moe_ffn_body(tm, te_ref, nu_ref, tok_ref, tok_next_ref, x_hbm, g_ref, w1_ref, w3_ref, w2_ref,
                  ys_ref, xbuf, sem, h_scr, acc):
    i = pl.program_id(0)
    f = pl.program_id(1)
    n_f = pl.num_programs(1)
    slot = lax.rem(i, 2)
    n_used = nu_ref[0]
    used = i < n_used
    rows_per_step = tm // MOE_F_STEPS
    start_cur, wait_cur, _ = _row_gather(x_hbm, tok_ref, xbuf, sem, slot, tm)
    _, wait_next, copy_next = _row_gather(x_hbm, tok_next_ref, xbuf, sem, 1 - slot, tm)

    @pl.when((f == 0) & (i <= n_used))
    def _():
        pl.when(i == 0)(start_cur)
        wait_cur()

    @pl.when((f == 0) & used)
    def _():
        h_scr[...] = (_rms(xbuf[slot], NORM_EPS) * g_ref[...]).astype(BF16)

    @pl.when(used)
    def _():
        hb = h_scr[...]
        u = _dot(hb, w1_ref[0], None)
        gate = _dot(hb, w3_ref[0], None)
        y = _dot((_silu(u) * gate).astype(BF16), w2_ref[0], None)
        for j in range(rows_per_step):
            copy_next(f * rows_per_step + j).start()

        @pl.when(f == 0)
        def _():
            acc[...] = y

        @pl.when(f > 0)
        def _():
            acc[...] += y

    @pl.when(f == n_f - 1)
    def _():
        ys_ref[...] = jnp.where(used, acc[...], 0.0)
        pl.when(used & (i == pl.num_programs(0) - 1))(wait_next)


def _moe_ffn(x, gain, tok, tile_expert, n_used, w1, w3, w2, tm, tf):
    n_tiles = tok.shape[0]
    n_f = D_EXP // tf

    def w_block(i, f, te, nu):
        last = nu[0] - 1
        return te[jnp.minimum(i, last)], jnp.where(i <= last, f, n_f - 1)

    def w13_idx(i, f, te, nu):
        e, ff = w_block(i, f, te, nu)
        return (e, 0, ff)

    def w2_idx(i, f, te, nu):
        e, ff = w_block(i, f, te, nu)
        return (e, ff, 0)

    smem_tile = lambda idx: pl.BlockSpec((1, 1, tm), idx, memory_space=pltpu.SMEM)
    grid_spec = pltpu.PrefetchScalarGridSpec(
        num_scalar_prefetch=2,
        grid=(n_tiles, n_f),
        in_specs=[smem_tile(lambda i, f, te, nu: (i, 0, 0)),
                  smem_tile(lambda i, f, te, nu: (jnp.minimum(i + 1, n_tiles - 1), 0, 0)),
                  pl.BlockSpec(memory_space=pl.ANY),
                  pl.BlockSpec((1, D_MODEL), lambda i, f, te, nu: (0, 0)),
                  pl.BlockSpec((1, D_MODEL, tf), w13_idx),
                  pl.BlockSpec((1, D_MODEL, tf), w13_idx),
                  pl.BlockSpec((1, tf, D_MODEL), w2_idx)],
        out_specs=pl.BlockSpec((tm, D_MODEL), lambda i, f, te, nu: (i, 0)),
        scratch_shapes=[pltpu.VMEM((2, tm, D_MODEL), F32),
                        pltpu.SemaphoreType.DMA((2,)),
                        pltpu.VMEM((tm, D_MODEL), BF16),
                        pltpu.VMEM((tm, D_MODEL), F32)])
    return pl.pallas_call(
        functools.partial(_moe_ffn_body, tm),
        grid_spec=grid_spec,
        out_shape=jax.ShapeDtypeStruct((n_tiles * tm, D_MODEL), F32),
        compiler_params=pltpu.CompilerParams(
            dimension_semantics=("arbitrary", "arbitrary"), vmem_limit_bytes=VMEM_LIMIT),
        name="moe_ffn",
    )(tile_expert, n_used, tok, tok, x, gain, w1, w3, w2)


def _moe_combine_body(tm, final_norm, *refs):
    if final_norm:
        rows_ref, rows_next_ref, x_ref, route_ref, ys_hbm, gf_ref, o_ref, ybuf, sem = refs
    else:
        rows_ref, rows_next_ref, x_ref, route_ref, ys_hbm, o_ref, ybuf, sem = refs
    i = pl.program_id(0)
    slot = lax.rem(i, 2)
    start_next, _, _ = _row_gather(ys_hbm, rows_next_ref, ybuf, sem, 1 - slot, 2 * tm)
    start_cur, wait_cur, _ = _row_gather(ys_hbm, rows_ref, ybuf, sem, slot, 2 * tm)
    pl.when(i == 0)(start_cur)
    pl.when(i + 1 < pl.num_programs(0))(start_next)
    wait_cur()
    route = route_ref[...]
    out = x_ref[...] + (route[:, 2:3] * ybuf[slot, 0:tm, :] + route[:, 3:4] * ybuf[slot, tm:2 * tm, :])
    if final_norm:
        out = _rms(out, NORM_EPS) * gf_ref[...]
    o_ref[...] = out


def _moe_combine(x, route, rows, ys, final_gain, tm):
    n = x.shape[0]
    n_tiles = n // tm
    final_norm = final_gain is not None
    smem_tile = lambda idx: pl.BlockSpec((1, 1, 2 * tm), idx, memory_space=pltpu.SMEM)
    in_specs = [smem_tile(lambda i: (i, 0, 0)),
                smem_tile(lambda i: (jnp.minimum(i + 1, n_tiles - 1), 0, 0)),
                pl.BlockSpec((tm, D_MODEL), lambda i: (i, 0)),
                pl.BlockSpec((tm, LANE), lambda i: (i, 0)),
                pl.BlockSpec(memory_space=pl.ANY)]
    args = [rows, rows, x, route, ys]
    if final_norm:
        in_specs.append(pl.BlockSpec((1, D_MODEL), lambda i: (0, 0)))
        args.append(final_gain)
    return pl.pallas_call(
        functools.partial(_moe_combine_body, tm, final_norm),
        grid=(n_tiles,),
        in_specs=in_specs,
        out_specs=pl.BlockSpec((tm, D_MODEL), lambda i: (i, 0)),
        out_shape=jax.ShapeDtypeStruct((n, D_MODEL), F32),
        scratch_shapes=[pltpu.VMEM((2, 2 * tm, D_MODEL), F32), pltpu.SemaphoreType.DMA((2,))],
        compiler_params=pltpu.CompilerParams(
            dimension_semantics=("arbitrary",), vmem_limit_bytes=VMEM_LIMIT),
        name="moe_combine",
    )(*args)


def _moe(x, gain, router, w1, w3, w2, final_gain):
    n = x.shape[0]
    tm = 512 if n >= 8192 else 256
    tm_c = 256
    n_tiles = -(-(2 * n + N_EXP * (tm - 1)) // tm)
    route = _route(x, gain, router)
    i1 = route[:, 0].astype(jnp.int32)
    i2 = route[:, 1].astype(jnp.int32)
    experts = jnp.arange(N_EXP, dtype=jnp.int32)
    hit = ((i1[:, None] == experts) | (i2[:, None] == experts)).astype(jnp.int32)
    rank = jnp.cumsum(hit, axis=0) - hit
    tiles_e = (jnp.sum(hit, axis=0) + tm - 1) // tm
    tile_end = jnp.cumsum(tiles_e)
    row_off = (tile_end - tiles_e) * tm
    row1 = row_off[i1] + jnp.take_along_axis(rank, i1[:, None], axis=1)[:, 0]
    row2 = row_off[i2] + jnp.take_along_axis(rank, i2[:, None], axis=1)[:, 0]
    token = jnp.arange(n, dtype=jnp.int32)
    tok = jnp.zeros((n_tiles * tm,), jnp.int32).at[jnp.concatenate([row1, row2])].set(
        jnp.concatenate([token, token]), unique_indices=True)
    tile_ids = jnp.arange(n_tiles, dtype=jnp.int32)
    tile_expert = jnp.minimum(
        jnp.sum((tile_end[None, :] <= tile_ids[:, None]).astype(jnp.int32), axis=1), N_EXP - 1)
    n_used = tile_end[N_EXP - 1:].astype(jnp.int32)
    ys = _moe_ffn(x, gain, tok.reshape(n_tiles, 1, tm), tile_expert, n_used, w1, w3, w2,
                  tm, D_EXP // MOE_F_STEPS)
    rows = jnp.concatenate([row1.reshape(n // tm_c, 1, tm_c), row2.reshape(n // tm_c, 1, tm_c)],
                           axis=2).astype(jnp.int32)
    return _moe_combine(x, route, rows, ys, final_gain, tm_c)


def _pad_last(a, width):
    return jnp.pad(a, [(0, 0)] * (a.ndim - 1) + [(0, width - a.shape[-1])])


def _pack_params(p):
    w_in = p["w_in"]
    w_in = jnp.concatenate([_pad_last(w_in[..., :DN_P], DN_SLAB),
                            _pad_last(w_in[..., DN_P:DN_P + RW_P], RW_SLAB),
                            w_in[..., DN_P + RW_P:]], axis=-1).astype(BF16)
    head_par = jnp.stack([_pad_last(p["dn_a_log"], LANE), _pad_last(p["dn_dt_bias"], LANE)], axis=1)
    g2 = jnp.pad(p["rw_g2"], ((0, 0), (0, RW_GATE_PAD - RW_GATE_R), (0, 0)))
    row = lambda a: a[:, None, :]
    return dict(
        ln_mix=row(p["ln_mix"]), w_in=w_in, dn_conv=p["dn_conv"], dn_head=head_par,
        dn_norm=row(jnp.tile(p["dn_norm"], (1, DN_HEADS))),
        rw=[(row(_pad_last(p["rw_mu"], RW_SLAB))[l], row(p["rw_w0"])[l], p["rw_w2"][l],
             row(p["rw_a0"])[l], p["rw_a2"][l], g2[l], row(p["rw_kk"])[l], row(p["rw_ka"])[l],
             p["rw_rk"].reshape(N_LAYERS, 1, RW_W)[l], row(p["rw_ln_g"])[l], row(p["rw_ln_b"])[l])
            for l in range(N_LAYERS)],
        rt_ln=row(p["rt_ln"]), w_out=p["w_out"].astype(BF16), ln_ffn=row(p["ln_ffn"]),
        ffn_w1=p["ffn_w1"].astype(BF16), ffn_w3=p["ffn_w3"].astype(BF16),
        ffn_w2=p["ffn_w2"].astype(BF16),
        moe_router=_pad_last(p["moe_router"], LANE),
        moe_w1=p["moe_w1"].astype(BF16), moe_w3=p["moe_w3"].astype(BF16),
        moe_w2=p["moe_w2"].astype(BF16),
        ln_final=p["ln_final"][None, :],
    )


def _trunk(x, n_seq, t_real, t_pad, c_len, n_par, states, pos0, w):
    n_valid = min(c_len, t_real)
    cos, sin = _rotary_tables(pos0, t_pad)
    convs, dns, shifts, rws, rts = [], [], [], [], []
    seq = lambda a: a.reshape(n_seq, t_pad, a.shape[-1])
    flat = lambda a: a.reshape(n_seq * t_pad, a.shape[-1])
    for l in range(N_LAYERS):
        z_dn, z_rw, z_rt = (seq(z) for z in _inproj(x, w["ln_mix"][l], w["w_in"][l]))
        if states is None:
            conv0 = s_dn0 = shift0 = s_rw0 = s_rt0 = None
        else:
            conv0, s_dn0, s_rw0, s_rt0 = states[0], states[1], states[3], states[4]
            shift0 = _pad_last(states[2], RW_SLAB)[:, :, None, :]
        o_dn, s_dn = _dn_group(z_dn, conv0, s_dn0, w["dn_conv"][l], w["dn_head"][l],
                               w["dn_norm"][l], l, c_len, n_valid, n_par)
        o_rw, s_rw = _rw_group(z_rw, shift0, s_rw0, w["rw"][l], l, c_len, n_valid, n_par)
        o_rt, s_rt = _rt_group(z_rt, cos, sin, s_rt0, w["rt_ln"][l], l, c_len, n_valid, n_par)
        x = _outproj(x, flat(o_dn), flat(o_rw), flat(o_rt), w["w_out"][l])
        if l % 2 == 0:
            x = _ffn(x, w["ln_ffn"][l], w["ffn_w1"][l // 2], w["ffn_w3"][l // 2], w["ffn_w2"][l // 2],
                     w["ln_final"] if l == N_LAYERS - 1 else None, D_FF // 2)
        else:
            xr = seq(x)[:, :t_real].reshape(n_seq * t_real, D_MODEL)
            xr = _moe(xr, w["ln_ffn"][l], w["moe_router"][l // 2], w["moe_w1"][l // 2],
                      w["moe_w3"][l // 2], w["moe_w2"][l // 2],
                      w["ln_final"] if l == N_LAYERS - 1 else None)
            x = xr if l == N_LAYERS - 1 else flat(
                jnp.pad(xr.reshape(n_seq, t_real, D_MODEL), ((0, 0), (0, t_pad - t_real), (0, 0))))
        if t_real >= CONV_W - 1:
            convs.append(z_dn[:, t_real - (CONV_W - 1):t_real, :DN_CONV])
        else:
            convs.append(jnp.concatenate([conv0[l][:, t_real:], z_dn[:, :t_real, :DN_CONV]], axis=1))
        shifts.append(z_rw[:, t_real - 1, :RW_P])
        dns.append(s_dn)
        rws.append(s_rw)
        rts.append(s_rt)
    return x, jnp.stack(convs), jnp.stack(dns), jnp.stack(shifts), jnp.stack(rws), jnp.stack(rts)


def kernel(x_prompt, x_sample, state_dn_conv, state_dn, state_rw_shift, state_rw, state_rt, ln_mix, w_in, dn_conv, dn_a_log, dn_dt_bias, dn_norm, rw_mu, rw_w0, rw_w2, rw_a0, rw_a2, rw_g2, rw_kk, rw_ka, rw_rk, rw_ln_g, rw_ln_b, rt_ln, w_out, ln_ffn, ffn_w1, ffn_w3, ffn_w2, moe_router, moe_w1, moe_w3, moe_w2, ln_final):
    w = _pack_params(dict(
        ln_mix=ln_mix, w_in=w_in, dn_conv=dn_conv, dn_a_log=dn_a_log, dn_dt_bias=dn_dt_bias,
        dn_norm=dn_norm, rw_mu=rw_mu, rw_w0=rw_w0, rw_w2=rw_w2, rw_a0=rw_a0, rw_a2=rw_a2,
        rw_g2=rw_g2, rw_kk=rw_kk, rw_ka=rw_ka, rw_rk=rw_rk, rw_ln_g=rw_ln_g, rw_ln_b=rw_ln_b,
        rt_ln=rt_ln, w_out=w_out, ln_ffn=ln_ffn, ffn_w1=ffn_w1, ffn_w3=ffn_w3, ffn_w2=ffn_w2,
        moe_router=moe_router, moe_w1=moe_w1, moe_w3=moe_w3, moe_w2=moe_w2, ln_final=ln_final))
    bp, tp, _ = x_prompt.shape
    bs, ts, _ = x_sample.shape

    yp, *p_states = _trunk(x_prompt.reshape(bp * tp, D_MODEL), bp, tp, tp, 64, 4, None, 0, w)
    y_prompt = yp.reshape(bp, tp, D_MODEL)

    ts_pad = 8
    xs = jnp.pad(x_sample, ((0, 0), (0, ts_pad - ts), (0, 0))).reshape(bs * ts_pad, D_MODEL)
    ys, *s_states = _trunk(xs, bs, ts, ts_pad, ts_pad, 8,
                           (state_dn_conv, state_dn, state_rw_shift, state_rw, state_rt),
                           PAST_LEN, w)
    y_sample = ys.reshape(bs, -1, D_MODEL)[:, :ts]
    return (y_prompt, y_sample, *p_states, *s_states)
```

```python
import functools
import math

import numpy as np
import jax
import jax.numpy as jnp
from jax import lax
from jax.experimental import pallas as pl
from jax.experimental.pallas import tpu as pltpu

F32 = jnp.float32
BF16 = jnp.bfloat16

D_MODEL = 1024
N_LAYERS = 2
PAST_LEN = 16384
HEAD = 64
DN_HEADS = 6
DN_QK = DN_HEADS * HEAD
DN_CONV = 3 * DN_QK
CONV_W = 4
RW_HEADS = 6
RW_W = RW_HEADS * HEAD
RW_GATE_R = 160
RT_HEADS = 4
RT_DK = 32
RT_QK = RT_HEADS * RT_DK
RT_W = RT_HEADS * HEAD
DN_P = DN_CONV + DN_QK + 2 * DN_HEADS
RW_P = 3 * RW_W + 64 + 64 + RW_GATE_R
RT_P = 2 * RT_QK + 2 * RT_W
LANE = 128
DN_SLAB = 1664
RW_SLAB = 1536
RT_SLAB = RT_P
P_SLABS = DN_SLAB + RW_SLAB + RT_SLAB
RW_GATE_PAD = RW_SLAB - (3 * RW_W + 128)
N_EXP = 8
D_FF = 2816
D_EXP = 3584
MOE_F_STEPS = 2
ROPE_BASE = 10000.0
NORM_EPS = 1e-6
RW_GN_EPS = 64e-5
RT_GN_EPS = 1e-5
TAIL = 8
VMEM_LIMIT = 56 * 1024 * 1024


def _dot(a, b, precision=None):
    return jnp.dot(a, b, preferred_element_type=F32, precision=precision)


def _split3(x):
    x1 = x.astype(BF16)
    r1 = x - x1.astype(F32)
    x2 = r1.astype(BF16)
    x3 = (r1 - x2.astype(F32)).astype(BF16)
    return x1, x2, x3


def _select_sum(sel, x):
    sel = sel.astype(BF16)
    return sum(jnp.dot(sel, p, preferred_element_type=F32) for p in _split3(x))


def _select_sum_nt(sel, x):
    sel = sel.astype(BF16)
    return sum(lax.dot_general(sel, p, (((1,), (1,)), ((), ())), preferred_element_type=F32)
               for p in _split3(x))


def _mm(a, b):
    return jnp.dot(a.astype(BF16), b.astype(BF16), preferred_element_type=F32)


def _mm_nt(a, b):
    return lax.dot_general(a.astype(BF16), b.astype(BF16), (((1,), (1,)), ((), ())),
                           preferred_element_type=F32)


def _rows(*parts):
    return jnp.concatenate(parts, axis=0)


def _cols(*parts):
    return jnp.concatenate(parts, axis=1)


def _silu(x):
    return x * jax.nn.sigmoid(x)


def _softplus(x):
    return jnp.maximum(x, 0.0) + jnp.log(1.0 + jnp.exp(-jnp.abs(x)))


def _rms(x, eps):
    return x * lax.rsqrt(jnp.mean(x * x, axis=-1, keepdims=True) + eps)


def _tri(c):
    r = lax.broadcasted_iota(jnp.int32, (c, c), 0)
    col = lax.broadcasted_iota(jnp.int32, (c, c), 1)
    return r >= col, r > col, r == col


def _group_ones():
    r = lax.broadcasted_iota(jnp.int32, (LANE, LANE), 0)
    c = lax.broadcasted_iota(jnp.int32, (LANE, LANE), 1)
    return ((r < HEAD) == (c < HEAD)).astype(BF16)


def _group_sum(x, ones_blk, terms=2):
    parts = _split3(x)[:terms]
    out = [sum(jnp.dot(p[:, j:j + LANE], ones_blk, preferred_element_type=F32) for p in parts)
           for j in range(0, x.shape[1], LANE)]
    return out[0] if len(out) == 1 else jnp.concatenate(out, axis=1)


def _bd(x):
    x = x.astype(BF16)
    first = lax.broadcasted_iota(jnp.int32, x.shape, 1) < x.shape[1] // 2
    zero = jnp.zeros_like(x)
    return jnp.concatenate([jnp.where(first, x, zero), jnp.where(first, zero, x)], axis=0)


class _PairTimeMasks:
    def __init__(self, c_len):
        row = lax.broadcasted_iota(jnp.int32, (c_len, 2 * c_len), 0)
        lane = lax.broadcasted_iota(jnp.int32, (c_len, 2 * c_len), 1)
        col = jnp.where(lane < c_len, lane, lane - c_len)
        self.incl = row >= col
        self.strict = row > col
        self.eye = (row == col).astype(F32)
        self.first = lax.broadcasted_iota(jnp.int32, (1, 2 * c_len), 1) < c_len


def _neumann_levels(n_rows):
    return max(0, math.ceil(math.log2(n_rows)) - 1)


def _inproj_body(x_ref, g_ref, w_ref, zdn_ref, zrw_ref, zrt_ref):
    h = (_rms(x_ref[...], NORM_EPS) * g_ref[...]).astype(BF16)
    zdn_ref[...] = _dot(h, w_ref[:, 0:DN_SLAB], None)
    zrw_ref[...] = _dot(h, w_ref[:, DN_SLAB:DN_SLAB + RW_SLAB], None)
    zrt_ref[...] = _dot(h, w_ref[:, DN_SLAB + RW_SLAB:P_SLABS], None)


def _inproj(x, gain, w):
    n = x.shape[0]
    tm = min(512, n)
    return pl.pallas_call(
        _inproj_body,
        grid=(n // tm,),
        in_specs=[pl.BlockSpec((tm, D_MODEL), lambda i: (i, 0)),
                  pl.BlockSpec((1, D_MODEL), lambda i: (0, 0)),
                  pl.BlockSpec((D_MODEL, P_SLABS), lambda i: (0, 0))],
        out_specs=[pl.BlockSpec((tm, DN_SLAB), lambda i: (i, 0)),
                   pl.BlockSpec((tm, RW_SLAB), lambda i: (i, 0)),
                   pl.BlockSpec((tm, RT_SLAB), lambda i: (i, 0))],
        out_shape=[jax.ShapeDtypeStruct((n, DN_SLAB), F32),
                   jax.ShapeDtypeStruct((n, RW_SLAB), F32),
                   jax.ShapeDtypeStruct((n, RT_SLAB), F32)],
        compiler_params=pltpu.CompilerParams(
            dimension_semantics=("arbitrary",), vmem_limit_bytes=VMEM_LIMIT),
        name="inproj",
    )(x, gain, w)


def _seq_specs(n_par, c_len, width):
    return pl.BlockSpec((n_par, c_len, width), lambda b, c: (b, c, 0))


def _state_spec(n_par, shape):
    return pl.BlockSpec((n_par,) + shape, lambda b, c: (b,) + (0,) * len(shape))


def _stacked_state_out(layer, stacked, n_par, n_seq, state, in_specs, args):
    shape = jax.ShapeDtypeStruct((N_LAYERS, n_seq) + state, F32)
    zeros = (0,) * len(state)
    if layer == 0:
        return pl.BlockSpec((N_LAYERS, n_par) + state, lambda b, c: (0, b) + zeros), shape, {}
    in_specs.append(pl.BlockSpec(memory_space=pl.ANY))
    args.append(stacked)
    return (pl.BlockSpec((None, n_par) + state, lambda b, c: (layer, b) + zeros), shape,
            {len(args) - 1: 1})


def _own_layer(so_ref, layer):
    if layer > 0:
        return so_ref
    so_ref[1:] = jnp.zeros((so_ref.shape[0] - 1,) + so_ref.shape[1:], F32)
    return so_ref.at[0]


def _const_spec(a):
    return pl.BlockSpec(a.shape, lambda b, c: (0,) * a.ndim)


def _dn_body(c_len, n_valid, fresh, n_par, layer, *refs):
    refs = list(refs)
    if layer > 0:
        del refs[-6]
    if fresh:
        z_ref, cw_ref, hp_ref, nw_ref, o_ref, so_ref, xe, s_scr, o_raw = refs
    else:
        z_ref, c0_ref, s0_ref, cw_ref, hp_ref, nw_ref, o_ref, so_ref, xe, s_scr, o_raw = refs
    ci = pl.program_id(1)

    @pl.when(ci == 0)
    def _():
        xe[:, 0:TAIL, :] = jnp.zeros((n_par, TAIL, DN_CONV), F32)
        s_scr[...] = jnp.zeros_like(s_scr)
        if not fresh:
            xe[:, TAIL - (CONV_W - 1):TAIL, :] = c0_ref[...]
            for p in range(DN_HEADS // 2):
                s_scr[:, p, 0:HEAD, 0:HEAD] = s0_ref[:, 2 * p]
                s_scr[:, p, HEAD:LANE, HEAD:LANE] = s0_ref[:, 2 * p + 1]

    tri_f = _tri(c_len)[0].astype(F32)
    live = lax.broadcasted_iota(jnp.int32, (c_len, LANE), 0) < n_valid
    neg_a = -jnp.exp(hp_ref[0:1, :])
    dt_bias = hp_ref[1:2, :]

    ones_blk = _group_ones()
    eye_l = _tri(LANE)[2].astype(F32)
    eye_8 = eye_l[0:8, :]

    def prepare():
        seqs = []
        for i in range(n_par):
            x = z_ref[i, :, 0:DN_CONV]
            xe[i, TAIL:TAIL + c_len, :] = x
            acc = x * cw_ref[CONV_W - 1:CONV_W, :]
            for j in range(CONV_W - 1):
                off = TAIL - (CONV_W - 1) + j
                acc = acc + xe[i, off:off + c_len, :] * cw_ref[j:j + 1, :]
            xe[i, 0:TAIL, :] = xe[i, c_len:c_len + TAIL, :]
            ab = z_ref[i, :, DN_CONV + DN_QK:DN_SLAB]
            g_tok = neg_a * _softplus(ab + dt_bias)
            beta = jax.nn.sigmoid(ab)
            if n_valid < c_len:
                g_tok = jnp.where(live, g_tok, 0.0)
                beta = jnp.where(live, beta, 0.0)
            conv = _silu(acc)
            seqs.append(dict(conv=conv, v=conv[:, 2 * DN_QK:DN_CONV], beta=beta, g_tok=g_tok,
                             gate=_silu(z_ref[i, :, DN_CONV:DN_CONV + DN_QK])))
        for sq in seqs:
            sq["g_cum"] = _select_sum(tri_f, sq["g_tok"])
            qk = sq["conv"][:, 0:2 * DN_QK]
            sq["qk"] = qk * lax.rsqrt(_group_sum(qk * qk, ones_blk) + 1e-6)
        for sq in seqs:
            sq["g_cum_t"] = _select_sum_nt(eye_8, sq["g_cum"])
        return seqs

    def advance(seqs):
        pt = _PairTimeMasks(c_len)
        first_h = lax.broadcasted_iota(jnp.int32, (1, LANE), 1) < HEAD
        first_rows = lax.broadcasted_iota(jnp.int32, (LANE, 1), 0) < HEAD
        chains = []
        for i, sq in enumerate(seqs):
            for p in range(DN_HEADS // 2):
                ha, hb = 2 * p, 2 * p + 1
                lo = LANE * p
                k = sq["qk"][:, DN_QK + lo:DN_QK + lo + LANE]
                v = sq["v"][:, lo:lo + LANE]
                gc_a, gc_b = sq["g_cum"][:, ha:ha + 1], sq["g_cum"][:, hb:hb + 1]
                gc_row = _cols(sq["g_cum_t"][ha:ha + 1, :], sq["g_cum_t"][hb:hb + 1, :])
                g_end = jnp.where(pt.first, gc_a[c_len - 1:c_len], gc_b[c_len - 1:c_len])
                b = jnp.where(first_h, sq["beta"][:, DN_HEADS + ha:DN_HEADS + ha + 1],
                              sq["beta"][:, DN_HEADS + hb:DN_HEADS + hb + 1])
                egc = jnp.exp(jnp.where(first_h, gc_a, gc_b))
                dmask = jnp.where(
                    pt.incl, jnp.exp(jnp.minimum(jnp.where(pt.first, gc_a, gc_b) - gc_row, 0.0)), 0.0)
                q = sq["qk"][:, lo:lo + LANE] * (HEAD ** -0.5)
                kb = k * b
                chains.append(dict(
                    i=i, p=p, k=k, kb=kb, q=q, qe=q * egc, vb=v * b, kbe=kb * egc, dmask=dmask,
                    k_decay=jnp.exp(g_end - gc_row),
                    s_decay=jnp.exp(jnp.where(first_rows, gc_a[c_len - 1:c_len],
                                              gc_b[c_len - 1:c_len]))))
        for c in chains:
            c["kq"] = _mm_nt(_rows(c["kb"], c["q"], eye_l), _bd(c["k"]))
        for c in chains:
            c["np"] = -jnp.where(pt.strict, c["kq"][0:c_len] * c["dmask"], 0.0)
            c["t"] = pt.eye + c["np"]
        for _ in range(_neumann_levels(n_valid)):
            for c in chains:
                c["np"] = _mm(c["np"], _bd(c["np"])).astype(BF16)
                c["np_bd"] = _bd(c["np"])
            for c in chains:
                c["t"] = c["t"] + _mm(c["t"], c["np_bd"])
        for c in chains:
            c["tx"] = _mm(c["t"], _cols(_bd(c["vb"]), _bd(c["kbe"])))
        for c in chains:
            qk = c["kq"][c_len:2 * c_len] * c["dmask"]
            kd_t = c["kq"][2 * c_len:2 * c_len + LANE] * c["k_decay"]
            c["w"] = _mm(_rows(qk, kd_t), _cols(_bd(c["tx"][:, 0:LANE]), _bd(c["tx"][:, LANE:2 * LANE])))
        for c in chains:
            w = c["w"]
            c["s"] = s_scr[c["i"], c["p"]]
            c["res"] = _mm(_rows(c["qe"] - w[0:c_len, LANE:2 * LANE],
                                 w[c_len:c_len + LANE, LANE:2 * LANE]), c["s"])
        for c in chains:
            w, res = c["w"], c["res"]
            lo = LANE * c["p"]
            s_scr[c["i"], c["p"]] = (c["s"] * c["s_decay"] - res[c_len:c_len + LANE]
                                     + w[c_len:c_len + LANE, 0:LANE])
            o_raw[c["i"], :, lo:lo + LANE] = res[0:c_len] + w[0:c_len, 0:LANE]
        for i, sq in enumerate(seqs):
            o = o_raw[i]
            ms = _group_sum(o * o, ones_blk, terms=1) * (1.0 / HEAD)
            o_ref[i] = o * lax.rsqrt(ms + NORM_EPS) * nw_ref[...] * sq["gate"]

    advance(prepare())

    @pl.when(ci == pl.num_programs(1) - 1)
    def _():
        so = _own_layer(so_ref, layer)
        for p in range(DN_HEADS // 2):
            so[:, 2 * p] = s_scr[:, p, 0:HEAD, 0:HEAD]
            so[:, 2 * p + 1] = s_scr[:, p, HEAD:LANE, HEAD:LANE]


def _dn_group(z, conv0, s0, stacked, conv_w, head_par, norm_w, layer, c_len, n_valid, n_par):
    n_seq, t_pad, _ = z.shape
    fresh = s0 is None
    nc = t_pad // c_len
    assert n_valid == c_len or nc == 1
    state = (DN_HEADS, HEAD, HEAD)
    in_specs = [_seq_specs(n_par, c_len, DN_SLAB)]
    args = [z]
    if not fresh:
        in_specs += [_state_spec(n_par, (CONV_W - 1, DN_CONV)), _state_spec(n_par, state)]
        args += [conv0, s0]
    for p in (conv_w, head_par, norm_w):
        in_specs.append(_const_spec(p))
        args.append(p)
    so_spec, so_shape, aliases = _stacked_state_out(layer, stacked, n_par, n_seq, state, in_specs, args)
    return pl.pallas_call(
        functools.partial(_dn_body, c_len, n_valid, fresh, n_par, layer),
        grid=(n_seq // n_par, nc),
        in_specs=in_specs,
        out_specs=[_seq_specs(n_par, c_len, DN_QK), so_spec],
        out_shape=[jax.ShapeDtypeStruct((n_seq, t_pad, DN_QK), F32), so_shape],
        input_output_aliases=aliases,
        scratch_shapes=[pltpu.VMEM((n_par, c_len + TAIL, DN_CONV), F32),
                        pltpu.VMEM((n_par, DN_HEADS // 2, LANE, LANE), F32),
                        pltpu.VMEM((n_par, c_len, DN_QK), F32)],
        compiler_params=pltpu.CompilerParams(
            dimension_semantics=("arbitrary", "arbitrary"), vmem_limit_bytes=VMEM_LIMIT),
        name="dn_group",
    )(*args)


def _rw_body(c_len, n_valid, fresh, n_par, layer, *refs):
    refs = list(refs)
    if layer > 0:
        del refs[-6]
    if fresh:
        (z_ref, mu_ref, w0_ref, w2_ref, a0_ref, a2_ref, g2_ref, kk_ref, ka_ref, rk_ref,
         lng_ref, lnb_ref, o_ref, so_ref, xe, s_scr, y_raw) = refs
    else:
        (z_ref, sh0_ref, s0_ref, mu_ref, w0_ref, w2_ref, a0_ref, a2_ref, g2_ref, kk_ref, ka_ref,
         rk_ref, lng_ref, lnb_ref, o_ref, so_ref, xe, s_scr, y_raw) = refs
    ci = pl.program_id(1)

    @pl.when(ci == 0)
    def _():
        xe[:, 0:TAIL, :] = jnp.zeros((n_par, TAIL, RW_SLAB), F32)
        s_scr[...] = jnp.zeros_like(s_scr)
        if not fresh:
            xe[:, TAIL - 1:TAIL, :] = sh0_ref[...]
            for p in range(RW_HEADS // 2):
                s_scr[:, p, 0:HEAD, 0:HEAD] = s0_ref[:, 2 * p]
                s_scr[:, p, HEAD:LANE, HEAD:LANE] = s0_ref[:, 2 * p + 1]

    tri_f = _tri(c_len)[0].astype(F32)
    live = lax.broadcasted_iota(jnp.int32, (c_len, RW_W), 0) < n_valid
    ones_blk = _group_ones()
    eye_l = _tri(LANE)[2].astype(F32)

    seqs = []
    for i in range(n_par):
        x = z_ref[i]
        xe[i, TAIL:TAIL + c_len, :] = x
        prev = xe[i, TAIL - 1:TAIL - 1 + c_len, :]
        zs = x + (prev - x) * mu_ref[...]
        xe[i, 0:TAIL, :] = xe[i, c_len:c_len + TAIL, :]
        seqs.append(dict(r=zs[:, 0:RW_W], k=zs[:, RW_W:2 * RW_W], v=zs[:, 2 * RW_W:3 * RW_W],
                         wl=zs[:, 3 * RW_W:3 * RW_W + 64], al=zs[:, 3 * RW_W + 64:3 * RW_W + 128],
                         gl=zs[:, 3 * RW_W + 128:RW_SLAB]))
    for sq in seqs:
        sq["w_lora"] = _mm(jnp.tanh(sq["wl"]), w2_ref[...])
        sq["a_lora"] = _mm(sq["al"], a2_ref[...])
        sq["g"] = _mm(jax.nn.sigmoid(sq["gl"]), g2_ref[...])
    for sq in seqs:
        w_log = -_softplus(-(w0_ref[...] + sq["w_lora"])) - 0.5
        a = jax.nn.sigmoid(a0_ref[...] + sq["a_lora"])
        kk_in = sq["k"] * kk_ref[...]
        k2 = sq["k"] * (1.0 + (a - 1.0) * ka_ref[...])
        lw = -jnp.exp(w_log)
        if n_valid < c_len:
            lw = jnp.where(live, lw, 0.0)
            k2 = jnp.where(live, k2, 0.0)
            kk_in = jnp.where(live, kk_in, 0.0)
        sq.update(a=a, kk_in=kk_in, k2=k2, lw=lw)
    for sq in seqs:
        sq["gcum"] = _select_sum(tri_f, sq["lw"])
        kk_in = sq["kk_in"]
        sq["kk"] = kk_in * lax.rsqrt(_group_sum(kk_in * kk_in, ones_blk) + 1e-6)
    for sq in seqs:
        gcum = sq["gcum"]
        e_neg = jnp.exp(-gcum)
        g_end = gcum[c_len - 1:c_len, :]
        e_rest = jnp.exp(g_end - gcum)
        bv = sq["kk"] * sq["a"]
        sq.update(at=-sq["kk"] * jnp.exp(gcum - sq["lw"]), bt=bv * e_neg, kt=sq["k2"] * e_neg,
                  rt=sq["r"] * jnp.exp(gcum), b_rest=bv * e_rest, k_rest=sq["k2"] * e_rest,
                  e_end=jnp.exp(g_end))

    pt = _PairTimeMasks(c_len)
    c2 = 2 * c_len
    chains = []
    for i, sq in enumerate(seqs):
        for p in range(RW_HEADS // 2):
            sl = slice(LANE * p, LANE * (p + 1))
            chains.append(dict(i=i, p=p, sl=sl, at=sq["at"][:, sl], rt=sq["rt"][:, sl],
                               bt_bd=_bd(sq["bt"][:, sl]), kt_bd=_bd(sq["kt"][:, sl]),
                               v_bd=_bd(sq["v"][:, sl]), e_end=sq["e_end"][:, sl],
                               rest=_rows(_bd(sq["b_rest"][:, sl]), _bd(sq["k_rest"][:, sl]))))
    for c in chains:
        c["at_bd"] = _bd(c["at"])
        c["prod"] = _mm_nt(_rows(c["at"], c["rt"]), _rows(c["bt_bd"], c["kt_bd"]))
        c["tr"] = _mm_nt(eye_l, _rows(c["at_bd"], c["v_bd"]))
    for c in chains:
        prod = c["prod"]
        c["np"] = jnp.where(pt.strict, prod[0:c_len, 0:c2], 0.0)
        c["t"] = pt.eye + c["np"]
        c["nak"] = jnp.where(pt.strict, prod[0:c_len, c2:2 * c2], 0.0)
        c["mix"] = _cols(jnp.where(pt.incl, prod[c_len:c2, 0:c2], 0.0),
                         jnp.where(pt.incl, prod[c_len:c2, c2:2 * c2], 0.0))
        c["at_t"] = c["tr"][:, 0:c2]
        c["v_t"] = c["tr"][:, c2:2 * c2]
    for c in chains:
        c["nakv"] = _mm(c["nak"], c["v_bd"])
        c["nakv_t"] = _mm_nt(c["v_t"], _bd(c["nak"]))
    for _ in range(_neumann_levels(n_valid)):
        for c in chains:
            c["np"] = _mm(c["np"], _bd(c["np"])).astype(BF16)
            c["np_bd"] = _bd(c["np"])
        for c in chains:
            c["t"] = c["t"] + _mm(c["t"], c["np_bd"])
    for c in chains:
        c["tx"] = _mm(c["t"], _cols(c["at_bd"], _bd(c["nakv"])))
        c["tx_t"] = _mm_nt(_rows(c["at_t"], c["nakv_t"]), _bd(c["t"]))
    for c in chains:
        tx, tx_t = c["tx"], c["tx_t"]
        c["zy"] = _mm(c["mix"], _rows(_cols(_bd(tx[:, 0:LANE]), _bd(tx[:, LANE:2 * LANE])),
                                      _cols(jnp.zeros((c2, LANE), F32), c["v_bd"])))
        ta_t = tx_t[0:LANE]
        c["m"] = _mm(_rows(_cols(ta_t, jnp.zeros_like(ta_t)), _cols(tx_t[LANE:2 * LANE], c["v_t"])),
                     c["rest"])
    for c in chains:
        c["s"] = s_scr[c["i"], c["p"]]
        c["y"] = _mm_nt(c["rt"] + c["zy"][:, 0:LANE], c["s"])
        c["sm"] = _mm(c["s"], c["m"][0:LANE])
    for c in chains:
        s_scr[c["i"], c["p"]] = c["s"] * c["e_end"] + c["sm"] + c["m"][LANE:2 * LANE]
        y_raw[c["i"], :, c["sl"]] = c["y"] + c["zy"][:, LANE:2 * LANE]
    for i, sq in enumerate(seqs):
        y = y_raw[i]
        d = y - _group_sum(y, ones_blk) * (1.0 / HEAD)
        yn = d * lax.rsqrt(_group_sum(d * d, ones_blk, terms=1) * (1.0 / HEAD) + RW_GN_EPS)
        yn = yn * lng_ref[...] + lnb_ref[...]
        bonus = _group_sum(sq["r"] * sq["k2"] * rk_ref[...], ones_blk, terms=1) * sq["v"]
        o_ref[i] = (yn + bonus) * sq["g"]

    @pl.when(ci == pl.num_programs(1) - 1)
    def _():
        so = _own_layer(so_ref, layer)
        for p in range(RW_HEADS // 2):
            so[:, 2 * p] = s_scr[:, p, 0:HEAD, 0:HEAD]
            so[:, 2 * p + 1] = s_scr[:, p, HEAD:LANE, HEAD:LANE]


def _rw_group(z, shift0, s0, stacked, par, layer, c_len, n_valid, n_par):
    n_seq, t_pad, _ = z.shape
    fresh = s0 is None
    nc = t_pad // c_len
    assert n_valid == c_len or nc == 1
    state = (RW_HEADS, HEAD, HEAD)
    in_specs = [_seq_specs(n_par, c_len, RW_SLAB)]
    args = [z]
    if not fresh:
        in_specs += [_state_spec(n_par, (1, RW_SLAB)), _state_spec(n_par, state)]
        args += [shift0, s0]
    for p in par:
        in_specs.append(_const_spec(p))
        args.append(p)
    so_spec, so_shape, aliases = _stacked_state_out(layer, stacked, n_par, n_seq, state, in_specs, args)
    return pl.pallas_call(
        functools.partial(_rw_body, c_len, n_valid, fresh, n_par, layer),
        grid=(n_seq // n_par, nc),
        in_specs=in_specs,
        out_specs=[_seq_specs(n_par, c_len, RW_W), so_spec],
        out_shape=[jax.ShapeDtypeStruct((n_seq, t_pad, RW_W), F32), so_shape],
        input_output_aliases=aliases,
        scratch_shapes=[pltpu.VMEM((n_par, c_len + TAIL, RW_SLAB), F32),
                        pltpu.VMEM((n_par, RW_HEADS // 2, LANE, LANE), F32),
                        pltpu.VMEM((n_par, c_len, RW_W), F32)],
        compiler_params=pltpu.CompilerParams(
            dimension_semantics=("arbitrary", "arbitrary"), vmem_limit_bytes=VMEM_LIMIT),
        name="rw_group",
    )(*args)


def _rt_body(c_len, n_valid, fresh, n_par, *refs):
    if fresh:
        z_ref, cos_ref, sin_ref, lng_ref, o_ref, so_ref, s_scr, o_raw = refs
    else:
        z_ref, cos_ref, sin_ref, s0_ref, lng_ref, o_ref, so_ref, s_scr, o_raw = refs
    ci = pl.program_id(1)

    @pl.when(ci == 0)
    def _():
        if fresh:
            s_scr[...] = jnp.zeros_like(s_scr)
        else:
            s_scr[...] = s0_ref[...]

    lane = lax.broadcasted_iota(jnp.int32, (c_len, RT_QK), 1)
    first_half = (lane & (RT_DK - 1)) < (RT_DK // 2)
    cos = cos_ref[...]
    sin = sin_ref[...]

    def rotary(t):
        partner = jnp.where(first_half,
                            pltpu.roll(t, RT_QK - RT_DK // 2, axis=1),
                            pltpu.roll(t, RT_DK // 2, axis=1))
        return t * cos + partner * sin

    incl, _, _ = _tri(c_len)
    ri = lax.broadcasted_iota(jnp.int32, (c_len, c_len), 0)
    cj = lax.broadcasted_iota(jnp.int32, (c_len, c_len), 1)
    delta = (ri - cj).astype(F32)
    idx = lax.broadcasted_iota(jnp.int32, (c_len, 1), 0).astype(F32)
    log_decay = [math.log1p(-(2.0 ** (-5.0 - h))) for h in range(RT_HEADS)]
    dmasks = [jnp.where(incl, jnp.exp(delta * lg), 0.0) for lg in log_decay]
    q_scale = [jnp.exp((idx + 1.0) * lg) for lg in log_decay]
    t_row = lax.broadcasted_iota(jnp.int32, (1, c_len), 1).astype(F32)
    k_scale = [jnp.where(t_row < n_valid, jnp.exp((n_valid - 1.0 - t_row) * lg), 0.0) for lg in log_decay]
    ones_blk = _group_ones()
    eye_k = _tri(RT_DK)[2].astype(F32)

    chains = []
    for i in range(n_par):
        q = rotary(z_ref[i, :, 0:RT_QK])
        k = rotary(z_ref[i, :, RT_QK:2 * RT_QK]) * (RT_DK ** -0.5)
        for h in range(RT_HEADS):
            chains.append(dict(
                i=i, h=h, qh=q[:, RT_DK * h:RT_DK * (h + 1)], kh=k[:, RT_DK * h:RT_DK * (h + 1)],
                vh=z_ref[i, :, 2 * RT_QK + HEAD * h:2 * RT_QK + HEAD * (h + 1)]))
    for c in chains:
        c["qk"] = _mm_nt(_rows(c["qh"], eye_k), c["kh"])
    for c in chains:
        h = c["h"]
        c["s"] = s_scr[c["i"], h]
        c["cross"] = _mm(c["qh"] * q_scale[h], c["s"])
        c["kv"] = _mm(c["qk"][c_len:c_len + RT_DK] * k_scale[h], c["vh"])
        c["inner"] = _mm(c["qk"][0:c_len] * dmasks[h], c["vh"])
    for c in chains:
        i, h = c["i"], c["h"]
        s_scr[i, h] = c["s"] * math.exp(n_valid * log_decay[h]) + c["kv"]
        o_raw[i, :, HEAD * h:HEAD * (h + 1)] = c["inner"] + c["cross"]
    for i in range(n_par):
        o = o_raw[i]
        d = o - _group_sum(o, ones_blk) * (1.0 / HEAD)
        on = d * lax.rsqrt(_group_sum(d * d, ones_blk, terms=1) * (1.0 / HEAD) + RT_GN_EPS)
        o_ref[i] = on * lng_ref[...] * _silu(z_ref[i, :, 2 * RT_QK + RT_W:RT_SLAB])

    @pl.when(ci == pl.num_programs(1) - 1)
    def _():
        so_ref[...] = s_scr[...]


def _rt_group(z, cos, sin, s0, ln_g, c_len, n_valid, n_par):
    n_seq, t_pad, _ = z.shape
    fresh = s0 is None
    nc = t_pad // c_len
    assert n_valid == c_len or nc == 1
    state = (RT_HEADS, RT_DK, HEAD)
    table = pl.BlockSpec((c_len, RT_QK), lambda b, c: (c, 0))
    in_specs = [_seq_specs(n_par, c_len, RT_SLAB), table, table]
    args = [z, cos, sin]
    if not fresh:
        in_specs.append(_state_spec(n_par, state))
        args.append(s0)
    in_specs.append(_const_spec(ln_g))
    args.append(ln_g)
    return pl.pallas_call(
        functools.partial(_rt_body, c_len, n_valid, fresh, n_par),
        grid=(n_seq // n_par, nc),
        in_specs=in_specs,
        out_specs=[_seq_specs(n_par, c_len, RT_W), _state_spec(n_par, state)],
        out_shape=[jax.ShapeDtypeStruct((n_seq, t_pad, RT_W), F32),
                   jax.ShapeDtypeStruct((n_seq,) + state, F32)],
        scratch_shapes=[pltpu.VMEM((n_par,) + state, F32),
                        pltpu.VMEM((n_par, c_len, RT_W), F32)],
        compiler_params=pltpu.CompilerParams(
            dimension_semantics=("arbitrary", "arbitrary"), vmem_limit_bytes=VMEM_LIMIT),
        name="rt_group",
    )(*args)


def _rotary_tables(pos0, t_pad):
    half = RT_DK // 2
    inv = ROPE_BASE ** (-np.arange(half, dtype=np.float64) / half)
    ang = (pos0 + np.arange(t_pad, dtype=np.float64))[:, None] * inv[None, :]
    cos = np.tile(np.concatenate([np.cos(ang), np.cos(ang)], axis=1), (1, RT_HEADS))
    sin = np.tile(np.concatenate([-np.sin(ang), np.sin(ang)], axis=1), (1, RT_HEADS))
    return jnp.asarray(cos, F32), jnp.asarray(sin, F32)


def _outproj_body(x_ref, odn_ref, orw_ref, ort_ref, w_ref, o_ref):
    acc = x_ref[...]
    acc = acc + _dot(odn_ref[...].astype(BF16), w_ref[0:DN_QK, :], None)
    acc = acc + _dot(orw_ref[...].astype(BF16), w_ref[DN_QK:DN_QK + RW_W, :], None)
    acc = acc + _dot(ort_ref[...].astype(BF16), w_ref[DN_QK + RW_W:D_MODEL, :], None)
    o_ref[...] = acc


def _outproj(x, o_dn, o_rw, o_rt, w):
    n = x.shape[0]
    tm = min(512, n)
    row = lambda i: (i, 0)
    return pl.pallas_call(
        _outproj_body,
        grid=(n // tm,),
        in_specs=[pl.BlockSpec((tm, D_MODEL), row),
                  pl.BlockSpec((tm, DN_QK), row),
                  pl.BlockSpec((tm, RW_W), row),
                  pl.BlockSpec((tm, RT_W), row),
                  pl.BlockSpec((D_MODEL, D_MODEL), lambda i: (0, 0))],
        out_specs=pl.BlockSpec((tm, D_MODEL), row),
        out_shape=jax.ShapeDtypeStruct((n, D_MODEL), F32),
        compiler_params=pltpu.CompilerParams(
            dimension_semantics=("arbitrary",), vmem_limit_bytes=VMEM_LIMIT),
        name="outproj",
    )(x, o_dn, o_rw, o_rt, w)


def _ffn_body(final_norm, *refs):
    if final_norm:
        x_ref, g_ref, w1_ref, w3_ref, w2_ref, gf_ref, o_ref, h_scr, acc = refs
    else:
        x_ref, g_ref, w1_ref, w3_ref, w2_ref, o_ref, h_scr, acc = refs
    f = pl.program_id(1)

    @pl.when(f == 0)
    def _():
        x = x_ref[...]
        h_scr[...] = (_rms(x, NORM_EPS) * g_ref[...]).astype(BF16)
        acc[...] = x

    hb = h_scr[...]
    u = _dot(hb, w1_ref[...], None)
    gate = _dot(hb, w3_ref[...], None)
    acc[...] += _dot((_silu(u) * gate).astype(BF16), w2_ref[...], None)

    @pl.when(f == pl.num_programs(1) - 1)
    def _():
        if final_norm:
            o_ref[...] = _rms(acc[...], NORM_EPS) * gf_ref[...]
        else:
            o_ref[...] = acc[...]


def _ffn(x, gain, w1, w3, w2, final_gain, tf):
    n = x.shape[0]
    d_f = w1.shape[1]
    final_norm = final_gain is not None
    tm = min(1024, n)
    in_specs = [pl.BlockSpec((tm, D_MODEL), lambda i, f: (i, 0)),
                pl.BlockSpec((1, D_MODEL), lambda i, f: (0, 0)),
                pl.BlockSpec((D_MODEL, tf), lambda i, f: (0, f)),
                pl.BlockSpec((D_MODEL, tf), lambda i, f: (0, f)),
                pl.BlockSpec((tf, D_MODEL), lambda i, f: (f, 0))]
    args = [x, gain, w1, w3, w2]
    if final_norm:
        in_specs.append(pl.BlockSpec((1, D_MODEL), lambda i, f: (0, 0)))
        args.append(final_gain)
    return pl.pallas_call(
        functools.partial(_ffn_body, final_norm),
        grid=(n // tm, d_f // tf),
        in_specs=in_specs,
        out_specs=pl.BlockSpec((tm, D_MODEL), lambda i, f: (i, 0)),
        out_shape=jax.ShapeDtypeStruct((n, D_MODEL), F32),
        scratch_shapes=[pltpu.VMEM((tm, D_MODEL), BF16), pltpu.VMEM((tm, D_MODEL), F32)],
        compiler_params=pltpu.CompilerParams(
            dimension_semantics=("arbitrary", "arbitrary"), vmem_limit_bytes=VMEM_LIMIT),
        name="ffn_dense",
    )(*args)


def _route_body(x_ref, g_ref, router_ref, o_ref):
    h = _rms(x_ref[...], NORM_EPS) * g_ref[...]
    h1, h2, _ = _split3(h)
    r1, r2, _ = _split3(router_ref[...])
    logits = (jnp.dot(h1, r1, preferred_element_type=F32) + jnp.dot(h1, r2, preferred_element_type=F32)
              + jnp.dot(h2, r1, preferred_element_type=F32))
    lane = lax.broadcasted_iota(jnp.int32, logits.shape, 1).astype(F32)
    lg = jnp.where(lane < N_EXP, logits, -jnp.inf)
    m1 = jnp.max(lg, axis=-1, keepdims=True)
    i1 = jnp.min(jnp.where(lg == m1, lane, float(LANE)), axis=-1, keepdims=True)
    lg2 = jnp.where(lane == i1, -jnp.inf, lg)
    m2 = jnp.max(lg2, axis=-1, keepdims=True)
    i2 = jnp.min(jnp.where(lg2 == m2, lane, float(LANE)), axis=-1, keepdims=True)
    e2 = jnp.exp(m2 - m1)
    den = 1.0 + e2
    o_ref[...] = (jnp.where(lane == 0.0, i1, 0.0) + jnp.where(lane == 1.0, i2, 0.0)
                  + jnp.where(lane == 2.0, 1.0 / den, 0.0) + jnp.where(lane == 3.0, e2 / den, 0.0))


def _route(x, gain, router):
    n = x.shape[0]
    tm = min(512, n)
    return pl.pallas_call(
        _route_body,
        grid=(n // tm,),
        in_specs=[pl.BlockSpec((tm, D_MODEL), lambda i: (i, 0)),
                  pl.BlockSpec((1, D_MODEL), lambda i: (0, 0)),
                  pl.BlockSpec((D_MODEL, LANE), lambda i: (0, 0))],
        out_specs=pl.BlockSpec((tm, LANE), lambda i: (i, 0)),
        out_shape=jax.ShapeDtypeStruct((n, LANE), F32),
        compiler_params=pltpu.CompilerParams(
            dimension_semantics=("arbitrary",), vmem_limit_bytes=VMEM_LIMIT),
        name="moe_route",
    )(x, gain, router)


def _row_gather(src_hbm, idx_ref, buf, sem, slot, n_rows):
    def copy(r):
        return pltpu.make_async_copy(src_hbm.at[pl.ds(idx_ref[0, 0, r], 1), :],
                                     buf.at[slot, pl.ds(r, 1), :], sem.at[slot])

    def start():
        def body(r, carry):
            copy(r).start()
            return carry
        lax.fori_loop(0, n_rows, body, 0, unroll=8)

    def wait():
        def body(r, carry):
            copy(r).wait()
            return carry
        lax.fori_loop(0, n_rows, body, 0, unroll=8)

    return start, wait, copy


def _moe_ffn_body(tm, te_ref, nu_ref, tok_ref, tok_next_ref, x_hbm, g_ref, w1_ref, w3_ref, w2_ref,
                  ys_ref, xbuf, sem, h_scr, acc):
    i = pl.program_id(0)
    f = pl.program_id(1)
    n_f = pl.num_programs(1)
    slot = lax.rem(i, 2)
    n_used = nu_ref[0]
    used = i < n_used
    rows_per_step = tm // MOE_F_STEPS
    start_cur, wait_cur, _ = _row_gather(x_hbm, tok_ref, xbuf, sem, slot, tm)
    _, wait_next, copy_next = _row_gather(x_hbm, tok_next_ref, xbuf, sem, 1 - slot, tm)

    @pl.when((f == 0) & (i <= n_used))
    def _():
        pl.when(i == 0)(start_cur)
        wait_cur()

    @pl.when((f == 0) & used)
    def _():
        h_scr[...] = (_rms(xbuf[slot], NORM_EPS) * g_ref[...]).astype(BF16)

    @pl.when(used)
    def _():
        hb = h_scr[...]
        u = _dot(hb, w1_ref[0], None)
        gate = _dot(hb, w3_ref[0], None)
        y = _dot((_silu(u) * gate).astype(BF16), w2_ref[0], None)
        for j in range(rows_per_step):
            copy_next(f * rows_per_step + j).start()

        @pl.when(f == 0)
        def _():
            acc[...] = y

        @pl.when(f > 0)
        def _():
            acc[...] += y

    @pl.when(f == n_f - 1)
    def _():
        ys_ref[...] = jnp.where(used, acc[...], 0.0)
        pl.when(used & (i == pl.num_programs(0) - 1))(wait_next)


def _moe_ffn(x, gain, tok, tile_expert, n_used, w1, w3, w2, tm, tf):
    n_tiles = tok.shape[0]
    n_f = D_EXP // tf

    def w_block(i, f, te, nu):
        last = nu[0] - 1
        return te[jnp.minimum(i, last)], jnp.where(i <= last, f, n_f - 1)

    def w13_idx(i, f, te, nu):
        e, ff = w_block(i, f, te, nu)
        return (e, 0, ff)

    def w2_idx(i, f, te, nu):
        e, ff = w_block(i, f, te, nu)
        return (e, ff, 0)

    smem_tile = lambda idx: pl.BlockSpec((1, 1, tm), idx, memory_space=pltpu.SMEM)
    grid_spec = pltpu.PrefetchScalarGridSpec(
        num_scalar_prefetch=2,
        grid=(n_tiles, n_f),
        in_specs=[smem_tile(lambda i, f, te, nu: (i, 0, 0)),
                  smem_tile(lambda i, f, te, nu: (jnp.minimum(i + 1, n_tiles - 1), 0, 0)),
                  pl.BlockSpec(memory_space=pl.ANY),
                  pl.BlockSpec((1, D_MODEL), lambda i, f, te, nu: (0, 0)),
                  pl.BlockSpec((1, D_MODEL, tf), w13_idx),
                  pl.BlockSpec((1, D_MODEL, tf), w13_idx),
                  pl.BlockSpec((1, tf, D_MODEL), w2_idx)],
        out_specs=pl.BlockSpec((tm, D_MODEL), lambda i, f, te, nu: (i, 0)),
        scratch_shapes=[pltpu.VMEM((2, tm, D_MODEL), F32),
                        pltpu.SemaphoreType.DMA((2,)),
                        pltpu.VMEM((tm, D_MODEL), BF16),
                        pltpu.VMEM((tm, D_MODEL), F32)])
    return pl.pallas_call(
        functools.partial(_moe_ffn_body, tm),
        grid_spec=grid_spec,
        out_shape=jax.ShapeDtypeStruct((n_tiles * tm, D_MODEL), F32),
        compiler_params=pltpu.CompilerParams(
            dimension_semantics=("arbitrary", "arbitrary"), vmem_limit_bytes=VMEM_LIMIT),
        name="moe_ffn",
    )(tile_expert, n_used, tok, tok, x, gain, w1, w3, w2)


def _moe_combine_body(tm, final_norm, *refs):
    if final_norm:
        rows_ref, rows_next_ref, x_ref, route_ref, ys_hbm, gf_ref, o_ref, ybuf, sem = refs
    else:
        rows_ref, rows_next_ref, x_ref, route_ref, ys_hbm, o_ref, ybuf, sem = refs
    i = pl.program_id(0)
    slot = lax.rem(i, 2)
    start_next, _, _ = _row_gather(ys_hbm, rows_next_ref, ybuf, sem, 1 - slot, 2 * tm)
    start_cur, wait_cur, _ = _row_gather(ys_hbm, rows_ref, ybuf, sem, slot, 2 * tm)
    pl.when(i == 0)(start_cur)
    pl.when(i + 1 < pl.num_programs(0))(start_next)
    wait_cur()
    route = route_ref[...]
    out = x_ref[...] + (route[:, 2:3] * ybuf[slot, 0:tm, :] + route[:, 3:4] * ybuf[slot, tm:2 * tm, :])
    if final_norm:
        out = _rms(out, NORM_EPS) * gf_ref[...]
    o_ref[...] = out


def _moe_combine(x, route, rows, ys, final_gain, tm):
    n = x.shape[0]
    n_tiles = n // tm
    final_norm = final_gain is not None
    smem_tile = lambda idx: pl.BlockSpec((1, 1, 2 * tm), idx, memory_space=pltpu.SMEM)
    in_specs = [smem_tile(lambda i: (i, 0, 0)),
                smem_tile(lambda i: (jnp.minimum(i + 1, n_tiles - 1), 0, 0)),
                pl.BlockSpec((tm, D_MODEL), lambda i: (i, 0)),
                pl.BlockSpec((tm, LANE), lambda i: (i, 0)),
                pl.BlockSpec(memory_space=pl.ANY)]
    args = [rows, rows, x, route, ys]
    if final_norm:
        in_specs.append(pl.BlockSpec((1, D_MODEL), lambda i: (0, 0)))
        args.append(final_gain)
    return pl.pallas_call(
        functools.partial(_moe_combine_body, tm, final_norm),
        grid=(n_tiles,),
        in_specs=in_specs,
        out_specs=pl.BlockSpec((tm, D_MODEL), lambda i: (i, 0)),
        out_shape=jax.ShapeDtypeStruct((n, D_MODEL), F32),
        scratch_shapes=[pltpu.VMEM((2, 2 * tm, D_MODEL), F32), pltpu.SemaphoreType.DMA((2,))],
        compiler_params=pltpu.CompilerParams(
            dimension_semantics=("arbitrary",), vmem_limit_bytes=VMEM_LIMIT),
        name="moe_combine",
    )(*args)


def _moe(x, gain, router, w1, w3, w2, final_gain):
    n = x.shape[0]
    tm = 512 if n >= 8192 else 256
    tm_c = 256
    n_tiles = -(-(2 * n + N_EXP * (tm - 1)) // tm)
    route = _route(x, gain, router)
    i1 = route[:, 0].astype(jnp.int32)
    i2 = route[:, 1].astype(jnp.int32)
    experts = jnp.arange(N_EXP, dtype=jnp.int32)
    hit = ((i1[:, None] == experts) | (i2[:, None] == experts)).astype(jnp.int32)
    rank = jnp.cumsum(hit, axis=0) - hit
    tiles_e = (jnp.sum(hit, axis=0) + tm - 1) // tm
    tile_end = jnp.cumsum(tiles_e)
    row_off = (tile_end - tiles_e) * tm
    row1 = row_off[i1] + jnp.take_along_axis(rank, i1[:, None], axis=1)[:, 0]
    row2 = row_off[i2] + jnp.take_along_axis(rank, i2[:, None], axis=1)[:, 0]
    token = jnp.arange(n, dtype=jnp.int32)
    tok = jnp.zeros((n_tiles * tm,), jnp.int32).at[jnp.concatenate([row1, row2])].set(
        jnp.concatenate([token, token]), unique_indices=True, mode="promise_in_bounds")
    tile_ids = jnp.arange(n_tiles, dtype=jnp.int32)
    tile_expert = jnp.minimum(
        jnp.sum((tile_end[None, :] <= tile_ids[:, None]).astype(jnp.int32), axis=1), N_EXP - 1)
    n_used = tile_end[N_EXP - 1:].astype(jnp.int32)
    ys = _moe_ffn(x, gain, tok.reshape(n_tiles, 1, tm), tile_expert, n_used, w1, w3, w2,
                  tm, D_EXP // MOE_F_STEPS)
    rows = jnp.concatenate([row1.reshape(n // tm_c, 1, tm_c), row2.reshape(n // tm_c, 1, tm_c)],
                           axis=2).astype(jnp.int32)
    return _moe_combine(x, route, rows, ys, final_gain, tm_c)


def _pad_last(a, width):
    return jnp.pad(a, [(0, 0)] * (a.ndim - 1) + [(0, width - a.shape[-1])])


def _pack_params(p):
    w_in = p["w_in"]
    w_in = jnp.concatenate([_pad_last(w_in[..., :DN_P], DN_SLAB),
                            _pad_last(w_in[..., DN_P:DN_P + RW_P], RW_SLAB),
                            w_in[..., DN_P + RW_P:]], axis=-1).astype(BF16)
    head_par = jnp.stack([_pad_last(p["dn_a_log"], LANE), _pad_last(p["dn_dt_bias"], LANE)], axis=1)
    g2 = jnp.pad(p["rw_g2"], ((0, 0), (0, RW_GATE_PAD - RW_GATE_R), (0, 0)))
    row = lambda a: a[:, None, :]
    return dict(
        ln_mix=row(p["ln_mix"]), w_in=w_in, dn_conv=p["dn_conv"], dn_head=head_par,
        dn_norm=row(jnp.tile(p["dn_norm"], (1, DN_HEADS))),
        rw=[(row(_pad_last(p["rw_mu"], RW_SLAB))[l], row(p["rw_w0"])[l], p["rw_w2"][l],
             row(p["rw_a0"])[l], p["rw_a2"][l], g2[l], row(p["rw_kk"])[l], row(p["rw_ka"])[l],
             p["rw_rk"].reshape(N_LAYERS, 1, RW_W)[l], row(p["rw_ln_g"])[l], row(p["rw_ln_b"])[l])
            for l in range(N_LAYERS)],
        rt_ln=row(p["rt_ln"]), w_out=p["w_out"].astype(BF16), ln_ffn=row(p["ln_ffn"]),
        ffn_w1=p["ffn_w1"].astype(BF16), ffn_w3=p["ffn_w3"].astype(BF16),
        ffn_w2=p["ffn_w2"].astype(BF16),
        moe_router=_pad_last(p["moe_router"], LANE),
        moe_w1=p["moe_w1"].astype(BF16), moe_w3=p["moe_w3"].astype(BF16),
        moe_w2=p["moe_w2"].astype(BF16),
        ln_final=p["ln_final"][None, :],
    )


def _trunk(x, n_seq, t_real, t_pad, c_len, n_par, states, pos0, w):
    n_valid = min(c_len, t_real)
    cos, sin = _rotary_tables(pos0, t_pad)
    convs, shifts, rts = [], [], []
    dns = rws = None
    seq = lambda a: a.reshape(n_seq, t_pad, a.shape[-1])
    flat = lambda a: a.reshape(n_seq * t_pad, a.shape[-1])
    for l in range(N_LAYERS):
        z_dn, z_rw, z_rt = (seq(z) for z in _inproj(x, w["ln_mix"][l], w["w_in"][l]))
        if states is None:
            conv0 = s_dn0 = shift0 = s_rw0 = s_rt0 = None
        else:
            conv0, s_dn0 = states[0][l], states[1][l]
            shift0 = _pad_last(states[2][l], RW_SLAB)[:, None, :]
            s_rw0, s_rt0 = states[3][l], states[4][l]
        o_dn, dns = _dn_group(z_dn, conv0, s_dn0, dns, w["dn_conv"][l], w["dn_head"][l],
                              w["dn_norm"][l], l, c_len, n_valid, n_par)
        o_rw, rws = _rw_group(z_rw, shift0, s_rw0, rws, w["rw"][l], l, c_len, n_valid, n_par)
        o_rt, s_rt = _rt_group(z_rt, cos, sin, s_rt0, w["rt_ln"][l], c_len, n_valid, n_par)
        x = _outproj(x, flat(o_dn), flat(o_rw), flat(o_rt), w["w_out"][l])
        if l % 2 == 0:
            x = _ffn(x, w["ln_ffn"][l], w["ffn_w1"][l // 2], w["ffn_w3"][l // 2], w["ffn_w2"][l // 2],
                     w["ln_final"] if l == N_LAYERS - 1 else None, D_FF // 2)
        else:
            xr = seq(x)[:, :t_real].reshape(n_seq * t_real, D_MODEL)
            xr = _moe(xr, w["ln_ffn"][l], w["moe_router"][l // 2], w["moe_w1"][l // 2],
                      w["moe_w3"][l // 2], w["moe_w2"][l // 2],
                      w["ln_final"] if l == N_LAYERS - 1 else None)
            x = xr if l == N_LAYERS - 1 else flat(
                jnp.pad(xr.reshape(n_seq, t_real, D_MODEL), ((0, 0), (0, t_pad - t_real), (0, 0))))
        if t_real >= CONV_W - 1:
            convs.append(z_dn[:, t_real - (CONV_W - 1):t_real, :DN_CONV])
        else:
            convs.append(jnp.concatenate([conv0[:, t_real:], z_dn[:, :t_real, :DN_CONV]], axis=1))
        shifts.append(z_rw[:, t_real - 1, :RW_P])
        rts.append(s_rt)
    return x, jnp.stack(convs), dns, jnp.stack(shifts), rws, jnp.stack(rts)


def kernel(x_prompt, x_sample, state_dn_conv, state_dn, state_rw_shift, state_rw, state_rt, ln_mix, w_in, dn_conv, dn_a_log, dn_dt_bias, dn_norm, rw_mu, rw_w0, rw_w2, rw_a0, rw_a2, rw_g2, rw_kk, rw_ka, rw_rk, rw_ln_g, rw_ln_b, rt_ln, w_out, ln_ffn, ffn_w1, ffn_w3, ffn_w2, moe_router, moe_w1, moe_w3, moe_w2, ln_final):
    w = _pack_params(dict(
        ln_mix=ln_mix, w_in=w_in, dn_conv=dn_conv, dn_a_log=dn_a_log, dn_dt_bias=dn_dt_bias,
        dn_norm=dn_norm, rw_mu=rw_mu, rw_w0=rw_w0, rw_w2=rw_w2, rw_a0=rw_a0, rw_a2=rw_a2,
        rw_g2=rw_g2, rw_kk=rw_kk, rw_ka=rw_ka, rw_rk=rw_rk, rw_ln_g=rw_ln_g, rw_ln_b=rw_ln_b,
        rt_ln=rt_ln, w_out=w_out, ln_ffn=ln_ffn, ffn_w1=ffn_w1, ffn_w3=ffn_w3, ffn_w2=ffn_w2,
        moe_router=moe_router, moe_w1=moe_w1, moe_w3=moe_w3, moe_w2=moe_w2, ln_final=ln_final))
    bp, tp, _ = x_prompt.shape
    bs, ts, _ = x_sample.shape

    yp, *p_states = _trunk(x_prompt.reshape(bp * tp, D_MODEL), bp, tp, tp, 64, 4, None, 0, w)
    y_prompt = yp.reshape(bp, tp, D_MODEL)

    ts_pad = 8
    xs = jnp.pad(x_sample, ((0, 0), (0, ts_pad - ts), (0, 0))).reshape(bs * ts_pad, D_MODEL)
    ys, *s_states = _trunk(xs, bs, ts, ts_pad, ts_pad, 8,
                           (state_dn_conv, state_dn, state_rw_shift, state_rw, state_rt),
                           PAST_LEN, w)
    y_sample = ys.reshape(bs, -1, D_MODEL)[:, :ts]
    return (y_prompt, y_sample, *p_states, *s_states)
```

```python
import functools
import math

import numpy as np
import jax
import jax.numpy as jnp
from jax import lax
from jax.experimental import pallas as pl
from jax.experimental.pallas import tpu as pltpu

F32 = jnp.float32
BF16 = jnp.bfloat16

D_MODEL = 1024
N_LAYERS = 2
PAST_LEN = 16384
HEAD = 64
DN_HEADS = 6
DN_QK = DN_HEADS * HEAD
DN_CONV = 3 * DN_QK
CONV_W = 4
RW_HEADS = 6
RW_W = RW_HEADS * HEAD
RW_GATE_R = 160
RT_HEADS = 4
RT_DK = 32
RT_QK = RT_HEADS * RT_DK
RT_W = RT_HEADS * HEAD
DN_P = DN_CONV + DN_QK + 2 * DN_HEADS
RW_P = 3 * RW_W + 64 + 64 + RW_GATE_R
RT_P = 2 * RT_QK + 2 * RT_W
LANE = 128
DN_SLAB = 1664
RW_SLAB = 1536
RT_SLAB = RT_P
P_SLABS = DN_SLAB + RW_SLAB + RT_SLAB
RW_GATE_PAD = RW_SLAB - (3 * RW_W + 128)
N_EXP = 8
D_FF = 2816
D_EXP = 3584
MOE_F_STEPS = 2
ROPE_BASE = 10000.0
NORM_EPS = 1e-6
RW_GN_EPS = 64e-5
RT_GN_EPS = 1e-5
TAIL = 8
VMEM_LIMIT = 56 * 1024 * 1024


def _dot(a, b, precision=None):
    return jnp.dot(a, b, preferred_element_type=F32, precision=precision)


def _split3(x):
    x1 = x.astype(BF16)
    r1 = x - x1.astype(F32)
    x2 = r1.astype(BF16)
    x3 = (r1 - x2.astype(F32)).astype(BF16)
    return x1, x2, x3


def _select_sum(sel, x):
    sel = sel.astype(BF16)
    return sum(jnp.dot(sel, p, preferred_element_type=F32) for p in _split3(x))


def _select_sum_nt(sel, x):
    sel = sel.astype(BF16)
    return sum(lax.dot_general(sel, p, (((1,), (1,)), ((), ())), preferred_element_type=F32)
               for p in _split3(x))


def _mm(a, b):
    return jnp.dot(a.astype(BF16), b.astype(BF16), preferred_element_type=F32)


def _mm_nt(a, b):
    return lax.dot_general(a.astype(BF16), b.astype(BF16), (((1,), (1,)), ((), ())),
                           preferred_element_type=F32)


def _rows(*parts):
    return jnp.concatenate(parts, axis=0)


def _cols(*parts):
    return jnp.concatenate(parts, axis=1)


def _silu(x):
    return x * jax.nn.sigmoid(x)


def _softplus(x):
    return jnp.maximum(x, 0.0) + jnp.log(1.0 + jnp.exp(-jnp.abs(x)))


def _rms(x, eps):
    return x * lax.rsqrt(jnp.mean(x * x, axis=-1, keepdims=True) + eps)


def _tri(c):
    r = lax.broadcasted_iota(jnp.int32, (c, c), 0)
    col = lax.broadcasted_iota(jnp.int32, (c, c), 1)
    return r >= col, r > col, r == col


def _group_ones():
    r = lax.broadcasted_iota(jnp.int32, (LANE, LANE), 0)
    c = lax.broadcasted_iota(jnp.int32, (LANE, LANE), 1)
    return ((r < HEAD) == (c < HEAD)).astype(BF16)


def _group_sum(x, ones_blk, terms=2):
    parts = _split3(x)[:terms]
    out = [sum(jnp.dot(p[:, j:j + LANE], ones_blk, preferred_element_type=F32) for p in parts)
           for j in range(0, x.shape[1], LANE)]
    return out[0] if len(out) == 1 else jnp.concatenate(out, axis=1)


def _bd(x):
    x = x.astype(BF16)
    first = lax.broadcasted_iota(jnp.int32, x.shape, 1) < x.shape[1] // 2
    zero = jnp.zeros_like(x)
    return jnp.concatenate([jnp.where(first, x, zero), jnp.where(first, zero, x)], axis=0)


class _PairTimeMasks:
    def __init__(self, c_len):
        row = lax.broadcasted_iota(jnp.int32, (c_len, 2 * c_len), 0)
        lane = lax.broadcasted_iota(jnp.int32, (c_len, 2 * c_len), 1)
        col = jnp.where(lane < c_len, lane, lane - c_len)
        self.incl = row >= col
        self.strict = row > col
        self.eye = (row == col).astype(F32)
        self.first = lax.broadcasted_iota(jnp.int32, (1, 2 * c_len), 1) < c_len


def _neumann_levels(n_rows):
    return max(0, math.ceil(math.log2(n_rows)) - 1)


def _inproj_body(x_ref, g_ref, w_ref, zdn_ref, zrw_ref, zrt_ref):
    h = (_rms(x_ref[...], NORM_EPS) * g_ref[...]).astype(BF16)
    zdn_ref[...] = _dot(h, w_ref[:, 0:DN_SLAB], None)
    zrw_ref[...] = _dot(h, w_ref[:, DN_SLAB:DN_SLAB + RW_SLAB], None)
    zrt_ref[...] = _dot(h, w_ref[:, DN_SLAB + RW_SLAB:P_SLABS], None)


def _inproj(x, gain, w):
    n = x.shape[0]
    tm = min(512, n)
    return pl.pallas_call(
        _inproj_body,
        grid=(n // tm,),
        in_specs=[pl.BlockSpec((tm, D_MODEL), lambda i: (i, 0)),
                  pl.BlockSpec((1, D_MODEL), lambda i: (0, 0)),
                  pl.BlockSpec((D_MODEL, P_SLABS), lambda i: (0, 0))],
        out_specs=[pl.BlockSpec((tm, DN_SLAB), lambda i: (i, 0)),
                   pl.BlockSpec((tm, RW_SLAB), lambda i: (i, 0)),
                   pl.BlockSpec((tm, RT_SLAB), lambda i: (i, 0))],
        out_shape=[jax.ShapeDtypeStruct((n, DN_SLAB), F32),
                   jax.ShapeDtypeStruct((n, RW_SLAB), F32),
                   jax.ShapeDtypeStruct((n, RT_SLAB), F32)],
        compiler_params=pltpu.CompilerParams(
            dimension_semantics=("arbitrary",), vmem_limit_bytes=VMEM_LIMIT),
        name="inproj",
    )(x, gain, w)


def _seq_specs(n_par, c_len, width):
    return pl.BlockSpec((n_par, c_len, width), lambda b, c: (b, c, 0))


def _state_spec(n_par, shape):
    return pl.BlockSpec((n_par,) + shape, lambda b, c: (b,) + (0,) * len(shape))


def _stacked_state_out(layer, stacked, n_par, n_seq, state, in_specs, args):
    shape = jax.ShapeDtypeStruct((N_LAYERS, n_seq) + state, F32)
    zeros = (0,) * len(state)
    if layer == 0:
        return pl.BlockSpec((N_LAYERS, n_par) + state, lambda b, c: (0, b) + zeros), shape, {}
    in_specs.append(pl.BlockSpec(memory_space=pl.ANY))
    args.append(stacked)
    return (pl.BlockSpec((None, n_par) + state, lambda b, c: (layer, b) + zeros), shape,
            {len(args) - 1: 1})


def _own_layer(so_ref, layer):
    if layer > 0:
        return so_ref
    so_ref[1:] = jnp.zeros((so_ref.shape[0] - 1,) + so_ref.shape[1:], F32)
    return so_ref.at[0]


def _const_spec(a):
    return pl.BlockSpec(a.shape, lambda b, c: (0,) * a.ndim)


def _dn_body(c_len, n_valid, fresh, n_par, layer, *refs):
    refs = list(refs)
    if layer > 0:
        del refs[-6]
    if fresh:
        z_ref, cw_ref, hp_ref, nw_ref, o_ref, so_ref, xe, s_scr, o_raw = refs
    else:
        z_ref, c0_ref, s0_ref, cw_ref, hp_ref, nw_ref, o_ref, so_ref, xe, s_scr, o_raw = refs
    ci = pl.program_id(1)

    @pl.when(ci == 0)
    def _():
        xe[:, 0:TAIL, :] = jnp.zeros((n_par, TAIL, DN_CONV), F32)
        s_scr[...] = jnp.zeros_like(s_scr)
        if not fresh:
            xe[:, TAIL - (CONV_W - 1):TAIL, :] = c0_ref[...]
            for p in range(DN_HEADS // 2):
                s_scr[:, p, 0:HEAD, 0:HEAD] = s0_ref[:, 2 * p]
                s_scr[:, p, HEAD:LANE, HEAD:LANE] = s0_ref[:, 2 * p + 1]

    tri_f = _tri(c_len)[0].astype(F32)
    live = lax.broadcasted_iota(jnp.int32, (c_len, LANE), 0) < n_valid
    neg_a = -jnp.exp(hp_ref[0:1, :])
    dt_bias = hp_ref[1:2, :]

    ones_blk = _group_ones()
    eye_l = _tri(LANE)[2].astype(F32)
    eye_8 = eye_l[0:8, :]

    def prepare():
        seqs = []
        for i in range(n_par):
            x = z_ref[i, :, 0:DN_CONV]
            xe[i, TAIL:TAIL + c_len, :] = x
            acc = x * cw_ref[CONV_W - 1:CONV_W, :]
            for j in range(CONV_W - 1):
                off = TAIL - (CONV_W - 1) + j
                acc = acc + xe[i, off:off + c_len, :] * cw_ref[j:j + 1, :]
            xe[i, 0:TAIL, :] = xe[i, c_len:c_len + TAIL, :]
            ab = z_ref[i, :, DN_CONV + DN_QK:DN_SLAB]
            g_tok = neg_a * _softplus(ab + dt_bias)
            beta = jax.nn.sigmoid(ab)
            if n_valid < c_len:
                g_tok = jnp.where(live, g_tok, 0.0)
                beta = jnp.where(live, beta, 0.0)
            conv = _silu(acc)
            seqs.append(dict(conv=conv, v=conv[:, 2 * DN_QK:DN_CONV], beta=beta, g_tok=g_tok,
                             gate=_silu(z_ref[i, :, DN_CONV:DN_CONV + DN_QK])))
        for sq in seqs:
            sq["g_cum"] = _select_sum(tri_f, sq["g_tok"])
            qk = sq["conv"][:, 0:2 * DN_QK]
            sq["qk"] = qk * lax.rsqrt(_group_sum(qk * qk, ones_blk) + 1e-6)
        for sq in seqs:
            sq["g_cum_t"] = _select_sum_nt(eye_8, sq["g_cum"])
        return seqs

    def advance(seqs):
        pt = _PairTimeMasks(c_len)
        first_h = lax.broadcasted_iota(jnp.int32, (1, LANE), 1) < HEAD
        first_rows = lax.broadcasted_iota(jnp.int32, (LANE, 1), 0) < HEAD
        chains = []
        for i, sq in enumerate(seqs):
            for p in range(DN_HEADS // 2):
                ha, hb = 2 * p, 2 * p + 1
                lo = LANE * p
                k = sq["qk"][:, DN_QK + lo:DN_QK + lo + LANE]
                v = sq["v"][:, lo:lo + LANE]
                gc_a, gc_b = sq["g_cum"][:, ha:ha + 1], sq["g_cum"][:, hb:hb + 1]
                gc_row = _cols(sq["g_cum_t"][ha:ha + 1, :], sq["g_cum_t"][hb:hb + 1, :])
                g_end = jnp.where(pt.first, gc_a[c_len - 1:c_len], gc_b[c_len - 1:c_len])
                b = jnp.where(first_h, sq["beta"][:, DN_HEADS + ha:DN_HEADS + ha + 1],
                              sq["beta"][:, DN_HEADS + hb:DN_HEADS + hb + 1])
                egc = jnp.exp(jnp.where(first_h, gc_a, gc_b))
                dmask = jnp.where(
                    pt.incl, jnp.exp(jnp.minimum(jnp.where(pt.first, gc_a, gc_b) - gc_row, 0.0)), 0.0)
                q = sq["qk"][:, lo:lo + LANE] * (HEAD ** -0.5)
                kb = k * b
                chains.append(dict(
                    i=i, p=p, k=k, kb=kb, q=q, qe=q * egc, vb=v * b, kbe=kb * egc, dmask=dmask,
                    k_decay=jnp.exp(g_end - gc_row),
                    s_decay=jnp.exp(jnp.where(first_rows, gc_a[c_len - 1:c_len],
                                              gc_b[c_len - 1:c_len]))))
        for c in chains:
            c["kq"] = _mm_nt(_rows(c["kb"], c["q"], eye_l), _bd(c["k"]))
        for c in chains:
            c["np"] = -jnp.where(pt.strict, c["kq"][0:c_len] * c["dmask"], 0.0)
            c["t"] = pt.eye + c["np"]
        for _ in range(_neumann_levels(n_valid)):
            for c in chains:
                c["np"] = _mm(c["np"], _bd(c["np"])).astype(BF16)
                c["np_bd"] = _bd(c["np"])
            for c in chains:
                c["t"] = c["t"] + _mm(c["t"], c["np_bd"])
        for c in chains:
            c["tx"] = _mm(c["t"], _cols(_bd(c["vb"]), _bd(c["kbe"])))
        for c in chains:
            qk = c["kq"][c_len:2 * c_len] * c["dmask"]
            kd_t = c["kq"][2 * c_len:2 * c_len + LANE] * c["k_decay"]
            c["w"] = _mm(_rows(qk, kd_t), _cols(_bd(c["tx"][:, 0:LANE]), _bd(c["tx"][:, LANE:2 * LANE])))
        for c in chains:
            w = c["w"]
            c["s"] = s_scr[c["i"], c["p"]]
            c["res"] = _mm(_rows(c["qe"] - w[0:c_len, LANE:2 * LANE],
                                 w[c_len:c_len + LANE, LANE:2 * LANE]), c["s"])
        for c in chains:
            w, res = c["w"], c["res"]
            lo = LANE * c["p"]
            s_scr[c["i"], c["p"]] = (c["s"] * c["s_decay"] - res[c_len:c_len + LANE]
                                     + w[c_len:c_len + LANE, 0:LANE])
            o_raw[c["i"], :, lo:lo + LANE] = res[0:c_len] + w[0:c_len, 0:LANE]
        for i, sq in enumerate(seqs):
            o = o_raw[i]
            ms = _group_sum(o * o, ones_blk, terms=1) * (1.0 / HEAD)
            o_ref[i] = o * lax.rsqrt(ms + NORM_EPS) * nw_ref[...] * sq["gate"]

    advance(prepare())

    @pl.when(ci == pl.num_programs(1) - 1)
    def _():
        so = _own_layer(so_ref, layer)
        for p in range(DN_HEADS // 2):
            so[:, 2 * p] = s_scr[:, p, 0:HEAD, 0:HEAD]
            so[:, 2 * p + 1] = s_scr[:, p, HEAD:LANE, HEAD:LANE]


def _dn_group(z, conv0, s0, stacked, conv_w, head_par, norm_w, layer, c_len, n_valid, n_par):
    n_seq, t_pad, _ = z.shape
    fresh = s0 is None
    nc = t_pad // c_len
    assert n_valid == c_len or nc == 1
    state = (DN_HEADS, HEAD, HEAD)
    in_specs = [_seq_specs(n_par, c_len, DN_SLAB)]
    args = [z]
    if not fresh:
        in_specs += [_state_spec(n_par, (CONV_W - 1, DN_CONV)), _state_spec(n_par, state)]
        args += [conv0, s0]
    for p in (conv_w, head_par, norm_w):
        in_specs.append(_const_spec(p))
        args.append(p)
    so_spec, so_shape, aliases = _stacked_state_out(layer, stacked, n_par, n_seq, state, in_specs, args)
    return pl.pallas_call(
        functools.partial(_dn_body, c_len, n_valid, fresh, n_par, layer),
        grid=(n_seq // n_par, nc),
        in_specs=in_specs,
        out_specs=[_seq_specs(n_par, c_len, DN_QK), so_spec],
        out_shape=[jax.ShapeDtypeStruct((n_seq, t_pad, DN_QK), F32), so_shape],
        input_output_aliases=aliases,
        scratch_shapes=[pltpu.VMEM((n_par, c_len + TAIL, DN_CONV), F32),
                        pltpu.VMEM((n_par, DN_HEADS // 2, LANE, LANE), F32),
                        pltpu.VMEM((n_par, c_len, DN_QK), F32)],
        compiler_params=pltpu.CompilerParams(
            dimension_semantics=("arbitrary", "arbitrary"), vmem_limit_bytes=VMEM_LIMIT),
        name="dn_group",
    )(*args)


def _rw_body(c_len, n_valid, fresh, n_par, layer, *refs):
    refs = list(refs)
    if layer > 0:
        del refs[-6]
    if fresh:
        (z_ref, mu_ref, w0_ref, w2_ref, a0_ref, a2_ref, g2_ref, kk_ref, ka_ref, rk_ref,
         lng_ref, lnb_ref, o_ref, so_ref, xe, s_scr, y_raw) = refs
    else:
        (z_ref, sh0_ref, s0_ref, mu_ref, w0_ref, w2_ref, a0_ref, a2_ref, g2_ref, kk_ref, ka_ref,
         rk_ref, lng_ref, lnb_ref, o_ref, so_ref, xe, s_scr, y_raw) = refs
    ci = pl.program_id(1)

    @pl.when(ci == 0)
    def _():
        xe[:, 0:TAIL, :] = jnp.zeros((n_par, TAIL, RW_SLAB), F32)
        s_scr[...] = jnp.zeros_like(s_scr)
        if not fresh:
            xe[:, TAIL - 1:TAIL, :] = sh0_ref[...]
            for p in range(RW_HEADS // 2):
                s_scr[:, p, 0:HEAD, 0:HEAD] = s0_ref[:, 2 * p]
                s_scr[:, p, HEAD:LANE, HEAD:LANE] = s0_ref[:, 2 * p + 1]

    tri_f = _tri(c_len)[0].astype(F32)
    live = lax.broadcasted_iota(jnp.int32, (c_len, RW_W), 0) < n_valid
    ones_blk = _group_ones()
    eye_l = _tri(LANE)[2].astype(F32)

    seqs = []
    for i in range(n_par):
        x = z_ref[i]
        xe[i, TAIL:TAIL + c_len, :] = x
        prev = xe[i, TAIL - 1:TAIL - 1 + c_len, :]
        zs = x + (prev - x) * mu_ref[...]
        xe[i, 0:TAIL, :] = xe[i, c_len:c_len + TAIL, :]
        seqs.append(dict(r=zs[:, 0:RW_W], k=zs[:, RW_W:2 * RW_W], v=zs[:, 2 * RW_W:3 * RW_W],
                         wl=zs[:, 3 * RW_W:3 * RW_W + 64], al=zs[:, 3 * RW_W + 64:3 * RW_W + 128],
                         gl=zs[:, 3 * RW_W + 128:RW_SLAB]))
    for sq in seqs:
        sq["w_lora"] = _mm(jnp.tanh(sq["wl"]), w2_ref[...])
        sq["a_lora"] = _mm(sq["al"], a2_ref[...])
        sq["g"] = _mm(jax.nn.sigmoid(sq["gl"]), g2_ref[...])
    for sq in seqs:
        w_log = -_softplus(-(w0_ref[...] + sq["w_lora"])) - 0.5
        a = jax.nn.sigmoid(a0_ref[...] + sq["a_lora"])
        kk_in = sq["k"] * kk_ref[...]
        k2 = sq["k"] * (1.0 + (a - 1.0) * ka_ref[...])
        lw = -jnp.exp(w_log)
        if n_valid < c_len:
            lw = jnp.where(live, lw, 0.0)
            k2 = jnp.where(live, k2, 0.0)
            kk_in = jnp.where(live, kk_in, 0.0)
        sq.update(a=a, kk_in=kk_in, k2=k2, lw=lw)
    for sq in seqs:
        sq["gcum"] = _select_sum(tri_f, sq["lw"])
        kk_in = sq["kk_in"]
        sq["kk"] = kk_in * lax.rsqrt(_group_sum(kk_in * kk_in, ones_blk) + 1e-6)
    for sq in seqs:
        gcum = sq["gcum"]
        e_neg = jnp.exp(-gcum)
        g_end = gcum[c_len - 1:c_len, :]
        e_rest = jnp.exp(g_end - gcum)
        bv = sq["kk"] * sq["a"]
        sq.update(at=-sq["kk"] * jnp.exp(gcum - sq["lw"]), bt=bv * e_neg, kt=sq["k2"] * e_neg,
                  rt=sq["r"] * jnp.exp(gcum), b_rest=bv * e_rest, k_rest=sq["k2"] * e_rest,
                  e_end=jnp.exp(g_end))

    pt = _PairTimeMasks(c_len)
    c2 = 2 * c_len
    chains = []
    for i, sq in enumerate(seqs):
        for p in range(RW_HEADS // 2):
            sl = slice(LANE * p, LANE * (p + 1))
            chains.append(dict(i=i, p=p, sl=sl, at=sq["at"][:, sl], rt=sq["rt"][:, sl],
                               bt_bd=_bd(sq["bt"][:, sl]), kt_bd=_bd(sq["kt"][:, sl]),
                               v_bd=_bd(sq["v"][:, sl]), e_end=sq["e_end"][:, sl],
                               rest=_rows(_bd(sq["b_rest"][:, sl]), _bd(sq["k_rest"][:, sl]))))
    for c in chains:
        c["at_bd"] = _bd(c["at"])
        c["prod"] = _mm_nt(_rows(c["at"], c["rt"]), _rows(c["bt_bd"], c["kt_bd"]))
        c["tr"] = _mm_nt(eye_l, _rows(c["at_bd"], c["v_bd"]))
    for c in chains:
        prod = c["prod"]
        c["np"] = jnp.where(pt.strict, prod[0:c_len, 0:c2], 0.0)
        c["t"] = pt.eye + c["np"]
        c["nak"] = jnp.where(pt.strict, prod[0:c_len, c2:2 * c2], 0.0)
        c["mix"] = _cols(jnp.where(pt.incl, prod[c_len:c2, 0:c2], 0.0),
                         jnp.where(pt.incl, prod[c_len:c2, c2:2 * c2], 0.0))
        c["at_t"] = c["tr"][:, 0:c2]
        c["v_t"] = c["tr"][:, c2:2 * c2]
    for c in chains:
        c["nakv"] = _mm(c["nak"], c["v_bd"])
        c["nakv_t"] = _mm_nt(c["v_t"], _bd(c["nak"]))
    for _ in range(_neumann_levels(n_valid)):
        for c in chains:
            c["np"] = _mm(c["np"], _bd(c["np"])).astype(BF16)
            c["np_bd"] = _bd(c["np"])
        for c in chains:
            c["t"] = c["t"] + _mm(c["t"], c["np_bd"])
    for c in chains:
        c["tx"] = _mm(c["t"], _cols(c["at_bd"], _bd(c["nakv"])))
        c["tx_t"] = _mm_nt(_rows(c["at_t"], c["nakv_t"]), _bd(c["t"]))
    for c in chains:
        tx, tx_t = c["tx"], c["tx_t"]
        c["zy"] = _mm(c["mix"], _rows(_cols(_bd(tx[:, 0:LANE]), _bd(tx[:, LANE:2 * LANE])),
                                      _cols(jnp.zeros((c2, LANE), F32), c["v_bd"])))
        ta_t = tx_t[0:LANE]
        c["m"] = _mm(_rows(_cols(ta_t, jnp.zeros_like(ta_t)), _cols(tx_t[LANE:2 * LANE], c["v_t"])),
                     c["rest"])
    for c in chains:
        c["s"] = s_scr[c["i"], c["p"]]
        c["y"] = _mm_nt(c["rt"] + c["zy"][:, 0:LANE], c["s"])
        c["sm"] = _mm(c["s"], c["m"][0:LANE])
    for c in chains:
        s_scr[c["i"], c["p"]] = c["s"] * c["e_end"] + c["sm"] + c["m"][LANE:2 * LANE]
        y_raw[c["i"], :, c["sl"]] = c["y"] + c["zy"][:, LANE:2 * LANE]
    for i, sq in enumerate(seqs):
        y = y_raw[i]
        d = y - _group_sum(y, ones_blk) * (1.0 / HEAD)
        yn = d * lax.rsqrt(_group_sum(d * d, ones_blk, terms=1) * (1.0 / HEAD) + RW_GN_EPS)
        yn = yn * lng_ref[...] + lnb_ref[...]
        bonus = _group_sum(sq["r"] * sq["k2"] * rk_ref[...], ones_blk, terms=1) * sq["v"]
        o_ref[i] = (yn + bonus) * sq["g"]

    @pl.when(ci == pl.num_programs(1) - 1)
    def _():
        so = _own_layer(so_ref, layer)
        for p in range(RW_HEADS // 2):
            so[:, 2 * p] = s_scr[:, p, 0:HEAD, 0:HEAD]
            so[:, 2 * p + 1] = s_scr[:, p, HEAD:LANE, HEAD:LANE]


def _rw_group(z, shift0, s0, stacked, par, layer, c_len, n_valid, n_par):
    n_seq, t_pad, _ = z.shape
    fresh = s0 is None
    nc = t_pad // c_len
    assert n_valid == c_len or nc == 1
    state = (RW_HEADS, HEAD, HEAD)
    in_specs = [_seq_specs(n_par, c_len, RW_SLAB)]
    args = [z]
    if not fresh:
        in_specs += [_state_spec(n_par, (1, RW_SLAB)), _state_spec(n_par, state)]
        args += [shift0, s0]
    for p in par:
        in_specs.append(_const_spec(p))
        args.append(p)
    so_spec, so_shape, aliases = _stacked_state_out(layer, stacked, n_par, n_seq, state, in_specs, args)
    return pl.pallas_call(
        functools.partial(_rw_body, c_len, n_valid, fresh, n_par, layer),
        grid=(n_seq // n_par, nc),
        in_specs=in_specs,
        out_specs=[_seq_specs(n_par, c_len, RW_W), so_spec],
        out_shape=[jax.ShapeDtypeStruct((n_seq, t_pad, RW_W), F32), so_shape],
        input_output_aliases=aliases,
        scratch_shapes=[pltpu.VMEM((n_par, c_len + TAIL, RW_SLAB), F32),
                        pltpu.VMEM((n_par, RW_HEADS // 2, LANE, LANE), F32),
                        pltpu.VMEM((n_par, c_len, RW_W), F32)],
        compiler_params=pltpu.CompilerParams(
            dimension_semantics=("arbitrary", "arbitrary"), vmem_limit_bytes=VMEM_LIMIT),
        name="rw_group",
    )(*args)


def _rt_body(c_len, n_valid, fresh, n_par, *refs):
    if fresh:
        z_ref, cos_ref, sin_ref, lng_ref, o_ref, so_ref, s_scr, o_raw = refs
    else:
        z_ref, cos_ref, sin_ref, s0_ref, lng_ref, o_ref, so_ref, s_scr, o_raw = refs
    ci = pl.program_id(1)

    @pl.when(ci == 0)
    def _():
        if fresh:
            s_scr[...] = jnp.zeros_like(s_scr)
        else:
            s_scr[...] = s0_ref[...]

    lane = lax.broadcasted_iota(jnp.int32, (c_len, RT_QK), 1)
    first_half = (lane & (RT_DK - 1)) < (RT_DK // 2)
    cos = cos_ref[...]
    sin = sin_ref[...]

    def rotary(t):
        partner = jnp.where(first_half,
                            pltpu.roll(t, RT_QK - RT_DK // 2, axis=1),
                            pltpu.roll(t, RT_DK // 2, axis=1))
        return t * cos + partner * sin

    incl, _, _ = _tri(c_len)
    ri = lax.broadcasted_iota(jnp.int32, (c_len, c_len), 0)
    cj = lax.broadcasted_iota(jnp.int32, (c_len, c_len), 1)
    delta = (ri - cj).astype(F32)
    idx = lax.broadcasted_iota(jnp.int32, (c_len, 1), 0).astype(F32)
    log_decay = [math.log1p(-(2.0 ** (-5.0 - h))) for h in range(RT_HEADS)]
    dmasks = [jnp.where(incl, jnp.exp(delta * lg), 0.0) for lg in log_decay]
    q_scale = [jnp.exp((idx + 1.0) * lg) for lg in log_decay]
    t_row = lax.broadcasted_iota(jnp.int32, (1, c_len), 1).astype(F32)
    k_scale = [jnp.where(t_row < n_valid, jnp.exp((n_valid - 1.0 - t_row) * lg), 0.0) for lg in log_decay]
    ones_blk = _group_ones()
    eye_k = _tri(RT_DK)[2].astype(F32)

    chains = []
    for i in range(n_par):
        q = rotary(z_ref[i, :, 0:RT_QK])
        k = rotary(z_ref[i, :, RT_QK:2 * RT_QK]) * (RT_DK ** -0.5)
        for h in range(RT_HEADS):
            chains.append(dict(
                i=i, h=h, qh=q[:, RT_DK * h:RT_DK * (h + 1)], kh=k[:, RT_DK * h:RT_DK * (h + 1)],
                vh=z_ref[i, :, 2 * RT_QK + HEAD * h:2 * RT_QK + HEAD * (h + 1)]))
    for c in chains:
        c["qk"] = _mm_nt(_rows(c["qh"], eye_k), c["kh"])
    for c in chains:
        h = c["h"]
        c["s"] = s_scr[c["i"], h]
        c["cross"] = _mm(c["qh"] * q_scale[h], c["s"])
        c["kv"] = _mm(c["qk"][c_len:c_len + RT_DK] * k_scale[h], c["vh"])
        c["inner"] = _mm(c["qk"][0:c_len] * dmasks[h], c["vh"])
    for c in chains:
        i, h = c["i"], c["h"]
        s_scr[i, h] = c["s"] * math.exp(n_valid * log_decay[h]) + c["kv"]
        o_raw[i, :, HEAD * h:HEAD * (h + 1)] = c["inner"] + c["cross"]
    for i in range(n_par):
        o = o_raw[i]
        d = o - _group_sum(o, ones_blk) * (1.0 / HEAD)
        on = d * lax.rsqrt(_group_sum(d * d, ones_blk, terms=1) * (1.0 / HEAD) + RT_GN_EPS)
        o_ref[i] = on * lng_ref[...] * _silu(z_ref[i, :, 2 * RT_QK + RT_W:RT_SLAB])

    @pl.when(ci == pl.num_programs(1) - 1)
    def _():
        so_ref[...] = s_scr[...]


def _rt_group(z, cos, sin, s0, ln_g, c_len, n_valid, n_par):
    n_seq, t_pad, _ = z.shape
    fresh = s0 is None
    nc = t_pad // c_len
    assert n_valid == c_len or nc == 1
    state = (RT_HEADS, RT_DK, HEAD)
    table = pl.BlockSpec((c_len, RT_QK), lambda b, c: (c, 0))
    in_specs = [_seq_specs(n_par, c_len, RT_SLAB), table, table]
    args = [z, cos, sin]
    if not fresh:
        in_specs.append(_state_spec(n_par, state))
        args.append(s0)
    in_specs.append(_const_spec(ln_g))
    args.append(ln_g)
    return pl.pallas_call(
        functools.partial(_rt_body, c_len, n_valid, fresh, n_par),
        grid=(n_seq // n_par, nc),
        in_specs=in_specs,
        out_specs=[_seq_specs(n_par, c_len, RT_W), _state_spec(n_par, state)],
        out_shape=[jax.ShapeDtypeStruct((n_seq, t_pad, RT_W), F32),
                   jax.ShapeDtypeStruct((n_seq,) + state, F32)],
        scratch_shapes=[pltpu.VMEM((n_par,) + state, F32),
                        pltpu.VMEM((n_par, c_len, RT_W), F32)],
        compiler_params=pltpu.CompilerParams(
            dimension_semantics=("arbitrary", "arbitrary"), vmem_limit_bytes=VMEM_LIMIT),
        name="rt_group",
    )(*args)


def _rotary_tables(pos0, t_pad):
    half = RT_DK // 2
    inv = ROPE_BASE ** (-np.arange(half, dtype=np.float64) / half)
    ang = (pos0 + np.arange(t_pad, dtype=np.float64))[:, None] * inv[None, :]
    cos = np.tile(np.concatenate([np.cos(ang), np.cos(ang)], axis=1), (1, RT_HEADS))
    sin = np.tile(np.concatenate([-np.sin(ang), np.sin(ang)], axis=1), (1, RT_HEADS))
    return jnp.asarray(cos, F32), jnp.asarray(sin, F32)


def _outproj_body(x_ref, odn_ref, orw_ref, ort_ref, w_ref, o_ref):
    acc = x_ref[...]
    acc = acc + _dot(odn_ref[...].astype(BF16), w_ref[0:DN_QK, :], None)
    acc = acc + _dot(orw_ref[...].astype(BF16), w_ref[DN_QK:DN_QK + RW_W, :], None)
    acc = acc + _dot(ort_ref[...].astype(BF16), w_ref[DN_QK + RW_W:D_MODEL, :], None)
    o_ref[...] = acc


def _outproj(x, o_dn, o_rw, o_rt, w):
    n = x.shape[0]
    tm = min(512, n)
    row = lambda i: (i, 0)
    return pl.pallas_call(
        _outproj_body,
        grid=(n // tm,),
        in_specs=[pl.BlockSpec((tm, D_MODEL), row),
                  pl.BlockSpec((tm, DN_QK), row),
                  pl.BlockSpec((tm, RW_W), row),
                  pl.BlockSpec((tm, RT_W), row),
                  pl.BlockSpec((D_MODEL, D_MODEL), lambda i: (0, 0))],
        out_specs=pl.BlockSpec((tm, D_MODEL), row),
        out_shape=jax.ShapeDtypeStruct((n, D_MODEL), F32),
        compiler_params=pltpu.CompilerParams(
            dimension_semantics=("arbitrary",), vmem_limit_bytes=VMEM_LIMIT),
        name="outproj",
    )(x, o_dn, o_rw, o_rt, w)


def _swiglu(hb, w1, w3, w2):
    return _dot((_silu(_dot(hb, w1)) * _dot(hb, w3)).astype(BF16), w2)


def _ffn_body(final_norm, *refs):
    if final_norm:
        x_ref, g_ref, w1_ref, w3_ref, w2_ref, gf_ref, o_ref, h_scr, acc = refs
    else:
        x_ref, g_ref, w1_ref, w3_ref, w2_ref, o_ref, h_scr, acc = refs
    f = pl.program_id(1)

    @pl.when(f == 0)
    def _():
        x = x_ref[...]
        h_scr[...] = (_rms(x, NORM_EPS) * g_ref[...]).astype(BF16)
        acc[...] = x

    acc[...] += _swiglu(h_scr[...], w1_ref[...], w3_ref[...], w2_ref[...])

    @pl.when(f == pl.num_programs(1) - 1)
    def _():
        if final_norm:
            o_ref[...] = _rms(acc[...], NORM_EPS) * gf_ref[...]
        else:
            o_ref[...] = acc[...]


def _ffn(x, gain, w1, w3, w2, final_gain, tf):
    n = x.shape[0]
    d_f = w1.shape[1]
    final_norm = final_gain is not None
    tm = min(1024, n)
    in_specs = [pl.BlockSpec((tm, D_MODEL), lambda i, f: (i, 0)),
                pl.BlockSpec((1, D_MODEL), lambda i, f: (0, 0)),
                pl.BlockSpec((D_MODEL, tf), lambda i, f: (0, f)),
                pl.BlockSpec((D_MODEL, tf), lambda i, f: (0, f)),
                pl.BlockSpec((tf, D_MODEL), lambda i, f: (f, 0))]
    args = [x, gain, w1, w3, w2]
    if final_norm:
        in_specs.append(pl.BlockSpec((1, D_MODEL), lambda i, f: (0, 0)))
        args.append(final_gain)
    return pl.pallas_call(
        functools.partial(_ffn_body, final_norm),
        grid=(n // tm, d_f // tf),
        in_specs=in_specs,
        out_specs=pl.BlockSpec((tm, D_MODEL), lambda i, f: (i, 0)),
        out_shape=jax.ShapeDtypeStruct((n, D_MODEL), F32),
        scratch_shapes=[pltpu.VMEM((tm, D_MODEL), BF16), pltpu.VMEM((tm, D_MODEL), F32)],
        compiler_params=pltpu.CompilerParams(
            dimension_semantics=("arbitrary", "arbitrary"), vmem_limit_bytes=VMEM_LIMIT),
        name="ffn_dense",
    )(*args)


def _route_body(x_ref, g_ref, router_ref, o_ref):
    h = _rms(x_ref[...], NORM_EPS) * g_ref[...]
    h1, h2, _ = _split3(h)
    r1, r2, _ = _split3(router_ref[...])
    logits = (jnp.dot(h1, r1, preferred_element_type=F32) + jnp.dot(h1, r2, preferred_element_type=F32)
              + jnp.dot(h2, r1, preferred_element_type=F32))
    lane = lax.broadcasted_iota(jnp.int32, logits.shape, 1).astype(F32)
    lg = jnp.where(lane < N_EXP, logits, -jnp.inf)
    m1 = jnp.max(lg, axis=-1, keepdims=True)
    i1 = jnp.min(jnp.where(lg == m1, lane, float(LANE)), axis=-1, keepdims=True)
    lg2 = jnp.where(lane == i1, -jnp.inf, lg)
    m2 = jnp.max(lg2, axis=-1, keepdims=True)
    i2 = jnp.min(jnp.where(lg2 == m2, lane, float(LANE)), axis=-1, keepdims=True)
    e2 = jnp.exp(m2 - m1)
    den = 1.0 + e2
    o_ref[...] = (jnp.where(lane == 0.0, i1, 0.0) + jnp.where(lane == 1.0, i2, 0.0)
                  + jnp.where(lane == 2.0, 1.0 / den, 0.0) + jnp.where(lane == 3.0, e2 / den, 0.0))


def _route(x, gain, router):
    n = x.shape[0]
    tm = min(512, n)
    return pl.pallas_call(
        _route_body,
        grid=(n // tm,),
        in_specs=[pl.BlockSpec((tm, D_MODEL), lambda i: (i, 0)),
                  pl.BlockSpec((1, D_MODEL), lambda i: (0, 0)),
                  pl.BlockSpec((D_MODEL, LANE), lambda i: (0, 0))],
        out_specs=pl.BlockSpec((tm, LANE), lambda i: (i, 0)),
        out_shape=jax.ShapeDtypeStruct((n, LANE), F32),
        compiler_params=pltpu.CompilerParams(
            dimension_semantics=("arbitrary",), vmem_limit_bytes=VMEM_LIMIT),
        name="moe_route",
    )(x, gain, router)


def _row_gather(src_hbm, idx_ref, buf, sem, slot, n_rows):
    def copy(r):
        return pltpu.make_async_copy(src_hbm.at[pl.ds(idx_ref[0, 0, r], 1), :],
                                     buf.at[slot, pl.ds(r, 1), :], sem.at[slot])

    def start():
        def body(r, carry):
            copy(r).start()
            return carry
        lax.fori_loop(0, n_rows, body, 0, unroll=8)

    def wait():
        def body(r, carry):
            copy(r).wait()
            return carry
        lax.fori_loop(0, n_rows, body, 0, unroll=8)

    return start, wait, copy


def _moe_ffn_body(tm, te_ref, nu_ref, tok_ref, tok_next_ref, x_hbm, g_ref, w1_ref, w3_ref, w2_ref,
                  ys_ref, xbuf, sem, h_scr, acc):
    i = pl.program_id(0)
    f = pl.program_id(1)
    n_f = pl.num_programs(1)
    slot = lax.rem(i, 2)
    n_used = nu_ref[0]
    used = i < n_used
    rows_per_step = tm // MOE_F_STEPS
    start_cur, wait_cur, _ = _row_gather(x_hbm, tok_ref, xbuf, sem, slot, tm)
    _, wait_next, copy_next = _row_gather(x_hbm, tok_next_ref, xbuf, sem, 1 - slot, tm)

    @pl.when((f == 0) & (i <= n_used))
    def _():
        pl.when(i == 0)(start_cur)
        wait_cur()

    @pl.when((f == 0) & used)
    def _():
        h_scr[...] = (_rms(xbuf[slot], NORM_EPS) * g_ref[...]).astype(BF16)

    @pl.when(used)
    def _():
        y = _swiglu(h_scr[...], w1_ref[0], w3_ref[0], w2_ref[0])
        for j in range(rows_per_step):
            copy_next(f * rows_per_step + j).start()

        @pl.when(f == 0)
        def _():
            acc[...] = y

        @pl.when(f > 0)
        def _():
            acc[...] += y

    @pl.when(f == n_f - 1)
    def _():
        ys_ref[...] = jnp.where(used, acc[...], 0.0)
        pl.when(used & (i == pl.num_programs(0) - 1))(wait_next)


def _moe_ffn(x, gain, tok, tile_expert, n_used, w1, w3, w2, tm, tf):
    n_tiles = tok.shape[0]
    n_f = D_EXP // tf

    def w_block(i, f, te, nu):
        last = nu[0] - 1
        return te[jnp.minimum(i, last)], jnp.where(i <= last, f, n_f - 1)

    def w13_idx(i, f, te, nu):
        e, ff = w_block(i, f, te, nu)
        return (e, 0, ff)

    def w2_idx(i, f, te, nu):
        e, ff = w_block(i, f, te, nu)
        return (e, ff, 0)

    smem_tile = lambda idx: pl.BlockSpec((1, 1, tm), idx, memory_space=pltpu.SMEM)
    grid_spec = pltpu.PrefetchScalarGridSpec(
        num_scalar_prefetch=2,
        grid=(n_tiles, n_f),
        in_specs=[smem_tile(lambda i, f, te, nu: (i, 0, 0)),
                  smem_tile(lambda i, f, te, nu: (jnp.minimum(i + 1, n_tiles - 1), 0, 0)),
                  pl.BlockSpec(memory_space=pl.ANY),
                  pl.BlockSpec((1, D_MODEL), lambda i, f, te, nu: (0, 0)),
                  pl.BlockSpec((1, D_MODEL, tf), w13_idx),
                  pl.BlockSpec((1, D_MODEL, tf), w13_idx),
                  pl.BlockSpec((1, tf, D_MODEL), w2_idx)],
        out_specs=pl.BlockSpec((tm, D_MODEL), lambda i, f, te, nu: (i, 0)),
        scratch_shapes=[pltpu.VMEM((2, tm, D_MODEL), F32),
                        pltpu.SemaphoreType.DMA((2,)),
                        pltpu.VMEM((tm, D_MODEL), BF16),
                        pltpu.VMEM((tm, D_MODEL), F32)])
    return pl.pallas_call(
        functools.partial(_moe_ffn_body, tm),
        grid_spec=grid_spec,
        out_shape=jax.ShapeDtypeStruct((n_tiles * tm, D_MODEL), F32),
        compiler_params=pltpu.CompilerParams(
            dimension_semantics=("arbitrary", "arbitrary"), vmem_limit_bytes=VMEM_LIMIT),
        name="moe_ffn",
    )(tile_expert, n_used, tok, tok, x, gain, w1, w3, w2)


def _moe_combine_body(tm, final_norm, *refs):
    if final_norm:
        rows_ref, rows_next_ref, x_ref, route_ref, ys_hbm, gf_ref, o_ref, ybuf, sem = refs
    else:
        rows_ref, rows_next_ref, x_ref, route_ref, ys_hbm, o_ref, ybuf, sem = refs
    i = pl.program_id(0)
    slot = lax.rem(i, 2)
    _, wait_next, copy_next = _row_gather(ys_hbm, rows_next_ref, ybuf, sem, 1 - slot, 2 * tm)
    start_cur, wait_cur, _ = _row_gather(ys_hbm, rows_ref, ybuf, sem, slot, 2 * tm)
    pl.when(i == 0)(start_cur)
    wait_cur()
    route = route_ref[...]
    out = x_ref[...] + (route[:, 2:3] * ybuf[slot, 0:tm, :] + route[:, 3:4] * ybuf[slot, tm:2 * tm, :])
    if final_norm:
        out = _rms(out, NORM_EPS) * gf_ref[...]
    o_ref[...] = out
    for r in range(2 * tm):
        copy_next(r).start()
    pl.when(i == pl.num_programs(0) - 1)(wait_next)


def _moe_combine(x, route, rows, ys, final_gain, tm):
    n = x.shape[0]
    n_tiles = n // tm
    final_norm = final_gain is not None
    smem_tile = lambda idx: pl.BlockSpec((1, 1, 2 * tm), idx, memory_space=pltpu.SMEM)
    in_specs = [smem_tile(lambda i: (i, 0, 0)),
                smem_tile(lambda i: (jnp.minimum(i + 1, n_tiles - 1), 0, 0)),
                pl.BlockSpec((tm, D_MODEL), lambda i: (i, 0)),
                pl.BlockSpec((tm, LANE), lambda i: (i, 0)),
                pl.BlockSpec(memory_space=pl.ANY)]
    args = [rows, rows, x, route, ys]
    if final_norm:
        in_specs.append(pl.BlockSpec((1, D_MODEL), lambda i: (0, 0)))
        args.append(final_gain)
    return pl.pallas_call(
        functools.partial(_moe_combine_body, tm, final_norm),
        grid=(n_tiles,),
        in_specs=in_specs,
        out_specs=pl.BlockSpec((tm, D_MODEL), lambda i: (i, 0)),
        out_shape=jax.ShapeDtypeStruct((n, D_MODEL), F32),
        scratch_shapes=[pltpu.VMEM((2, 2 * tm, D_MODEL), F32), pltpu.SemaphoreType.DMA((2,))],
        compiler_params=pltpu.CompilerParams(
            dimension_semantics=("arbitrary",), vmem_limit_bytes=VMEM_LIMIT),
        name="moe_combine",
    )(*args)


def _moe(x, gain, router, w1, w3, w2, final_gain):
    n = x.shape[0]
    tm = 512 if n >= 8192 else 256
    tm_c = 256
    n_tiles = -(-(2 * n + N_EXP * (tm - 1)) // tm)
    route = _route(x, gain, router)
    i1 = route[:, 0].astype(jnp.int32)
    i2 = route[:, 1].astype(jnp.int32)
    experts = jnp.arange(N_EXP, dtype=jnp.int32)
    hit = ((i1[:, None] == experts) | (i2[:, None] == experts)).astype(jnp.int32)
    rank = jnp.cumsum(hit, axis=0) - hit
    tiles_e = (jnp.sum(hit, axis=0) + tm - 1) // tm
    tile_end = jnp.cumsum(tiles_e)
    row_off = (tile_end - tiles_e) * tm
    row1 = row_off[i1] + jnp.take_along_axis(rank, i1[:, None], axis=1)[:, 0]
    row2 = row_off[i2] + jnp.take_along_axis(rank, i2[:, None], axis=1)[:, 0]
    token = jnp.arange(n, dtype=jnp.int32)
    tok = jnp.zeros((n_tiles * tm,), jnp.int32).at[jnp.concatenate([row1, row2])].set(
        jnp.concatenate([token, token]), unique_indices=True, mode="promise_in_bounds")
    tile_ids = jnp.arange(n_tiles, dtype=jnp.int32)
    tile_expert = jnp.minimum(
        jnp.sum((tile_end[None, :] <= tile_ids[:, None]).astype(jnp.int32), axis=1), N_EXP - 1)
    n_used = tile_end[N_EXP - 1:].astype(jnp.int32)
    ys = _moe_ffn(x, gain, tok.reshape(n_tiles, 1, tm), tile_expert, n_used, w1, w3, w2,
                  tm, D_EXP // MOE_F_STEPS)
    rows = jnp.concatenate([row1.reshape(n // tm_c, 1, tm_c), row2.reshape(n // tm_c, 1, tm_c)],
                           axis=2).astype(jnp.int32)
    return _moe_combine(x, route, rows, ys, final_gain, tm_c)


def _pad_last(a, width):
    return jnp.pad(a, [(0, 0)] * (a.ndim - 1) + [(0, width - a.shape[-1])])


def _pack_params(p):
    w_in = p["w_in"]
    w_in = jnp.concatenate([_pad_last(w_in[..., :DN_P], DN_SLAB),
                            _pad_last(w_in[..., DN_P:DN_P + RW_P], RW_SLAB),
                            w_in[..., DN_P + RW_P:]], axis=-1).astype(BF16)
    head_par = jnp.stack([_pad_last(p["dn_a_log"], LANE), _pad_last(p["dn_dt_bias"], LANE)], axis=1)
    g2 = jnp.pad(p["rw_g2"], ((0, 0), (0, RW_GATE_PAD - RW_GATE_R), (0, 0)))
    row = lambda a: a[:, None, :]
    return dict(
        ln_mix=row(p["ln_mix"]), w_in=w_in, dn_conv=p["dn_conv"], dn_head=head_par,
        dn_norm=row(jnp.tile(p["dn_norm"], (1, DN_HEADS))),
        rw=[(row(_pad_last(p["rw_mu"], RW_SLAB))[l], row(p["rw_w0"])[l], p["rw_w2"][l],
             row(p["rw_a0"])[l], p["rw_a2"][l], g2[l], row(p["rw_kk"])[l], row(p["rw_ka"])[l],
             p["rw_rk"].reshape(N_LAYERS, 1, RW_W)[l], row(p["rw_ln_g"])[l], row(p["rw_ln_b"])[l])
            for l in range(N_LAYERS)],
        rt_ln=row(p["rt_ln"]), w_out=p["w_out"].astype(BF16), ln_ffn=row(p["ln_ffn"]),
        ffn_w1=p["ffn_w1"].astype(BF16), ffn_w3=p["ffn_w3"].astype(BF16),
        ffn_w2=p["ffn_w2"].astype(BF16),
        moe_router=_pad_last(p["moe_router"], LANE),
        moe_w1=p["moe_w1"].astype(BF16), moe_w3=p["moe_w3"].astype(BF16),
        moe_w2=p["moe_w2"].astype(BF16),
        ln_final=p["ln_final"][None, :],
    )


def _trunk(x, n_seq, t_real, t_pad, c_len, n_par, states, pos0, w):
    n_valid = min(c_len, t_real)
    cos, sin = _rotary_tables(pos0, t_pad)
    convs, shifts, rts = [], [], []
    dns = rws = None
    seq = lambda a: a.reshape(n_seq, t_pad, a.shape[-1])
    flat = lambda a: a.reshape(n_seq * t_pad, a.shape[-1])
    for l in range(N_LAYERS):
        z_dn, z_rw, z_rt = (seq(z) for z in _inproj(x, w["ln_mix"][l], w["w_in"][l]))
        if states is None:
            conv0 = s_dn0 = shift0 = s_rw0 = s_rt0 = None
        else:
            conv0, s_dn0 = states[0][l], states[1][l]
            shift0 = _pad_last(states[2][l], RW_SLAB)[:, None, :]
            s_rw0, s_rt0 = states[3][l], states[4][l]
        o_dn, dns = _dn_group(z_dn, conv0, s_dn0, dns, w["dn_conv"][l], w["dn_head"][l],
                              w["dn_norm"][l], l, c_len, n_valid, n_par)
        o_rw, rws = _rw_group(z_rw, shift0, s_rw0, rws, w["rw"][l], l, c_len, n_valid, n_par)
        o_rt, s_rt = _rt_group(z_rt, cos, sin, s_rt0, w["rt_ln"][l], c_len, n_valid, n_par)
        x = _outproj(x, flat(o_dn), flat(o_rw), flat(o_rt), w["w_out"][l])
        if l % 2 == 0:
            x = _ffn(x, w["ln_ffn"][l], w["ffn_w1"][l // 2], w["ffn_w3"][l // 2], w["ffn_w2"][l // 2],
                     w["ln_final"] if l == N_LAYERS - 1 else None, D_FF // 2)
        else:
            xr = seq(x)[:, :t_real].reshape(n_seq * t_real, D_MODEL)
            xr = _moe(xr, w["ln_ffn"][l], w["moe_router"][l // 2], w["moe_w1"][l // 2],
                      w["moe_w3"][l // 2], w["moe_w2"][l // 2],
                      w["ln_final"] if l == N_LAYERS - 1 else None)
            x = xr if l == N_LAYERS - 1 else flat(
                jnp.pad(xr.reshape(n_seq, t_real, D_MODEL), ((0, 0), (0, t_pad - t_real), (0, 0))))
        if t_real >= CONV_W - 1:
            convs.append(z_dn[:, t_real - (CONV_W - 1):t_real, :DN_CONV])
        else:
            convs.append(jnp.concatenate([conv0[:, t_real:], z_dn[:, :t_real, :DN_CONV]], axis=1))
        shifts.append(z_rw[:, t_real - 1, :RW_P])
        rts.append(s_rt)
    return x, jnp.stack(convs), dns, jnp.stack(shifts), rws, jnp.stack(rts)


def kernel(x_prompt, x_sample, state_dn_conv, state_dn, state_rw_shift, state_rw, state_rt, ln_mix, w_in, dn_conv, dn_a_log, dn_dt_bias, dn_norm, rw_mu, rw_w0, rw_w2, rw_a0, rw_a2, rw_g2, rw_kk, rw_ka, rw_rk, rw_ln_g, rw_ln_b, rt_ln, w_out, ln_ffn, ffn_w1, ffn_w3, ffn_w2, moe_router, moe_w1, moe_w3, moe_w2, ln_final):
    w = _pack_params(dict(
        ln_mix=ln_mix, w_in=w_in, dn_conv=dn_conv, dn_a_log=dn_a_log, dn_dt_bias=dn_dt_bias,
        dn_norm=dn_norm, rw_mu=rw_mu, rw_w0=rw_w0, rw_w2=rw_w2, rw_a0=rw_a0, rw_a2=rw_a2,
        rw_g2=rw_g2, rw_kk=rw_kk, rw_ka=rw_ka, rw_rk=rw_rk, rw_ln_g=rw_ln_g, rw_ln_b=rw_ln_b,
        rt_ln=rt_ln, w_out=w_out, ln_ffn=ln_ffn, ffn_w1=ffn_w1, ffn_w3=ffn_w3, ffn_w2=ffn_w2,
        moe_router=moe_router, moe_w1=moe_w1, moe_w3=moe_w3, moe_w2=moe_w2, ln_final=ln_final))
    bp, tp, _ = x_prompt.shape
    bs, ts, _ = x_sample.shape

    yp, *p_states = _trunk(x_prompt.reshape(bp * tp, D_MODEL), bp, tp, tp, 64, 4, None, 0, w)
    y_prompt = yp.reshape(bp, tp, D_MODEL)

    ts_pad = 8
    xs = jnp.pad(x_sample, ((0, 0), (0, ts_pad - ts), (0, 0))).reshape(bs * ts_pad, D_MODEL)
    ys, *s_states = _trunk(xs, bs, ts, ts_pad, ts_pad, 8,
                           (state_dn_conv, state_dn, state_rw_shift, state_rw, state_rt),
                           PAST_LEN, w)
    y_sample = ys.reshape(bs, -1, D_MODEL)[:, :ts]
    return (y_prompt, y_sample, *p_states, *s_states)
```

```python
import functools
import math

import numpy as np
import jax
import jax.numpy as jnp
from jax import lax
from jax.experimental import pallas as pl
from jax.experimental.pallas import tpu as pltpu

F32 = jnp.float32
BF16 = jnp.bfloat16

D_MODEL = 1024
N_LAYERS = 2
PAST_LEN = 16384
HEAD = 64
DN_HEADS = 6
DN_QK = DN_HEADS * HEAD
DN_CONV = 3 * DN_QK
CONV_W = 4
RW_HEADS = 6
RW_W = RW_HEADS * HEAD
RW_GATE_R = 160
RT_HEADS = 4
RT_DK = 32
RT_QK = RT_HEADS * RT_DK
RT_W = RT_HEADS * HEAD
DN_P = DN_CONV + DN_QK + 2 * DN_HEADS
RW_P = 3 * RW_W + 64 + 64 + RW_GATE_R
RT_P = 2 * RT_QK + 2 * RT_W
LANE = 128
DN_SLAB = 1664
RW_SLAB = 1536
RT_SLAB = RT_P
P_SLABS = DN_SLAB + RW_SLAB + RT_SLAB
RW_GATE_PAD = RW_SLAB - (3 * RW_W + 128)
N_EXP = 8
D_FF = 2816
D_EXP = 3584
MOE_F_STEPS = 2
ROPE_BASE = 10000.0
NORM_EPS = 1e-6
RW_GN_EPS = 64e-5
RT_GN_EPS = 1e-5
TAIL = 8
VMEM_LIMIT = 56 * 1024 * 1024


def _dot(a, b, precision=None):
    return jnp.dot(a, b, preferred_element_type=F32, precision=precision)


def _split3(x):
    x1 = x.astype(BF16)
    r1 = x - x1.astype(F32)
    x2 = r1.astype(BF16)
    x3 = (r1 - x2.astype(F32)).astype(BF16)
    return x1, x2, x3


def _select_sum(sel, x):
    sel = sel.astype(BF16)
    return sum(jnp.dot(sel, p, preferred_element_type=F32) for p in _split3(x))


def _select_sum_nt(sel, x):
    sel = sel.astype(BF16)
    return sum(lax.dot_general(sel, p, (((1,), (1,)), ((), ())), preferred_element_type=F32)
               for p in _split3(x))


def _mm(a, b):
    return jnp.dot(a.astype(BF16), b.astype(BF16), preferred_element_type=F32)


def _mm_nt(a, b):
    return lax.dot_general(a.astype(BF16), b.astype(BF16), (((1,), (1,)), ((), ())),
                           preferred_element_type=F32)


def _rows(*parts):
    return jnp.concatenate(parts, axis=0)


def _cols(*parts):
    return jnp.concatenate(parts, axis=1)


def _silu(x):
    return x * jax.nn.sigmoid(x)


def _softplus(x):
    return jnp.maximum(x, 0.0) + jnp.log(1.0 + jnp.exp(-jnp.abs(x)))


def _rms(x, eps):
    return x * lax.rsqrt(jnp.mean(x * x, axis=-1, keepdims=True) + eps)


def _tri(c):
    r = lax.broadcasted_iota(jnp.int32, (c, c), 0)
    col = lax.broadcasted_iota(jnp.int32, (c, c), 1)
    return r >= col, r > col, r == col


def _group_ones():
    r = lax.broadcasted_iota(jnp.int32, (LANE, LANE), 0)
    c = lax.broadcasted_iota(jnp.int32, (LANE, LANE), 1)
    return ((r < HEAD) == (c < HEAD)).astype(BF16)


def _group_sum(x, ones_blk, terms=2):
    parts = _split3(x)[:terms]
    out = [sum(jnp.dot(p[:, j:j + LANE], ones_blk, preferred_element_type=F32) for p in parts)
           for j in range(0, x.shape[1], LANE)]
    return out[0] if len(out) == 1 else jnp.concatenate(out, axis=1)


def _bd(x):
    x = x.astype(BF16)
    first = lax.broadcasted_iota(jnp.int32, x.shape, 1) < x.shape[1] // 2
    zero = jnp.zeros_like(x)
    return jnp.concatenate([jnp.where(first, x, zero), jnp.where(first, zero, x)], axis=0)


class _PairTimeMasks:
    def __init__(self, c_len):
        row = lax.broadcasted_iota(jnp.int32, (c_len, 2 * c_len), 0)
        lane = lax.broadcasted_iota(jnp.int32, (c_len, 2 * c_len), 1)
        col = jnp.where(lane < c_len, lane, lane - c_len)
        self.incl = row >= col
        self.strict = row > col
        self.eye = (row == col).astype(F32)
        self.first = lax.broadcasted_iota(jnp.int32, (1, 2 * c_len), 1) < c_len


def _neumann_levels(n_rows):
    return max(0, math.ceil(math.log2(n_rows)) - 1)


def _inproj_body(x_ref, g_ref, w_ref, zdn_ref, zrw_ref, zrt_ref):
    h = (_rms(x_ref[...], NORM_EPS) * g_ref[...]).astype(BF16)
    zdn_ref[...] = _dot(h, w_ref[:, 0:DN_SLAB], None)
    zrw_ref[...] = _dot(h, w_ref[:, DN_SLAB:DN_SLAB + RW_SLAB], None)
    zrt_ref[...] = _dot(h, w_ref[:, DN_SLAB + RW_SLAB:P_SLABS], None)


def _inproj(x, gain, w):
    n = x.shape[0]
    tm = min(512, n)
    return pl.pallas_call(
        _inproj_body,
        grid=(n // tm,),
        in_specs=[pl.BlockSpec((tm, D_MODEL), lambda i: (i, 0)),
                  pl.BlockSpec((1, D_MODEL), lambda i: (0, 0)),
                  pl.BlockSpec((D_MODEL, P_SLABS), lambda i: (0, 0))],
        out_specs=[pl.BlockSpec((tm, DN_SLAB), lambda i: (i, 0)),
                   pl.BlockSpec((tm, RW_SLAB), lambda i: (i, 0)),
                   pl.BlockSpec((tm, RT_SLAB), lambda i: (i, 0))],
        out_shape=[jax.ShapeDtypeStruct((n, DN_SLAB), F32),
                   jax.ShapeDtypeStruct((n, RW_SLAB), F32),
                   jax.ShapeDtypeStruct((n, RT_SLAB), F32)],
        compiler_params=pltpu.CompilerParams(
            dimension_semantics=("arbitrary",), vmem_limit_bytes=VMEM_LIMIT),
        name="inproj",
    )(x, gain, w)


def _seq_specs(n_par, c_len, width):
    return pl.BlockSpec((n_par, c_len, width), lambda b, c: (b, c, 0))


def _state_spec(n_par, shape):
    return pl.BlockSpec((n_par,) + shape, lambda b, c: (b,) + (0,) * len(shape))


def _stacked_state_out(layer, stacked, n_par, n_seq, state, in_specs, args):
    shape = jax.ShapeDtypeStruct((N_LAYERS, n_seq) + state, F32)
    zeros = (0,) * len(state)
    if layer == 0:
        return pl.BlockSpec((N_LAYERS, n_par) + state, lambda b, c: (0, b) + zeros), shape, {}
    in_specs.append(pl.BlockSpec(memory_space=pl.ANY))
    args.append(stacked)
    return (pl.BlockSpec((None, n_par) + state, lambda b, c: (layer, b) + zeros), shape,
            {len(args) - 1: 1})


def _own_layer(so_ref, layer):
    if layer > 0:
        return so_ref
    so_ref[1:] = jnp.zeros((so_ref.shape[0] - 1,) + so_ref.shape[1:], F32)
    return so_ref.at[0]


def _const_spec(a):
    return pl.BlockSpec(a.shape, lambda b, c: (0,) * a.ndim)


def _dn_body(c_len, n_valid, fresh, n_par, layer, *refs):
    refs = list(refs)
    if layer > 0:
        del refs[-6]
    if fresh:
        z_ref, cw_ref, hp_ref, nw_ref, o_ref, so_ref, xe, s_scr, o_raw = refs
    else:
        z_ref, c0_ref, s0_ref, cw_ref, hp_ref, nw_ref, o_ref, so_ref, xe, s_scr, o_raw = refs
    ci = pl.program_id(1)

    @pl.when(ci == 0)
    def _():
        xe[:, 0:TAIL, :] = jnp.zeros((n_par, TAIL, DN_CONV), F32)
        s_scr[...] = jnp.zeros_like(s_scr)
        if not fresh:
            xe[:, TAIL - (CONV_W - 1):TAIL, :] = c0_ref[...]
            for p in range(DN_HEADS // 2):
                s_scr[:, p, 0:HEAD, 0:HEAD] = s0_ref[:, 2 * p]
                s_scr[:, p, HEAD:LANE, HEAD:LANE] = s0_ref[:, 2 * p + 1]

    tri_f = _tri(c_len)[0].astype(F32)
    live = lax.broadcasted_iota(jnp.int32, (c_len, LANE), 0) < n_valid
    neg_a = -jnp.exp(hp_ref[0:1, :])
    dt_bias = hp_ref[1:2, :]

    ones_blk = _group_ones()
    eye_l = _tri(LANE)[2].astype(F32)
    eye_8 = eye_l[0:8, :]

    def prepare():
        seqs = []
        for i in range(n_par):
            x = z_ref[i, :, 0:DN_CONV]
            xe[i, TAIL:TAIL + c_len, :] = x
            acc = x * cw_ref[CONV_W - 1:CONV_W, :]
            for j in range(CONV_W - 1):
                off = TAIL - (CONV_W - 1) + j
                acc = acc + xe[i, off:off + c_len, :] * cw_ref[j:j + 1, :]
            xe[i, 0:TAIL, :] = xe[i, c_len:c_len + TAIL, :]
            ab = z_ref[i, :, DN_CONV + DN_QK:DN_SLAB]
            g_tok = neg_a * _softplus(ab + dt_bias)
            beta = jax.nn.sigmoid(ab)
            if n_valid < c_len:
                g_tok = jnp.where(live, g_tok, 0.0)
                beta = jnp.where(live, beta, 0.0)
            conv = _silu(acc)
            seqs.append(dict(conv=conv, v=conv[:, 2 * DN_QK:DN_CONV], beta=beta, g_tok=g_tok,
                             gate=_silu(z_ref[i, :, DN_CONV:DN_CONV + DN_QK])))
        for sq in seqs:
            sq["g_cum"] = _select_sum(tri_f, sq["g_tok"])
            qk = sq["conv"][:, 0:2 * DN_QK]
            sq["qk"] = qk * lax.rsqrt(_group_sum(qk * qk, ones_blk) + 1e-6)
        for sq in seqs:
            sq["g_cum_t"] = _select_sum_nt(eye_8, sq["g_cum"])
        return seqs

    def advance(seqs):
        pt = _PairTimeMasks(c_len)
        first_h = lax.broadcasted_iota(jnp.int32, (1, LANE), 1) < HEAD
        first_rows = lax.broadcasted_iota(jnp.int32, (LANE, 1), 0) < HEAD
        chains = []
        for i, sq in enumerate(seqs):
            for p in range(DN_HEADS // 2):
                ha, hb = 2 * p, 2 * p + 1
                lo = LANE * p
                k = sq["qk"][:, DN_QK + lo:DN_QK + lo + LANE]
                v = sq["v"][:, lo:lo + LANE]
                gc_a, gc_b = sq["g_cum"][:, ha:ha + 1], sq["g_cum"][:, hb:hb + 1]
                gc_row = _cols(sq["g_cum_t"][ha:ha + 1, :], sq["g_cum_t"][hb:hb + 1, :])
                g_end = jnp.where(pt.first, gc_a[c_len - 1:c_len], gc_b[c_len - 1:c_len])
                b = jnp.where(first_h, sq["beta"][:, DN_HEADS + ha:DN_HEADS + ha + 1],
                              sq["beta"][:, DN_HEADS + hb:DN_HEADS + hb + 1])
                egc = jnp.exp(jnp.where(first_h, gc_a, gc_b))
                dmask = jnp.where(
                    pt.incl, jnp.exp(jnp.minimum(jnp.where(pt.first, gc_a, gc_b) - gc_row, 0.0)), 0.0)
                q = sq["qk"][:, lo:lo + LANE] * (HEAD ** -0.5)
                kb = k * b
                chains.append(dict(
                    i=i, p=p, k=k, kb=kb, q=q, qe=q * egc, vb=v * b, kbe=kb * egc, dmask=dmask,
                    k_decay=jnp.exp(g_end - gc_row),
                    s_decay=jnp.exp(jnp.where(first_rows, gc_a[c_len - 1:c_len],
                                              gc_b[c_len - 1:c_len]))))
        for c in chains:
            c["kq"] = _mm_nt(_rows(c["kb"], c["q"], eye_l), _bd(c["k"]))
        for c in chains:
            c["np"] = -jnp.where(pt.strict, c["kq"][0:c_len] * c["dmask"], 0.0)
            c["t"] = pt.eye + c["np"]
        for _ in range(_neumann_levels(n_valid)):
            for c in chains:
                c["np"] = _mm(c["np"], _bd(c["np"])).astype(BF16)
                c["np_bd"] = _bd(c["np"])
            for c in chains:
                c["t"] = c["t"] + _mm(c["t"], c["np_bd"])
        for c in chains:
            c["tx"] = _mm(c["t"], _cols(_bd(c["vb"]), _bd(c["kbe"])))
        for c in chains:
            qk = c["kq"][c_len:2 * c_len] * c["dmask"]
            kd_t = c["kq"][2 * c_len:2 * c_len + LANE] * c["k_decay"]
            c["w"] = _mm(_rows(qk, kd_t), _cols(_bd(c["tx"][:, 0:LANE]), _bd(c["tx"][:, LANE:2 * LANE])))
        for c in chains:
            w = c["w"]
            c["s"] = s_scr[c["i"], c["p"]]
            c["res"] = _mm(_rows(c["qe"] - w[0:c_len, LANE:2 * LANE],
                                 w[c_len:c_len + LANE, LANE:2 * LANE]), c["s"])
        for c in chains:
            w, res = c["w"], c["res"]
            lo = LANE * c["p"]
            s_scr[c["i"], c["p"]] = (c["s"] * c["s_decay"] - res[c_len:c_len + LANE]
                                     + w[c_len:c_len + LANE, 0:LANE])
            o_raw[c["i"], :, lo:lo + LANE] = res[0:c_len] + w[0:c_len, 0:LANE]
        for i, sq in enumerate(seqs):
            o = o_raw[i]
            ms = _group_sum(o * o, ones_blk, terms=1) * (1.0 / HEAD)
            o_ref[i] = o * lax.rsqrt(ms + NORM_EPS) * nw_ref[...] * sq["gate"]

    advance(prepare())

    @pl.when(ci == pl.num_programs(1) - 1)
    def _():
        so = _own_layer(so_ref, layer)
        for p in range(DN_HEADS // 2):
            so[:, 2 * p] = s_scr[:, p, 0:HEAD, 0:HEAD]
            so[:, 2 * p + 1] = s_scr[:, p, HEAD:LANE, HEAD:LANE]


def _dn_group(z, conv0, s0, stacked, conv_w, head_par, norm_w, layer, c_len, n_valid, n_par):
    n_seq, t_pad, _ = z.shape
    fresh = s0 is None
    nc = t_pad // c_len
    assert n_valid == c_len or nc == 1
    state = (DN_HEADS, HEAD, HEAD)
    in_specs = [_seq_specs(n_par, c_len, DN_SLAB)]
    args = [z]
    if not fresh:
        in_specs += [_state_spec(n_par, (CONV_W - 1, DN_CONV)), _state_spec(n_par, state)]
        args += [conv0, s0]
    for p in (conv_w, head_par, norm_w):
        in_specs.append(_const_spec(p))
        args.append(p)
    so_spec, so_shape, aliases = _stacked_state_out(layer, stacked, n_par, n_seq, state, in_specs, args)
    return pl.pallas_call(
        functools.partial(_dn_body, c_len, n_valid, fresh, n_par, layer),
        grid=(n_seq // n_par, nc),
        in_specs=in_specs,
        out_specs=[_seq_specs(n_par, c_len, DN_QK), so_spec],
        out_shape=[jax.ShapeDtypeStruct((n_seq, t_pad, DN_QK), F32), so_shape],
        input_output_aliases=aliases,
        scratch_shapes=[pltpu.VMEM((n_par, c_len + TAIL, DN_CONV), F32),
                        pltpu.VMEM((n_par, DN_HEADS // 2, LANE, LANE), F32),
                        pltpu.VMEM((n_par, c_len, DN_QK), F32)],
        compiler_params=pltpu.CompilerParams(
            dimension_semantics=("arbitrary", "arbitrary"), vmem_limit_bytes=VMEM_LIMIT),
        name="dn_group",
    )(*args)


def _rw_body(c_len, n_valid, fresh, n_par, layer, *refs):
    refs = list(refs)
    if layer > 0:
        del refs[-6]
    if fresh:
        (z_ref, mu_ref, w0_ref, w2_ref, a0_ref, a2_ref, g2_ref, kk_ref, ka_ref, rk_ref,
         lng_ref, lnb_ref, o_ref, so_ref, xe, s_scr, y_raw) = refs
    else:
        (z_ref, sh0_ref, s0_ref, mu_ref, w0_ref, w2_ref, a0_ref, a2_ref, g2_ref, kk_ref, ka_ref,
         rk_ref, lng_ref, lnb_ref, o_ref, so_ref, xe, s_scr, y_raw) = refs
    ci = pl.program_id(1)

    @pl.when(ci == 0)
    def _():
        xe[:, 0:TAIL, :] = jnp.zeros((n_par, TAIL, RW_SLAB), F32)
        s_scr[...] = jnp.zeros_like(s_scr)
        if not fresh:
            xe[:, TAIL - 1:TAIL, :] = sh0_ref[...]
            for p in range(RW_HEADS // 2):
                s_scr[:, p, 0:HEAD, 0:HEAD] = s0_ref[:, 2 * p]
                s_scr[:, p, HEAD:LANE, HEAD:LANE] = s0_ref[:, 2 * p + 1]

    tri_f = _tri(c_len)[0].astype(F32)
    live = lax.broadcasted_iota(jnp.int32, (c_len, RW_W), 0) < n_valid
    ones_blk = _group_ones()
    eye_l = _tri(LANE)[2].astype(F32)

    seqs = []
    for i in range(n_par):
        x = z_ref[i]
        xe[i, TAIL:TAIL + c_len, :] = x
        prev = xe[i, TAIL - 1:TAIL - 1 + c_len, :]
        zs = x + (prev - x) * mu_ref[...]
        xe[i, 0:TAIL, :] = xe[i, c_len:c_len + TAIL, :]
        seqs.append(dict(r=zs[:, 0:RW_W], k=zs[:, RW_W:2 * RW_W], v=zs[:, 2 * RW_W:3 * RW_W],
                         wl=zs[:, 3 * RW_W:3 * RW_W + 64], al=zs[:, 3 * RW_W + 64:3 * RW_W + 128],
                         gl=zs[:, 3 * RW_W + 128:RW_SLAB]))
    for sq in seqs:
        sq["w_lora"] = _mm(jnp.tanh(sq["wl"]), w2_ref[...])
        sq["a_lora"] = _mm(sq["al"], a2_ref[...])
        sq["g"] = _mm(jax.nn.sigmoid(sq["gl"]), g2_ref[...])
    for sq in seqs:
        w_log = -_softplus(-(w0_ref[...] + sq["w_lora"])) - 0.5
        a = jax.nn.sigmoid(a0_ref[...] + sq["a_lora"])
        kk_in = sq["k"] * kk_ref[...]
        k2 = sq["k"] * (1.0 + (a - 1.0) * ka_ref[...])
        lw = -jnp.exp(w_log)
        if n_valid < c_len:
            lw = jnp.where(live, lw, 0.0)
            k2 = jnp.where(live, k2, 0.0)
            kk_in = jnp.where(live, kk_in, 0.0)
        sq.update(a=a, kk_in=kk_in, k2=k2, lw=lw)
    for sq in seqs:
        sq["gcum"] = _select_sum(tri_f, sq["lw"])
        kk_in = sq["kk_in"]
        sq["kk"] = kk_in * lax.rsqrt(_group_sum(kk_in * kk_in, ones_blk) + 1e-6)
    for sq in seqs:
        gcum = sq["gcum"]
        e_neg = jnp.exp(-gcum)
        g_end = gcum[c_len - 1:c_len, :]
        e_rest = jnp.exp(g_end - gcum)
        bv = sq["kk"] * sq["a"]
        sq.update(at=-sq["kk"] * jnp.exp(gcum - sq["lw"]), bt=bv * e_neg, kt=sq["k2"] * e_neg,
                  rt=sq["r"] * jnp.exp(gcum), b_rest=bv * e_rest, k_rest=sq["k2"] * e_rest,
                  e_end=jnp.exp(g_end))

    pt = _PairTimeMasks(c_len)
    c2 = 2 * c_len
    chains = []
    for i, sq in enumerate(seqs):
        for p in range(RW_HEADS // 2):
            sl = slice(LANE * p, LANE * (p + 1))
            chains.append(dict(i=i, p=p, sl=sl, at=sq["at"][:, sl], rt=sq["rt"][:, sl],
                               bt_bd=_bd(sq["bt"][:, sl]), kt_bd=_bd(sq["kt"][:, sl]),
                               v_bd=_bd(sq["v"][:, sl]), e_end=sq["e_end"][:, sl],
                               rest=_rows(_bd(sq["b_rest"][:, sl]), _bd(sq["k_rest"][:, sl]))))
    for c in chains:
        c["at_bd"] = _bd(c["at"])
        c["prod"] = _mm_nt(_rows(c["at"], c["rt"]), _rows(c["bt_bd"], c["kt_bd"]))
        c["tr"] = _mm_nt(eye_l, _rows(c["at_bd"], c["v_bd"]))
    for c in chains:
        prod = c["prod"]
        c["np"] = jnp.where(pt.strict, prod[0:c_len, 0:c2], 0.0)
        c["t"] = pt.eye + c["np"]
        c["nak"] = jnp.where(pt.strict, prod[0:c_len, c2:2 * c2], 0.0)
        c["mix"] = _cols(jnp.where(pt.incl, prod[c_len:c2, 0:c2], 0.0),
                         jnp.where(pt.incl, prod[c_len:c2, c2:2 * c2], 0.0))
        c["at_t"] = c["tr"][:, 0:c2]
        c["v_t"] = c["tr"][:, c2:2 * c2]
    for c in chains:
        c["nakv"] = _mm(c["nak"], c["v_bd"])
        c["nakv_t"] = _mm_nt(c["v_t"], _bd(c["nak"]))
    for _ in range(_neumann_levels(n_valid)):
        for c in chains:
            c["np"] = _mm(c["np"], _bd(c["np"])).astype(BF16)
            c["np_bd"] = _bd(c["np"])
        for c in chains:
            c["t"] = c["t"] + _mm(c["t"], c["np_bd"])
    for c in chains:
        c["tx"] = _mm(c["t"], _cols(c["at_bd"], _bd(c["nakv"])))
        c["tx_t"] = _mm_nt(_rows(c["at_t"], c["nakv_t"]), _bd(c["t"]))
    for c in chains:
        tx, tx_t = c["tx"], c["tx_t"]
        c["zy"] = _mm(c["mix"], _rows(_cols(_bd(tx[:, 0:LANE]), _bd(tx[:, LANE:2 * LANE])),
                                      _cols(jnp.zeros((c2, LANE), F32), c["v_bd"])))
        ta_t = tx_t[0:LANE]
        c["m"] = _mm(_rows(_cols(ta_t, jnp.zeros_like(ta_t)), _cols(tx_t[LANE:2 * LANE], c["v_t"])),
                     c["rest"])
    for c in chains:
        c["s"] = s_scr[c["i"], c["p"]]
        c["y"] = _mm_nt(c["rt"] + c["zy"][:, 0:LANE], c["s"])
        c["sm"] = _mm(c["s"], c["m"][0:LANE])
    for c in chains:
        s_scr[c["i"], c["p"]] = c["s"] * c["e_end"] + c["sm"] + c["m"][LANE:2 * LANE]
        y_raw[c["i"], :, c["sl"]] = c["y"] + c["zy"][:, LANE:2 * LANE]
    for i, sq in enumerate(seqs):
        y = y_raw[i]
        d = y - _group_sum(y, ones_blk) * (1.0 / HEAD)
        yn = d * lax.rsqrt(_group_sum(d * d, ones_blk, terms=1) * (1.0 / HEAD) + RW_GN_EPS)
        yn = yn * lng_ref[...] + lnb_ref[...]
        bonus = _group_sum(sq["r"] * sq["k2"] * rk_ref[...], ones_blk, terms=1) * sq["v"]
        o_ref[i] = (yn + bonus) * sq["g"]

    @pl.when(ci == pl.num_programs(1) - 1)
    def _():
        so = _own_layer(so_ref, layer)
        for p in range(RW_HEADS // 2):
            so[:, 2 * p] = s_scr[:, p, 0:HEAD, 0:HEAD]
            so[:, 2 * p + 1] = s_scr[:, p, HEAD:LANE, HEAD:LANE]


def _rw_group(z, shift0, s0, stacked, par, layer, c_len, n_valid, n_par):
    n_seq, t_pad, _ = z.shape
    fresh = s0 is None
    nc = t_pad // c_len
    assert n_valid == c_len or nc == 1
    state = (RW_HEADS, HEAD, HEAD)
    in_specs = [_seq_specs(n_par, c_len, RW_SLAB)]
    args = [z]
    if not fresh:
        in_specs += [_state_spec(n_par, (1, RW_SLAB)), _state_spec(n_par, state)]
        args += [shift0, s0]
    for p in par:
        in_specs.append(_const_spec(p))
        args.append(p)
    so_spec, so_shape, aliases = _stacked_state_out(layer, stacked, n_par, n_seq, state, in_specs, args)
    return pl.pallas_call(
        functools.partial(_rw_body, c_len, n_valid, fresh, n_par, layer),
        grid=(n_seq // n_par, nc),
        in_specs=in_specs,
        out_specs=[_seq_specs(n_par, c_len, RW_W), so_spec],
        out_shape=[jax.ShapeDtypeStruct((n_seq, t_pad, RW_W), F32), so_shape],
        input_output_aliases=aliases,
        scratch_shapes=[pltpu.VMEM((n_par, c_len + TAIL, RW_SLAB), F32),
                        pltpu.VMEM((n_par, RW_HEADS // 2, LANE, LANE), F32),
                        pltpu.VMEM((n_par, c_len, RW_W), F32)],
        compiler_params=pltpu.CompilerParams(
            dimension_semantics=("arbitrary", "arbitrary"), vmem_limit_bytes=VMEM_LIMIT),
        name="rw_group",
    )(*args)


def _rt_body(c_len, n_valid, fresh, n_par, *refs):
    if fresh:
        z_ref, cos_ref, sin_ref, lng_ref, o_ref, so_ref, s_scr, o_raw = refs
    else:
        z_ref, cos_ref, sin_ref, s0_ref, lng_ref, o_ref, so_ref, s_scr, o_raw = refs
    ci = pl.program_id(1)

    @pl.when(ci == 0)
    def _():
        if fresh:
            s_scr[...] = jnp.zeros_like(s_scr)
        else:
            s_scr[...] = s0_ref[...]

    lane = lax.broadcasted_iota(jnp.int32, (c_len, RT_QK), 1)
    first_half = (lane & (RT_DK - 1)) < (RT_DK // 2)
    cos = cos_ref[...]
    sin = sin_ref[...]

    def rotary(t):
        partner = jnp.where(first_half,
                            pltpu.roll(t, RT_QK - RT_DK // 2, axis=1),
                            pltpu.roll(t, RT_DK // 2, axis=1))
        return t * cos + partner * sin

    incl, _, _ = _tri(c_len)
    ri = lax.broadcasted_iota(jnp.int32, (c_len, c_len), 0)
    cj = lax.broadcasted_iota(jnp.int32, (c_len, c_len), 1)
    delta = (ri - cj).astype(F32)
    idx = lax.broadcasted_iota(jnp.int32, (c_len, 1), 0).astype(F32)
    log_decay = [math.log1p(-(2.0 ** (-5.0 - h))) for h in range(RT_HEADS)]
    dmasks = [jnp.where(incl, jnp.exp(delta * lg), 0.0) for lg in log_decay]
    q_scale = [jnp.exp((idx + 1.0) * lg) for lg in log_decay]
    t_row = lax.broadcasted_iota(jnp.int32, (1, c_len), 1).astype(F32)
    k_scale = [jnp.where(t_row < n_valid, jnp.exp((n_valid - 1.0 - t_row) * lg), 0.0) for lg in log_decay]
    ones_blk = _group_ones()
    eye_k = _tri(RT_DK)[2].astype(F32)

    chains = []
    for i in range(n_par):
        q = rotary(z_ref[i, :, 0:RT_QK])
        k = rotary(z_ref[i, :, RT_QK:2 * RT_QK]) * (RT_DK ** -0.5)
        for h in range(RT_HEADS):
            chains.append(dict(
                i=i, h=h, qh=q[:, RT_DK * h:RT_DK * (h + 1)], kh=k[:, RT_DK * h:RT_DK * (h + 1)],
                vh=z_ref[i, :, 2 * RT_QK + HEAD * h:2 * RT_QK + HEAD * (h + 1)]))
    for c in chains:
        c["qk"] = _mm_nt(_rows(c["qh"], eye_k), c["kh"])
    for c in chains:
        h = c["h"]
        c["s"] = s_scr[c["i"], h]
        c["cross"] = _mm(c["qh"] * q_scale[h], c["s"])
        c["kv"] = _mm(c["qk"][c_len:c_len + RT_DK] * k_scale[h], c["vh"])
        c["inner"] = _mm(c["qk"][0:c_len] * dmasks[h], c["vh"])
    for c in chains:
        i, h = c["i"], c["h"]
        s_scr[i, h] = c["s"] * math.exp(n_valid * log_decay[h]) + c["kv"]
        o_raw[i, :, HEAD * h:HEAD * (h + 1)] = c["inner"] + c["cross"]
    for i in range(n_par):
        o = o_raw[i]
        d = o - _group_sum(o, ones_blk) * (1.0 / HEAD)
        on = d * lax.rsqrt(_group_sum(d * d, ones_blk, terms=1) * (1.0 / HEAD) + RT_GN_EPS)
        o_ref[i] = on * lng_ref[...] * _silu(z_ref[i, :, 2 * RT_QK + RT_W:RT_SLAB])

    @pl.when(ci == pl.num_programs(1) - 1)
    def _():
        so_ref[...] = s_scr[...]


def _rt_group(z, cos, sin, s0, ln_g, c_len, n_valid, n_par):
    n_seq, t_pad, _ = z.shape
    fresh = s0 is None
    nc = t_pad // c_len
    assert n_valid == c_len or nc == 1
    state = (RT_HEADS, RT_DK, HEAD)
    table = pl.BlockSpec((c_len, RT_QK), lambda b, c: (c, 0))
    in_specs = [_seq_specs(n_par, c_len, RT_SLAB), table, table]
    args = [z, cos, sin]
    if not fresh:
        in_specs.append(_state_spec(n_par, state))
        args.append(s0)
    in_specs.append(_const_spec(ln_g))
    args.append(ln_g)
    return pl.pallas_call(
        functools.partial(_rt_body, c_len, n_valid, fresh, n_par),
        grid=(n_seq // n_par, nc),
        in_specs=in_specs,
        out_specs=[_seq_specs(n_par, c_len, RT_W), _state_spec(n_par, state)],
        out_shape=[jax.ShapeDtypeStruct((n_seq, t_pad, RT_W), F32),
                   jax.ShapeDtypeStruct((n_seq,) + state, F32)],
        scratch_shapes=[pltpu.VMEM((n_par,) + state, F32),
                        pltpu.VMEM((n_par, c_len, RT_W), F32)],
        compiler_params=pltpu.CompilerParams(
            dimension_semantics=("arbitrary", "arbitrary"), vmem_limit_bytes=VMEM_LIMIT),
        name="rt_group",
    )(*args)


def _rotary_tables(pos0, t_pad):
    half = RT_DK // 2
    inv = ROPE_BASE ** (-np.arange(half, dtype=np.float64) / half)
    ang = (pos0 + np.arange(t_pad, dtype=np.float64))[:, None] * inv[None, :]
    cos = np.tile(np.concatenate([np.cos(ang), np.cos(ang)], axis=1), (1, RT_HEADS))
    sin = np.tile(np.concatenate([-np.sin(ang), np.sin(ang)], axis=1), (1, RT_HEADS))
    return jnp.asarray(cos, F32), jnp.asarray(sin, F32)


def _outproj_body(x_ref, odn_ref, orw_ref, ort_ref, w_ref, o_ref):
    acc = x_ref[...]
    acc = acc + _dot(odn_ref[...].astype(BF16), w_ref[0:DN_QK, :], None)
    acc = acc + _dot(orw_ref[...].astype(BF16), w_ref[DN_QK:DN_QK + RW_W, :], None)
    acc = acc + _dot(ort_ref[...].astype(BF16), w_ref[DN_QK + RW_W:D_MODEL, :], None)
    o_ref[...] = acc


def _outproj(x, o_dn, o_rw, o_rt, w):
    n = x.shape[0]
    tm = min(512, n)
    row = lambda i: (i, 0)
    return pl.pallas_call(
        _outproj_body,
        grid=(n // tm,),
        in_specs=[pl.BlockSpec((tm, D_MODEL), row),
                  pl.BlockSpec((tm, DN_QK), row),
                  pl.BlockSpec((tm, RW_W), row),
                  pl.BlockSpec((tm, RT_W), row),
                  pl.BlockSpec((D_MODEL, D_MODEL), lambda i: (0, 0))],
        out_specs=pl.BlockSpec((tm, D_MODEL), row),
        out_shape=jax.ShapeDtypeStruct((n, D_MODEL), F32),
        compiler_params=pltpu.CompilerParams(
            dimension_semantics=("arbitrary",), vmem_limit_bytes=VMEM_LIMIT),
        name="outproj",
    )(x, o_dn, o_rw, o_rt, w)


def _swiglu(hb, w1, w3, w2):
    return _dot((_silu(_dot(hb, w1)) * _dot(hb, w3)).astype(BF16), w2)


def _ffn_body(final_norm, *refs):
    if final_norm:
        x_ref, g_ref, w1_ref, w3_ref, w2_ref, gf_ref, o_ref, h_scr, acc = refs
    else:
        x_ref, g_ref, w1_ref, w3_ref, w2_ref, o_ref, h_scr, acc = refs
    f = pl.program_id(1)

    @pl.when(f == 0)
    def _():
        x = x_ref[...]
        h_scr[...] = (_rms(x, NORM_EPS) * g_ref[...]).astype(BF16)
        acc[...] = x

    acc[...] += _swiglu(h_scr[...], w1_ref[...], w3_ref[...], w2_ref[...])

    @pl.when(f == pl.num_programs(1) - 1)
    def _():
        if final_norm:
            o_ref[...] = _rms(acc[...], NORM_EPS) * gf_ref[...]
        else:
            o_ref[...] = acc[...]


def _ffn(x, gain, w1, w3, w2, final_gain, tf):
    n = x.shape[0]
    d_f = w1.shape[1]
    final_norm = final_gain is not None
    tm = min(1024, n)
    in_specs = [pl.BlockSpec((tm, D_MODEL), lambda i, f: (i, 0)),
                pl.BlockSpec((1, D_MODEL), lambda i, f: (0, 0)),
                pl.BlockSpec((D_MODEL, tf), lambda i, f: (0, f)),
                pl.BlockSpec((D_MODEL, tf), lambda i, f: (0, f)),
                pl.BlockSpec((tf, D_MODEL), lambda i, f: (f, 0))]
    args = [x, gain, w1, w3, w2]
    if final_norm:
        in_specs.append(pl.BlockSpec((1, D_MODEL), lambda i, f: (0, 0)))
        args.append(final_gain)
    return pl.pallas_call(
        functools.partial(_ffn_body, final_norm),
        grid=(n // tm, d_f // tf),
        in_specs=in_specs,
        out_specs=pl.BlockSpec((tm, D_MODEL), lambda i, f: (i, 0)),
        out_shape=jax.ShapeDtypeStruct((n, D_MODEL), F32),
        scratch_shapes=[pltpu.VMEM((tm, D_MODEL), BF16), pltpu.VMEM((tm, D_MODEL), F32)],
        compiler_params=pltpu.CompilerParams(
            dimension_semantics=("arbitrary", "arbitrary"), vmem_limit_bytes=VMEM_LIMIT),
        name="ffn_dense",
    )(*args)


def _route_body(x_ref, g_ref, router_ref, o_ref):
    h = _rms(x_ref[...], NORM_EPS) * g_ref[...]
    h1, h2, _ = _split3(h)
    r1, r2, _ = _split3(router_ref[...])
    logits = (jnp.dot(h1, r1, preferred_element_type=F32) + jnp.dot(h1, r2, preferred_element_type=F32)
              + jnp.dot(h2, r1, preferred_element_type=F32))
    lane = lax.broadcasted_iota(jnp.int32, logits.shape, 1).astype(F32)
    lg = jnp.where(lane < N_EXP, logits, -jnp.inf)
    m1 = jnp.max(lg, axis=-1, keepdims=True)
    i1 = jnp.min(jnp.where(lg == m1, lane, float(LANE)), axis=-1, keepdims=True)
    lg2 = jnp.where(lane == i1, -jnp.inf, lg)
    m2 = jnp.max(lg2, axis=-1, keepdims=True)
    i2 = jnp.min(jnp.where(lg2 == m2, lane, float(LANE)), axis=-1, keepdims=True)
    e2 = jnp.exp(m2 - m1)
    den = 1.0 + e2
    o_ref[...] = (jnp.where(lane == 0.0, i1, 0.0) + jnp.where(lane == 1.0, i2, 0.0)
                  + jnp.where(lane == 2.0, 1.0 / den, 0.0) + jnp.where(lane == 3.0, e2 / den, 0.0))


def _route(x, gain, router):
    n = x.shape[0]
    tm = min(512, n)
    return pl.pallas_call(
        _route_body,
        grid=(n // tm,),
        in_specs=[pl.BlockSpec((tm, D_MODEL), lambda i: (i, 0)),
                  pl.BlockSpec((1, D_MODEL), lambda i: (0, 0)),
                  pl.BlockSpec((D_MODEL, LANE), lambda i: (0, 0))],
        out_specs=pl.BlockSpec((tm, LANE), lambda i: (i, 0)),
        out_shape=jax.ShapeDtypeStruct((n, LANE), F32),
        compiler_params=pltpu.CompilerParams(
            dimension_semantics=("arbitrary",), vmem_limit_bytes=VMEM_LIMIT),
        name="moe_route",
    )(x, gain, router)


def _row_gather(src_hbm, idx_ref, buf, sem, slot, n_rows):
    def copy(r):
        return pltpu.make_async_copy(src_hbm.at[pl.ds(idx_ref[0, 0, r], 1), :],
                                     buf.at[slot, pl.ds(r, 1), :], sem.at[slot])

    def start():
        def body(r, carry):
            copy(r).start()
            return carry
        lax.fori_loop(0, n_rows, body, 0, unroll=8)

    def wait():
        def body(r, carry):
            copy(r).wait()
            return carry
        lax.fori_loop(0, n_rows, body, 0, unroll=8)

    return start, wait, copy


def _moe_ffn_body(tm, te_ref, nu_ref, tok_ref, tok_next_ref, x_hbm, g_ref, w1_ref, w3_ref, w2_ref,
                  ys_ref, xbuf, sem, h_scr, acc):
    i = pl.program_id(0)
    f = pl.program_id(1)
    n_f = pl.num_programs(1)
    slot = lax.rem(i, 2)
    n_used = nu_ref[0]
    used = i < n_used
    rows_per_step = tm // MOE_F_STEPS
    start_cur, wait_cur, _ = _row_gather(x_hbm, tok_ref, xbuf, sem, slot, tm)
    _, wait_next, copy_next = _row_gather(x_hbm, tok_next_ref, xbuf, sem, 1 - slot, tm)

    @pl.when((f == 0) & (i <= n_used))
    def _():
        pl.when(i == 0)(start_cur)
        wait_cur()

    @pl.when((f == 0) & used)
    def _():
        h_scr[...] = (_rms(xbuf[slot], NORM_EPS) * g_ref[...]).astype(BF16)

    @pl.when(used)
    def _():
        y = _swiglu(h_scr[...], w1_ref[0], w3_ref[0], w2_ref[0])
        for j in range(rows_per_step):
            copy_next(f * rows_per_step + j).start()

        @pl.when(f == 0)
        def _():
            acc[...] = y

        @pl.when(f > 0)
        def _():
            acc[...] += y

    @pl.when(f == n_f - 1)
    def _():
        ys_ref[...] = jnp.where(used, acc[...], 0.0)
        pl.when(used & (i == pl.num_programs(0) - 1))(wait_next)


def _moe_ffn(x, gain, tok, tile_expert, n_used, w1, w3, w2, tm, tf):
    n_tiles = tok.shape[0]
    n_f = D_EXP // tf

    def w_block(i, f, te, nu):
        last = nu[0] - 1
        return te[jnp.minimum(i, last)], jnp.where(i <= last, f, n_f - 1)

    def w13_idx(i, f, te, nu):
        e, ff = w_block(i, f, te, nu)
        return (e, 0, ff)

    def w2_idx(i, f, te, nu):
        e, ff = w_block(i, f, te, nu)
        return (e, ff, 0)

    smem_tile = lambda idx: pl.BlockSpec((1, 1, tm), idx, memory_space=pltpu.SMEM)
    grid_spec = pltpu.PrefetchScalarGridSpec(
        num_scalar_prefetch=2,
        grid=(n_tiles, n_f),
        in_specs=[smem_tile(lambda i, f, te, nu: (i, 0, 0)),
                  smem_tile(lambda i, f, te, nu: (jnp.minimum(i + 1, n_tiles - 1), 0, 0)),
                  pl.BlockSpec(memory_space=pl.ANY),
                  pl.BlockSpec((1, D_MODEL), lambda i, f, te, nu: (0, 0)),
                  pl.BlockSpec((1, D_MODEL, tf), w13_idx),
                  pl.BlockSpec((1, D_MODEL, tf), w13_idx),
                  pl.BlockSpec((1, tf, D_MODEL), w2_idx)],
        out_specs=pl.BlockSpec((tm, D_MODEL), lambda i, f, te, nu: (i, 0)),
        scratch_shapes=[pltpu.VMEM((2, tm, D_MODEL), F32),
                        pltpu.SemaphoreType.DMA((2,)),
                        pltpu.VMEM((tm, D_MODEL), BF16),
                        pltpu.VMEM((tm, D_MODEL), F32)])
    return pl.pallas_call(
        functools.partial(_moe_ffn_body, tm),
        grid_spec=grid_spec,
        out_shape=jax.ShapeDtypeStruct((n_tiles * tm, D_MODEL), F32),
        compiler_params=pltpu.CompilerParams(
            dimension_semantics=("arbitrary", "arbitrary"), vmem_limit_bytes=VMEM_LIMIT),
        name="moe_ffn",
    )(tile_expert, n_used, tok, tok, x, gain, w1, w3, w2)


def _moe_combine_body(tm, final_norm, *refs):
    if final_norm:
        rows_ref, rows_next_ref, x_ref, route_ref, ys_hbm, gf_ref, o_ref, ybuf, sem = refs
    else:
        rows_ref, rows_next_ref, x_ref, route_ref, ys_hbm, o_ref, ybuf, sem = refs
    i = pl.program_id(0)
    slot = lax.rem(i, 2)
    _, wait_next, copy_next = _row_gather(ys_hbm, rows_next_ref, ybuf, sem, 1 - slot, 2 * tm)
    start_cur, wait_cur, _ = _row_gather(ys_hbm, rows_ref, ybuf, sem, slot, 2 * tm)
    pl.when(i == 0)(start_cur)
    wait_cur()
    route = route_ref[...]
    out = x_ref[...] + (route[:, 2:3] * ybuf[slot, 0:tm, :] + route[:, 3:4] * ybuf[slot, tm:2 * tm, :])
    if final_norm:
        out = _rms(out, NORM_EPS) * gf_ref[...]
    o_ref[...] = out
    for r in range(2 * tm):
        copy_next(r).start(priority=r % 2)
    pl.when(i == pl.num_programs(0) - 1)(wait_next)


def _moe_combine(x, route, rows, ys, final_gain, tm):
    n = x.shape[0]
    n_tiles = n // tm
    final_norm = final_gain is not None
    smem_tile = lambda idx: pl.BlockSpec((1, 1, 2 * tm), idx, memory_space=pltpu.SMEM)
    in_specs = [smem_tile(lambda i: (i, 0, 0)),
                smem_tile(lambda i: (jnp.minimum(i + 1, n_tiles - 1), 0, 0)),
                pl.BlockSpec((tm, D_MODEL), lambda i: (i, 0)),
                pl.BlockSpec((tm, LANE), lambda i: (i, 0)),
                pl.BlockSpec(memory_space=pl.ANY)]
    args = [rows, rows, x, route, ys]
    if final_norm:
        in_specs.append(pl.BlockSpec((1, D_MODEL), lambda i: (0, 0)))
        args.append(final_gain)
    return pl.pallas_call(
        functools.partial(_moe_combine_body, tm, final_norm),
        grid=(n_tiles,),
        in_specs=in_specs,
        out_specs=pl.BlockSpec((tm, D_MODEL), lambda i: (i, 0)),
        out_shape=jax.ShapeDtypeStruct((n, D_MODEL), F32),
        scratch_shapes=[pltpu.VMEM((2, 2 * tm, D_MODEL), F32), pltpu.SemaphoreType.DMA((2,))],
        compiler_params=pltpu.CompilerParams(
            dimension_semantics=("arbitrary",), vmem_limit_bytes=VMEM_LIMIT),
        name="moe_combine",
    )(*args)


def _moe(x, gain, router, w1, w3, w2, final_gain):
    n = x.shape[0]
    tm = 512 if n >= 8192 else 256
    tm_c = 256
    n_tiles = -(-(2 * n + N_EXP * (tm - 1)) // tm)
    route = _route(x, gain, router)
    i1 = route[:, 0].astype(jnp.int32)
    i2 = route[:, 1].astype(jnp.int32)
    experts = jnp.arange(N_EXP, dtype=jnp.int32)
    hit = ((i1[:, None] == experts) | (i2[:, None] == experts)).astype(jnp.int32)
    rank = jnp.cumsum(hit, axis=0) - hit
    tiles_e = (jnp.sum(hit, axis=0) + tm - 1) // tm
    tile_end = jnp.cumsum(tiles_e)
    row_off = (tile_end - tiles_e) * tm
    row1 = row_off[i1] + jnp.take_along_axis(rank, i1[:, None], axis=1)[:, 0]
    row2 = row_off[i2] + jnp.take_along_axis(rank, i2[:, None], axis=1)[:, 0]
    token = jnp.arange(n, dtype=jnp.int32)
    tok = jnp.zeros((n_tiles * tm,), jnp.int32).at[jnp.concatenate([row1, row2])].set(
        jnp.concatenate([token, token]), unique_indices=True, mode="promise_in_bounds")
    tile_ids = jnp.arange(n_tiles, dtype=jnp.int32)
    tile_expert = jnp.minimum(
        jnp.sum((tile_end[None, :] <= tile_ids[:, None]).astype(jnp.int32), axis=1), N_EXP - 1)
    n_used = tile_end[N_EXP - 1:].astype(jnp.int32)
    ys = _moe_ffn(x, gain, tok.reshape(n_tiles, 1, tm), tile_expert, n_used, w1, w3, w2,
                  tm, D_EXP // MOE_F_STEPS)
    rows = jnp.concatenate([row1.reshape(n // tm_c, 1, tm_c), row2.reshape(n // tm_c, 1, tm_c)],
                           axis=2).astype(jnp.int32)
    return _moe_combine(x, route, rows, ys, final_gain, tm_c)


def _pad_last(a, width):
    return jnp.pad(a, [(0, 0)] * (a.ndim - 1) + [(0, width - a.shape[-1])])


def _pack_params(p):
    w_in = p["w_in"]
    w_in = jnp.concatenate([_pad_last(w_in[..., :DN_P], DN_SLAB),
                            _pad_last(w_in[..., DN_P:DN_P + RW_P], RW_SLAB),
                            w_in[..., DN_P + RW_P:]], axis=-1).astype(BF16)
    head_par = jnp.stack([_pad_last(p["dn_a_log"], LANE), _pad_last(p["dn_dt_bias"], LANE)], axis=1)
    g2 = jnp.pad(p["rw_g2"], ((0, 0), (0, RW_GATE_PAD - RW_GATE_R), (0, 0)))
    row = lambda a: a[:, None, :]
    return dict(
        ln_mix=row(p["ln_mix"]), w_in=w_in, dn_conv=p["dn_conv"], dn_head=head_par,
        dn_norm=row(jnp.tile(p["dn_norm"], (1, DN_HEADS))),
        rw=[(row(_pad_last(p["rw_mu"], RW_SLAB))[l], row(p["rw_w0"])[l], p["rw_w2"][l],
             row(p["rw_a0"])[l], p["rw_a2"][l], g2[l], row(p["rw_kk"])[l], row(p["rw_ka"])[l],
             p["rw_rk"].reshape(N_LAYERS, 1, RW_W)[l], row(p["rw_ln_g"])[l], row(p["rw_ln_b"])[l])
            for l in range(N_LAYERS)],
        rt_ln=row(p["rt_ln"]), w_out=p["w_out"].astype(BF16), ln_ffn=row(p["ln_ffn"]),
        ffn_w1=p["ffn_w1"].astype(BF16), ffn_w3=p["ffn_w3"].astype(BF16),
        ffn_w2=p["ffn_w2"].astype(BF16),
        moe_router=_pad_last(p["moe_router"], LANE),
        moe_w1=p["moe_w1"].astype(BF16), moe_w3=p["moe_w3"].astype(BF16),
        moe_w2=p["moe_w2"].astype(BF16),
        ln_final=p["ln_final"][None, :],
    )


def _trunk(x, n_seq, t_real, t_pad, c_len, n_par, states, pos0, w):
    n_valid = min(c_len, t_real)
    cos, sin = _rotary_tables(pos0, t_pad)
    convs, shifts, rts = [], [], []
    dns = rws = None
    seq = lambda a: a.reshape(n_seq, t_pad, a.shape[-1])
    flat = lambda a: a.reshape(n_seq * t_pad, a.shape[-1])
    for l in range(N_LAYERS):
        z_dn, z_rw, z_rt = (seq(z) for z in _inproj(x, w["ln_mix"][l], w["w_in"][l]))
        if states is None:
            conv0 = s_dn0 = shift0 = s_rw0 = s_rt0 = None
        else:
            conv0, s_dn0 = states[0][l], states[1][l]
            shift0 = _pad_last(states[2][l], RW_SLAB)[:, None, :]
            s_rw0, s_rt0 = states[3][l], states[4][l]
        o_dn, dns = _dn_group(z_dn, conv0, s_dn0, dns, w["dn_conv"][l], w["dn_head"][l],
                              w["dn_norm"][l], l, c_len, n_valid, n_par)
        o_rw, rws = _rw_group(z_rw, shift0, s_rw0, rws, w["rw"][l], l, c_len, n_valid, n_par)
        o_rt, s_rt = _rt_group(z_rt, cos, sin, s_rt0, w["rt_ln"][l], c_len, n_valid, n_par)
        x = _outproj(x, flat(o_dn), flat(o_rw), flat(o_rt), w["w_out"][l])
        if l % 2 == 0:
            x = _ffn(x, w["ln_ffn"][l], w["ffn_w1"][l // 2], w["ffn_w3"][l // 2], w["ffn_w2"][l // 2],
                     w["ln_final"] if l == N_LAYERS - 1 else None, D_FF // 2)
        else:
            xr = seq(x)[:, :t_real].reshape(n_seq * t_real, D_MODEL)
            xr = _moe(xr, w["ln_ffn"][l], w["moe_router"][l // 2], w["moe_w1"][l // 2],
                      w["moe_w3"][l // 2], w["moe_w2"][l // 2],
                      w["ln_final"] if l == N_LAYERS - 1 else None)
            x = xr if l == N_LAYERS - 1 else flat(
                jnp.pad(xr.reshape(n_seq, t_real, D_MODEL), ((0, 0), (0, t_pad - t_real), (0, 0))))
        if t_real >= CONV_W - 1:
            convs.append(z_dn[:, t_real - (CONV_W - 1):t_real, :DN_CONV])
        else:
            convs.append(jnp.concatenate([conv0[:, t_real:], z_dn[:, :t_real, :DN_CONV]], axis=1))
        shifts.append(z_rw[:, t_real - 1, :RW_P])
        rts.append(s_rt)
    return x, jnp.stack(convs), dns, jnp.stack(shifts), rws, jnp.stack(rts)


def kernel(x_prompt, x_sample, state_dn_conv, state_dn, state_rw_shift, state_rw, state_rt, ln_mix, w_in, dn_conv, dn_a_log, dn_dt_bias, dn_norm, rw_mu, rw_w0, rw_w2, rw_a0, rw_a2, rw_g2, rw_kk, rw_ka, rw_rk, rw_ln_g, rw_ln_b, rt_ln, w_out, ln_ffn, ffn_w1, ffn_w3, ffn_w2, moe_router, moe_w1, moe_w3, moe_w2, ln_final):
    w = _pack_params(dict(
        ln_mix=ln_mix, w_in=w_in, dn_conv=dn_conv, dn_a_log=dn_a_log, dn_dt_bias=dn_dt_bias,
        dn_norm=dn_norm, rw_mu=rw_mu, rw_w0=rw_w0, rw_w2=rw_w2, rw_a0=rw_a0, rw_a2=rw_a2,
        rw_g2=rw_g2, rw_kk=rw_kk, rw_ka=rw_ka, rw_rk=rw_rk, rw_ln_g=rw_ln_g, rw_ln_b=rw_ln_b,
        rt_ln=rt_ln, w_out=w_out, ln_ffn=ln_ffn, ffn_w1=ffn_w1, ffn_w3=ffn_w3, ffn_w2=ffn_w2,
        moe_router=moe_router, moe_w1=moe_w1, moe_w3=moe_w3, moe_w2=moe_w2, ln_final=ln_final))
    bp, tp, _ = x_prompt.shape
    bs, ts, _ = x_sample.shape

    yp, *p_states = _trunk(x_prompt.reshape(bp * tp, D_MODEL), bp, tp, tp, 64, 8, None, 0, w)
    y_prompt = yp.reshape(bp, tp, D_MODEL)

    ts_pad = 8
    xs = jnp.pad(x_sample, ((0, 0), (0, ts_pad - ts), (0, 0))).reshape(bs * ts_pad, D_MODEL)
    ys, *s_states = _trunk(xs, bs, ts, ts_pad, ts_pad, 8,
                           (state_dn_conv, state_dn, state_rw_shift, state_rw, state_rt),
                           PAST_LEN, w)
    y_sample = ys.reshape(bs, -1, D_MODEL)[:, :ts]
    return (y_prompt, y_sample, *p_states, *s_states)
```

```python
import functools
import math

import numpy as np
import jax
import jax.numpy as jnp
from jax import lax
from jax.experimental import pallas as pl
from jax.experimental.pallas import tpu as pltpu

F32 = jnp.float32
BF16 = jnp.bfloat16

D_MODEL = 1024
N_LAYERS = 2
PAST_LEN = 16384
HEAD = 64
DN_HEADS = 6
DN_QK = DN_HEADS * HEAD
DN_CONV = 3 * DN_QK
CONV_W = 4
RW_HEADS = 6
RW_W = RW_HEADS * HEAD
RW_GATE_R = 160
RT_HEADS = 4
RT_DK = 32
RT_QK = RT_HEADS * RT_DK
RT_W = RT_HEADS * HEAD
DN_P = DN_CONV + DN_QK + 2 * DN_HEADS
RW_P = 3 * RW_W + 64 + 64 + RW_GATE_R
RT_P = 2 * RT_QK + 2 * RT_W
LANE = 128
DN_SLAB = 1664
RW_SLAB = 1536
RT_SLAB = RT_P
P_SLABS = DN_SLAB + RW_SLAB + RT_SLAB
RW_GATE_PAD = RW_SLAB - (3 * RW_W + 128)
N_EXP = 8
D_FF = 2816
D_EXP = 3584
HIDDEN_STEPS = 2
ROPE_BASE = 10000.0
NORM_EPS = 1e-6
RW_GN_EPS = 64e-5
RT_GN_EPS = 1e-5
TAIL = 8
VMEM_LIMIT = 56 * 1024 * 1024


def _dot(a, b, precision=None):
    return jnp.dot(a, b, preferred_element_type=F32, precision=precision)


def _split3(x):
    x1 = x.astype(BF16)
    r1 = x - x1.astype(F32)
    x2 = r1.astype(BF16)
    x3 = (r1 - x2.astype(F32)).astype(BF16)
    return x1, x2, x3


def _select_sum(sel, x):
    sel = sel.astype(BF16)
    return sum(jnp.dot(sel, p, preferred_element_type=F32) for p in _split3(x))


def _select_sum_nt(sel, x):
    sel = sel.astype(BF16)
    return sum(lax.dot_general(sel, p, (((1,), (1,)), ((), ())), preferred_element_type=F32)
               for p in _split3(x))


def _mm(a, b):
    return jnp.dot(a.astype(BF16), b.astype(BF16), preferred_element_type=F32)


def _mm_nt(a, b):
    return lax.dot_general(a.astype(BF16), b.astype(BF16), (((1,), (1,)), ((), ())),
                           preferred_element_type=F32)


def _rows(*parts):
    return jnp.concatenate(parts, axis=0)


def _cols(*parts):
    return jnp.concatenate(parts, axis=1)


def _silu(x):
    return x * jax.nn.sigmoid(x)


def _softplus(x):
    return jnp.maximum(x, 0.0) + jnp.log(1.0 + jnp.exp(-jnp.abs(x)))


def _rms(x, eps):
    return x * lax.rsqrt(jnp.mean(x * x, axis=-1, keepdims=True) + eps)


def _tri(c):
    r = lax.broadcasted_iota(jnp.int32, (c, c), 0)
    col = lax.broadcasted_iota(jnp.int32, (c, c), 1)
    return r >= col, r > col, r == col


def _group_ones():
    r = lax.broadcasted_iota(jnp.int32, (LANE, LANE), 0)
    c = lax.broadcasted_iota(jnp.int32, (LANE, LANE), 1)
    return ((r < HEAD) == (c < HEAD)).astype(BF16)


def _group_sum(x, ones_blk, terms=2):
    parts = _split3(x)[:terms]
    out = [sum(jnp.dot(p[:, j:j + LANE], ones_blk, preferred_element_type=F32) for p in parts)
           for j in range(0, x.shape[1], LANE)]
    return out[0] if len(out) == 1 else jnp.concatenate(out, axis=1)


def _bd(x):
    x = x.astype(BF16)
    first = lax.broadcasted_iota(jnp.int32, x.shape, 1) < x.shape[1] // 2
    zero = jnp.zeros_like(x)
    return jnp.concatenate([jnp.where(first, x, zero), jnp.where(first, zero, x)], axis=0)


class _PairTimeMasks:
    def __init__(self, c_len):
        row = lax.broadcasted_iota(jnp.int32, (c_len, 2 * c_len), 0)
        lane = lax.broadcasted_iota(jnp.int32, (c_len, 2 * c_len), 1)
        col = jnp.where(lane < c_len, lane, lane - c_len)
        self.incl = row >= col
        self.strict = row > col
        self.eye = (row == col).astype(F32)
        self.first = lax.broadcasted_iota(jnp.int32, (1, 2 * c_len), 1) < c_len


def _neumann_levels(n_rows):
    return max(0, math.ceil(math.log2(n_rows)) - 1)


def _inproj_body(x_ref, g_ref, w_ref, zdn_ref, zrw_ref, zrt_ref):
    h = (_rms(x_ref[...], NORM_EPS) * g_ref[...]).astype(BF16)
    zdn_ref[...] = _dot(h, w_ref[:, 0:DN_SLAB], None)
    zrw_ref[...] = _dot(h, w_ref[:, DN_SLAB:DN_SLAB + RW_SLAB], None)
    zrt_ref[...] = _dot(h, w_ref[:, DN_SLAB + RW_SLAB:P_SLABS], None)


def _inproj(x, gain, w):
    n = x.shape[0]
    tm = min(512, n)
    return pl.pallas_call(
        _inproj_body,
        grid=(n // tm,),
        in_specs=[pl.BlockSpec((tm, D_MODEL), lambda i: (i, 0)),
                  pl.BlockSpec((1, D_MODEL), lambda i: (0, 0)),
                  pl.BlockSpec((D_MODEL, P_SLABS), lambda i: (0, 0))],
        out_specs=[pl.BlockSpec((tm, DN_SLAB), lambda i: (i, 0)),
                   pl.BlockSpec((tm, RW_SLAB), lambda i: (i, 0)),
                   pl.BlockSpec((tm, RT_SLAB), lambda i: (i, 0))],
        out_shape=[jax.ShapeDtypeStruct((n, DN_SLAB), F32),
                   jax.ShapeDtypeStruct((n, RW_SLAB), F32),
                   jax.ShapeDtypeStruct((n, RT_SLAB), F32)],
        compiler_params=pltpu.CompilerParams(
            dimension_semantics=("arbitrary",), vmem_limit_bytes=VMEM_LIMIT),
        name="inproj",
    )(x, gain, w)


def _seq_specs(n_par, c_len, width):
    return pl.BlockSpec((n_par, c_len, width), lambda b, c: (b, c, 0))


def _state_spec(n_par, shape):
    return pl.BlockSpec((n_par,) + shape, lambda b, c: (b,) + (0,) * len(shape))


def _stacked_state_out(layer, stacked, n_par, n_seq, state, in_specs, args):
    shape = jax.ShapeDtypeStruct((N_LAYERS, n_seq) + state, F32)
    zeros = (0,) * len(state)
    if layer == 0:
        return pl.BlockSpec((N_LAYERS, n_par) + state, lambda b, c: (0, b) + zeros), shape, {}
    in_specs.append(pl.BlockSpec(memory_space=pl.ANY))
    args.append(stacked)
    return (pl.BlockSpec((None, n_par) + state, lambda b, c: (layer, b) + zeros), shape,
            {len(args) - 1: 1})


def _own_layer(so_ref, layer):
    if layer > 0:
        return so_ref
    so_ref[1:] = jnp.zeros((so_ref.shape[0] - 1,) + so_ref.shape[1:], F32)
    return so_ref.at[0]


def _const_spec(a):
    return pl.BlockSpec(a.shape, lambda b, c: (0,) * a.ndim)


def _dn_body(c_len, n_valid, fresh, n_par, layer, *refs):
    refs = list(refs)
    if layer > 0:
        del refs[-6]
    if fresh:
        z_ref, cw_ref, hp_ref, nw_ref, o_ref, so_ref, xe, s_scr, o_raw = refs
    else:
        z_ref, c0_ref, s0_ref, cw_ref, hp_ref, nw_ref, o_ref, so_ref, xe, s_scr, o_raw = refs
    ci = pl.program_id(1)

    @pl.when(ci == 0)
    def _():
        xe[:, 0:TAIL, :] = jnp.zeros((n_par, TAIL, DN_CONV), F32)
        s_scr[...] = jnp.zeros_like(s_scr)
        if not fresh:
            xe[:, TAIL - (CONV_W - 1):TAIL, :] = c0_ref[...]
            for p in range(DN_HEADS // 2):
                s_scr[:, p, 0:HEAD, 0:HEAD] = s0_ref[:, 2 * p]
                s_scr[:, p, HEAD:LANE, HEAD:LANE] = s0_ref[:, 2 * p + 1]

    tri_f = _tri(c_len)[0].astype(F32)
    live = lax.broadcasted_iota(jnp.int32, (c_len, LANE), 0) < n_valid
    neg_a = -jnp.exp(hp_ref[0:1, :])
    dt_bias = hp_ref[1:2, :]

    ones_blk = _group_ones()
    eye_l = _tri(LANE)[2].astype(F32)
    eye_8 = eye_l[0:8, :]

    def prepare():
        seqs = []
        for i in range(n_par):
            x = z_ref[i, :, 0:DN_CONV]
            xe[i, TAIL:TAIL + c_len, :] = x
            acc = x * cw_ref[CONV_W - 1:CONV_W, :]
            for j in range(CONV_W - 1):
                off = TAIL - (CONV_W - 1) + j
                acc = acc + xe[i, off:off + c_len, :] * cw_ref[j:j + 1, :]
            xe[i, 0:TAIL, :] = xe[i, c_len:c_len + TAIL, :]
            ab = z_ref[i, :, DN_CONV + DN_QK:DN_SLAB]
            g_tok = neg_a * _softplus(ab + dt_bias)
            beta = jax.nn.sigmoid(ab)
            if n_valid < c_len:
                g_tok = jnp.where(live, g_tok, 0.0)
                beta = jnp.where(live, beta, 0.0)
            conv = _silu(acc)
            seqs.append(dict(conv=conv, v=conv[:, 2 * DN_QK:DN_CONV], beta=beta, g_tok=g_tok,
                             gate=_silu(z_ref[i, :, DN_CONV:DN_CONV + DN_QK])))
        for sq in seqs:
            sq["g_cum"] = _select_sum(tri_f, sq["g_tok"])
            qk = sq["conv"][:, 0:2 * DN_QK]
            sq["qk"] = qk * lax.rsqrt(_group_sum(qk * qk, ones_blk) + 1e-6)
        for sq in seqs:
            sq["g_cum_t"] = _select_sum_nt(eye_8, sq["g_cum"])
        return seqs

    def advance(seqs):
        pt = _PairTimeMasks(c_len)
        first_h = lax.broadcasted_iota(jnp.int32, (1, LANE), 1) < HEAD
        first_rows = lax.broadcasted_iota(jnp.int32, (LANE, 1), 0) < HEAD
        chains = []
        for i, sq in enumerate(seqs):
            for p in range(DN_HEADS // 2):
                ha, hb = 2 * p, 2 * p + 1
                lo = LANE * p
                k = sq["qk"][:, DN_QK + lo:DN_QK + lo + LANE]
                v = sq["v"][:, lo:lo + LANE]
                gc_a, gc_b = sq["g_cum"][:, ha:ha + 1], sq["g_cum"][:, hb:hb + 1]
                gc_row = _cols(sq["g_cum_t"][ha:ha + 1, :], sq["g_cum_t"][hb:hb + 1, :])
                g_end = jnp.where(pt.first, gc_a[c_len - 1:c_len], gc_b[c_len - 1:c_len])
                b = jnp.where(first_h, sq["beta"][:, DN_HEADS + ha:DN_HEADS + ha + 1],
                              sq["beta"][:, DN_HEADS + hb:DN_HEADS + hb + 1])
                egc = jnp.exp(jnp.where(first_h, gc_a, gc_b))
                dmask = jnp.where(
                    pt.incl, jnp.exp(jnp.minimum(jnp.where(pt.first, gc_a, gc_b) - gc_row, 0.0)), 0.0)
                q = sq["qk"][:, lo:lo + LANE] * (HEAD ** -0.5)
                kb = k * b
                chains.append(dict(
                    i=i, p=p, k=k, kb=kb, q=q, qe=q * egc, vb=v * b, kbe=kb * egc, dmask=dmask,
                    k_decay=jnp.exp(g_end - gc_row),
                    s_decay=jnp.exp(jnp.where(first_rows, gc_a[c_len - 1:c_len],
                                              gc_b[c_len - 1:c_len]))))
        for c in chains:
            c["kq"] = _mm_nt(_rows(c["kb"], c["q"], eye_l), _bd(c["k"]))
        for c in chains:
            c["np"] = -jnp.where(pt.strict, c["kq"][0:c_len] * c["dmask"], 0.0)
            c["t"] = pt.eye + c["np"]
        for _ in range(_neumann_levels(n_valid)):
            for c in chains:
                c["np"] = _mm(c["np"], _bd(c["np"])).astype(BF16)
                c["np_bd"] = _bd(c["np"])
            for c in chains:
                c["t"] = c["t"] + _mm(c["t"], c["np_bd"])
        for c in chains:
            c["tx"] = _mm(c["t"], _cols(_bd(c["vb"]), _bd(c["kbe"])))
        for c in chains:
            qk = c["kq"][c_len:2 * c_len] * c["dmask"]
            kd_t = c["kq"][2 * c_len:2 * c_len + LANE] * c["k_decay"]
            c["w"] = _mm(_rows(qk, kd_t), _cols(_bd(c["tx"][:, 0:LANE]), _bd(c["tx"][:, LANE:2 * LANE])))
        for c in chains:
            w = c["w"]
            c["s"] = s_scr[c["i"], c["p"]]
            c["res"] = _mm(_rows(c["qe"] - w[0:c_len, LANE:2 * LANE],
                                 w[c_len:c_len + LANE, LANE:2 * LANE]), c["s"])
        for c in chains:
            w, res = c["w"], c["res"]
            lo = LANE * c["p"]
            s_scr[c["i"], c["p"]] = (c["s"] * c["s_decay"] - res[c_len:c_len + LANE]
                                     + w[c_len:c_len + LANE, 0:LANE])
            o_raw[c["i"], :, lo:lo + LANE] = res[0:c_len] + w[0:c_len, 0:LANE]
        for i, sq in enumerate(seqs):
            o = o_raw[i]
            ms = _group_sum(o * o, ones_blk, terms=1) * (1.0 / HEAD)
            o_ref[i] = o * lax.rsqrt(ms + NORM_EPS) * nw_ref[...] * sq["gate"]

    advance(prepare())

    @pl.when(ci == pl.num_programs(1) - 1)
    def _():
        so = _own_layer(so_ref, layer)
        for p in range(DN_HEADS // 2):
            so[:, 2 * p] = s_scr[:, p, 0:HEAD, 0:HEAD]
            so[:, 2 * p + 1] = s_scr[:, p, HEAD:LANE, HEAD:LANE]


def _dn_group(z, conv0, s0, stacked, conv_w, head_par, norm_w, layer, c_len, n_valid, n_par):
    n_seq, t_pad, _ = z.shape
    fresh = s0 is None
    nc = t_pad // c_len
    assert n_valid == c_len or nc == 1
    state = (DN_HEADS, HEAD, HEAD)
    in_specs = [_seq_specs(n_par, c_len, DN_SLAB)]
    args = [z]
    if not fresh:
        in_specs += [_state_spec(n_par, (CONV_W - 1, DN_CONV)), _state_spec(n_par, state)]
        args += [conv0, s0]
    for p in (conv_w, head_par, norm_w):
        in_specs.append(_const_spec(p))
        args.append(p)
    so_spec, so_shape, aliases = _stacked_state_out(layer, stacked, n_par, n_seq, state, in_specs, args)
    return pl.pallas_call(
        functools.partial(_dn_body, c_len, n_valid, fresh, n_par, layer),
        grid=(n_seq // n_par, nc),
        in_specs=in_specs,
        out_specs=[_seq_specs(n_par, c_len, DN_QK), so_spec],
        out_shape=[jax.ShapeDtypeStruct((n_seq, t_pad, DN_QK), F32), so_shape],
        input_output_aliases=aliases,
        scratch_shapes=[pltpu.VMEM((n_par, c_len + TAIL, DN_CONV), F32),
                        pltpu.VMEM((n_par, DN_HEADS // 2, LANE, LANE), F32),
                        pltpu.VMEM((n_par, c_len, DN_QK), F32)],
        compiler_params=pltpu.CompilerParams(
            dimension_semantics=("arbitrary", "arbitrary"), vmem_limit_bytes=VMEM_LIMIT),
        name="dn_group",
    )(*args)


def _rw_body(c_len, n_valid, fresh, n_par, layer, *refs):
    refs = list(refs)
    if layer > 0:
        del refs[-6]
    if fresh:
        (z_ref, mu_ref, w0_ref, w2_ref, a0_ref, a2_ref, g2_ref, kk_ref, ka_ref, rk_ref,
         lng_ref, lnb_ref, o_ref, so_ref, xe, s_scr, y_raw) = refs
    else:
        (z_ref, sh0_ref, s0_ref, mu_ref, w0_ref, w2_ref, a0_ref, a2_ref, g2_ref, kk_ref, ka_ref,
         rk_ref, lng_ref, lnb_ref, o_ref, so_ref, xe, s_scr, y_raw) = refs
    ci = pl.program_id(1)

    @pl.when(ci == 0)
    def _():
        xe[:, 0:TAIL, :] = jnp.zeros((n_par, TAIL, RW_SLAB), F32)
        s_scr[...] = jnp.zeros_like(s_scr)
        if not fresh:
            xe[:, TAIL - 1:TAIL, :] = sh0_ref[...]
            for p in range(RW_HEADS // 2):
                s_scr[:, p, 0:HEAD, 0:HEAD] = s0_ref[:, 2 * p]
                s_scr[:, p, HEAD:LANE, HEAD:LANE] = s0_ref[:, 2 * p + 1]

    tri_f = _tri(c_len)[0].astype(F32)
    live = lax.broadcasted_iota(jnp.int32, (c_len, RW_W), 0) < n_valid
    ones_blk = _group_ones()
    eye_l = _tri(LANE)[2].astype(F32)

    seqs = []
    for i in range(n_par):
        x = z_ref[i]
        xe[i, TAIL:TAIL + c_len, :] = x
        prev = xe[i, TAIL - 1:TAIL - 1 + c_len, :]
        zs = x + (prev - x) * mu_ref[...]
        xe[i, 0:TAIL, :] = xe[i, c_len:c_len + TAIL, :]
        seqs.append(dict(r=zs[:, 0:RW_W], k=zs[:, RW_W:2 * RW_W], v=zs[:, 2 * RW_W:3 * RW_W],
                         wl=zs[:, 3 * RW_W:3 * RW_W + 64], al=zs[:, 3 * RW_W + 64:3 * RW_W + 128],
                         gl=zs[:, 3 * RW_W + 128:RW_SLAB]))
    for sq in seqs:
        sq["w_lora"] = _mm(jnp.tanh(sq["wl"]), w2_ref[...])
        sq["a_lora"] = _mm(sq["al"], a2_ref[...])
        sq["g"] = _mm(jax.nn.sigmoid(sq["gl"]), g2_ref[...])
    for sq in seqs:
        w_log = -_softplus(-(w0_ref[...] + sq["w_lora"])) - 0.5
        a = jax.nn.sigmoid(a0_ref[...] + sq["a_lora"])
        kk_in = sq["k"] * kk_ref[...]
        k2 = sq["k"] * (1.0 + (a - 1.0) * ka_ref[...])
        lw = -jnp.exp(w_log)
        if n_valid < c_len:
            lw = jnp.where(live, lw, 0.0)
            k2 = jnp.where(live, k2, 0.0)
            kk_in = jnp.where(live, kk_in, 0.0)
        sq.update(a=a, kk_in=kk_in, k2=k2, lw=lw)
    for sq in seqs:
        sq["gcum"] = _select_sum(tri_f, sq["lw"])
        kk_in = sq["kk_in"]
        sq["kk"] = kk_in * lax.rsqrt(_group_sum(kk_in * kk_in, ones_blk) + 1e-6)
    for sq in seqs:
        gcum = sq["gcum"]
        e_neg = jnp.exp(-gcum)
        g_end = gcum[c_len - 1:c_len, :]
        e_rest = jnp.exp(g_end - gcum)
        bv = sq["kk"] * sq["a"]
        sq.update(at=-sq["kk"] * jnp.exp(gcum - sq["lw"]), bt=bv * e_neg, kt=sq["k2"] * e_neg,
                  rt=sq["r"] * jnp.exp(gcum), b_rest=bv * e_rest, k_rest=sq["k2"] * e_rest,
                  e_end=jnp.exp(g_end))

    pt = _PairTimeMasks(c_len)
    c2 = 2 * c_len
    chains = []
    for i, sq in enumerate(seqs):
        for p in range(RW_HEADS // 2):
            sl = slice(LANE * p, LANE * (p + 1))
            chains.append(dict(i=i, p=p, sl=sl, at=sq["at"][:, sl], rt=sq["rt"][:, sl],
                               bt_bd=_bd(sq["bt"][:, sl]), kt_bd=_bd(sq["kt"][:, sl]),
                               v_bd=_bd(sq["v"][:, sl]), e_end=sq["e_end"][:, sl],
                               rest=_rows(_bd(sq["b_rest"][:, sl]), _bd(sq["k_rest"][:, sl]))))
    for c in chains:
        c["at_bd"] = _bd(c["at"])
        c["prod"] = _mm_nt(_rows(c["at"], c["rt"]), _rows(c["bt_bd"], c["kt_bd"]))
        c["tr"] = _mm_nt(eye_l, _rows(c["at_bd"], c["v_bd"]))
    for c in chains:
        prod = c["prod"]
        c["np"] = jnp.where(pt.strict, prod[0:c_len, 0:c2], 0.0)
        c["t"] = pt.eye + c["np"]
        c["nak"] = jnp.where(pt.strict, prod[0:c_len, c2:2 * c2], 0.0)
        c["mix"] = _cols(jnp.where(pt.incl, prod[c_len:c2, 0:c2], 0.0),
                         jnp.where(pt.incl, prod[c_len:c2, c2:2 * c2], 0.0))
        c["at_t"] = c["tr"][:, 0:c2]
        c["v_t"] = c["tr"][:, c2:2 * c2]
    for c in chains:
        c["nakv"] = _mm(c["nak"], c["v_bd"])
        c["nakv_t"] = _mm_nt(c["v_t"], _bd(c["nak"]))
    for _ in range(_neumann_levels(n_valid)):
        for c in chains:
            c["np"] = _mm(c["np"], _bd(c["np"])).astype(BF16)
            c["np_bd"] = _bd(c["np"])
        for c in chains:
            c["t"] = c["t"] + _mm(c["t"], c["np_bd"])
    for c in chains:
        c["tx"] = _mm(c["t"], _cols(c["at_bd"], _bd(c["nakv"])))
        c["tx_t"] = _mm_nt(_rows(c["at_t"], c["nakv_t"]), _bd(c["t"]))
    for c in chains:
        tx, tx_t = c["tx"], c["tx_t"]
        c["zy"] = _mm(c["mix"], _rows(_cols(_bd(tx[:, 0:LANE]), _bd(tx[:, LANE:2 * LANE])),
                                      _cols(jnp.zeros((c2, LANE), F32), c["v_bd"])))
        ta_t = tx_t[0:LANE]
        c["m"] = _mm(_rows(_cols(ta_t, jnp.zeros_like(ta_t)), _cols(tx_t[LANE:2 * LANE], c["v_t"])),
                     c["rest"])
    for c in chains:
        c["s"] = s_scr[c["i"], c["p"]]
        c["y"] = _mm_nt(c["rt"] + c["zy"][:, 0:LANE], c["s"])
        c["sm"] = _mm(c["s"], c["m"][0:LANE])
    for c in chains:
        s_scr[c["i"], c["p"]] = c["s"] * c["e_end"] + c["sm"] + c["m"][LANE:2 * LANE]
        y_raw[c["i"], :, c["sl"]] = c["y"] + c["zy"][:, LANE:2 * LANE]
    for i, sq in enumerate(seqs):
        y = y_raw[i]
        d = y - _group_sum(y, ones_blk) * (1.0 / HEAD)
        yn = d * lax.rsqrt(_group_sum(d * d, ones_blk, terms=1) * (1.0 / HEAD) + RW_GN_EPS)
        yn = yn * lng_ref[...] + lnb_ref[...]
        bonus = _group_sum(sq["r"] * sq["k2"] * rk_ref[...], ones_blk, terms=1) * sq["v"]
        o_ref[i] = (yn + bonus) * sq["g"]

    @pl.when(ci == pl.num_programs(1) - 1)
    def _():
        so = _own_layer(so_ref, layer)
        for p in range(RW_HEADS // 2):
            so[:, 2 * p] = s_scr[:, p, 0:HEAD, 0:HEAD]
            so[:, 2 * p + 1] = s_scr[:, p, HEAD:LANE, HEAD:LANE]


def _rw_group(z, shift0, s0, stacked, par, layer, c_len, n_valid, n_par):
    n_seq, t_pad, _ = z.shape
    fresh = s0 is None
    nc = t_pad // c_len
    assert n_valid == c_len or nc == 1
    state = (RW_HEADS, HEAD, HEAD)
    in_specs = [_seq_specs(n_par, c_len, RW_SLAB)]
    args = [z]
    if not fresh:
        in_specs += [_state_spec(n_par, (1, RW_SLAB)), _state_spec(n_par, state)]
        args += [shift0, s0]
    for p in par:
        in_specs.append(_const_spec(p))
        args.append(p)
    so_spec, so_shape, aliases = _stacked_state_out(layer, stacked, n_par, n_seq, state, in_specs, args)
    return pl.pallas_call(
        functools.partial(_rw_body, c_len, n_valid, fresh, n_par, layer),
        grid=(n_seq // n_par, nc),
        in_specs=in_specs,
        out_specs=[_seq_specs(n_par, c_len, RW_W), so_spec],
        out_shape=[jax.ShapeDtypeStruct((n_seq, t_pad, RW_W), F32), so_shape],
        input_output_aliases=aliases,
        scratch_shapes=[pltpu.VMEM((n_par, c_len + TAIL, RW_SLAB), F32),
                        pltpu.VMEM((n_par, RW_HEADS // 2, LANE, LANE), F32),
                        pltpu.VMEM((n_par, c_len, RW_W), F32)],
        compiler_params=pltpu.CompilerParams(
            dimension_semantics=("arbitrary", "arbitrary"), vmem_limit_bytes=VMEM_LIMIT),
        name="rw_group",
    )(*args)


def _rt_body(c_len, n_valid, fresh, n_par, *refs):
    if fresh:
        z_ref, cos_ref, sin_ref, lng_ref, o_ref, so_ref, s_scr, o_raw = refs
    else:
        z_ref, cos_ref, sin_ref, s0_ref, lng_ref, o_ref, so_ref, s_scr, o_raw = refs
    ci = pl.program_id(1)

    @pl.when(ci == 0)
    def _():
        if fresh:
            s_scr[...] = jnp.zeros_like(s_scr)
        else:
            s_scr[...] = s0_ref[...]

    lane = lax.broadcasted_iota(jnp.int32, (c_len, RT_QK), 1)
    first_half = (lane & (RT_DK - 1)) < (RT_DK // 2)
    cos = cos_ref[...]
    sin = sin_ref[...]

    def rotary(t):
        partner = jnp.where(first_half,
                            pltpu.roll(t, RT_QK - RT_DK // 2, axis=1),
                            pltpu.roll(t, RT_DK // 2, axis=1))
        return t * cos + partner * sin

    incl, _, _ = _tri(c_len)
    ri = lax.broadcasted_iota(jnp.int32, (c_len, c_len), 0)
    cj = lax.broadcasted_iota(jnp.int32, (c_len, c_len), 1)
    delta = (ri - cj).astype(F32)
    idx = lax.broadcasted_iota(jnp.int32, (c_len, 1), 0).astype(F32)
    log_decay = [math.log1p(-(2.0 ** (-5.0 - h))) for h in range(RT_HEADS)]
    dmasks = [jnp.where(incl, jnp.exp(delta * lg), 0.0) for lg in log_decay]
    q_scale = [jnp.exp((idx + 1.0) * lg) for lg in log_decay]
    t_row = lax.broadcasted_iota(jnp.int32, (1, c_len), 1).astype(F32)
    k_scale = [jnp.where(t_row < n_valid, jnp.exp((n_valid - 1.0 - t_row) * lg), 0.0) for lg in log_decay]
    ones_blk = _group_ones()
    eye_k = _tri(RT_DK)[2].astype(F32)

    chains = []
    for i in range(n_par):
        q = rotary(z_ref[i, :, 0:RT_QK])
        k = rotary(z_ref[i, :, RT_QK:2 * RT_QK]) * (RT_DK ** -0.5)
        for h in range(RT_HEADS):
            chains.append(dict(
                i=i, h=h, qh=q[:, RT_DK * h:RT_DK * (h + 1)], kh=k[:, RT_DK * h:RT_DK * (h + 1)],
                vh=z_ref[i, :, 2 * RT_QK + HEAD * h:2 * RT_QK + HEAD * (h + 1)]))
    for c in chains:
        c["qk"] = _mm_nt(_rows(c["qh"], eye_k), c["kh"])
    for c in chains:
        h = c["h"]
        c["s"] = s_scr[c["i"], h]
        c["cross"] = _mm(c["qh"] * q_scale[h], c["s"])
        c["kv"] = _mm(c["qk"][c_len:c_len + RT_DK] * k_scale[h], c["vh"])
        c["inner"] = _mm(c["qk"][0:c_len] * dmasks[h], c["vh"])
    for c in chains:
        i, h = c["i"], c["h"]
        s_scr[i, h] = c["s"] * math.exp(n_valid * log_decay[h]) + c["kv"]
        o_raw[i, :, HEAD * h:HEAD * (h + 1)] = c["inner"] + c["cross"]
    for i in range(n_par):
        o = o_raw[i]
        d = o - _group_sum(o, ones_blk) * (1.0 / HEAD)
        on = d * lax.rsqrt(_group_sum(d * d, ones_blk, terms=1) * (1.0 / HEAD) + RT_GN_EPS)
        o_ref[i] = on * lng_ref[...] * _silu(z_ref[i, :, 2 * RT_QK + RT_W:RT_SLAB])

    @pl.when(ci == pl.num_programs(1) - 1)
    def _():
        so_ref[...] = s_scr[...]


def _rt_group(z, cos, sin, s0, ln_g, c_len, n_valid, n_par):
    n_seq, t_pad, _ = z.shape
    fresh = s0 is None
    nc = t_pad // c_len
    assert n_valid == c_len or nc == 1
    state = (RT_HEADS, RT_DK, HEAD)
    table = pl.BlockSpec((c_len, RT_QK), lambda b, c: (c, 0))
    in_specs = [_seq_specs(n_par, c_len, RT_SLAB), table, table]
    args = [z, cos, sin]
    if not fresh:
        in_specs.append(_state_spec(n_par, state))
        args.append(s0)
    in_specs.append(_const_spec(ln_g))
    args.append(ln_g)
    return pl.pallas_call(
        functools.partial(_rt_body, c_len, n_valid, fresh, n_par),
        grid=(n_seq // n_par, nc),
        in_specs=in_specs,
        out_specs=[_seq_specs(n_par, c_len, RT_W), _state_spec(n_par, state)],
        out_shape=[jax.ShapeDtypeStruct((n_seq, t_pad, RT_W), F32),
                   jax.ShapeDtypeStruct((n_seq,) + state, F32)],
        scratch_shapes=[pltpu.VMEM((n_par,) + state, F32),
                        pltpu.VMEM((n_par, c_len, RT_W), F32)],
        compiler_params=pltpu.CompilerParams(
            dimension_semantics=("arbitrary", "arbitrary"), vmem_limit_bytes=VMEM_LIMIT),
        name="rt_group",
    )(*args)


def _rotary_tables(pos0, t_pad):
    half = RT_DK // 2
    inv = ROPE_BASE ** (-np.arange(half, dtype=np.float64) / half)
    ang = (pos0 + np.arange(t_pad, dtype=np.float64))[:, None] * inv[None, :]
    cos = np.tile(np.concatenate([np.cos(ang), np.cos(ang)], axis=1), (1, RT_HEADS))
    sin = np.tile(np.concatenate([-np.sin(ang), np.sin(ang)], axis=1), (1, RT_HEADS))
    return jnp.asarray(cos, F32), jnp.asarray(sin, F32)


def _outproj_body(x_ref, odn_ref, orw_ref, ort_ref, w_ref, o_ref):
    acc = x_ref[...]
    acc = acc + _dot(odn_ref[...].astype(BF16), w_ref[0:DN_QK, :], None)
    acc = acc + _dot(orw_ref[...].astype(BF16), w_ref[DN_QK:DN_QK + RW_W, :], None)
    acc = acc + _dot(ort_ref[...].astype(BF16), w_ref[DN_QK + RW_W:D_MODEL, :], None)
    o_ref[...] = acc


def _outproj(x, o_dn, o_rw, o_rt, w):
    n = x.shape[0]
    tm = min(512, n)
    row = lambda i: (i, 0)
    return pl.pallas_call(
        _outproj_body,
        grid=(n // tm,),
        in_specs=[pl.BlockSpec((tm, D_MODEL), row),
                  pl.BlockSpec((tm, DN_QK), row),
                  pl.BlockSpec((tm, RW_W), row),
                  pl.BlockSpec((tm, RT_W), row),
                  pl.BlockSpec((D_MODEL, D_MODEL), lambda i: (0, 0))],
        out_specs=pl.BlockSpec((tm, D_MODEL), row),
        out_shape=jax.ShapeDtypeStruct((n, D_MODEL), F32),
        compiler_params=pltpu.CompilerParams(
            dimension_semantics=("arbitrary",), vmem_limit_bytes=VMEM_LIMIT),
        name="outproj",
    )(x, o_dn, o_rw, o_rt, w)


def _swiglu(hb, w1, w3, w2):
    return _dot((_silu(_dot(hb, w1)) * _dot(hb, w3)).astype(BF16), w2)


def _ffn_body(final_norm, *refs):
    if final_norm:
        x_ref, g_ref, w1_ref, w3_ref, w2_ref, gf_ref, o_ref, h_scr, acc = refs
    else:
        x_ref, g_ref, w1_ref, w3_ref, w2_ref, o_ref, h_scr, acc = refs
    f = pl.program_id(1)

    @pl.when(f == 0)
    def _():
        x = x_ref[...]
        h_scr[...] = (_rms(x, NORM_EPS) * g_ref[...]).astype(BF16)
        acc[...] = x

    acc[...] += _swiglu(h_scr[...], w1_ref[...], w3_ref[...], w2_ref[...])

    @pl.when(f == pl.num_programs(1) - 1)
    def _():
        if final_norm:
            o_ref[...] = _rms(acc[...], NORM_EPS) * gf_ref[...]
        else:
            o_ref[...] = acc[...]


def _ffn(x, gain, w1, w3, w2, final_gain):
    n = x.shape[0]
    n_f, _, tf = w1.shape
    final_norm = final_gain is not None
    tm = min(1024, n)
    in_specs = [pl.BlockSpec((tm, D_MODEL), lambda i, f: (i, 0)),
                pl.BlockSpec((1, D_MODEL), lambda i, f: (0, 0)),
                pl.BlockSpec((None, D_MODEL, tf), lambda i, f: (f, 0, 0)),
                pl.BlockSpec((None, D_MODEL, tf), lambda i, f: (f, 0, 0)),
                pl.BlockSpec((tf, D_MODEL), lambda i, f: (f, 0))]
    args = [x, gain, w1, w3, w2]
    if final_norm:
        in_specs.append(pl.BlockSpec((1, D_MODEL), lambda i, f: (0, 0)))
        args.append(final_gain)
    return pl.pallas_call(
        functools.partial(_ffn_body, final_norm),
        grid=(n // tm, n_f),
        in_specs=in_specs,
        out_specs=pl.BlockSpec((tm, D_MODEL), lambda i, f: (i, 0)),
        out_shape=jax.ShapeDtypeStruct((n, D_MODEL), F32),
        scratch_shapes=[pltpu.VMEM((tm, D_MODEL), BF16), pltpu.VMEM((tm, D_MODEL), F32)],
        compiler_params=pltpu.CompilerParams(
            dimension_semantics=("arbitrary", "arbitrary"), vmem_limit_bytes=VMEM_LIMIT),
        name="ffn_dense",
    )(*args)


def _route_body(x_ref, g_ref, router_ref, o_ref):
    h = _rms(x_ref[...], NORM_EPS) * g_ref[...]
    h1, h2, _ = _split3(h)
    r1, r2, _ = _split3(router_ref[...])
    logits = (jnp.dot(h1, r1, preferred_element_type=F32) + jnp.dot(h1, r2, preferred_element_type=F32)
              + jnp.dot(h2, r1, preferred_element_type=F32))
    lane = lax.broadcasted_iota(jnp.int32, logits.shape, 1).astype(F32)
    lg = jnp.where(lane < N_EXP, logits, -jnp.inf)
    m1 = jnp.max(lg, axis=-1, keepdims=True)
    i1 = jnp.min(jnp.where(lg == m1, lane, float(LANE)), axis=-1, keepdims=True)
    lg2 = jnp.where(lane == i1, -jnp.inf, lg)
    m2 = jnp.max(lg2, axis=-1, keepdims=True)
    i2 = jnp.min(jnp.where(lg2 == m2, lane, float(LANE)), axis=-1, keepdims=True)
    e2 = jnp.exp(m2 - m1)
    den = 1.0 + e2
    o_ref[...] = (jnp.where(lane == 0.0, i1, 0.0) + jnp.where(lane == 1.0, i2, 0.0)
                  + jnp.where(lane == 2.0, 1.0 / den, 0.0) + jnp.where(lane == 3.0, e2 / den, 0.0))


def _route(x, gain, router):
    n = x.shape[0]
    tm = min(512, n)
    return pl.pallas_call(
        _route_body,
        grid=(n // tm,),
        in_specs=[pl.BlockSpec((tm, D_MODEL), lambda i: (i, 0)),
                  pl.BlockSpec((1, D_MODEL), lambda i: (0, 0)),
                  pl.BlockSpec((D_MODEL, LANE), lambda i: (0, 0))],
        out_specs=pl.BlockSpec((tm, LANE), lambda i: (i, 0)),
        out_shape=jax.ShapeDtypeStruct((n, LANE), F32),
        compiler_params=pltpu.CompilerParams(
            dimension_semantics=("arbitrary",), vmem_limit_bytes=VMEM_LIMIT),
        name="moe_route",
    )(x, gain, router)


def _row_gather(src_hbm, idx_ref, buf, sem, slot, n_rows):
    def copy(r):
        return pltpu.make_async_copy(src_hbm.at[pl.ds(idx_ref[0, 0, r], 1), :],
                                     buf.at[slot, pl.ds(r, 1), :], sem.at[slot])

    def start():
        def body(r, carry):
            copy(r).start()
            return carry
        lax.fori_loop(0, n_rows, body, 0, unroll=8)

    def wait():
        def body(r, carry):
            copy(r).wait()
            return carry
        lax.fori_loop(0, n_rows, body, 0, unroll=8)

    return start, wait, copy


def _moe_ffn_body(tm, te_ref, nu_ref, tok_ref, tok_next_ref, x_hbm, g_ref, w1_ref, w3_ref, w2_ref,
                  ys_ref, xbuf, sem, h_scr, acc):
    i = pl.program_id(0)
    f = pl.program_id(1)
    n_f = pl.num_programs(1)
    slot = lax.rem(i, 2)
    n_used = nu_ref[0]
    used = i < n_used
    rows_per_step = tm // HIDDEN_STEPS
    start_cur, wait_cur, _ = _row_gather(x_hbm, tok_ref, xbuf, sem, slot, tm)
    _, wait_next, copy_next = _row_gather(x_hbm, tok_next_ref, xbuf, sem, 1 - slot, tm)

    @pl.when((f == 0) & (i <= n_used))
    def _():
        pl.when(i == 0)(start_cur)
        wait_cur()

    @pl.when((f == 0) & used)
    def _():
        h_scr[...] = (_rms(xbuf[slot], NORM_EPS) * g_ref[...]).astype(BF16)

    @pl.when(used)
    def _():
        y = _swiglu(h_scr[...], w1_ref[...], w3_ref[...], w2_ref[0])
        for j in range(rows_per_step):
            copy_next(f * rows_per_step + j).start()

        @pl.when(f == 0)
        def _():
            acc[...] = y

        @pl.when(f > 0)
        def _():
            acc[...] += y

    @pl.when(f == n_f - 1)
    def _():
        ys_ref[...] = jnp.where(used, acc[...], 0.0)
        pl.when(used & (i == pl.num_programs(0) - 1))(wait_next)


def _moe_ffn(x, gain, tok, tile_expert, n_used, w1, w3, w2, tm):
    n_tiles = tok.shape[0]
    _, n_f, _, tf = w1.shape

    def w_block(i, f, te, nu):
        last = nu[0] - 1
        return te[jnp.minimum(i, last)], jnp.where(i <= last, f, n_f - 1)

    def w13_idx(i, f, te, nu):
        e, ff = w_block(i, f, te, nu)
        return (e, ff, 0, 0)

    def w2_idx(i, f, te, nu):
        e, ff = w_block(i, f, te, nu)
        return (e, ff, 0)

    smem_tile = lambda idx: pl.BlockSpec((1, 1, tm), idx, memory_space=pltpu.SMEM)
    grid_spec = pltpu.PrefetchScalarGridSpec(
        num_scalar_prefetch=2,
        grid=(n_tiles, n_f),
        in_specs=[smem_tile(lambda i, f, te, nu: (i, 0, 0)),
                  smem_tile(lambda i, f, te, nu: (jnp.minimum(i + 1, n_tiles - 1), 0, 0)),
                  pl.BlockSpec(memory_space=pl.ANY),
                  pl.BlockSpec((1, D_MODEL), lambda i, f, te, nu: (0, 0)),
                  pl.BlockSpec((None, None, D_MODEL, tf), w13_idx),
                  pl.BlockSpec((None, None, D_MODEL, tf), w13_idx),
                  pl.BlockSpec((1, tf, D_MODEL), w2_idx)],
        out_specs=pl.BlockSpec((tm, D_MODEL), lambda i, f, te, nu: (i, 0)),
        scratch_shapes=[pltpu.VMEM((2, tm, D_MODEL), F32),
                        pltpu.SemaphoreType.DMA((2,)),
                        pltpu.VMEM((tm, D_MODEL), BF16),
                        pltpu.VMEM((tm, D_MODEL), F32)])
    return pl.pallas_call(
        functools.partial(_moe_ffn_body, tm),
        grid_spec=grid_spec,
        out_shape=jax.ShapeDtypeStruct((n_tiles * tm, D_MODEL), F32),
        compiler_params=pltpu.CompilerParams(
            dimension_semantics=("arbitrary", "arbitrary"), vmem_limit_bytes=VMEM_LIMIT),
        name="moe_ffn",
    )(tile_expert, n_used, tok, tok, x, gain, w1, w3, w2)


def _moe_combine_body(tm, final_norm, *refs):
    if final_norm:
        rows_ref, rows_next_ref, x_ref, route_ref, ys_hbm, gf_ref, o_ref, ybuf, sem = refs
    else:
        rows_ref, rows_next_ref, x_ref, route_ref, ys_hbm, o_ref, ybuf, sem = refs
    i = pl.program_id(0)
    slot = lax.rem(i, 2)
    _, wait_next, copy_next = _row_gather(ys_hbm, rows_next_ref, ybuf, sem, 1 - slot, 2 * tm)
    start_cur, wait_cur, _ = _row_gather(ys_hbm, rows_ref, ybuf, sem, slot, 2 * tm)
    pl.when(i == 0)(start_cur)
    wait_cur()
    route = route_ref[...]
    out = x_ref[...] + (route[:, 2:3] * ybuf[slot, 0:tm, :] + route[:, 3:4] * ybuf[slot, tm:2 * tm, :])
    if final_norm:
        out = _rms(out, NORM_EPS) * gf_ref[...]
    o_ref[...] = out
    for r in range(2 * tm):
        copy_next(r).start(priority=r % 2)
    pl.when(i == pl.num_programs(0) - 1)(wait_next)


def _moe_combine(x, route, rows, ys, final_gain, tm):
    n = x.shape[0]
    n_tiles = n // tm
    final_norm = final_gain is not None
    smem_tile = lambda idx: pl.BlockSpec((1, 1, 2 * tm), idx, memory_space=pltpu.SMEM)
    in_specs = [smem_tile(lambda i: (i, 0, 0)),
                smem_tile(lambda i: (jnp.minimum(i + 1, n_tiles - 1), 0, 0)),
                pl.BlockSpec((tm, D_MODEL), lambda i: (i, 0)),
                pl.BlockSpec((tm, LANE), lambda i: (i, 0)),
                pl.BlockSpec(memory_space=pl.ANY)]
    args = [rows, rows, x, route, ys]
    if final_norm:
        in_specs.append(pl.BlockSpec((1, D_MODEL), lambda i: (0, 0)))
        args.append(final_gain)
    return pl.pallas_call(
        functools.partial(_moe_combine_body, tm, final_norm),
        grid=(n_tiles,),
        in_specs=in_specs,
        out_specs=pl.BlockSpec((tm, D_MODEL), lambda i: (i, 0)),
        out_shape=jax.ShapeDtypeStruct((n, D_MODEL), F32),
        scratch_shapes=[pltpu.VMEM((2, 2 * tm, D_MODEL), F32), pltpu.SemaphoreType.DMA((2,))],
        compiler_params=pltpu.CompilerParams(
            dimension_semantics=("arbitrary",), vmem_limit_bytes=VMEM_LIMIT),
        name="moe_combine",
    )(*args)


def _moe(x, gain, router, w1, w3, w2, final_gain):
    n = x.shape[0]
    tm = 512 if n >= 8192 else 256
    tm_c = 256
    n_tiles = -(-(2 * n + N_EXP * (tm - 1)) // tm)
    route = _route(x, gain, router)
    i1 = route[:, 0].astype(jnp.int32)
    i2 = route[:, 1].astype(jnp.int32)
    experts = jnp.arange(N_EXP, dtype=jnp.int32)
    hit = ((i1[:, None] == experts) | (i2[:, None] == experts)).astype(jnp.int32)
    rank = jnp.cumsum(hit, axis=0) - hit
    tiles_e = (jnp.sum(hit, axis=0) + tm - 1) // tm
    tile_end = jnp.cumsum(tiles_e)
    row_off = (tile_end - tiles_e) * tm
    row1 = row_off[i1] + jnp.take_along_axis(rank, i1[:, None], axis=1)[:, 0]
    row2 = row_off[i2] + jnp.take_along_axis(rank, i2[:, None], axis=1)[:, 0]
    token = jnp.arange(n, dtype=jnp.int32)
    tok = jnp.zeros((n_tiles * tm,), jnp.int32).at[jnp.concatenate([row1, row2])].set(
        jnp.concatenate([token, token]), unique_indices=True, mode="promise_in_bounds")
    tile_ids = jnp.arange(n_tiles, dtype=jnp.int32)
    tile_expert = jnp.minimum(
        jnp.sum((tile_end[None, :] <= tile_ids[:, None]).astype(jnp.int32), axis=1), N_EXP - 1)
    n_used = tile_end[N_EXP - 1:].astype(jnp.int32)
    ys = _moe_ffn(x, gain, tok.reshape(n_tiles, 1, tm), tile_expert, n_used, w1, w3, w2, tm)
    rows = jnp.concatenate([row1.reshape(n // tm_c, 1, tm_c), row2.reshape(n // tm_c, 1, tm_c)],
                           axis=2).astype(jnp.int32)
    return _moe_combine(x, route, rows, ys, final_gain, tm_c)


def _pad_last(a, width):
    return jnp.pad(a, [(0, 0)] * (a.ndim - 1) + [(0, width - a.shape[-1])])


def _hidden_slices(w):
    lead = w.shape[:-2]
    w = w.astype(BF16).reshape(lead + (D_MODEL, HIDDEN_STEPS, w.shape[-1] // HIDDEN_STEPS))
    return jnp.swapaxes(w, -3, -2)


def _pack_params(p):
    w_in = p["w_in"]
    w_in = jnp.concatenate([_pad_last(w_in[..., :DN_P], DN_SLAB),
                            _pad_last(w_in[..., DN_P:DN_P + RW_P], RW_SLAB),
                            w_in[..., DN_P + RW_P:]], axis=-1).astype(BF16)
    head_par = jnp.stack([_pad_last(p["dn_a_log"], LANE), _pad_last(p["dn_dt_bias"], LANE)], axis=1)
    g2 = jnp.pad(p["rw_g2"], ((0, 0), (0, RW_GATE_PAD - RW_GATE_R), (0, 0)))
    row = lambda a: a[:, None, :]
    return dict(
        ln_mix=row(p["ln_mix"]), w_in=w_in, dn_conv=p["dn_conv"], dn_head=head_par,
        dn_norm=row(jnp.tile(p["dn_norm"], (1, DN_HEADS))),
        rw=[(row(_pad_last(p["rw_mu"], RW_SLAB))[l], row(p["rw_w0"])[l], p["rw_w2"][l],
             row(p["rw_a0"])[l], p["rw_a2"][l], g2[l], row(p["rw_kk"])[l], row(p["rw_ka"])[l],
             p["rw_rk"].reshape(N_LAYERS, 1, RW_W)[l], row(p["rw_ln_g"])[l], row(p["rw_ln_b"])[l])
            for l in range(N_LAYERS)],
        rt_ln=row(p["rt_ln"]), w_out=p["w_out"].astype(BF16), ln_ffn=row(p["ln_ffn"]),
        ffn_w1=_hidden_slices(p["ffn_w1"]), ffn_w3=_hidden_slices(p["ffn_w3"]),
        ffn_w2=p["ffn_w2"].astype(BF16),
        moe_router=_pad_last(p["moe_router"], LANE),
        moe_w1=_hidden_slices(p["moe_w1"]), moe_w3=_hidden_slices(p["moe_w3"]),
        moe_w2=p["moe_w2"].astype(BF16),
        ln_final=p["ln_final"][None, :],
    )


def _trunk(x, n_seq, t_real, t_pad, c_len, n_par, states, pos0, w):
    n_valid = min(c_len, t_real)
    cos, sin = _rotary_tables(pos0, t_pad)
    convs, shifts, rts = [], [], []
    dns = rws = None
    seq = lambda a: a.reshape(n_seq, t_pad, a.shape[-1])
    flat = lambda a: a.reshape(n_seq * t_pad, a.shape[-1])
    for l in range(N_LAYERS):
        z_dn, z_rw, z_rt = (seq(z) for z in _inproj(x, w["ln_mix"][l], w["w_in"][l]))
        if states is None:
            conv0 = s_dn0 = shift0 = s_rw0 = s_rt0 = None
        else:
            conv0, s_dn0 = states[0][l], states[1][l]
            shift0 = _pad_last(states[2][l], RW_SLAB)[:, None, :]
            s_rw0, s_rt0 = states[3][l], states[4][l]
        o_dn, dns = _dn_group(z_dn, conv0, s_dn0, dns, w["dn_conv"][l], w["dn_head"][l],
                              w["dn_norm"][l], l, c_len, n_valid, n_par)
        o_rw, rws = _rw_group(z_rw, shift0, s_rw0, rws, w["rw"][l], l, c_len, n_valid, n_par)
        o_rt, s_rt = _rt_group(z_rt, cos, sin, s_rt0, w["rt_ln"][l], c_len, n_valid, n_par)
        x = _outproj(x, flat(o_dn), flat(o_rw), flat(o_rt), w["w_out"][l])
        if l % 2 == 0:
            x = _ffn(x, w["ln_ffn"][l], w["ffn_w1"][l // 2], w["ffn_w3"][l // 2], w["ffn_w2"][l // 2],
                     w["ln_final"] if l == N_LAYERS - 1 else None)
        else:
            xr = seq(x)[:, :t_real].reshape(n_seq * t_real, D_MODEL)
            xr = _moe(xr, w["ln_ffn"][l], w["moe_router"][l // 2], w["moe_w1"][l // 2],
                      w["moe_w3"][l // 2], w["moe_w2"][l // 2],
                      w["ln_final"] if l == N_LAYERS - 1 else None)
            x = xr if l == N_LAYERS - 1 else flat(
                jnp.pad(xr.reshape(n_seq, t_real, D_MODEL), ((0, 0), (0, t_pad - t_real), (0, 0))))
        if t_real >= CONV_W - 1:
            convs.append(z_dn[:, t_real - (CONV_W - 1):t_real, :DN_CONV])
        else:
            convs.append(jnp.concatenate([conv0[:, t_real:], z_dn[:, :t_real, :DN_CONV]], axis=1))
        shifts.append(z_rw[:, t_real - 1, :RW_P])
        rts.append(s_rt)
    return x, jnp.stack(convs), dns, jnp.stack(shifts), rws, jnp.stack(rts)


def kernel(x_prompt, x_sample, state_dn_conv, state_dn, state_rw_shift, state_rw, state_rt, ln_mix, w_in, dn_conv, dn_a_log, dn_dt_bias, dn_norm, rw_mu, rw_w0, rw_w2, rw_a0, rw_a2, rw_g2, rw_kk, rw_ka, rw_rk, rw_ln_g, rw_ln_b, rt_ln, w_out, ln_ffn, ffn_w1, ffn_w3, ffn_w2, moe_router, moe_w1, moe_w3, moe_w2, ln_final):
    w = _pack_params(dict(
        ln_mix=ln_mix, w_in=w_in, dn_conv=dn_conv, dn_a_log=dn_a_log, dn_dt_bias=dn_dt_bias,
        dn_norm=dn_norm, rw_mu=rw_mu, rw_w0=rw_w0, rw_w2=rw_w2, rw_a0=rw_a0, rw_a2=rw_a2,
        rw_g2=rw_g2, rw_kk=rw_kk, rw_ka=rw_ka, rw_rk=rw_rk, rw_ln_g=rw_ln_g, rw_ln_b=rw_ln_b,
        rt_ln=rt_ln, w_out=w_out, ln_ffn=ln_ffn, ffn_w1=ffn_w1, ffn_w3=ffn_w3, ffn_w2=ffn_w2,
        moe_router=moe_router, moe_w1=moe_w1, moe_w3=moe_w3, moe_w2=moe_w2, ln_final=ln_final))
    bp, tp, _ = x_prompt.shape
    bs, ts, _ = x_sample.shape

    yp, *p_states = _trunk(x_prompt.reshape(bp * tp, D_MODEL), bp, tp, tp, 64, 8, None, 0, w)
    y_prompt = yp.reshape(bp, tp, D_MODEL)

    ts_pad = 8
    xs = jnp.pad(x_sample, ((0, 0), (0, ts_pad - ts), (0, 0))).reshape(bs * ts_pad, D_MODEL)
    ys, *s_states = _trunk(xs, bs, ts, ts_pad, ts_pad, 8,
                           (state_dn_conv, state_dn, state_rw_shift, state_rw, state_rt),
                           PAST_LEN, w)
    y_sample = ys.reshape(bs, -1, D_MODEL)[:, :ts]
    return (y_prompt, y_sample, *p_states, *s_states)
```

```python
import functools
import math

import numpy as np
import jax
import jax.numpy as jnp
from jax import lax
from jax.experimental import pallas as pl
from jax.experimental.pallas import tpu as pltpu

F32 = jnp.float32
BF16 = jnp.bfloat16

D_MODEL = 1024
N_LAYERS = 2
PAST_LEN = 16384
HEAD = 64
DN_HEADS = 6
DN_QK = DN_HEADS * HEAD
DN_CONV = 3 * DN_QK
CONV_W = 4
RW_HEADS = 6
RW_W = RW_HEADS * HEAD
RW_GATE_R = 160
RT_HEADS = 4
RT_DK = 32
RT_QK = RT_HEADS * RT_DK
RT_W = RT_HEADS * HEAD
DN_P = DN_CONV + DN_QK + 2 * DN_HEADS
RW_P = 3 * RW_W + 64 + 64 + RW_GATE_R
RT_P = 2 * RT_QK + 2 * RT_W
LANE = 128
DN_SLAB = 1664
RW_SLAB = 1536
RT_SLAB = RT_P
P_SLABS = DN_SLAB + RW_SLAB + RT_SLAB
RW_GATE_PAD = RW_SLAB - (3 * RW_W + 128)
N_EXP = 8
D_FF = 2816
D_EXP = 3584
MOE_F_STEPS = 2
ROPE_BASE = 10000.0
NORM_EPS = 1e-6
RW_GN_EPS = 64e-5
RT_GN_EPS = 1e-5
TAIL = 8
VMEM_LIMIT = 56 * 1024 * 1024


def _dot(a, b, precision=None):
    return jnp.dot(a, b, preferred_element_type=F32, precision=precision)


def _split3(x):
    x1 = x.astype(BF16)
    r1 = x - x1.astype(F32)
    x2 = r1.astype(BF16)
    x3 = (r1 - x2.astype(F32)).astype(BF16)
    return x1, x2, x3


def _select_sum(sel, x):
    sel = sel.astype(BF16)
    return sum(jnp.dot(sel, p, preferred_element_type=F32) for p in _split3(x))


def _select_sum_nt(sel, x):
    sel = sel.astype(BF16)
    return sum(lax.dot_general(sel, p, (((1,), (1,)), ((), ())), preferred_element_type=F32)
               for p in _split3(x))


def _mm(a, b):
    return jnp.dot(a.astype(BF16), b.astype(BF16), preferred_element_type=F32)


def _mm_nt(a, b):
    return lax.dot_general(a.astype(BF16), b.astype(BF16), (((1,), (1,)), ((), ())),
                           preferred_element_type=F32)


def _rows(*parts):
    return jnp.concatenate(parts, axis=0)


def _cols(*parts):
    return jnp.concatenate(parts, axis=1)


def _silu(x):
    return x * jax.nn.sigmoid(x)


def _softplus(x):
    return jnp.maximum(x, 0.0) + jnp.log(1.0 + jnp.exp(-jnp.abs(x)))


def _rms(x, eps):
    return x * lax.rsqrt(jnp.mean(x * x, axis=-1, keepdims=True) + eps)


def _tri(c):
    r = lax.broadcasted_iota(jnp.int32, (c, c), 0)
    col = lax.broadcasted_iota(jnp.int32, (c, c), 1)
    return r >= col, r > col, r == col


def _group_ones():
    r = lax.broadcasted_iota(jnp.int32, (LANE, LANE), 0)
    c = lax.broadcasted_iota(jnp.int32, (LANE, LANE), 1)
    return ((r < HEAD) == (c < HEAD)).astype(BF16)


def _group_sum(x, ones_blk, terms=2):
    parts = _split3(x)[:terms]
    out = [sum(jnp.dot(p[:, j:j + LANE], ones_blk, preferred_element_type=F32) for p in parts)
           for j in range(0, x.shape[1], LANE)]
    return out[0] if len(out) == 1 else jnp.concatenate(out, axis=1)


def _bd(x):
    x = x.astype(BF16)
    first = lax.broadcasted_iota(jnp.int32, x.shape, 1) < x.shape[1] // 2
    zero = jnp.zeros_like(x)
    return jnp.concatenate([jnp.where(first, x, zero), jnp.where(first, zero, x)], axis=0)


class _PairTimeMasks:
    def __init__(self, c_len):
        row = lax.broadcasted_iota(jnp.int32, (c_len, 2 * c_len), 0)
        lane = lax.broadcasted_iota(jnp.int32, (c_len, 2 * c_len), 1)
        col = jnp.where(lane < c_len, lane, lane - c_len)
        self.incl = row >= col
        self.strict = row > col
        self.eye = (row == col).astype(F32)
        self.first = lax.broadcasted_iota(jnp.int32, (1, 2 * c_len), 1) < c_len


def _neumann_levels(n_rows):
    return max(0, math.ceil(math.log2(n_rows)) - 1)


def _inproj_body(x_ref, g_ref, w_ref, zdn_ref, zrw_ref, zrt_ref):
    h = (_rms(x_ref[...], NORM_EPS) * g_ref[...]).astype(BF16)
    zdn_ref[...] = _dot(h, w_ref[:, 0:DN_SLAB], None)
    zrw_ref[...] = _dot(h, w_ref[:, DN_SLAB:DN_SLAB + RW_SLAB], None)
    zrt_ref[...] = _dot(h, w_ref[:, DN_SLAB + RW_SLAB:P_SLABS], None)


def _inproj(x, gain, w):
    n = x.shape[0]
    tm = min(512, n)
    return pl.pallas_call(
        _inproj_body,
        grid=(n // tm,),
        in_specs=[pl.BlockSpec((tm, D_MODEL), lambda i: (i, 0)),
                  pl.BlockSpec((1, D_MODEL), lambda i: (0, 0)),
                  pl.BlockSpec((D_MODEL, P_SLABS), lambda i: (0, 0))],
        out_specs=[pl.BlockSpec((tm, DN_SLAB), lambda i: (i, 0)),
                   pl.BlockSpec((tm, RW_SLAB), lambda i: (i, 0)),
                   pl.BlockSpec((tm, RT_SLAB), lambda i: (i, 0))],
        out_shape=[jax.ShapeDtypeStruct((n, DN_SLAB), F32),
                   jax.ShapeDtypeStruct((n, RW_SLAB), F32),
                   jax.ShapeDtypeStruct((n, RT_SLAB), F32)],
        compiler_params=pltpu.CompilerParams(
            dimension_semantics=("arbitrary",), vmem_limit_bytes=VMEM_LIMIT),
        name="inproj",
    )(x, gain, w)


def _seq_specs(n_par, c_len, width):
    return pl.BlockSpec((n_par, c_len, width), lambda b, c: (b, c, 0))


def _state_spec(n_par, shape):
    return pl.BlockSpec((n_par,) + shape, lambda b, c: (b,) + (0,) * len(shape))


def _stacked_state_out(layer, stacked, n_par, n_seq, state, in_specs, args):
    shape = jax.ShapeDtypeStruct((N_LAYERS, n_seq) + state, F32)
    zeros = (0,) * len(state)
    if layer == 0:
        return pl.BlockSpec((N_LAYERS, n_par) + state, lambda b, c: (0, b) + zeros), shape, {}
    in_specs.append(pl.BlockSpec(memory_space=pl.ANY))
    args.append(stacked)
    return (pl.BlockSpec((None, n_par) + state, lambda b, c: (layer, b) + zeros), shape,
            {len(args) - 1: 1})


def _own_layer(so_ref, layer):
    if layer > 0:
        return so_ref
    so_ref[1:] = jnp.zeros((so_ref.shape[0] - 1,) + so_ref.shape[1:], F32)
    return so_ref.at[0]


def _const_spec(a):
    return pl.BlockSpec(a.shape, lambda b, c: (0,) * a.ndim)


def _dn_body(c_len, n_valid, fresh, n_par, layer, *refs):
    refs = list(refs)
    if layer > 0:
        del refs[-6]
    if fresh:
        z_ref, cw_ref, hp_ref, nw_ref, o_ref, so_ref, xe, s_scr, o_raw = refs
    else:
        z_ref, c0_ref, s0_ref, cw_ref, hp_ref, nw_ref, o_ref, so_ref, xe, s_scr, o_raw = refs
    ci = pl.program_id(1)

    @pl.when(ci == 0)
    def _():
        xe[:, 0:TAIL, :] = jnp.zeros((n_par, TAIL, DN_CONV), F32)
        s_scr[...] = jnp.zeros_like(s_scr)
        if not fresh:
            xe[:, TAIL - (CONV_W - 1):TAIL, :] = c0_ref[...]
            for p in range(DN_HEADS // 2):
                s_scr[:, p, 0:HEAD, 0:HEAD] = s0_ref[:, 2 * p]
                s_scr[:, p, HEAD:LANE, HEAD:LANE] = s0_ref[:, 2 * p + 1]

    tri_f = _tri(c_len)[0].astype(F32)
    live = lax.broadcasted_iota(jnp.int32, (c_len, LANE), 0) < n_valid
    neg_a = -jnp.exp(hp_ref[0:1, :])
    dt_bias = hp_ref[1:2, :]

    ones_blk = _group_ones()
    eye_l = _tri(LANE)[2].astype(F32)
    eye_8 = eye_l[0:8, :]

    def prepare():
        seqs = []
        for i in range(n_par):
            x = z_ref[i, :, 0:DN_CONV]
            xe[i, TAIL:TAIL + c_len, :] = x
            acc = x * cw_ref[CONV_W - 1:CONV_W, :]
            for j in range(CONV_W - 1):
                off = TAIL - (CONV_W - 1) + j
                acc = acc + xe[i, off:off + c_len, :] * cw_ref[j:j + 1, :]
            xe[i, 0:TAIL, :] = xe[i, c_len:c_len + TAIL, :]
            ab = z_ref[i, :, DN_CONV + DN_QK:DN_SLAB]
            g_tok = neg_a * _softplus(ab + dt_bias)
            beta = jax.nn.sigmoid(ab)
            if n_valid < c_len:
                g_tok = jnp.where(live, g_tok, 0.0)
                beta = jnp.where(live, beta, 0.0)
            conv = _silu(acc)
            seqs.append(dict(conv=conv, v=conv[:, 2 * DN_QK:DN_CONV], beta=beta, g_tok=g_tok,
                             gate=_silu(z_ref[i, :, DN_CONV:DN_CONV + DN_QK])))
        for sq in seqs:
            sq["g_cum"] = _select_sum(tri_f, sq["g_tok"])
            qk = sq["conv"][:, 0:2 * DN_QK]
            sq["qk"] = qk * lax.rsqrt(_group_sum(qk * qk, ones_blk) + 1e-6)
        for sq in seqs:
            sq["g_cum_t"] = _select_sum_nt(eye_8, sq["g_cum"])
        return seqs

    def advance(seqs):
        pt = _PairTimeMasks(c_len)
        first_h = lax.broadcasted_iota(jnp.int32, (1, LANE), 1) < HEAD
        first_rows = lax.broadcasted_iota(jnp.int32, (LANE, 1), 0) < HEAD
        chains = []
        for i, sq in enumerate(seqs):
            for p in range(DN_HEADS // 2):
                ha, hb = 2 * p, 2 * p + 1
                lo = LANE * p
                k = sq["qk"][:, DN_QK + lo:DN_QK + lo + LANE]
                v = sq["v"][:, lo:lo + LANE]
                gc_a, gc_b = sq["g_cum"][:, ha:ha + 1], sq["g_cum"][:, hb:hb + 1]
                gc_row = _cols(sq["g_cum_t"][ha:ha + 1, :], sq["g_cum_t"][hb:hb + 1, :])
                g_end = jnp.where(pt.first, gc_a[c_len - 1:c_len], gc_b[c_len - 1:c_len])
                b = jnp.where(first_h, sq["beta"][:, DN_HEADS + ha:DN_HEADS + ha + 1],
                              sq["beta"][:, DN_HEADS + hb:DN_HEADS + hb + 1])
                egc = jnp.exp(jnp.where(first_h, gc_a, gc_b))
                dmask = jnp.where(
                    pt.incl, jnp.exp(jnp.minimum(jnp.where(pt.first, gc_a, gc_b) - gc_row, 0.0)), 0.0)
                q = sq["qk"][:, lo:lo + LANE] * (HEAD ** -0.5)
                kb = k * b
                chains.append(dict(
                    i=i, p=p, k=k, kb=kb, q=q, qe=q * egc, vb=v * b, kbe=kb * egc, dmask=dmask,
                    k_decay=jnp.exp(g_end - gc_row),
                    s_decay=jnp.exp(jnp.where(first_rows, gc_a[c_len - 1:c_len],
                                              gc_b[c_len - 1:c_len]))))
        for c in chains:
            c["kq"] = _mm_nt(_rows(c["kb"], c["q"], eye_l), _bd(c["k"]))
        for c in chains:
            c["np"] = -jnp.where(pt.strict, c["kq"][0:c_len] * c["dmask"], 0.0)
            c["t"] = pt.eye + c["np"]
        for _ in range(_neumann_levels(n_valid)):
            for c in chains:
                c["np"] = _mm(c["np"], _bd(c["np"])).astype(BF16)
                c["np_bd"] = _bd(c["np"])
            for c in chains:
                c["t"] = c["t"] + _mm(c["t"], c["np_bd"])
        for c in chains:
            c["tx"] = _mm(c["t"], _cols(_bd(c["vb"]), _bd(c["kbe"])))
        for c in chains:
            qk = c["kq"][c_len:2 * c_len] * c["dmask"]
            kd_t = c["kq"][2 * c_len:2 * c_len + LANE] * c["k_decay"]
            c["w"] = _mm(_rows(qk, kd_t), _cols(_bd(c["tx"][:, 0:LANE]), _bd(c["tx"][:, LANE:2 * LANE])))
        for c in chains:
            w = c["w"]
            c["s"] = s_scr[c["i"], c["p"]]
            c["res"] = _mm(_rows(c["qe"] - w[0:c_len, LANE:2 * LANE],
                                 w[c_len:c_len + LANE, LANE:2 * LANE]), c["s"])
        for c in chains:
            w, res = c["w"], c["res"]
            lo = LANE * c["p"]
            s_scr[c["i"], c["p"]] = (c["s"] * c["s_decay"] - res[c_len:c_len + LANE]
                                     + w[c_len:c_len + LANE, 0:LANE])
            o_raw[c["i"], :, lo:lo + LANE] = res[0:c_len] + w[0:c_len, 0:LANE]
        for i, sq in enumerate(seqs):
            o = o_raw[i]
            ms = _group_sum(o * o, ones_blk, terms=1) * (1.0 / HEAD)
            o_ref[i] = o * lax.rsqrt(ms + NORM_EPS) * nw_ref[...] * sq["gate"]

    advance(prepare())

    @pl.when(ci == pl.num_programs(1) - 1)
    def _():
        so = _own_layer(so_ref, layer)
        for p in range(DN_HEADS // 2):
            so[:, 2 * p] = s_scr[:, p, 0:HEAD, 0:HEAD]
            so[:, 2 * p + 1] = s_scr[:, p, HEAD:LANE, HEAD:LANE]


def _dn_group(z, conv0, s0, stacked, conv_w, head_par, norm_w, layer, c_len, n_valid, n_par):
    n_seq, t_pad, _ = z.shape
    fresh = s0 is None
    nc = t_pad // c_len
    assert n_valid == c_len or nc == 1
    state = (DN_HEADS, HEAD, HEAD)
    in_specs = [_seq_specs(n_par, c_len, DN_SLAB)]
    args = [z]
    if not fresh:
        in_specs += [_state_spec(n_par, (CONV_W - 1, DN_CONV)), _state_spec(n_par, state)]
        args += [conv0, s0]
    for p in (conv_w, head_par, norm_w):
        in_specs.append(_const_spec(p))
        args.append(p)
    so_spec, so_shape, aliases = _stacked_state_out(layer, stacked, n_par, n_seq, state, in_specs, args)
    return pl.pallas_call(
        functools.partial(_dn_body, c_len, n_valid, fresh, n_par, layer),
        grid=(n_seq // n_par, nc),
        in_specs=in_specs,
        out_specs=[_seq_specs(n_par, c_len, DN_QK), so_spec],
        out_shape=[jax.ShapeDtypeStruct((n_seq, t_pad, DN_QK), F32), so_shape],
        input_output_aliases=aliases,
        scratch_shapes=[pltpu.VMEM((n_par, c_len + TAIL, DN_CONV), F32),
                        pltpu.VMEM((n_par, DN_HEADS // 2, LANE, LANE), F32),
                        pltpu.VMEM((n_par, c_len, DN_QK), F32)],
        compiler_params=pltpu.CompilerParams(
            dimension_semantics=("arbitrary", "arbitrary"), vmem_limit_bytes=VMEM_LIMIT),
        name="dn_group",
    )(*args)


def _rw_body(c_len, n_valid, fresh, n_par, layer, *refs):
    refs = list(refs)
    if layer > 0:
        del refs[-6]
    if fresh:
        (z_ref, mu_ref, w0_ref, w2_ref, a0_ref, a2_ref, g2_ref, kk_ref, ka_ref, rk_ref,
         lng_ref, lnb_ref, o_ref, so_ref, xe, s_scr, y_raw) = refs
    else:
        (z_ref, sh0_ref, s0_ref, mu_ref, w0_ref, w2_ref, a0_ref, a2_ref, g2_ref, kk_ref, ka_ref,
         rk_ref, lng_ref, lnb_ref, o_ref, so_ref, xe, s_scr, y_raw) = refs
    ci = pl.program_id(1)

    @pl.when(ci == 0)
    def _():
        xe[:, 0:TAIL, :] = jnp.zeros((n_par, TAIL, RW_SLAB), F32)
        s_scr[...] = jnp.zeros_like(s_scr)
        if not fresh:
            xe[:, TAIL - 1:TAIL, :] = sh0_ref[...]
            for p in range(RW_HEADS // 2):
                s_scr[:, p, 0:HEAD, 0:HEAD] = s0_ref[:, 2 * p]
                s_scr[:, p, HEAD:LANE, HEAD:LANE] = s0_ref[:, 2 * p + 1]

    tri_f = _tri(c_len)[0].astype(F32)
    live = lax.broadcasted_iota(jnp.int32, (c_len, RW_W), 0) < n_valid
    ones_blk = _group_ones()
    eye_l = _tri(LANE)[2].astype(F32)

    seqs = []
    for i in range(n_par):
        x = z_ref[i]
        xe[i, TAIL:TAIL + c_len, :] = x
        prev = xe[i, TAIL - 1:TAIL - 1 + c_len, :]
        zs = x + (prev - x) * mu_ref[...]
        xe[i, 0:TAIL, :] = xe[i, c_len:c_len + TAIL, :]
        seqs.append(dict(r=zs[:, 0:RW_W], k=zs[:, RW_W:2 * RW_W], v=zs[:, 2 * RW_W:3 * RW_W],
                         wl=zs[:, 3 * RW_W:3 * RW_W + 64], al=zs[:, 3 * RW_W + 64:3 * RW_W + 128],
                         gl=zs[:, 3 * RW_W + 128:RW_SLAB]))
    for sq in seqs:
        sq["w_lora"] = _mm(jnp.tanh(sq["wl"]), w2_ref[...])
        sq["a_lora"] = _mm(sq["al"], a2_ref[...])
        sq["g"] = _mm(jax.nn.sigmoid(sq["gl"]), g2_ref[...])
    for sq in seqs:
        w_log = -_softplus(-(w0_ref[...] + sq["w_lora"])) - 0.5
        a = jax.nn.sigmoid(a0_ref[...] + sq["a_lora"])
        kk_in = sq["k"] * kk_ref[...]
        k2 = sq["k"] * (1.0 + (a - 1.0) * ka_ref[...])
        lw = -jnp.exp(w_log)
        if n_valid < c_len:
            lw = jnp.where(live, lw, 0.0)
            k2 = jnp.where(live, k2, 0.0)
            kk_in = jnp.where(live, kk_in, 0.0)
        sq.update(a=a, kk_in=kk_in, k2=k2, lw=lw)
    for sq in seqs:
        sq["gcum"] = _select_sum(tri_f, sq["lw"])
        kk_in = sq["kk_in"]
        sq["kk"] = kk_in * lax.rsqrt(_group_sum(kk_in * kk_in, ones_blk) + 1e-6)
    for sq in seqs:
        gcum = sq["gcum"]
        e_neg = jnp.exp(-gcum)
        g_end = gcum[c_len - 1:c_len, :]
        e_rest = jnp.exp(g_end - gcum)
        bv = sq["kk"] * sq["a"]
        sq.update(at=-sq["kk"] * jnp.exp(gcum - sq["lw"]), bt=bv * e_neg, kt=sq["k2"] * e_neg,
                  rt=sq["r"] * jnp.exp(gcum), b_rest=bv * e_rest, k_rest=sq["k2"] * e_rest,
                  e_end=jnp.exp(g_end))

    pt = _PairTimeMasks(c_len)
    c2 = 2 * c_len
    chains = []
    for i, sq in enumerate(seqs):
        for p in range(RW_HEADS // 2):
            sl = slice(LANE * p, LANE * (p + 1))
            chains.append(dict(i=i, p=p, sl=sl, at=sq["at"][:, sl], rt=sq["rt"][:, sl],
                               bt_bd=_bd(sq["bt"][:, sl]), kt_bd=_bd(sq["kt"][:, sl]),
                               v_bd=_bd(sq["v"][:, sl]), e_end=sq["e_end"][:, sl],
                               rest=_rows(_bd(sq["b_rest"][:, sl]), _bd(sq["k_rest"][:, sl]))))
    for c in chains:
        c["at_bd"] = _bd(c["at"])
        c["prod"] = _mm_nt(_rows(c["at"], c["rt"]), _rows(c["bt_bd"], c["kt_bd"]))
        c["tr"] = _mm_nt(eye_l, _rows(c["at_bd"], c["v_bd"]))
    for c in chains:
        prod = c["prod"]
        c["np"] = jnp.where(pt.strict, prod[0:c_len, 0:c2], 0.0)
        c["t"] = pt.eye + c["np"]
        c["nak"] = jnp.where(pt.strict, prod[0:c_len, c2:2 * c2], 0.0)
        c["mix"] = _cols(jnp.where(pt.incl, prod[c_len:c2, 0:c2], 0.0),
                         jnp.where(pt.incl, prod[c_len:c2, c2:2 * c2], 0.0))
        c["at_t"] = c["tr"][:, 0:c2]
        c["v_t"] = c["tr"][:, c2:2 * c2]
    for c in chains:
        c["nakv"] = _mm(c["nak"], c["v_bd"])
        c["nakv_t"] = _mm_nt(c["v_t"], _bd(c["nak"]))
    for _ in range(_neumann_levels(n_valid)):
        for c in chains:
            c["np"] = _mm(c["np"], _bd(c["np"])).astype(BF16)
            c["np_bd"] = _bd(c["np"])
        for c in chains:
            c["t"] = c["t"] + _mm(c["t"], c["np_bd"])
    for c in chains:
        c["tx"] = _mm(c["t"], _cols(c["at_bd"], _bd(c["nakv"])))
        c["tx_t"] = _mm_nt(_rows(c["at_t"], c["nakv_t"]), _bd(c["t"]))
    for c in chains:
        tx, tx_t = c["tx"], c["tx_t"]
        c["zy"] = _mm(c["mix"], _rows(_cols(_bd(tx[:, 0:LANE]), _bd(tx[:, LANE:2 * LANE])),
                                      _cols(jnp.zeros((c2, LANE), F32), c["v_bd"])))
        ta_t = tx_t[0:LANE]
        c["m"] = _mm(_rows(_cols(ta_t, jnp.zeros_like(ta_t)), _cols(tx_t[LANE:2 * LANE], c["v_t"])),
                     c["rest"])
    for c in chains:
        c["s"] = s_scr[c["i"], c["p"]]
        c["y"] = _mm_nt(c["rt"] + c["zy"][:, 0:LANE], c["s"])
        c["sm"] = _mm(c["s"], c["m"][0:LANE])
    for c in chains:
        s_scr[c["i"], c["p"]] = c["s"] * c["e_end"] + c["sm"] + c["m"][LANE:2 * LANE]
        y_raw[c["i"], :, c["sl"]] = c["y"] + c["zy"][:, LANE:2 * LANE]
    for i, sq in enumerate(seqs):
        y = y_raw[i]
        d = y - _group_sum(y, ones_blk) * (1.0 / HEAD)
        yn = d * lax.rsqrt(_group_sum(d * d, ones_blk, terms=1) * (1.0 / HEAD) + RW_GN_EPS)
        yn = yn * lng_ref[...] + lnb_ref[...]
        bonus = _group_sum(sq["r"] * sq["k2"] * rk_ref[...], ones_blk, terms=1) * sq["v"]
        o_ref[i] = (yn + bonus) * sq["g"]

    @pl.when(ci == pl.num_programs(1) - 1)
    def _():
        so = _own_layer(so_ref, layer)
        for p in range(RW_HEADS // 2):
            so[:, 2 * p] = s_scr[:, p, 0:HEAD, 0:HEAD]
            so[:, 2 * p + 1] = s_scr[:, p, HEAD:LANE, HEAD:LANE]


def _rw_group(z, shift0, s0, stacked, par, layer, c_len, n_valid, n_par):
    n_seq, t_pad, _ = z.shape
    fresh = s0 is None
    nc = t_pad // c_len
    assert n_valid == c_len or nc == 1
    state = (RW_HEADS, HEAD, HEAD)
    in_specs = [_seq_specs(n_par, c_len, RW_SLAB)]
    args = [z]
    if not fresh:
        in_specs += [_state_spec(n_par, (1, RW_SLAB)), _state_spec(n_par, state)]
        args += [shift0, s0]
    for p in par:
        in_specs.append(_const_spec(p))
        args.append(p)
    so_spec, so_shape, aliases = _stacked_state_out(layer, stacked, n_par, n_seq, state, in_specs, args)
    return pl.pallas_call(
        functools.partial(_rw_body, c_len, n_valid, fresh, n_par, layer),
        grid=(n_seq // n_par, nc),
        in_specs=in_specs,
        out_specs=[_seq_specs(n_par, c_len, RW_W), so_spec],
        out_shape=[jax.ShapeDtypeStruct((n_seq, t_pad, RW_W), F32), so_shape],
        input_output_aliases=aliases,
        scratch_shapes=[pltpu.VMEM((n_par, c_len + TAIL, RW_SLAB), F32),
                        pltpu.VMEM((n_par, RW_HEADS // 2, LANE, LANE), F32),
                        pltpu.VMEM((n_par, c_len, RW_W), F32)],
        compiler_params=pltpu.CompilerParams(
            dimension_semantics=("arbitrary", "arbitrary"), vmem_limit_bytes=VMEM_LIMIT),
        name="rw_group",
    )(*args)


def _rt_body(c_len, n_valid, fresh, n_par, *refs):
    if fresh:
        z_ref, cos_ref, sin_ref, lng_ref, o_ref, so_ref, s_scr, o_raw = refs
    else:
        z_ref, cos_ref, sin_ref, s0_ref, lng_ref, o_ref, so_ref, s_scr, o_raw = refs
    ci = pl.program_id(1)

    @pl.when(ci == 0)
    def _():
        if fresh:
            s_scr[...] = jnp.zeros_like(s_scr)
        else:
            s_scr[...] = s0_ref[...]

    lane = lax.broadcasted_iota(jnp.int32, (c_len, RT_QK), 1)
    first_half = (lane & (RT_DK - 1)) < (RT_DK // 2)
    cos = cos_ref[...]
    sin = sin_ref[...]

    def rotary(t):
        partner = jnp.where(first_half,
                            pltpu.roll(t, RT_QK - RT_DK // 2, axis=1),
                            pltpu.roll(t, RT_DK // 2, axis=1))
        return t * cos + partner * sin

    incl, _, _ = _tri(c_len)
    ri = lax.broadcasted_iota(jnp.int32, (c_len, c_len), 0)
    cj = lax.broadcasted_iota(jnp.int32, (c_len, c_len), 1)
    delta = (ri - cj).astype(F32)
    idx = lax.broadcasted_iota(jnp.int32, (c_len, 1), 0).astype(F32)
    log_decay = [math.log1p(-(2.0 ** (-5.0 - h))) for h in range(RT_HEADS)]
    dmasks = [jnp.where(incl, jnp.exp(delta * lg), 0.0) for lg in log_decay]
    q_scale = [jnp.exp((idx + 1.0) * lg) for lg in log_decay]
    t_row = lax.broadcasted_iota(jnp.int32, (1, c_len), 1).astype(F32)
    k_scale = [jnp.where(t_row < n_valid, jnp.exp((n_valid - 1.0 - t_row) * lg), 0.0) for lg in log_decay]
    ones_blk = _group_ones()
    eye_k = _tri(RT_DK)[2].astype(F32)

    chains = []
    for i in range(n_par):
        q = rotary(z_ref[i, :, 0:RT_QK])
        k = rotary(z_ref[i, :, RT_QK:2 * RT_QK]) * (RT_DK ** -0.5)
        for h in range(RT_HEADS):
            chains.append(dict(
                i=i, h=h, qh=q[:, RT_DK * h:RT_DK * (h + 1)], kh=k[:, RT_DK * h:RT_DK * (h + 1)],
                vh=z_ref[i, :, 2 * RT_QK + HEAD * h:2 * RT_QK + HEAD * (h + 1)]))
    for c in chains:
        c["qk"] = _mm_nt(_rows(c["qh"], eye_k), c["kh"])
    for c in chains:
        h = c["h"]
        c["s"] = s_scr[c["i"], h]
        c["cross"] = _mm(c["qh"] * q_scale[h], c["s"])
        c["kv"] = _mm(c["qk"][c_len:c_len + RT_DK] * k_scale[h], c["vh"])
        c["inner"] = _mm(c["qk"][0:c_len] * dmasks[h], c["vh"])
    for c in chains:
        i, h = c["i"], c["h"]
        s_scr[i, h] = c["s"] * math.exp(n_valid * log_decay[h]) + c["kv"]
        o_raw[i, :, HEAD * h:HEAD * (h + 1)] = c["inner"] + c["cross"]
    for i in range(n_par):
        o = o_raw[i]
        d = o - _group_sum(o, ones_blk) * (1.0 / HEAD)
        on = d * lax.rsqrt(_group_sum(d * d, ones_blk, terms=1) * (1.0 / HEAD) + RT_GN_EPS)
        o_ref[i] = on * lng_ref[...] * _silu(z_ref[i, :, 2 * RT_QK + RT_W:RT_SLAB])

    @pl.when(ci == pl.num_programs(1) - 1)
    def _():
        so_ref[...] = s_scr[...]


def _rt_group(z, cos, sin, s0, ln_g, c_len, n_valid, n_par):
    n_seq, t_pad, _ = z.shape
    fresh = s0 is None
    nc = t_pad // c_len
    assert n_valid == c_len or nc == 1
    state = (RT_HEADS, RT_DK, HEAD)
    table = pl.BlockSpec((c_len, RT_QK), lambda b, c: (c, 0))
    in_specs = [_seq_specs(n_par, c_len, RT_SLAB), table, table]
    args = [z, cos, sin]
    if not fresh:
        in_specs.append(_state_spec(n_par, state))
        args.append(s0)
    in_specs.append(_const_spec(ln_g))
    args.append(ln_g)
    return pl.pallas_call(
        functools.partial(_rt_body, c_len, n_valid, fresh, n_par),
        grid=(n_seq // n_par, nc),
        in_specs=in_specs,
        out_specs=[_seq_specs(n_par, c_len, RT_W), _state_spec(n_par, state)],
        out_shape=[jax.ShapeDtypeStruct((n_seq, t_pad, RT_W), F32),
                   jax.ShapeDtypeStruct((n_seq,) + state, F32)],
        scratch_shapes=[pltpu.VMEM((n_par,) + state, F32),
                        pltpu.VMEM((n_par, c_len, RT_W), F32)],
        compiler_params=pltpu.CompilerParams(
            dimension_semantics=("arbitrary", "arbitrary"), vmem_limit_bytes=VMEM_LIMIT),
        name="rt_group",
    )(*args)


def _rotary_tables(pos0, t_pad):
    half = RT_DK // 2
    inv = ROPE_BASE ** (-np.arange(half, dtype=np.float64) / half)
    ang = (pos0 + np.arange(t_pad, dtype=np.float64))[:, None] * inv[None, :]
    cos = np.tile(np.concatenate([np.cos(ang), np.cos(ang)], axis=1), (1, RT_HEADS))
    sin = np.tile(np.concatenate([-np.sin(ang), np.sin(ang)], axis=1), (1, RT_HEADS))
    return jnp.asarray(cos, F32), jnp.asarray(sin, F32)


def _outproj_body(x_ref, odn_ref, orw_ref, ort_ref, w_ref, o_ref):
    acc = x_ref[...]
    acc = acc + _dot(odn_ref[...].astype(BF16), w_ref[0:DN_QK, :], None)
    acc = acc + _dot(orw_ref[...].astype(BF16), w_ref[DN_QK:DN_QK + RW_W, :], None)
    acc = acc + _dot(ort_ref[...].astype(BF16), w_ref[DN_QK + RW_W:D_MODEL, :], None)
    o_ref[...] = acc


def _outproj(x, o_dn, o_rw, o_rt, w):
    n = x.shape[0]
    tm = min(512, n)
    row = lambda i: (i, 0)
    return pl.pallas_call(
        _outproj_body,
        grid=(n // tm,),
        in_specs=[pl.BlockSpec((tm, D_MODEL), row),
                  pl.BlockSpec((tm, DN_QK), row),
                  pl.BlockSpec((tm, RW_W), row),
                  pl.BlockSpec((tm, RT_W), row),
                  pl.BlockSpec((D_MODEL, D_MODEL), lambda i: (0, 0))],
        out_specs=pl.BlockSpec((tm, D_MODEL), row),
        out_shape=jax.ShapeDtypeStruct((n, D_MODEL), F32),
        compiler_params=pltpu.CompilerParams(
            dimension_semantics=("arbitrary",), vmem_limit_bytes=VMEM_LIMIT),
        name="outproj",
    )(x, o_dn, o_rw, o_rt, w)


def _swiglu(hb, w1, w3, w2):
    return _dot((_silu(_dot(hb, w1)) * _dot(hb, w3)).astype(BF16), w2)


def _ffn_body(final_norm, *refs):
    if final_norm:
        x_ref, g_ref, w1_ref, w3_ref, w2_ref, gf_ref, o_ref, h_scr, acc = refs
    else:
        x_ref, g_ref, w1_ref, w3_ref, w2_ref, o_ref, h_scr, acc = refs
    f = pl.program_id(1)

    @pl.when(f == 0)
    def _():
        x = x_ref[...]
        h_scr[...] = (_rms(x, NORM_EPS) * g_ref[...]).astype(BF16)
        acc[...] = x

    acc[...] += _swiglu(h_scr[...], w1_ref[...], w3_ref[...], w2_ref[...])

    @pl.when(f == pl.num_programs(1) - 1)
    def _():
        if final_norm:
            o_ref[...] = _rms(acc[...], NORM_EPS) * gf_ref[...]
        else:
            o_ref[...] = acc[...]


def _ffn(x, gain, w1, w3, w2, final_gain, tf):
    n = x.shape[0]
    d_f = w1.shape[1]
    final_norm = final_gain is not None
    tm = min(1024, n)
    in_specs = [pl.BlockSpec((tm, D_MODEL), lambda i, f: (i, 0)),
                pl.BlockSpec((1, D_MODEL), lambda i, f: (0, 0)),
                pl.BlockSpec((D_MODEL, tf), lambda i, f: (0, f)),
                pl.BlockSpec((D_MODEL, tf), lambda i, f: (0, f)),
                pl.BlockSpec((tf, D_MODEL), lambda i, f: (f, 0))]
    args = [x, gain, w1, w3, w2]
    if final_norm:
        in_specs.append(pl.BlockSpec((1, D_MODEL), lambda i, f: (0, 0)))
        args.append(final_gain)
    return pl.pallas_call(
        functools.partial(_ffn_body, final_norm),
        grid=(n // tm, d_f // tf),
        in_specs=in_specs,
        out_specs=pl.BlockSpec((tm, D_MODEL), lambda i, f: (i, 0)),
        out_shape=jax.ShapeDtypeStruct((n, D_MODEL), F32),
        scratch_shapes=[pltpu.VMEM((tm, D_MODEL), BF16), pltpu.VMEM((tm, D_MODEL), F32)],
        compiler_params=pltpu.CompilerParams(
            dimension_semantics=("arbitrary", "arbitrary"), vmem_limit_bytes=VMEM_LIMIT),
        name="ffn_dense",
    )(*args)


def _route_body(x_ref, g_ref, router_ref, o_ref):
    h = _rms(x_ref[...], NORM_EPS) * g_ref[...]
    h1, h2, _ = _split3(h)
    r1, r2, _ = _split3(router_ref[...])
    logits = (jnp.dot(h1, r1, preferred_element_type=F32) + jnp.dot(h1, r2, preferred_element_type=F32)
              + jnp.dot(h2, r1, preferred_element_type=F32))
    lane = lax.broadcasted_iota(jnp.int32, logits.shape, 1).astype(F32)
    lg = jnp.where(lane < N_EXP, logits, -jnp.inf)
    m1 = jnp.max(lg, axis=-1, keepdims=True)
    i1 = jnp.min(jnp.where(lg == m1, lane, float(LANE)), axis=-1, keepdims=True)
    lg2 = jnp.where(lane == i1, -jnp.inf, lg)
    m2 = jnp.max(lg2, axis=-1, keepdims=True)
    i2 = jnp.min(jnp.where(lg2 == m2, lane, float(LANE)), axis=-1, keepdims=True)
    e2 = jnp.exp(m2 - m1)
    den = 1.0 + e2
    o_ref[...] = (jnp.where(lane == 0.0, i1, 0.0) + jnp.where(lane == 1.0, i2, 0.0)
                  + jnp.where(lane == 2.0, 1.0 / den, 0.0) + jnp.where(lane == 3.0, e2 / den, 0.0))


def _route(x, gain, router):
    n = x.shape[0]
    tm = min(512, n)
    return pl.pallas_call(
        _route_body,
        grid=(n // tm,),
        in_specs=[pl.BlockSpec((tm, D_MODEL), lambda i: (i, 0)),
                  pl.BlockSpec((1, D_MODEL), lambda i: (0, 0)),
                  pl.BlockSpec((D_MODEL, LANE), lambda i: (0, 0))],
        out_specs=pl.BlockSpec((tm, LANE), lambda i: (i, 0)),
        out_shape=jax.ShapeDtypeStruct((n, LANE), F32),
        compiler_params=pltpu.CompilerParams(
            dimension_semantics=("arbitrary",), vmem_limit_bytes=VMEM_LIMIT),
        name="moe_route",
    )(x, gain, router)


def _row_gather(src_hbm, idx_ref, buf, sem, slot, n_rows):
    def copy(r):
        return pltpu.make_async_copy(src_hbm.at[pl.ds(idx_ref[0, 0, r], 1), :],
                                     buf.at[slot, pl.ds(r, 1), :], sem.at[slot])

    def start():
        def body(r, carry):
            copy(r).start()
            return carry
        lax.fori_loop(0, n_rows, body, 0, unroll=8)

    def wait():
        def body(r, carry):
            copy(r).wait()
            return carry
        lax.fori_loop(0, n_rows, body, 0, unroll=8)

    return start, wait, copy


def _moe_ffn_body(tm, te_ref, nu_ref, tok_ref, tok_next_ref, x_hbm, g_ref, w1_ref, w3_ref, w2_ref,
                  ys_ref, xbuf, sem, h_scr, acc):
    i = pl.program_id(0)
    f = pl.program_id(1)
    n_f = pl.num_programs(1)
    slot = lax.rem(i, 2)
    n_used = nu_ref[0]
    used = i < n_used
    rows_per_step = tm // MOE_F_STEPS
    start_cur, wait_cur, _ = _row_gather(x_hbm, tok_ref, xbuf, sem, slot, tm)
    _, wait_next, copy_next = _row_gather(x_hbm, tok_next_ref, xbuf, sem, 1 - slot, tm)

    @pl.when((f == 0) & (i <= n_used))
    def _():
        pl.when(i == 0)(start_cur)
        wait_cur()

    @pl.when((f == 0) & used)
    def _():
        h_scr[...] = (_rms(xbuf[slot], NORM_EPS) * g_ref[...]).astype(BF16)

    @pl.when(used)
    def _():
        y = _swiglu(h_scr[...], w1_ref[0], w3_ref[0], w2_ref[0])
        for j in range(rows_per_step):
            copy_next(f * rows_per_step + j).start()

        @pl.when(f == 0)
        def _():
            acc[...] = y

        @pl.when(f > 0)
        def _():
            acc[...] += y

    @pl.when(f == n_f - 1)
    def _():
        ys_ref[...] = jnp.where(used, acc[...], 0.0)
        pl.when(used & (i == pl.num_programs(0) - 1))(wait_next)


def _moe_ffn(x, gain, tok, tile_expert, n_used, w1, w3, w2, tm, tf):
    n_tiles = tok.shape[0]
    n_f = D_EXP // tf

    def w_block(i, f, te, nu):
        last = nu[0] - 1
        return te[jnp.minimum(i, last)], jnp.where(i <= last, f, n_f - 1)

    def w13_idx(i, f, te, nu):
        e, ff = w_block(i, f, te, nu)
        return (e, 0, ff)

    def w2_idx(i, f, te, nu):
        e, ff = w_block(i, f, te, nu)
        return (e, ff, 0)

    smem_tile = lambda idx: pl.BlockSpec((1, 1, tm), idx, memory_space=pltpu.SMEM)
    grid_spec = pltpu.PrefetchScalarGridSpec(
        num_scalar_prefetch=2,
        grid=(n_tiles, n_f),
        in_specs=[smem_tile(lambda i, f, te, nu: (i, 0, 0)),
                  smem_tile(lambda i, f, te, nu: (jnp.minimum(i + 1, n_tiles - 1), 0, 0)),
                  pl.BlockSpec(memory_space=pl.ANY),
                  pl.BlockSpec((1, D_MODEL), lambda i, f, te, nu: (0, 0)),
                  pl.BlockSpec((1, D_MODEL, tf), w13_idx),
                  pl.BlockSpec((1, D_MODEL, tf), w13_idx),
                  pl.BlockSpec((1, tf, D_MODEL), w2_idx)],
        out_specs=pl.BlockSpec((tm, D_MODEL), lambda i, f, te, nu: (i, 0)),
        scratch_shapes=[pltpu.VMEM((2, tm, D_MODEL), F32),
                        pltpu.SemaphoreType.DMA((2,)),
                        pltpu.VMEM((tm, D_MODEL), BF16),
                        pltpu.VMEM((tm, D_MODEL), F32)])
    return pl.pallas_call(
        functools.partial(_moe_ffn_body, tm),
        grid_spec=grid_spec,
        out_shape=jax.ShapeDtypeStruct((n_tiles * tm, D_MODEL), F32),
        compiler_params=pltpu.CompilerParams(
            dimension_semantics=("arbitrary", "arbitrary"), vmem_limit_bytes=VMEM_LIMIT),
        name="moe_ffn",
    )(tile_expert, n_used, tok, tok, x, gain, w1, w3, w2)


def _moe_combine_body(tm, final_norm, *refs):
    if final_norm:
        rows_ref, rows_next_ref, x_ref, route_ref, ys_hbm, gf_ref, o_ref, ybuf, sem = refs
    else:
        rows_ref, rows_next_ref, x_ref, route_ref, ys_hbm, o_ref, ybuf, sem = refs
    i = pl.program_id(0)
    slot = lax.rem(i, 2)
    _, wait_next, copy_next = _row_gather(ys_hbm, rows_next_ref, ybuf, sem, 1 - slot, 2 * tm)
    start_cur, wait_cur, _ = _row_gather(ys_hbm, rows_ref, ybuf, sem, slot, 2 * tm)
    pl.when(i == 0)(start_cur)
    wait_cur()
    route = route_ref[...]
    out = x_ref[...] + (route[:, 2:3] * ybuf[slot, 0:tm, :] + route[:, 3:4] * ybuf[slot, tm:2 * tm, :])
    if final_norm:
        out = _rms(out, NORM_EPS) * gf_ref[...]
    o_ref[...] = out
    for r in range(2 * tm):
        copy_next(r).start(priority=r % 2)
    pl.when(i == pl.num_programs(0) - 1)(wait_next)


def _moe_combine(x, route, rows, ys, final_gain, tm):
    n = x.shape[0]
    n_tiles = n // tm
    final_norm = final_gain is not None
    smem_tile = lambda idx: pl.BlockSpec((1, 1, 2 * tm), idx, memory_space=pltpu.SMEM)
    in_specs = [smem_tile(lambda i: (i, 0, 0)),
                smem_tile(lambda i: (jnp.minimum(i + 1, n_tiles - 1), 0, 0)),
                pl.BlockSpec((tm, D_MODEL), lambda i: (i, 0)),
                pl.BlockSpec((tm, LANE), lambda i: (i, 0)),
                pl.BlockSpec(memory_space=pl.ANY)]
    args = [rows, rows, x, route, ys]
    if final_norm:
        in_specs.append(pl.BlockSpec((1, D_MODEL), lambda i: (0, 0)))
        args.append(final_gain)
    return pl.pallas_call(
        functools.partial(_moe_combine_body, tm, final_norm),
        grid=(n_tiles,),
        in_specs=in_specs,
        out_specs=pl.BlockSpec((tm, D_MODEL), lambda i: (i, 0)),
        out_shape=jax.ShapeDtypeStruct((n, D_MODEL), F32),
        scratch_shapes=[pltpu.VMEM((2, 2 * tm, D_MODEL), F32), pltpu.SemaphoreType.DMA((2,))],
        compiler_params=pltpu.CompilerParams(
            dimension_semantics=("arbitrary",), vmem_limit_bytes=VMEM_LIMIT),
        name="moe_combine",
    )(*args)


def _moe(x, gain, router, w1, w3, w2, final_gain):
    n = x.shape[0]
    tm = 512 if n >= 8192 else 256
    tm_c = 256
    n_tiles = -(-(2 * n + N_EXP * (tm - 1)) // tm)
    route = _route(x, gain, router)
    i1 = route[:, 0].astype(jnp.int32)
    i2 = route[:, 1].astype(jnp.int32)
    experts = jnp.arange(N_EXP, dtype=jnp.int32)
    hit = ((i1[:, None] == experts) | (i2[:, None] == experts)).astype(jnp.int32)
    rank = jnp.cumsum(hit, axis=0) - hit
    tiles_e = (jnp.sum(hit, axis=0) + tm - 1) // tm
    tile_end = jnp.cumsum(tiles_e)
    row_off = (tile_end - tiles_e) * tm
    row1 = row_off[i1] + jnp.take_along_axis(rank, i1[:, None], axis=1)[:, 0]
    row2 = row_off[i2] + jnp.take_along_axis(rank, i2[:, None], axis=1)[:, 0]
    token = jnp.arange(n, dtype=jnp.int32)
    tok = jnp.zeros((n_tiles * tm,), jnp.int32).at[jnp.concatenate([row1, row2])].set(
        jnp.concatenate([token, token]), unique_indices=True, mode="promise_in_bounds")
    tile_ids = jnp.arange(n_tiles, dtype=jnp.int32)
    tile_expert = jnp.minimum(
        jnp.sum((tile_end[None, :] <= tile_ids[:, None]).astype(jnp.int32), axis=1), N_EXP - 1)
    n_used = tile_end[N_EXP - 1:].astype(jnp.int32)
    ys = _moe_ffn(x, gain, tok.reshape(n_tiles, 1, tm), tile_expert, n_used, w1, w3, w2,
                  tm, D_EXP // MOE_F_STEPS)
    rows = jnp.concatenate([row1.reshape(n // tm_c, 1, tm_c), row2.reshape(n // tm_c, 1, tm_c)],
                           axis=2).astype(jnp.int32)
    return _moe_combine(x, route, rows, ys, final_gain, tm_c)


def _pad_last(a, width):
    return jnp.pad(a, [(0, 0)] * (a.ndim - 1) + [(0, width - a.shape[-1])])


def _pack_params(p):
    w_in = p["w_in"]
    w_in = jnp.concatenate([_pad_last(w_in[..., :DN_P], DN_SLAB),
                            _pad_last(w_in[..., DN_P:DN_P + RW_P], RW_SLAB),
                            w_in[..., DN_P + RW_P:]], axis=-1).astype(BF16)
    head_par = jnp.stack([_pad_last(p["dn_a_log"], LANE), _pad_last(p["dn_dt_bias"], LANE)], axis=1)
    g2 = jnp.pad(p["rw_g2"], ((0, 0), (0, RW_GATE_PAD - RW_GATE_R), (0, 0)))
    row = lambda a: a[:, None, :]
    return dict(
        ln_mix=row(p["ln_mix"]), w_in=w_in, dn_conv=p["dn_conv"], dn_head=head_par,
        dn_norm=row(jnp.tile(p["dn_norm"], (1, DN_HEADS))),
        rw=[(row(_pad_last(p["rw_mu"], RW_SLAB))[l], row(p["rw_w0"])[l], p["rw_w2"][l],
             row(p["rw_a0"])[l], p["rw_a2"][l], g2[l], row(p["rw_kk"])[l], row(p["rw_ka"])[l],
             p["rw_rk"].reshape(N_LAYERS, 1, RW_W)[l], row(p["rw_ln_g"])[l], row(p["rw_ln_b"])[l])
            for l in range(N_LAYERS)],
        rt_ln=row(p["rt_ln"]), w_out=p["w_out"].astype(BF16), ln_ffn=row(p["ln_ffn"]),
        ffn_w1=p["ffn_w1"].astype(BF16), ffn_w3=p["ffn_w3"].astype(BF16),
        ffn_w2=p["ffn_w2"].astype(BF16),
        moe_router=_pad_last(p["moe_router"], LANE),
        moe_w1=p["moe_w1"].astype(BF16), moe_w3=p["moe_w3"].astype(BF16),
        moe_w2=p["moe_w2"].astype(BF16),
        ln_final=p["ln_final"][None, :],
    )


def _trunk(x, n_seq, t_real, t_pad, c_len, n_par, states, pos0, w):
    n_valid = min(c_len, t_real)
    cos, sin = _rotary_tables(pos0, t_pad)
    convs, shifts, rts = [], [], []
    dns = rws = None
    seq = lambda a: a.reshape(n_seq, t_pad, a.shape[-1])
    flat = lambda a: a.reshape(n_seq * t_pad, a.shape[-1])
    for l in range(N_LAYERS):
        z_dn, z_rw, z_rt = (seq(z) for z in _inproj(x, w["ln_mix"][l], w["w_in"][l]))
        if states is None:
            conv0 = s_dn0 = shift0 = s_rw0 = s_rt0 = None
        else:
            conv0, s_dn0 = states[0][l], states[1][l]
            shift0 = _pad_last(states[2][l], RW_SLAB)[:, None, :]
            s_rw0, s_rt0 = states[3][l], states[4][l]
        o_dn, dns = _dn_group(z_dn, conv0, s_dn0, dns, w["dn_conv"][l], w["dn_head"][l],
                              w["dn_norm"][l], l, c_len, n_valid, n_par)
        o_rw, rws = _rw_group(z_rw, shift0, s_rw0, rws, w["rw"][l], l, c_len, n_valid, n_par)
        c_rt = 2 * c_len if n_valid == c_len and t_pad % (2 * c_len) == 0 else c_len
        o_rt, s_rt = _rt_group(z_rt, cos, sin, s_rt0, w["rt_ln"][l], c_rt, min(c_rt, t_real), n_par)
        x = _outproj(x, flat(o_dn), flat(o_rw), flat(o_rt), w["w_out"][l])
        if l % 2 == 0:
            x = _ffn(x, w["ln_ffn"][l], w["ffn_w1"][l // 2], w["ffn_w3"][l // 2], w["ffn_w2"][l // 2],
                     w["ln_final"] if l == N_LAYERS - 1 else None, D_FF // 2)
        else:
            xr = seq(x)[:, :t_real].reshape(n_seq * t_real, D_MODEL)
            xr = _moe(xr, w["ln_ffn"][l], w["moe_router"][l // 2], w["moe_w1"][l // 2],
                      w["moe_w3"][l // 2], w["moe_w2"][l // 2],
                      w["ln_final"] if l == N_LAYERS - 1 else None)
            x = xr if l == N_LAYERS - 1 else flat(
                jnp.pad(xr.reshape(n_seq, t_real, D_MODEL), ((0, 0), (0, t_pad - t_real), (0, 0))))
        if t_real >= CONV_W - 1:
            convs.append(z_dn[:, t_real - (CONV_W - 1):t_real, :DN_CONV])
        else:
            convs.append(jnp.concatenate([conv0[:, t_real:], z_dn[:, :t_real, :DN_CONV]], axis=1))
        shifts.append(z_rw[:, t_real - 1, :RW_P])
        rts.append(s_rt)
    return x, jnp.stack(convs), dns, jnp.stack(shifts), rws, jnp.stack(rts)


def kernel(x_prompt, x_sample, state_dn_conv, state_dn, state_rw_shift, state_rw, state_rt, ln_mix, w_in, dn_conv, dn_a_log, dn_dt_bias, dn_norm, rw_mu, rw_w0, rw_w2, rw_a0, rw_a2, rw_g2, rw_kk, rw_ka, rw_rk, rw_ln_g, rw_ln_b, rt_ln, w_out, ln_ffn, ffn_w1, ffn_w3, ffn_w2, moe_router, moe_w1, moe_w3, moe_w2, ln_final):
    w = _pack_params(dict(
        ln_mix=ln_mix, w_in=w_in, dn_conv=dn_conv, dn_a_log=dn_a_log, dn_dt_bias=dn_dt_bias,
        dn_norm=dn_norm, rw_mu=rw_mu, rw_w0=rw_w0, rw_w2=rw_w2, rw_a0=rw_a0, rw_a2=rw_a2,
        rw_g2=rw_g2, rw_kk=rw_kk, rw_ka=rw_ka, rw_rk=rw_rk, rw_ln_g=rw_ln_g, rw_ln_b=rw_ln_b,
        rt_ln=rt_ln, w_out=w_out, ln_ffn=ln_ffn, ffn_w1=ffn_w1, ffn_w3=ffn_w3, ffn_w2=ffn_w2,
        moe_router=moe_router, moe_w1=moe_w1, moe_w3=moe_w3, moe_w2=moe_w2, ln_final=ln_final))
    bp, tp, _ = x_prompt.shape
    bs, ts, _ = x_sample.shape

    yp, *p_states = _trunk(x_prompt.reshape(bp * tp, D_MODEL), bp, tp, tp, 64, 8, None, 0, w)
    y_prompt = yp.reshape(bp, tp, D_MODEL)

    ts_pad = 8
    xs = jnp.pad(x_sample, ((0, 0), (0, ts_pad - ts), (0, 0))).reshape(bs * ts_pad, D_MODEL)
    ys, *s_states = _trunk(xs, bs, ts, ts_pad, ts_pad, 8,
                           (state_dn_conv, state_dn, state_rw_shift, state_rw, state_rt),
                           PAST_LEN, w)
    y_sample = ys.reshape(bs, -1, D_MODEL)[:, :ts]
    return (y_prompt, y_sample, *p_states, *s_states)
```

```python
import functools
import math

import numpy as np
import jax
import jax.numpy as jnp
from jax import lax
from jax.experimental import pallas as pl
from jax.experimental.pallas import tpu as pltpu

F32 = jnp.float32
BF16 = jnp.bfloat16

D_MODEL = 1024
N_LAYERS = 2
PAST_LEN = 16384
HEAD = 64
DN_HEADS = 6
DN_QK = DN_HEADS * HEAD
DN_CONV = 3 * DN_QK
CONV_W = 4
RW_HEADS = 6
RW_W = RW_HEADS * HEAD
RW_GATE_R = 160
RT_HEADS = 4
RT_DK = 32
RT_QK = RT_HEADS * RT_DK
RT_W = RT_HEADS * HEAD
DN_P = DN_CONV + DN_QK + 2 * DN_HEADS
RW_P = 3 * RW_W + 64 + 64 + RW_GATE_R
RT_P = 2 * RT_QK + 2 * RT_W
LANE = 128
DN_SLAB = 1664
RW_SLAB = 1536
RT_SLAB = RT_P
P_SLABS = DN_SLAB + RW_SLAB + RT_SLAB
RW_GATE_PAD = RW_SLAB - (3 * RW_W + 128)
N_EXP = 8
D_FF = 2816
D_EXP = 3584
MOE_F_STEPS = 2
ROPE_BASE = 10000.0
NORM_EPS = 1e-6
RW_GN_EPS = 64e-5
RT_GN_EPS = 1e-5
TAIL = 8
VMEM_LIMIT = 56 * 1024 * 1024


def _dot(a, b, precision=None):
    return jnp.dot(a, b, preferred_element_type=F32, precision=precision)


def _split3(x):
    x1 = x.astype(BF16)
    r1 = x - x1.astype(F32)
    x2 = r1.astype(BF16)
    x3 = (r1 - x2.astype(F32)).astype(BF16)
    return x1, x2, x3


def _select_sum(sel, x):
    sel = sel.astype(BF16)
    return sum(jnp.dot(sel, p, preferred_element_type=F32) for p in _split3(x))


def _select_sum_nt(sel, x):
    sel = sel.astype(BF16)
    return sum(lax.dot_general(sel, p, (((1,), (1,)), ((), ())), preferred_element_type=F32)
               for p in _split3(x))


def _mm(a, b):
    return jnp.dot(a.astype(BF16), b.astype(BF16), preferred_element_type=F32)


def _mm_nt(a, b):
    return lax.dot_general(a.astype(BF16), b.astype(BF16), (((1,), (1,)), ((), ())),
                           preferred_element_type=F32)


def _rows(*parts):
    return jnp.concatenate(parts, axis=0)


def _cols(*parts):
    return jnp.concatenate(parts, axis=1)


def _silu(x):
    return x * jax.nn.sigmoid(x)


def _softplus(x):
    return jnp.maximum(x, 0.0) + jnp.log(1.0 + jnp.exp(-jnp.abs(x)))


def _rms(x, eps):
    return x * lax.rsqrt(jnp.mean(x * x, axis=-1, keepdims=True) + eps)


def _tri(c):
    r = lax.broadcasted_iota(jnp.int32, (c, c), 0)
    col = lax.broadcasted_iota(jnp.int32, (c, c), 1)
    return r >= col, r > col, r == col


def _group_ones():
    r = lax.broadcasted_iota(jnp.int32, (LANE, LANE), 0)
    c = lax.broadcasted_iota(jnp.int32, (LANE, LANE), 1)
    return ((r < HEAD) == (c < HEAD)).astype(BF16)


def _group_sum(x, ones_blk, terms=2):
    parts = _split3(x)[:terms]
    out = [sum(jnp.dot(p[:, j:j + LANE], ones_blk, preferred_element_type=F32) for p in parts)
           for j in range(0, x.shape[1], LANE)]
    return out[0] if len(out) == 1 else jnp.concatenate(out, axis=1)


def _bd(x):
    x = x.astype(BF16)
    first = lax.broadcasted_iota(jnp.int32, x.shape, 1) < x.shape[1] // 2
    zero = jnp.zeros_like(x)
    return jnp.concatenate([jnp.where(first, x, zero), jnp.where(first, zero, x)], axis=0)


class _PairTimeMasks:
    def __init__(self, c_len):
        row = lax.broadcasted_iota(jnp.int32, (c_len, 2 * c_len), 0)
        lane = lax.broadcasted_iota(jnp.int32, (c_len, 2 * c_len), 1)
        col = jnp.where(lane < c_len, lane, lane - c_len)
        self.incl = row >= col
        self.strict = row > col
        self.eye = (row == col).astype(F32)
        self.first = lax.broadcasted_iota(jnp.int32, (1, 2 * c_len), 1) < c_len


def _neumann_levels(n_rows):
    return max(0, math.ceil(math.log2(n_rows)) - 1)


def _inproj_body(x_ref, g_ref, w_ref, zdn_ref, zrw_ref, zrt_ref):
    h = (_rms(x_ref[...], NORM_EPS) * g_ref[...]).astype(BF16)
    zdn_ref[...] = _dot(h, w_ref[:, 0:DN_SLAB], None)
    zrw_ref[...] = _dot(h, w_ref[:, DN_SLAB:DN_SLAB + RW_SLAB], None)
    zrt_ref[...] = _dot(h, w_ref[:, DN_SLAB + RW_SLAB:P_SLABS], None)


def _inproj(x, gain, w):
    n = x.shape[0]
    tm = min(1024, n)
    return pl.pallas_call(
        _inproj_body,
        grid=(n // tm,),
        in_specs=[pl.BlockSpec((tm, D_MODEL), lambda i: (i, 0)),
                  pl.BlockSpec((1, D_MODEL), lambda i: (0, 0)),
                  pl.BlockSpec((D_MODEL, P_SLABS), lambda i: (0, 0), pipeline_mode=pl.Buffered(1))],
        out_specs=[pl.BlockSpec((tm, DN_SLAB), lambda i: (i, 0)),
                   pl.BlockSpec((tm, RW_SLAB), lambda i: (i, 0)),
                   pl.BlockSpec((tm, RT_SLAB), lambda i: (i, 0))],
        out_shape=[jax.ShapeDtypeStruct((n, DN_SLAB), F32),
                   jax.ShapeDtypeStruct((n, RW_SLAB), F32),
                   jax.ShapeDtypeStruct((n, RT_SLAB), F32)],
        compiler_params=pltpu.CompilerParams(
            dimension_semantics=("arbitrary",), vmem_limit_bytes=VMEM_LIMIT),
        name="inproj",
    )(x, gain, w)


def _seq_specs(n_par, c_len, width):
    return pl.BlockSpec((n_par, c_len, width), lambda b, c: (b, c, 0))


def _state_spec(n_par, shape):
    return pl.BlockSpec((n_par,) + shape, lambda b, c: (b,) + (0,) * len(shape))


def _stacked_state_out(layer, stacked, n_par, n_seq, state, in_specs, args):
    shape = jax.ShapeDtypeStruct((N_LAYERS, n_seq) + state, F32)
    zeros = (0,) * len(state)
    if layer == 0:
        return pl.BlockSpec((N_LAYERS, n_par) + state, lambda b, c: (0, b) + zeros), shape, {}
    in_specs.append(pl.BlockSpec(memory_space=pl.ANY))
    args.append(stacked)
    return (pl.BlockSpec((None, n_par) + state, lambda b, c: (layer, b) + zeros), shape,
            {len(args) - 1: 1})


def _own_layer(so_ref, layer):
    if layer > 0:
        return so_ref
    so_ref[1:] = jnp.zeros((so_ref.shape[0] - 1,) + so_ref.shape[1:], F32)
    return so_ref.at[0]


def _const_spec(a):
    return pl.BlockSpec(a.shape, lambda b, c: (0,) * a.ndim)


def _dn_body(c_len, n_valid, fresh, n_par, layer, *refs):
    refs = list(refs)
    if layer > 0:
        del refs[-6]
    if fresh:
        z_ref, cw_ref, hp_ref, nw_ref, o_ref, so_ref, xe, s_scr, o_raw = refs
    else:
        z_ref, c0_ref, s0_ref, cw_ref, hp_ref, nw_ref, o_ref, so_ref, xe, s_scr, o_raw = refs
    ci = pl.program_id(1)

    @pl.when(ci == 0)
    def _():
        xe[:, 0:TAIL, :] = jnp.zeros((n_par, TAIL, DN_CONV), F32)
        s_scr[...] = jnp.zeros_like(s_scr)
        if not fresh:
            xe[:, TAIL - (CONV_W - 1):TAIL, :] = c0_ref[...]
            for p in range(DN_HEADS // 2):
                s_scr[:, p, 0:HEAD, 0:HEAD] = s0_ref[:, 2 * p]
                s_scr[:, p, HEAD:LANE, HEAD:LANE] = s0_ref[:, 2 * p + 1]

    tri_f = _tri(c_len)[0].astype(F32)
    live = lax.broadcasted_iota(jnp.int32, (c_len, LANE), 0) < n_valid
    neg_a = -jnp.exp(hp_ref[0:1, :])
    dt_bias = hp_ref[1:2, :]

    ones_blk = _group_ones()
    eye_l = _tri(LANE)[2].astype(F32)
    eye_8 = eye_l[0:8, :]

    def prepare():
        seqs = []
        for i in range(n_par):
            x = z_ref[i, :, 0:DN_CONV]
            xe[i, TAIL:TAIL + c_len, :] = x
            acc = x * cw_ref[CONV_W - 1:CONV_W, :]
            for j in range(CONV_W - 1):
                off = TAIL - (CONV_W - 1) + j
                acc = acc + xe[i, off:off + c_len, :] * cw_ref[j:j + 1, :]
            xe[i, 0:TAIL, :] = xe[i, c_len:c_len + TAIL, :]
            ab = z_ref[i, :, DN_CONV + DN_QK:DN_SLAB]
            g_tok = neg_a * _softplus(ab + dt_bias)
            beta = jax.nn.sigmoid(ab)
            if n_valid < c_len:
                g_tok = jnp.where(live, g_tok, 0.0)
                beta = jnp.where(live, beta, 0.0)
            conv = _silu(acc)
            seqs.append(dict(conv=conv, v=conv[:, 2 * DN_QK:DN_CONV], beta=beta, g_tok=g_tok,
                             gate=_silu(z_ref[i, :, DN_CONV:DN_CONV + DN_QK])))
        for sq in seqs:
            sq["g_cum"] = _select_sum(tri_f, sq["g_tok"])
            qk = sq["conv"][:, 0:2 * DN_QK]
            sq["qk"] = qk * lax.rsqrt(_group_sum(qk * qk, ones_blk) + 1e-6)
        for sq in seqs:
            sq["g_cum_t"] = _select_sum_nt(eye_8, sq["g_cum"])
        return seqs

    def advance(seqs):
        pt = _PairTimeMasks(c_len)
        first_h = lax.broadcasted_iota(jnp.int32, (1, LANE), 1) < HEAD
        first_rows = lax.broadcasted_iota(jnp.int32, (LANE, 1), 0) < HEAD
        chains = []
        for i, sq in enumerate(seqs):
            for p in range(DN_HEADS // 2):
                ha, hb = 2 * p, 2 * p + 1
                lo = LANE * p
                k = sq["qk"][:, DN_QK + lo:DN_QK + lo + LANE]
                v = sq["v"][:, lo:lo + LANE]
                gc_a, gc_b = sq["g_cum"][:, ha:ha + 1], sq["g_cum"][:, hb:hb + 1]
                gc_row = _cols(sq["g_cum_t"][ha:ha + 1, :], sq["g_cum_t"][hb:hb + 1, :])
                g_end = jnp.where(pt.first, gc_a[c_len - 1:c_len], gc_b[c_len - 1:c_len])
                b = jnp.where(first_h, sq["beta"][:, DN_HEADS + ha:DN_HEADS + ha + 1],
                              sq["beta"][:, DN_HEADS + hb:DN_HEADS + hb + 1])
                egc = jnp.exp(jnp.where(first_h, gc_a, gc_b))
                dmask = jnp.where(
                    pt.incl, jnp.exp(jnp.minimum(jnp.where(pt.first, gc_a, gc_b) - gc_row, 0.0)), 0.0)
                q = sq["qk"][:, lo:lo + LANE] * (HEAD ** -0.5)
                kb = k * b
                chains.append(dict(
                    i=i, p=p, k=k, kb=kb, q=q, qe=q * egc, vb=v * b, kbe=kb * egc, dmask=dmask,
                    k_decay=jnp.exp(g_end - gc_row),
                    s_decay=jnp.exp(jnp.where(first_rows, gc_a[c_len - 1:c_len],
                                              gc_b[c_len - 1:c_len]))))
        for c in chains:
            c["kq"] = _mm_nt(_rows(c["kb"], c["q"], eye_l), _bd(c["k"]))
        for c in chains:
            c["np"] = -jnp.where(pt.strict, c["kq"][0:c_len] * c["dmask"], 0.0)
            c["t"] = pt.eye + c["np"]
        for _ in range(_neumann_levels(n_valid)):
            for c in chains:
                c["np"] = _mm(c["np"], _bd(c["np"])).astype(BF16)
                c["np_bd"] = _bd(c["np"])
            for c in chains:
                c["t"] = c["t"] + _mm(c["t"], c["np_bd"])
        for c in chains:
            c["tx"] = _mm(c["t"], _cols(_bd(c["vb"]), _bd(c["kbe"])))
        for c in chains:
            qk = c["kq"][c_len:2 * c_len] * c["dmask"]
            kd_t = c["kq"][2 * c_len:2 * c_len + LANE] * c["k_decay"]
            c["w"] = _mm(_rows(qk, kd_t), _cols(_bd(c["tx"][:, 0:LANE]), _bd(c["tx"][:, LANE:2 * LANE])))
        for c in chains:
            w = c["w"]
            c["s"] = s_scr[c["i"], c["p"]]
            c["res"] = _mm(_rows(c["qe"] - w[0:c_len, LANE:2 * LANE],
                                 w[c_len:c_len + LANE, LANE:2 * LANE]), c["s"])
        for c in chains:
            w, res = c["w"], c["res"]
            lo = LANE * c["p"]
            s_scr[c["i"], c["p"]] = (c["s"] * c["s_decay"] - res[c_len:c_len + LANE]
                                     + w[c_len:c_len + LANE, 0:LANE])
            o_raw[c["i"], :, lo:lo + LANE] = res[0:c_len] + w[0:c_len, 0:LANE]
        for i, sq in enumerate(seqs):
            o = o_raw[i]
            ms = _group_sum(o * o, ones_blk, terms=1) * (1.0 / HEAD)
            o_ref[i] = o * lax.rsqrt(ms + NORM_EPS) * nw_ref[...] * sq["gate"]

    advance(prepare())

    @pl.when(ci == pl.num_programs(1) - 1)
    def _():
        so = _own_layer(so_ref, layer)
        for p in range(DN_HEADS // 2):
            so[:, 2 * p] = s_scr[:, p, 0:HEAD, 0:HEAD]
            so[:, 2 * p + 1] = s_scr[:, p, HEAD:LANE, HEAD:LANE]


def _dn_group(z, conv0, s0, stacked, conv_w, head_par, norm_w, layer, c_len, n_valid, n_par):
    n_seq, t_pad, _ = z.shape
    fresh = s0 is None
    nc = t_pad // c_len
    assert n_valid == c_len or nc == 1
    state = (DN_HEADS, HEAD, HEAD)
    in_specs = [_seq_specs(n_par, c_len, DN_SLAB)]
    args = [z]
    if not fresh:
        in_specs += [_state_spec(n_par, (CONV_W - 1, DN_CONV)), _state_spec(n_par, state)]
        args += [conv0, s0]
    for p in (conv_w, head_par, norm_w):
        in_specs.append(_const_spec(p))
        args.append(p)
    so_spec, so_shape, aliases = _stacked_state_out(layer, stacked, n_par, n_seq, state, in_specs, args)
    return pl.pallas_call(
        functools.partial(_dn_body, c_len, n_valid, fresh, n_par, layer),
        grid=(n_seq // n_par, nc),
        in_specs=in_specs,
        out_specs=[_seq_specs(n_par, c_len, DN_QK), so_spec],
        out_shape=[jax.ShapeDtypeStruct((n_seq, t_pad, DN_QK), F32), so_shape],
        input_output_aliases=aliases,
        scratch_shapes=[pltpu.VMEM((n_par, c_len + TAIL, DN_CONV), F32),
                        pltpu.VMEM((n_par, DN_HEADS // 2, LANE, LANE), F32),
                        pltpu.VMEM((n_par, c_len, DN_QK), F32)],
        compiler_params=pltpu.CompilerParams(
            dimension_semantics=("arbitrary", "arbitrary"), vmem_limit_bytes=VMEM_LIMIT),
        name="dn_group",
    )(*args)


def _rw_body(c_len, n_valid, fresh, n_par, layer, *refs):
    refs = list(refs)
    if layer > 0:
        del refs[-6]
    if fresh:
        (z_ref, mu_ref, w0_ref, w2_ref, a0_ref, a2_ref, g2_ref, kk_ref, ka_ref, rk_ref,
         lng_ref, lnb_ref, o_ref, so_ref, xe, s_scr, y_raw) = refs
    else:
        (z_ref, sh0_ref, s0_ref, mu_ref, w0_ref, w2_ref, a0_ref, a2_ref, g2_ref, kk_ref, ka_ref,
         rk_ref, lng_ref, lnb_ref, o_ref, so_ref, xe, s_scr, y_raw) = refs
    ci = pl.program_id(1)

    @pl.when(ci == 0)
    def _():
        xe[:, 0:TAIL, :] = jnp.zeros((n_par, TAIL, RW_SLAB), F32)
        s_scr[...] = jnp.zeros_like(s_scr)
        if not fresh:
            xe[:, TAIL - 1:TAIL, :] = sh0_ref[...]
            for p in range(RW_HEADS // 2):
                s_scr[:, p, 0:HEAD, 0:HEAD] = s0_ref[:, 2 * p]
                s_scr[:, p, HEAD:LANE, HEAD:LANE] = s0_ref[:, 2 * p + 1]

    tri_f = _tri(c_len)[0].astype(F32)
    live = lax.broadcasted_iota(jnp.int32, (c_len, RW_W), 0) < n_valid
    ones_blk = _group_ones()
    eye_l = _tri(LANE)[2].astype(F32)

    seqs = []
    for i in range(n_par):
        x = z_ref[i]
        xe[i, TAIL:TAIL + c_len, :] = x
        prev = xe[i, TAIL - 1:TAIL - 1 + c_len, :]
        zs = x + (prev - x) * mu_ref[...]
        xe[i, 0:TAIL, :] = xe[i, c_len:c_len + TAIL, :]
        seqs.append(dict(r=zs[:, 0:RW_W], k=zs[:, RW_W:2 * RW_W], v=zs[:, 2 * RW_W:3 * RW_W],
                         wl=zs[:, 3 * RW_W:3 * RW_W + 64], al=zs[:, 3 * RW_W + 64:3 * RW_W + 128],
                         gl=zs[:, 3 * RW_W + 128:RW_SLAB]))
    for sq in seqs:
        sq["w_lora"] = _mm(jnp.tanh(sq["wl"]), w2_ref[...])
        sq["a_lora"] = _mm(sq["al"], a2_ref[...])
        sq["g"] = _mm(jax.nn.sigmoid(sq["gl"]), g2_ref[...])
    for sq in seqs:
        w_log = -_softplus(-(w0_ref[...] + sq["w_lora"])) - 0.5
        a = jax.nn.sigmoid(a0_ref[...] + sq["a_lora"])
        kk_in = sq["k"] * kk_ref[...]
        k2 = sq["k"] * (1.0 + (a - 1.0) * ka_ref[...])
        lw = -jnp.exp(w_log)
        if n_valid < c_len:
            lw = jnp.where(live, lw, 0.0)
            k2 = jnp.where(live, k2, 0.0)
            kk_in = jnp.where(live, kk_in, 0.0)
        sq.update(a=a, kk_in=kk_in, k2=k2, lw=lw)
    for sq in seqs:
        sq["gcum"] = _select_sum(tri_f, sq["lw"])
        kk_in = sq["kk_in"]
        sq["kk"] = kk_in * lax.rsqrt(_group_sum(kk_in * kk_in, ones_blk) + 1e-6)
    for sq in seqs:
        gcum = sq["gcum"]
        e_neg = jnp.exp(-gcum)
        g_end = gcum[c_len - 1:c_len, :]
        e_rest = jnp.exp(g_end - gcum)
        bv = sq["kk"] * sq["a"]
        sq.update(at=-sq["kk"] * jnp.exp(gcum - sq["lw"]), bt=bv * e_neg, kt=sq["k2"] * e_neg,
                  rt=sq["r"] * jnp.exp(gcum), b_rest=bv * e_rest, k_rest=sq["k2"] * e_rest,
                  e_end=jnp.exp(g_end))

    pt = _PairTimeMasks(c_len)
    c2 = 2 * c_len
    chains = []
    for i, sq in enumerate(seqs):
        for p in range(RW_HEADS // 2):
            sl = slice(LANE * p, LANE * (p + 1))
            chains.append(dict(i=i, p=p, sl=sl, at=sq["at"][:, sl], rt=sq["rt"][:, sl],
                               bt_bd=_bd(sq["bt"][:, sl]), kt_bd=_bd(sq["kt"][:, sl]),
                               v_bd=_bd(sq["v"][:, sl]), e_end=sq["e_end"][:, sl],
                               rest=_rows(_bd(sq["b_rest"][:, sl]), _bd(sq["k_rest"][:, sl]))))
    for c in chains:
        c["at_bd"] = _bd(c["at"])
        c["prod"] = _mm_nt(_rows(c["at"], c["rt"]), _rows(c["bt_bd"], c["kt_bd"]))
        c["tr"] = _mm_nt(eye_l, _rows(c["at_bd"], c["v_bd"]))
    for c in chains:
        prod = c["prod"]
        c["np"] = jnp.where(pt.strict, prod[0:c_len, 0:c2], 0.0)
        c["t"] = pt.eye + c["np"]
        c["nak"] = jnp.where(pt.strict, prod[0:c_len, c2:2 * c2], 0.0)
        c["mix"] = _cols(jnp.where(pt.incl, prod[c_len:c2, 0:c2], 0.0),
                         jnp.where(pt.incl, prod[c_len:c2, c2:2 * c2], 0.0))
        c["at_t"] = c["tr"][:, 0:c2]
        c["v_t"] = c["tr"][:, c2:2 * c2]
    for c in chains:
        c["nakv"] = _mm(c["nak"], c["v_bd"])
        c["nakv_t"] = _mm_nt(c["v_t"], _bd(c["nak"]))
    for _ in range(_neumann_levels(n_valid)):
        for c in chains:
            c["np"] = _mm(c["np"], _bd(c["np"])).astype(BF16)
            c["np_bd"] = _bd(c["np"])
        for c in chains:
            c["t"] = c["t"] + _mm(c["t"], c["np_bd"])
    for c in chains:
        c["tx"] = _mm(c["t"], _cols(c["at_bd"], _bd(c["nakv"])))
        c["tx_t"] = _mm_nt(_rows(c["at_t"], c["nakv_t"]), _bd(c["t"]))
    for c in chains:
        tx, tx_t = c["tx"], c["tx_t"]
        c["zy"] = _mm(c["mix"], _rows(_cols(_bd(tx[:, 0:LANE]), _bd(tx[:, LANE:2 * LANE])),
                                      _cols(jnp.zeros((c2, LANE), F32), c["v_bd"])))
        ta_t = tx_t[0:LANE]
        c["m"] = _mm(_rows(_cols(ta_t, jnp.zeros_like(ta_t)), _cols(tx_t[LANE:2 * LANE], c["v_t"])),
                     c["rest"])
    for c in chains:
        c["s"] = s_scr[c["i"], c["p"]]
        c["y"] = _mm_nt(c["rt"] + c["zy"][:, 0:LANE], c["s"])
        c["sm"] = _mm(c["s"], c["m"][0:LANE])
    for c in chains:
        s_scr[c["i"], c["p"]] = c["s"] * c["e_end"] + c["sm"] + c["m"][LANE:2 * LANE]
        y_raw[c["i"], :, c["sl"]] = c["y"] + c["zy"][:, LANE:2 * LANE]
    for i, sq in enumerate(seqs):
        y = y_raw[i]
        d = y - _group_sum(y, ones_blk) * (1.0 / HEAD)
        yn = d * lax.rsqrt(_group_sum(d * d, ones_blk, terms=1) * (1.0 / HEAD) + RW_GN_EPS)
        yn = yn * lng_ref[...] + lnb_ref[...]
        bonus = _group_sum(sq["r"] * sq["k2"] * rk_ref[...], ones_blk, terms=1) * sq["v"]
        o_ref[i] = (yn + bonus) * sq["g"]

    @pl.when(ci == pl.num_programs(1) - 1)
    def _():
        so = _own_layer(so_ref, layer)
        for p in range(RW_HEADS // 2):
            so[:, 2 * p] = s_scr[:, p, 0:HEAD, 0:HEAD]
            so[:, 2 * p + 1] = s_scr[:, p, HEAD:LANE, HEAD:LANE]


def _rw_group(z, shift0, s0, stacked, par, layer, c_len, n_valid, n_par):
    n_seq, t_pad, _ = z.shape
    fresh = s0 is None
    nc = t_pad // c_len
    assert n_valid == c_len or nc == 1
    state = (RW_HEADS, HEAD, HEAD)
    in_specs = [_seq_specs(n_par, c_len, RW_SLAB)]
    args = [z]
    if not fresh:
        in_specs += [_state_spec(n_par, (1, RW_SLAB)), _state_spec(n_par, state)]
        args += [shift0, s0]
    for p in par:
        in_specs.append(_const_spec(p))
        args.append(p)
    so_spec, so_shape, aliases = _stacked_state_out(layer, stacked, n_par, n_seq, state, in_specs, args)
    return pl.pallas_call(
        functools.partial(_rw_body, c_len, n_valid, fresh, n_par, layer),
        grid=(n_seq // n_par, nc),
        in_specs=in_specs,
        out_specs=[_seq_specs(n_par, c_len, RW_W), so_spec],
        out_shape=[jax.ShapeDtypeStruct((n_seq, t_pad, RW_W), F32), so_shape],
        input_output_aliases=aliases,
        scratch_shapes=[pltpu.VMEM((n_par, c_len + TAIL, RW_SLAB), F32),
                        pltpu.VMEM((n_par, RW_HEADS // 2, LANE, LANE), F32),
                        pltpu.VMEM((n_par, c_len, RW_W), F32)],
        compiler_params=pltpu.CompilerParams(
            dimension_semantics=("arbitrary", "arbitrary"), vmem_limit_bytes=VMEM_LIMIT),
        name="rw_group",
    )(*args)


def _rt_body(c_len, n_valid, fresh, n_par, *refs):
    if fresh:
        z_ref, cos_ref, sin_ref, lng_ref, o_ref, so_ref, s_scr, o_raw = refs
    else:
        z_ref, cos_ref, sin_ref, s0_ref, lng_ref, o_ref, so_ref, s_scr, o_raw = refs
    ci = pl.program_id(1)

    @pl.when(ci == 0)
    def _():
        if fresh:
            s_scr[...] = jnp.zeros_like(s_scr)
        else:
            s_scr[...] = s0_ref[...]

    lane = lax.broadcasted_iota(jnp.int32, (c_len, RT_QK), 1)
    first_half = (lane & (RT_DK - 1)) < (RT_DK // 2)
    cos = cos_ref[...]
    sin = sin_ref[...]

    def rotary(t):
        partner = jnp.where(first_half,
                            pltpu.roll(t, RT_QK - RT_DK // 2, axis=1),
                            pltpu.roll(t, RT_DK // 2, axis=1))
        return t * cos + partner * sin

    incl, _, _ = _tri(c_len)
    ri = lax.broadcasted_iota(jnp.int32, (c_len, c_len), 0)
    cj = lax.broadcasted_iota(jnp.int32, (c_len, c_len), 1)
    delta = (ri - cj).astype(F32)
    idx = lax.broadcasted_iota(jnp.int32, (c_len, 1), 0).astype(F32)
    log_decay = [math.log1p(-(2.0 ** (-5.0 - h))) for h in range(RT_HEADS)]
    dmasks = [jnp.where(incl, jnp.exp(delta * lg), 0.0) for lg in log_decay]
    q_scale = [jnp.exp((idx + 1.0) * lg) for lg in log_decay]
    t_row = lax.broadcasted_iota(jnp.int32, (1, c_len), 1).astype(F32)
    k_scale = [jnp.where(t_row < n_valid, jnp.exp((n_valid - 1.0 - t_row) * lg), 0.0) for lg in log_decay]
    ones_blk = _group_ones()
    eye_k = _tri(RT_DK)[2].astype(F32)

    chains = []
    for i in range(n_par):
        q = rotary(z_ref[i, :, 0:RT_QK])
        k = rotary(z_ref[i, :, RT_QK:2 * RT_QK]) * (RT_DK ** -0.5)
        for h in range(RT_HEADS):
            chains.append(dict(
                i=i, h=h, qh=q[:, RT_DK * h:RT_DK * (h + 1)], kh=k[:, RT_DK * h:RT_DK * (h + 1)],
                vh=z_ref[i, :, 2 * RT_QK + HEAD * h:2 * RT_QK + HEAD * (h + 1)]))
    for c in chains:
        c["qk"] = _mm_nt(_rows(c["qh"], eye_k), c["kh"])
    for c in chains:
        h = c["h"]
        c["s"] = s_scr[c["i"], h]
        c["cross"] = _mm(c["qh"] * q_scale[h], c["s"])
        c["kv"] = _mm(c["qk"][c_len:c_len + RT_DK] * k_scale[h], c["vh"])
        c["inner"] = _mm(c["qk"][0:c_len] * dmasks[h], c["vh"])
    for c in chains:
        i, h = c["i"], c["h"]
        s_scr[i, h] = c["s"] * math.exp(n_valid * log_decay[h]) + c["kv"]
        o_raw[i, :, HEAD * h:HEAD * (h + 1)] = c["inner"] + c["cross"]
    for i in range(n_par):
        o = o_raw[i]
        d = o - _group_sum(o, ones_blk) * (1.0 / HEAD)
        on = d * lax.rsqrt(_group_sum(d * d, ones_blk, terms=1) * (1.0 / HEAD) + RT_GN_EPS)
        o_ref[i] = on * lng_ref[...] * _silu(z_ref[i, :, 2 * RT_QK + RT_W:RT_SLAB])

    @pl.when(ci == pl.num_programs(1) - 1)
    def _():
        so_ref[...] = s_scr[...]


def _rt_group(z, cos, sin, s0, ln_g, c_len, n_valid, n_par):
    n_seq, t_pad, _ = z.shape
    fresh = s0 is None
    nc = t_pad // c_len
    assert n_valid == c_len or nc == 1
    state = (RT_HEADS, RT_DK, HEAD)
    table = pl.BlockSpec((c_len, RT_QK), lambda b, c: (c, 0))
    in_specs = [_seq_specs(n_par, c_len, RT_SLAB), table, table]
    args = [z, cos, sin]
    if not fresh:
        in_specs.append(_state_spec(n_par, state))
        args.append(s0)
    in_specs.append(_const_spec(ln_g))
    args.append(ln_g)
    return pl.pallas_call(
        functools.partial(_rt_body, c_len, n_valid, fresh, n_par),
        grid=(n_seq // n_par, nc),
        in_specs=in_specs,
        out_specs=[_seq_specs(n_par, c_len, RT_W), _state_spec(n_par, state)],
        out_shape=[jax.ShapeDtypeStruct((n_seq, t_pad, RT_W), F32),
                   jax.ShapeDtypeStruct((n_seq,) + state, F32)],
        scratch_shapes=[pltpu.VMEM((n_par,) + state, F32),
                        pltpu.VMEM((n_par, c_len, RT_W), F32)],
        compiler_params=pltpu.CompilerParams(
            dimension_semantics=("arbitrary", "arbitrary"), vmem_limit_bytes=VMEM_LIMIT),
        name="rt_group",
    )(*args)


def _rotary_tables(pos0, t_pad):
    half = RT_DK // 2
    inv = ROPE_BASE ** (-np.arange(half, dtype=np.float64) / half)
    ang = (pos0 + np.arange(t_pad, dtype=np.float64))[:, None] * inv[None, :]
    cos = np.tile(np.concatenate([np.cos(ang), np.cos(ang)], axis=1), (1, RT_HEADS))
    sin = np.tile(np.concatenate([-np.sin(ang), np.sin(ang)], axis=1), (1, RT_HEADS))
    return jnp.asarray(cos, F32), jnp.asarray(sin, F32)


def _outproj_body(x_ref, odn_ref, orw_ref, ort_ref, w_ref, o_ref):
    acc = x_ref[...]
    acc = acc + _dot(odn_ref[...].astype(BF16), w_ref[0:DN_QK, :], None)
    acc = acc + _dot(orw_ref[...].astype(BF16), w_ref[DN_QK:DN_QK + RW_W, :], None)
    acc = acc + _dot(ort_ref[...].astype(BF16), w_ref[DN_QK + RW_W:D_MODEL, :], None)
    o_ref[...] = acc


def _outproj(x, o_dn, o_rw, o_rt, w):
    n = x.shape[0]
    tm = min(512, n)
    row = lambda i: (i, 0)
    return pl.pallas_call(
        _outproj_body,
        grid=(n // tm,),
        in_specs=[pl.BlockSpec((tm, D_MODEL), row),
                  pl.BlockSpec((tm, DN_QK), row),
                  pl.BlockSpec((tm, RW_W), row),
                  pl.BlockSpec((tm, RT_W), row),
                  pl.BlockSpec((D_MODEL, D_MODEL), lambda i: (0, 0))],
        out_specs=pl.BlockSpec((tm, D_MODEL), row),
        out_shape=jax.ShapeDtypeStruct((n, D_MODEL), F32),
        compiler_params=pltpu.CompilerParams(
            dimension_semantics=("arbitrary",), vmem_limit_bytes=VMEM_LIMIT),
        name="outproj",
    )(x, o_dn, o_rw, o_rt, w)


def _swiglu(hb, w1, w3, w2):
    return _dot((_silu(_dot(hb, w1)) * _dot(hb, w3)).astype(BF16), w2)


def _ffn_body(final_norm, *refs):
    if final_norm:
        x_ref, g_ref, w1_ref, w3_ref, w2_ref, gf_ref, o_ref, h_scr, acc = refs
    else:
        x_ref, g_ref, w1_ref, w3_ref, w2_ref, o_ref, h_scr, acc = refs
    f = pl.program_id(1)

    @pl.when(f == 0)
    def _():
        x = x_ref[...]
        h_scr[...] = (_rms(x, NORM_EPS) * g_ref[...]).astype(BF16)
        acc[...] = x

    acc[...] += _swiglu(h_scr[...], w1_ref[...], w3_ref[...], w2_ref[...])

    @pl.when(f == pl.num_programs(1) - 1)
    def _():
        if final_norm:
            o_ref[...] = _rms(acc[...], NORM_EPS) * gf_ref[...]
        else:
            o_ref[...] = acc[...]


def _ffn(x, gain, w1, w3, w2, final_gain, tf):
    n = x.shape[0]
    d_f = w1.shape[1]
    final_norm = final_gain is not None
    tm = min(1024, n)
    in_specs = [pl.BlockSpec((tm, D_MODEL), lambda i, f: (i, 0)),
                pl.BlockSpec((1, D_MODEL), lambda i, f: (0, 0)),
                pl.BlockSpec((D_MODEL, tf), lambda i, f: (0, f)),
                pl.BlockSpec((D_MODEL, tf), lambda i, f: (0, f)),
                pl.BlockSpec((tf, D_MODEL), lambda i, f: (f, 0))]
    args = [x, gain, w1, w3, w2]
    if final_norm:
        in_specs.append(pl.BlockSpec((1, D_MODEL), lambda i, f: (0, 0)))
        args.append(final_gain)
    return pl.pallas_call(
        functools.partial(_ffn_body, final_norm),
        grid=(n // tm, d_f // tf),
        in_specs=in_specs,
        out_specs=pl.BlockSpec((tm, D_MODEL), lambda i, f: (i, 0)),
        out_shape=jax.ShapeDtypeStruct((n, D_MODEL), F32),
        scratch_shapes=[pltpu.VMEM((tm, D_MODEL), BF16), pltpu.VMEM((tm, D_MODEL), F32)],
        compiler_params=pltpu.CompilerParams(
            dimension_semantics=("arbitrary", "arbitrary"), vmem_limit_bytes=VMEM_LIMIT),
        name="ffn_dense",
    )(*args)


def _route_body(x_ref, g_ref, router_ref, o_ref):
    h = _rms(x_ref[...], NORM_EPS) * g_ref[...]
    h1, h2, _ = _split3(h)
    r1, r2, _ = _split3(router_ref[...])
    logits = (jnp.dot(h1, r1, preferred_element_type=F32) + jnp.dot(h1, r2, preferred_element_type=F32)
              + jnp.dot(h2, r1, preferred_element_type=F32))
    lane = lax.broadcasted_iota(jnp.int32, logits.shape, 1).astype(F32)
    lg = jnp.where(lane < N_EXP, logits, -jnp.inf)
    m1 = jnp.max(lg, axis=-1, keepdims=True)
    i1 = jnp.min(jnp.where(lg == m1, lane, float(LANE)), axis=-1, keepdims=True)
    lg2 = jnp.where(lane == i1, -jnp.inf, lg)
    m2 = jnp.max(lg2, axis=-1, keepdims=True)
    i2 = jnp.min(jnp.where(lg2 == m2, lane, float(LANE)), axis=-1, keepdims=True)
    e2 = jnp.exp(m2 - m1)
    den = 1.0 + e2
    o_ref[...] = (jnp.where(lane == 0.0, i1, 0.0) + jnp.where(lane == 1.0, i2, 0.0)
                  + jnp.where(lane == 2.0, 1.0 / den, 0.0) + jnp.where(lane == 3.0, e2 / den, 0.0))


def _route(x, gain, router):
    n = x.shape[0]
    tm = min(512, n)
    return pl.pallas_call(
        _route_body,
        grid=(n // tm,),
        in_specs=[pl.BlockSpec((tm, D_MODEL), lambda i: (i, 0)),
                  pl.BlockSpec((1, D_MODEL), lambda i: (0, 0)),
                  pl.BlockSpec((D_MODEL, LANE), lambda i: (0, 0))],
        out_specs=pl.BlockSpec((tm, LANE), lambda i: (i, 0)),
        out_shape=jax.ShapeDtypeStruct((n, LANE), F32),
        compiler_params=pltpu.CompilerParams(
            dimension_semantics=("arbitrary",), vmem_limit_bytes=VMEM_LIMIT),
        name="moe_route",
    )(x, gain, router)


def _row_gather(src_hbm, idx_ref, buf, sem, slot, n_rows):
    def copy(r):
        return pltpu.make_async_copy(src_hbm.at[pl.ds(idx_ref[0, 0, r], 1), :],
                                     buf.at[slot, pl.ds(r, 1), :], sem.at[slot])

    def start():
        def body(r, carry):
            copy(r).start()
            return carry
        lax.fori_loop(0, n_rows, body, 0, unroll=8)

    def wait():
        def body(r, carry):
            copy(r).wait()
            return carry
        lax.fori_loop(0, n_rows, body, 0, unroll=8)

    return start, wait, copy


def _moe_ffn_body(tm, te_ref, nu_ref, tok_ref, tok_next_ref, x_hbm, g_ref, w1_ref, w3_ref, w2_ref,
                  ys_ref, xbuf, sem, h_scr, acc):
    i = pl.program_id(0)
    f = pl.program_id(1)
    n_f = pl.num_programs(1)
    slot = lax.rem(i, 2)
    n_used = nu_ref[0]
    used = i < n_used
    rows_per_step = tm // MOE_F_STEPS
    start_cur, wait_cur, _ = _row_gather(x_hbm, tok_ref, xbuf, sem, slot, tm)
    _, wait_next, copy_next = _row_gather(x_hbm, tok_next_ref, xbuf, sem, 1 - slot, tm)

    @pl.when((f == 0) & (i <= n_used))
    def _():
        pl.when(i == 0)(start_cur)
        wait_cur()

    @pl.when((f == 0) & used)
    def _():
        h_scr[...] = (_rms(xbuf[slot], NORM_EPS) * g_ref[...]).astype(BF16)

    @pl.when(used)
    def _():
        y = _swiglu(h_scr[...], w1_ref[0], w3_ref[0], w2_ref[0])
        for j in range(rows_per_step):
            copy_next(f * rows_per_step + j).start()

        @pl.when(f == 0)
        def _():
            acc[...] = y

        @pl.when(f > 0)
        def _():
            acc[...] += y

    @pl.when(f == n_f - 1)
    def _():
        ys_ref[...] = jnp.where(used, acc[...], 0.0)
        pl.when(used & (i == pl.num_programs(0) - 1))(wait_next)


def _moe_ffn(x, gain, tok, tile_expert, n_used, w1, w3, w2, tm, tf):
    n_tiles = tok.shape[0]
    n_f = D_EXP // tf

    def w_block(i, f, te, nu):
        last = nu[0] - 1
        return te[jnp.minimum(i, last)], jnp.where(i <= last, f, n_f - 1)

    def w13_idx(i, f, te, nu):
        e, ff = w_block(i, f, te, nu)
        return (e, 0, ff)

    def w2_idx(i, f, te, nu):
        e, ff = w_block(i, f, te, nu)
        return (e, ff, 0)

    smem_tile = lambda idx: pl.BlockSpec((1, 1, tm), idx, memory_space=pltpu.SMEM)
    grid_spec = pltpu.PrefetchScalarGridSpec(
        num_scalar_prefetch=2,
        grid=(n_tiles, n_f),
        in_specs=[smem_tile(lambda i, f, te, nu: (i, 0, 0)),
                  smem_tile(lambda i, f, te, nu: (jnp.minimum(i + 1, n_tiles - 1), 0, 0)),
                  pl.BlockSpec(memory_space=pl.ANY),
                  pl.BlockSpec((1, D_MODEL), lambda i, f, te, nu: (0, 0)),
                  pl.BlockSpec((1, D_MODEL, tf), w13_idx),
                  pl.BlockSpec((1, D_MODEL, tf), w13_idx),
                  pl.BlockSpec((1, tf, D_MODEL), w2_idx)],
        out_specs=pl.BlockSpec((tm, D_MODEL), lambda i, f, te, nu: (i, 0)),
        scratch_shapes=[pltpu.VMEM((2, tm, D_MODEL), F32),
                        pltpu.SemaphoreType.DMA((2,)),
                        pltpu.VMEM((tm, D_MODEL), BF16),
                        pltpu.VMEM((tm, D_MODEL), F32)])
    return pl.pallas_call(
        functools.partial(_moe_ffn_body, tm),
        grid_spec=grid_spec,
        out_shape=jax.ShapeDtypeStruct((n_tiles * tm, D_MODEL), F32),
        compiler_params=pltpu.CompilerParams(
            dimension_semantics=("arbitrary", "arbitrary"), vmem_limit_bytes=VMEM_LIMIT),
        name="moe_ffn",
    )(tile_expert, n_used, tok, tok, x, gain, w1, w3, w2)


def _moe_combine_body(tm, final_norm, *refs):
    if final_norm:
        rows_ref, rows_next_ref, x_ref, route_ref, ys_hbm, gf_ref, o_ref, ybuf, sem = refs
    else:
        rows_ref, rows_next_ref, x_ref, route_ref, ys_hbm, o_ref, ybuf, sem = refs
    i = pl.program_id(0)
    slot = lax.rem(i, 2)
    _, wait_next, copy_next = _row_gather(ys_hbm, rows_next_ref, ybuf, sem, 1 - slot, 2 * tm)
    start_cur, wait_cur, _ = _row_gather(ys_hbm, rows_ref, ybuf, sem, slot, 2 * tm)
    pl.when(i == 0)(start_cur)
    wait_cur()
    route = route_ref[...]
    out = x_ref[...] + (route[:, 2:3] * ybuf[slot, 0:tm, :] + route[:, 3:4] * ybuf[slot, tm:2 * tm, :])
    if final_norm:
        out = _rms(out, NORM_EPS) * gf_ref[...]
    o_ref[...] = out
    for r in range(2 * tm):
        copy_next(r).start(priority=r % 2)
    pl.when(i == pl.num_programs(0) - 1)(wait_next)


def _moe_combine(x, route, rows, ys, final_gain, tm):
    n = x.shape[0]
    n_tiles = n // tm
    final_norm = final_gain is not None
    smem_tile = lambda idx: pl.BlockSpec((1, 1, 2 * tm), idx, memory_space=pltpu.SMEM)
    in_specs = [smem_tile(lambda i: (i, 0, 0)),
                smem_tile(lambda i: (jnp.minimum(i + 1, n_tiles - 1), 0, 0)),
                pl.BlockSpec((tm, D_MODEL), lambda i: (i, 0)),
                pl.BlockSpec((tm, LANE), lambda i: (i, 0)),
                pl.BlockSpec(memory_space=pl.ANY)]
    args = [rows, rows, x, route, ys]
    if final_norm:
        in_specs.append(pl.BlockSpec((1, D_MODEL), lambda i: (0, 0)))
        args.append(final_gain)
    return pl.pallas_call(
        functools.partial(_moe_combine_body, tm, final_norm),
        grid=(n_tiles,),
        in_specs=in_specs,
        out_specs=pl.BlockSpec((tm, D_MODEL), lambda i: (i, 0)),
        out_shape=jax.ShapeDtypeStruct((n, D_MODEL), F32),
        scratch_shapes=[pltpu.VMEM((2, 2 * tm, D_MODEL), F32), pltpu.SemaphoreType.DMA((2,))],
        compiler_params=pltpu.CompilerParams(
            dimension_semantics=("arbitrary",), vmem_limit_bytes=VMEM_LIMIT),
        name="moe_combine",
    )(*args)


def _moe(x, gain, router, w1, w3, w2, final_gain):
    n = x.shape[0]
    tm = 512 if n >= 8192 else 256
    tm_c = 256
    n_tiles = -(-(2 * n + N_EXP * (tm - 1)) // tm)
    route = _route(x, gain, router)
    i1 = route[:, 0].astype(jnp.int32)
    i2 = route[:, 1].astype(jnp.int32)
    experts = jnp.arange(N_EXP, dtype=jnp.int32)
    hit = ((i1[:, None] == experts) | (i2[:, None] == experts)).astype(jnp.int32)
    rank = jnp.cumsum(hit, axis=0) - hit
    tiles_e = (jnp.sum(hit, axis=0) + tm - 1) // tm
    tile_end = jnp.cumsum(tiles_e)
    row_off = (tile_end - tiles_e) * tm
    row1 = row_off[i1] + jnp.take_along_axis(rank, i1[:, None], axis=1)[:, 0]
    row2 = row_off[i2] + jnp.take_along_axis(rank, i2[:, None], axis=1)[:, 0]
    token = jnp.arange(n, dtype=jnp.int32)
    tok = jnp.zeros((n_tiles * tm,), jnp.int32).at[jnp.concatenate([row1, row2])].set(
        jnp.concatenate([token, token]), unique_indices=True, mode="promise_in_bounds")
    tile_ids = jnp.arange(n_tiles, dtype=jnp.int32)
    tile_expert = jnp.minimum(
        jnp.sum((tile_end[None, :] <= tile_ids[:, None]).astype(jnp.int32), axis=1), N_EXP - 1)
    n_used = tile_end[N_EXP - 1:].astype(jnp.int32)
    ys = _moe_ffn(x, gain, tok.reshape(n_tiles, 1, tm), tile_expert, n_used, w1, w3, w2,
                  tm, D_EXP // MOE_F_STEPS)
    rows = jnp.concatenate([row1.reshape(n // tm_c, 1, tm_c), row2.reshape(n // tm_c, 1, tm_c)],
                           axis=2).astype(jnp.int32)
    return _moe_combine(x, route, rows, ys, final_gain, tm_c)


def _pad_last(a, width):
    return jnp.pad(a, [(0, 0)] * (a.ndim - 1) + [(0, width - a.shape[-1])])


def _pack_params(p):
    w_in = p["w_in"]
    w_in = jnp.concatenate([_pad_last(w_in[..., :DN_P], DN_SLAB),
                            _pad_last(w_in[..., DN_P:DN_P + RW_P], RW_SLAB),
                            w_in[..., DN_P + RW_P:]], axis=-1).astype(BF16)
    head_par = jnp.stack([_pad_last(p["dn_a_log"], LANE), _pad_last(p["dn_dt_bias"], LANE)], axis=1)
    g2 = jnp.pad(p["rw_g2"], ((0, 0), (0, RW_GATE_PAD - RW_GATE_R), (0, 0)))
    row = lambda a: a[:, None, :]
    return dict(
        ln_mix=row(p["ln_mix"]), w_in=w_in, dn_conv=p["dn_conv"], dn_head=head_par,
        dn_norm=row(jnp.tile(p["dn_norm"], (1, DN_HEADS))),
        rw=[(row(_pad_last(p["rw_mu"], RW_SLAB))[l], row(p["rw_w0"])[l], p["rw_w2"][l],
             row(p["rw_a0"])[l], p["rw_a2"][l], g2[l], row(p["rw_kk"])[l], row(p["rw_ka"])[l],
             p["rw_rk"].reshape(N_LAYERS, 1, RW_W)[l], row(p["rw_ln_g"])[l], row(p["rw_ln_b"])[l])
            for l in range(N_LAYERS)],
        rt_ln=row(p["rt_ln"]), w_out=p["w_out"].astype(BF16), ln_ffn=row(p["ln_ffn"]),
        ffn_w1=p["ffn_w1"].astype(BF16), ffn_w3=p["ffn_w3"].astype(BF16),
        ffn_w2=p["ffn_w2"].astype(BF16),
        moe_router=_pad_last(p["moe_router"], LANE),
        moe_w1=p["moe_w1"].astype(BF16), moe_w3=p["moe_w3"].astype(BF16),
        moe_w2=p["moe_w2"].astype(BF16),
        ln_final=p["ln_final"][None, :],
    )


def _trunk(x, n_seq, t_real, t_pad, c_len, n_par, states, pos0, w):
    n_valid = min(c_len, t_real)
    cos, sin = _rotary_tables(pos0, t_pad)
    convs, shifts, rts = [], [], []
    dns = rws = None
    seq = lambda a: a.reshape(n_seq, t_pad, a.shape[-1])
    flat = lambda a: a.reshape(n_seq * t_pad, a.shape[-1])
    for l in range(N_LAYERS):
        z_dn, z_rw, z_rt = (seq(z) for z in _inproj(x, w["ln_mix"][l], w["w_in"][l]))
        if states is None:
            conv0 = s_dn0 = shift0 = s_rw0 = s_rt0 = None
        else:
            conv0, s_dn0 = states[0][l], states[1][l]
            shift0 = _pad_last(states[2][l], RW_SLAB)[:, None, :]
            s_rw0, s_rt0 = states[3][l], states[4][l]
        o_dn, dns = _dn_group(z_dn, conv0, s_dn0, dns, w["dn_conv"][l], w["dn_head"][l],
                              w["dn_norm"][l], l, c_len, n_valid, n_par)
        o_rw, rws = _rw_group(z_rw, shift0, s_rw0, rws, w["rw"][l], l, c_len, n_valid, n_par)
        c_rt = 2 * c_len if n_valid == c_len and t_pad % (2 * c_len) == 0 else c_len
        o_rt, s_rt = _rt_group(z_rt, cos, sin, s_rt0, w["rt_ln"][l], c_rt, min(c_rt, t_real), n_par)
        x = _outproj(x, flat(o_dn), flat(o_rw), flat(o_rt), w["w_out"][l])
        if l % 2 == 0:
            x = _ffn(x, w["ln_ffn"][l], w["ffn_w1"][l // 2], w["ffn_w3"][l // 2], w["ffn_w2"][l // 2],
                     w["ln_final"] if l == N_LAYERS - 1 else None, D_FF // 2)
        else:
            xr = seq(x)[:, :t_real].reshape(n_seq * t_real, D_MODEL)
            xr = _moe(xr, w["ln_ffn"][l], w["moe_router"][l // 2], w["moe_w1"][l // 2],
                      w["moe_w3"][l // 2], w["moe_w2"][l // 2],
                      w["ln_final"] if l == N_LAYERS - 1 else None)
            x = xr if l == N_LAYERS - 1 else flat(
                jnp.pad(xr.reshape(n_seq, t_real, D_MODEL), ((0, 0), (0, t_pad - t_real), (0, 0))))
        if t_real >= CONV_W - 1:
            convs.append(z_dn[:, t_real - (CONV_W - 1):t_real, :DN_CONV])
        else:
            convs.append(jnp.concatenate([conv0[:, t_real:], z_dn[:, :t_real, :DN_CONV]], axis=1))
        shifts.append(z_rw[:, t_real - 1, :RW_P])
        rts.append(s_rt)
    return x, jnp.stack(convs), dns, jnp.stack(shifts), rws, jnp.stack(rts)


def kernel(x_prompt, x_sample, state_dn_conv, state_dn, state_rw_shift, state_rw, state_rt, ln_mix, w_in, dn_conv, dn_a_log, dn_dt_bias, dn_norm, rw_mu, rw_w0, rw_w2, rw_a0, rw_a2, rw_g2, rw_kk, rw_ka, rw_rk, rw_ln_g, rw_ln_b, rt_ln, w_out, ln_ffn, ffn_w1, ffn_w3, ffn_w2, moe_router, moe_w1, moe_w3, moe_w2, ln_final):
    w = _pack_params(dict(
        ln_mix=ln_mix, w_in=w_in, dn_conv=dn_conv, dn_a_log=dn_a_log, dn_dt_bias=dn_dt_bias,
        dn_norm=dn_norm, rw_mu=rw_mu, rw_w0=rw_w0, rw_w2=rw_w2, rw_a0=rw_a0, rw_a2=rw_a2,
        rw_g2=rw_g2, rw_kk=rw_kk, rw_ka=rw_ka, rw_rk=rw_rk, rw_ln_g=rw_ln_g, rw_ln_b=rw_ln_b,
        rt_ln=rt_ln, w_out=w_out, ln_ffn=ln_ffn, ffn_w1=ffn_w1, ffn_w3=ffn_w3, ffn_w2=ffn_w2,
        moe_router=moe_router, moe_w1=moe_w1, moe_w3=moe_w3, moe_w2=moe_w2, ln_final=ln_final))
    bp, tp, _ = x_prompt.shape
    bs, ts, _ = x_sample.shape

    yp, *p_states = _trunk(x_prompt.reshape(bp * tp, D_MODEL), bp, tp, tp, 64, 8, None, 0, w)
    y_prompt = yp.reshape(bp, tp, D_MODEL)

    ts_pad = 8
    xs = jnp.pad(x_sample, ((0, 0), (0, ts_pad - ts), (0, 0))).reshape(bs * ts_pad, D_MODEL)
    ys, *s_states = _trunk(xs, bs, ts, ts_pad, ts_pad, 8,
                           (state_dn_conv, state_dn, state_rw_shift, state_rw, state_rt),
                           PAST_LEN, w)
    y_sample = ys.reshape(bs, -1, D_MODEL)[:, :ts]
    return (y_prompt, y_sample, *p_states, *s_states)
```
